```python
import math
import jax
import jax.numpy as jnp
from jax import lax
import numpy as np

D_MODEL = 4096
BATCH = 4
SEQ = 2048
DEPTH = 1
DEC_BATCH = 128
DEC_SEQ = 1
PAST_LEN = 2048
PAGE_SIZE = 128

HEAD_DIM = 128
DSA_HEADS = 16
DSA_KV_HEADS = 4
IDX_HEADS = 16
IDX_DIM = 64
IDX_TOPK = 256
NSA_HEADS = 16
NSA_KV_GROUPS = 4
CMP_STRIDE = 16
CMP_BLOCK = 32
SLC_BLOCK = 64
SLC_TOPN = 16
WINDOW = 512
WIN_BLOCK = 128
Q_BLOCK = 128
SLC_Q_BLOCK = 32
N_GROUPS = 8
EXPERTS_PER_GROUP = 8
N_EXPERTS = N_GROUPS * EXPERTS_PER_GROUP
EXPERT_TOPK = 2
D_EXPERT = 1024
DEEPNORM_ALPHA = (2.0 * DEPTH) ** 0.25
DEEPNORM_BETA = (8.0 * DEPTH) ** -0.25
LN_EPS = 1e-5
NEG_INF = -1e30
FORCE_SCORE = 1e9
PROJ_LAYOUT = (
    ('dsa_q', DSA_HEADS * HEAD_DIM),
    ('dsa_k', DSA_KV_HEADS * HEAD_DIM),
    ('dsa_v', DSA_KV_HEADS * HEAD_DIM),
    ('idx_q', IDX_HEADS * IDX_DIM),
    ('idx_k', IDX_DIM),
    ('idx_w', IDX_HEADS),
    ('nsa_q', NSA_HEADS * HEAD_DIM),
    ('cmp_k', NSA_KV_GROUPS * HEAD_DIM),
    ('cmp_v', NSA_KV_GROUPS * HEAD_DIM),
    ('slc_k', NSA_KV_GROUPS * HEAD_DIM),
    ('slc_v', NSA_KV_GROUPS * HEAD_DIM),
    ('win_k', NSA_KV_GROUPS * HEAD_DIM),
    ('win_v', NSA_KV_GROUPS * HEAD_DIM),
    ('nsa_gate', NSA_HEADS * 3),
)
PROJ_WIDTH = sum(s for _, s in PROJ_LAYOUT)
MIX_WIDTH = (DSA_HEADS + NSA_HEADS) * HEAD_DIM

kernel_name = 'hybrid_dsa_nsa_hmoe_decode_step'


def alibi_slopes(n):
    return 2.0 ** (-8.0 * jnp.arange(1, n + 1, dtype=jnp.float32) / n)


def layer_norm(x, g, b):
    xf = x.astype(jnp.float32)
    mu = jnp.mean(xf, axis=-1, keepdims=True)
    var = jnp.mean(jnp.square(xf - mu), axis=-1, keepdims=True)
    return ((xf - mu) * lax.rsqrt(var + LN_EPS) * g + b).astype(x.dtype)


def masked_softmax(s, valid):
    p = jax.nn.softmax(jnp.where(valid, s, NEG_INF), axis=-1)
    return p * valid


def to_chunks(a, qb):
    b, s = a.shape[:2]
    return jnp.moveaxis(a.reshape((b, s // qb, qb) + a.shape[2:]), 1, 0)


def from_chunks(a):
    a = jnp.moveaxis(a, 0, 1)
    return a.reshape((a.shape[0], a.shape[1] * a.shape[2]) + a.shape[3:])


def split_proj(h):
    lead = h.shape[:-1]
    cuts = np.cumsum([s for _, s in PROJ_LAYOUT])[:-1].tolist()
    a = dict(zip([n for n, _ in PROJ_LAYOUT], jnp.split(h, cuts, axis=-1)))

    def heads(v, n, d):
        return v.reshape(lead + (n, d))

    def kv(k, v, n):
        return jnp.stack([heads(k, n, HEAD_DIM), heads(v, n, HEAD_DIM)], axis=-2)

    return {
        'dsa_q': heads(a['dsa_q'], DSA_HEADS, HEAD_DIM),
        'dsa_kv': kv(a['dsa_k'], a['dsa_v'], DSA_KV_HEADS),
        'idx_q': heads(a['idx_q'], IDX_HEADS, IDX_DIM),
        'idx_k': a['idx_k'],
        'idx_w': a['idx_w'] * IDX_HEADS ** -0.5,
        'nsa_q': heads(a['nsa_q'], NSA_HEADS, HEAD_DIM),
        'cmp_kv': kv(a['cmp_k'], a['cmp_v'], NSA_KV_GROUPS),
        'slc_kv': kv(a['slc_k'], a['slc_v'], NSA_KV_GROUPS),
        'win_kv': kv(a['win_k'], a['win_v'], NSA_KV_GROUPS),
        'nsa_gate': jax.nn.sigmoid(heads(a['nsa_gate'], NSA_HEADS, 3)),
    }


def dense_attend(q, kv, dist, valid, slopes):
    n, nq, h, dh = q.shape
    g = kv.shape[2]
    qg = (q * HEAD_DIM ** -0.5).reshape(n, nq, g, h // g, dh)
    s = jnp.einsum('nqgrd,nkgd->nqgrk', qg, kv[..., 0, :]).astype(jnp.float32)
    s = s - slopes.reshape(g, h // g)[:, :, None] * dist.astype(jnp.float32)[:, None, None, :]
    p = masked_softmax(s, valid[:, None, None, :])
    o = jnp.einsum('nqgrk,nkgd->nqgrd', p.astype(kv.dtype), kv[..., 1, :])
    return o.reshape(n, nq, h, dh), p


def gathered_attend(q, kv, dist, slopes):
    n, nq, h, dh = q.shape
    g = kv.shape[3]
    qg = (q * HEAD_DIM ** -0.5).reshape(n, nq, g, h // g, dh)
    s = jnp.einsum('nqgrd,nqkgd->nqgrk', qg, kv[..., 0, :]).astype(jnp.float32)
    s = s - slopes.reshape(g, h // g)[:, :, None] * dist.astype(jnp.float32)[:, :, None, None, :]
    p = masked_softmax(s, (dist >= 0)[:, :, None, None, :])
    o = jnp.einsum('nqgrk,nqkgd->nqgrd', p.astype(kv.dtype), kv[..., 1, :])
    return o.reshape(n, nq, h, dh)


def slc_attend(q, kvb, pos, t, slopes):
    n, nq, h, dh = q.shape
    g = kvb.shape[2]
    kvf = kvb.reshape((n, nq, g, -1) + kvb.shape[-2:])
    dist = (t[None, :, None, None] - pos.reshape(n, nq, g, -1)).astype(jnp.float32)
    qg = (q * HEAD_DIM ** -0.5).reshape(n, nq, g, h // g, dh)
    s = jnp.einsum('nqgrd,nqgmd->nqgrm', qg, kvf[..., 0, :]).astype(jnp.float32)
    s = s - slopes.reshape(g, h // g)[None, None, :, :, None] * dist[:, :, :, None, :]
    p = masked_softmax(s, (dist >= 0)[:, :, :, None, :])
    o = jnp.einsum('nqgrm,nqgmd->nqgrd', p.astype(kvf.dtype), kvf[..., 1, :])
    return o.reshape(n, nq, h, dh)


def indexer_topk(qi, wi, ki, t, k_keep):
    a = jax.nn.relu(jnp.einsum('nqhe,nle->nqhl', qi, ki).astype(jnp.float32))
    sc = jnp.einsum('nqhl,nqh->nql', a, wi.astype(jnp.float32))
    causal = jnp.arange(ki.shape[1])[None, :] <= t[:, None]
    return lax.top_k(jnp.where(causal[None], sc, NEG_INF), k_keep)[1]


def compress_partial(rows, w1):
    n, lr, g, c, dh = rows.shape
    sub = rows.reshape(n, lr // CMP_STRIDE, CMP_STRIDE, g, c, dh)
    w = w1.reshape(c, CMP_BLOCK // CMP_STRIDE, CMP_STRIDE, dh, -1)
    return jnp.einsum('nsjgcd,cmjde->nsmgce', sub, w)


def compress_finish(part, w1, w2, pe):
    r = CMP_BLOCK // CMP_STRIDE
    n_cmp = part.shape[1] - r + 1
    h = part[:, 0:n_cmp, 0]
    for m in range(1, r):
        h = h + part[:, m:m + n_cmp, m]
    h = h + jnp.einsum('cjd,cjde->ce', pe, w1)
    return jnp.einsum('nkgce,cef->nkgcf', jax.nn.gelu(h), w2)


def cmp_attend_select(q, ckv, t, seq_len, slopes):
    nc = ckv.shape[1]
    ends = jnp.arange(nc) * CMP_STRIDE + (CMP_BLOCK - 1)
    dist = t[:, None] - ends[None, :]
    o_c, p = dense_attend(q, ckv, dist, dist >= 0, slopes)
    imp = jnp.sum(p, axis=3)
    n_slc = -(-seq_len // SLC_BLOCK)
    cs = jnp.arange(nc) * CMP_STRIDE
    bs = jnp.arange(n_slc) * SLC_BLOCK
    cover = ((cs[:, None] < bs[None, :] + SLC_BLOCK) & (cs[:, None] + CMP_BLOCK > bs[None, :])).astype(jnp.float32)
    score = jnp.einsum('nqgk,kj->nqgj', imp, cover)
    j = jnp.arange(n_slc)[None, :]
    cur = (t // SLC_BLOCK)[:, None]
    forced = (j == 0) | (j == cur) | (j == cur - 1)
    admissible = j * SLC_BLOCK <= t[:, None]
    score = jnp.where(forced[None, :, None, :], FORCE_SCORE,
                      jnp.where(admissible[None, :, None, :], score, NEG_INF))
    blk = lax.top_k(score, min(SLC_TOPN, n_slc))[1]
    return o_c, blk


def window_prompt(q, wkv, slopes):
    b, s_len, h, dh = q.shape
    g = wkv.shape[2]
    n_c = s_len // WIN_BLOCK
    pad = jnp.pad(wkv, ((0, 0), (WINDOW, 0), (0, 0), (0, 0), (0, 0)))
    band = jnp.concatenate(
        [pad[:, o * WIN_BLOCK:o * WIN_BLOCK + s_len].reshape(b, n_c, WIN_BLOCK, g, 2, dh)
         for o in range(WINDOW // WIN_BLOCK + 1)], axis=2)
    i = jnp.arange(WIN_BLOCK)
    j = jnp.arange(WINDOW + WIN_BLOCK)

    def chunk(args):
        qc, kc, c = args
        tq = c * WIN_BLOCK + i
        sk = c * WIN_BLOCK - WINDOW + j
        dist = tq[:, None] - sk[None, :]
        valid = (dist >= 0) & (dist <= WINDOW) & (sk >= 0)[None, :]
        return dense_attend(qc, kc, dist, valid, slopes)[0]

    return from_chunks(lax.map(chunk, (to_chunks(q, WIN_BLOCK), jnp.moveaxis(band, 1, 0), jnp.arange(n_c))))


def combine_branches(gate, o_c, o_s, o_w):
    return gate[..., 0:1] * o_c + gate[..., 1:2] * o_s + gate[..., 2:3] * o_w


def gather_paged(pool, layer, page_table, new_rows, pos, head=None):
    n, page = page_table.shape[0], pool.shape[2]
    past_len = page_table.shape[1] * page
    bidx = jnp.arange(n).reshape((n,) + (1,) * (pos.ndim - 1))
    pc = jnp.clip(pos, 0, past_len - 1)
    nc = jnp.clip(pos - past_len, 0, new_rows.shape[1] - 1)
    phys = page_table[bidx, pc // page]
    if head is None:
        past, new = pool[layer, phys, pc % page], new_rows[bidx, nc]
    else:
        past, new = pool[layer, phys, pc % page, head], new_rows[bidx, nc, head]
    is_past = (pos < past_len).reshape(pos.shape + (1,) * (past.ndim - pos.ndim))
    return jnp.where(is_past, past, new)


def mixer_prompt(x, w_in, w_out, cmp_w1, cmp_w2, cmp_pe):
    b, s_len, _ = x.shape
    p = split_proj(x @ w_in)
    t = jnp.arange(s_len)
    sl_a = alibi_slopes(DSA_HEADS)
    sl_b = alibi_slopes(NSA_HEADS)
    k_keep = min(IDX_TOPK, s_len // 4)
    kv_a, ki = p['dsa_kv'], p['idx_k']

    def dsa_chunk(args):
        qc, qic, wic, tc = args
        idx = indexer_topk(qic, wic, ki, tc, k_keep)
        kvg = jax.vmap(lambda r, ii: r[ii])(kv_a, idx)
        return gathered_attend(qc, kvg, tc[None, :, None] - idx, sl_a)

    o_a = from_chunks(lax.map(dsa_chunk, (to_chunks(p['dsa_q'], Q_BLOCK), to_chunks(p['idx_q'], Q_BLOCK),
                                          to_chunks(p['idx_w'], Q_BLOCK), t.reshape(-1, Q_BLOCK))))
    q_b = p['nsa_q']
    ckv = compress_finish(compress_partial(p['cmp_kv'], cmp_w1), cmp_w1, cmp_w2, cmp_pe)
    o_c, blk = cmp_attend_select(q_b, ckv, t, s_len, sl_b)
    slc_kv = p['slc_kv']
    bidx = jnp.arange(b)[:, None, None, None, None]
    gidx = jnp.arange(NSA_KV_GROUPS)[None, None, :, None, None]

    def slc_chunk(args):
        qc, bc, tc = args
        pos = bc[..., None] * SLC_BLOCK + jnp.arange(SLC_BLOCK)
        return slc_attend(qc, slc_kv[bidx, jnp.clip(pos, 0, s_len - 1), gidx], pos, tc, sl_b)

    o_s = from_chunks(lax.map(slc_chunk, (to_chunks(q_b, SLC_Q_BLOCK), to_chunks(blk, SLC_Q_BLOCK),
                                          t.reshape(-1, SLC_Q_BLOCK))))
    o_w = window_prompt(q_b, p['win_kv'], sl_b)
    o_b = combine_branches(p['nsa_gate'], o_c, o_s, o_w)
    mix = jnp.concatenate([o_a.reshape(b, s_len, -1), o_b.reshape(b, s_len, -1)], axis=-1) @ w_out
    win_state = p['win_kv'][:, s_len - min(WINDOW, s_len):]
    return mix, (p['dsa_kv'], p['idx_k'], p['cmp_kv'], p['slc_kv'], win_state)


def mixer_sample(x, layer, cache_dsa_kv, cache_dsa_kidx, cache_cmp_kv, cache_slc_kv, win_state,
                 page_table, w_in, w_out, cmp_w1, cmp_w2, cmp_pe):
    n, t_new, _ = x.shape
    past_len = page_table.shape[1] * cache_dsa_kv.shape[2]
    total = past_len + t_new
    p = split_proj(x @ w_in)
    t = past_len + jnp.arange(t_new)
    sl_a = alibi_slopes(DSA_HEADS)
    sl_b = alibi_slopes(NSA_HEADS)
    ki_all = jnp.concatenate([cache_dsa_kidx[layer, page_table].reshape(n, past_len, IDX_DIM), p['idx_k']], axis=1)
    idx = indexer_topk(p['idx_q'], p['idx_w'], ki_all, t, min(IDX_TOPK, total // 4))
    kvg = gather_paged(cache_dsa_kv, layer, page_table, p['dsa_kv'], idx)
    o_a = gathered_attend(p['dsa_q'], kvg, t[None, :, None] - idx, sl_a)
    past_cmp = cache_cmp_kv[layer, page_table].reshape((n, past_len) + cache_cmp_kv.shape[3:])
    part = compress_partial(past_cmp, cmp_w1)
    n_new = (t_new // CMP_STRIDE) * CMP_STRIDE
    if n_new > 0:
        part = jnp.concatenate([part, compress_partial(p['cmp_kv'][:, :n_new], cmp_w1)], axis=1)
    ckv = compress_finish(part, cmp_w1, cmp_w2, cmp_pe)
    q_b = p['nsa_q']
    o_c, blk = cmp_attend_select(q_b, ckv, t, total, sl_b)
    pos = blk[..., None] * SLC_BLOCK + jnp.arange(SLC_BLOCK)
    gidx = jnp.arange(NSA_KV_GROUPS)[None, None, :, None, None]
    kvb = gather_paged(cache_slc_kv, layer, page_table, p['slc_kv'], pos, head=gidx)
    o_s = slc_attend(q_b, kvb, pos, t, sl_b)
    keys = jnp.concatenate([win_state, p['win_kv']], axis=1)
    n_buf = win_state.shape[1]
    dist = t[:, None] - (past_len - n_buf + jnp.arange(n_buf + t_new))[None, :]
    o_w = dense_attend(q_b, keys, dist, (dist >= 0) & (dist <= WINDOW), sl_b)[0]
    o_b = combine_branches(p['nsa_gate'], o_c, o_s, o_w)
    mix = jnp.concatenate([o_a.reshape(n, t_new, -1), o_b.reshape(n, t_new, -1)], axis=-1) @ w_out
    return mix, (p['dsa_kv'], p['idx_k'], p['cmp_kv'], p['slc_kv'], keys[:, t_new:])


def hier_moe(x, layer, w_rg, b_rg, w_re, b_re, w_gate, w_up, w_down):
    lead, dm = x.shape[:-1], x.shape[-1]
    xt = x.reshape(-1, dm)
    n = xt.shape[0]
    rows = jnp.arange(n)
    lg = (xt @ w_rg[layer] + b_rg[layer]).astype(jnp.float32)
    grp = jnp.argmax(lg, axis=-1)
    p_grp = jax.nn.softmax(lg, axis=-1)[rows, grp]
    le = (xt @ w_re[layer] + b_re[layer]).astype(jnp.float32).reshape(n, N_GROUPS, EXPERTS_PER_GROUP)[rows, grp]
    top_p, top_j = lax.top_k(jax.nn.softmax(le, axis=-1), EXPERT_TOPK)
    gate = p_grp[:, None] * top_p / jnp.sum(top_p, axis=-1, keepdims=True)
    e = (grp[:, None] * EXPERTS_PER_GROUP + top_j).reshape(-1).astype(jnp.int32)
    a_n = e.shape[0]
    tb = min(128, max(8, 1 << max(0, (a_n // N_EXPERTS).bit_length() - 1)))
    nb = -(-(a_n + N_EXPERTS * (tb - 1)) // tb)
    counts = jax.ops.segment_sum(jnp.ones_like(e), e, num_segments=N_EXPERTS)
    order = jnp.argsort(e)
    e_sorted = e[order]
    starts = jnp.cumsum(counts) - counts
    pad_counts = (counts + tb - 1) // tb * tb
    pad_ends = jnp.cumsum(pad_counts)
    pad_starts = pad_ends - pad_counts
    dest_sorted = pad_starts[e_sorted] + jnp.arange(a_n, dtype=jnp.int32) - starts[e_sorted]
    dest = jnp.zeros_like(e).at[order].set(dest_sorted)
    tok = jnp.arange(a_n) // EXPERT_TOPK
    xbuf = jnp.zeros((nb * tb, dm), xt.dtype).at[dest].set(xt[tok])
    block_e = jnp.minimum(jnp.searchsorted(pad_ends, jnp.arange(nb, dtype=jnp.int32) * tb, side='right'), N_EXPERTS - 1)

    def expert_block(args):
        xb, eb = args
        hid = jax.nn.silu(xb @ w_gate[layer, eb]) * (xb @ w_up[layer, eb])
        return hid @ w_down[layer, eb]

    ybuf = lax.map(expert_block, (xbuf.reshape(nb, tb, dm), block_e)).reshape(nb * tb, dm)
    y = jnp.einsum('nk,nkd->nd', gate.astype(xt.dtype), ybuf[dest].reshape(n, EXPERT_TOPK, dm))
    return y.reshape(lead + (dm,))


def deepnorm_sublayers(x, mix, layer, ln1_g, ln1_b, w_rg, b_rg, w_re, b_re, w_gate, w_up, w_down, ln2_g, ln2_b):
    h = layer_norm(DEEPNORM_ALPHA * x + mix, ln1_g[layer], ln1_b[layer])
    f = hier_moe(h, layer, w_rg, b_rg, w_re, b_re, w_gate, w_up, w_down)
    return layer_norm(DEEPNORM_ALPHA * h + f, ln2_g[layer], ln2_b[layer])


def setup_inputs(seed: int = 0) -> dict:
    key = jax.random.key(seed)
    ks = jax.random.split(key, 24)
    n_pages = PAST_LEN // PAGE_SIZE
    n_used = DEC_BATCH * n_pages
    n_pool = (5 * n_used) // 4
    n_win = min(WINDOW, PAST_LEN)

    def nrm(k, shape, scale=1.0):
        return jax.random.normal(k, shape, jnp.float32) * scale

    page_table = jax.random.permutation(ks[7], n_pool)[:n_used].reshape(DEC_BATCH, n_pages).astype(jnp.int32)
    return {
        'x_prompt': nrm(ks[0], (BATCH, SEQ, D_MODEL)),
        'x_sample': nrm(ks[1], (DEC_BATCH, DEC_SEQ, D_MODEL)),
        'cache_dsa_kv': nrm(ks[2], (DEPTH, n_pool, PAGE_SIZE, DSA_KV_HEADS, 2, HEAD_DIM)),
        'cache_dsa_kidx': nrm(ks[3], (DEPTH, n_pool, PAGE_SIZE, IDX_DIM)),
        'cache_nsa_cmp_kv': nrm(ks[4], (DEPTH, n_pool, PAGE_SIZE, NSA_KV_GROUPS, 2, HEAD_DIM)),
        'cache_nsa_slc_kv': nrm(ks[5], (DEPTH, n_pool, PAGE_SIZE, NSA_KV_GROUPS, 2, HEAD_DIM)),
        'state_nsa_win_kv': nrm(ks[6], (DEPTH, DEC_BATCH, n_win, NSA_KV_GROUPS, 2, HEAD_DIM)),
        'page_table': page_table,
        'w_in': nrm(ks[8], (DEPTH, D_MODEL, PROJ_WIDTH), D_MODEL ** -0.5),
        'w_out': nrm(ks[9], (DEPTH, MIX_WIDTH, D_MODEL), MIX_WIDTH ** -0.5 * DEEPNORM_BETA),
        'cmp_w1': nrm(ks[10], (DEPTH, 2, CMP_BLOCK, HEAD_DIM, HEAD_DIM), (CMP_BLOCK * HEAD_DIM) ** -0.5),
        'cmp_w2': nrm(ks[11], (DEPTH, 2, HEAD_DIM, HEAD_DIM), HEAD_DIM ** -0.5),
        'cmp_pe': nrm(ks[12], (DEPTH, 2, CMP_BLOCK, HEAD_DIM), 0.1),
        'ln1_g': 1.0 + nrm(ks[13], (DEPTH, D_MODEL), 0.02),
        'ln1_b': nrm(ks[14], (DEPTH, D_MODEL), 0.02),
        'w_router_group': nrm(ks[15], (DEPTH, D_MODEL, N_GROUPS), D_MODEL ** -0.5),
        'b_router_group': nrm(ks[16], (DEPTH, N_GROUPS), 0.01),
        'w_router_expert': nrm(ks[17], (DEPTH, D_MODEL, N_EXPERTS), D_MODEL ** -0.5),
        'b_router_expert': nrm(ks[18], (DEPTH, N_EXPERTS), 0.01),
        'w_gate': nrm(ks[19], (DEPTH, N_EXPERTS, D_MODEL, D_EXPERT), D_MODEL ** -0.5),
        'w_up': nrm(ks[20], (DEPTH, N_EXPERTS, D_MODEL, D_EXPERT), D_MODEL ** -0.5),
        'w_down': nrm(ks[21], (DEPTH, N_EXPERTS, D_EXPERT, D_MODEL), D_EXPERT ** -0.5 * DEEPNORM_BETA),
        'ln2_g': 1.0 + nrm(ks[22], (DEPTH, D_MODEL), 0.02),
        'ln2_b': nrm(ks[23], (DEPTH, D_MODEL), 0.02),
    }


def reference(x_prompt, x_sample, cache_dsa_kv, cache_dsa_kidx, cache_nsa_cmp_kv, cache_nsa_slc_kv,
              state_nsa_win_kv, page_table, w_in, w_out, cmp_w1, cmp_w2, cmp_pe, ln1_g, ln1_b,
              w_router_group, b_router_group, w_router_expert, b_router_expert, w_gate, w_up, w_down,
              ln2_g, ln2_b):
    xp, xs = x_prompt, x_sample
    new_p = [[] for _ in range(5)]
    new_s = [[] for _ in range(5)]
    for l in range(DEPTH):
        mix_p, st_p = mixer_prompt(xp, w_in[l], w_out[l], cmp_w1[l], cmp_w2[l], cmp_pe[l])
        mix_s, st_s = mixer_sample(xs, l, cache_dsa_kv, cache_dsa_kidx, cache_nsa_cmp_kv, cache_nsa_slc_kv,
                                   state_nsa_win_kv[l], page_table, w_in[l], w_out[l], cmp_w1[l], cmp_w2[l], cmp_pe[l])
        ffn = (ln1_g, ln1_b, w_router_group, b_router_group, w_router_expert, b_router_expert,
               w_gate, w_up, w_down, ln2_g, ln2_b)
        xp = deepnorm_sublayers(xp, mix_p, l, *ffn)
        xs = deepnorm_sublayers(xs, mix_s, l, *ffn)
        for i in range(5):
            new_p[i].append(st_p[i])
            new_s[i].append(st_s[i])
    dsa_kv_p, kidx_p, cmp_kv_p, slc_kv_p, win_kv_p = [jnp.stack(a) for a in new_p]
    dsa_kv_s, kidx_s, cmp_kv_s, slc_kv_s, win_kv_s = [jnp.stack(a) for a in new_s]
    return (xp, xs, dsa_kv_p, dsa_kv_s, kidx_p, kidx_s, cmp_kv_p, cmp_kv_s, slc_kv_p, slc_kv_s, win_kv_p, win_kv_s)
```

```python
import functools
import math

import jax
import jax.numpy as jnp
import numpy as np
from jax import lax
from jax.experimental import pallas as pl
from jax.experimental.pallas import tpu as pltpu

F32 = jnp.float32
BF16 = jnp.bfloat16
I32 = jnp.int32

HEAD_DIM = 128
DSA_HEADS = 16
DSA_KV_HEADS = 4
IDX_HEADS = 16
IDX_DIM = 64
IDX_TOPK = 256
NSA_HEADS = 16
NSA_KV_GROUPS = 4
CMP_STRIDE = 16
CMP_BLOCK = 32
SLC_BLOCK = 64
SLC_TOPN = 16
WINDOW = 512
N_GROUPS = 8
EXPERTS_PER_GROUP = 8
N_EXPERTS = N_GROUPS * EXPERTS_PER_GROUP
EXPERT_TOPK = 2
LN_EPS = 1e-5
NEG_INF = -1e30
FORCE_SCORE = 1e9

KV_ROW = 2 * DSA_KV_HEADS * HEAD_DIM
KV_SLABS = 2 * DSA_KV_HEADS
Q_ROW = DSA_HEADS * HEAD_DIM
Q_TILE = 128
LANES = 128
MOE_ROWS = 128
VMEM_LIMIT = 56 * 1024 * 1024
SCALE = HEAD_DIM ** -0.5
INT_MIN = -(2 ** 31)


def _slopes(n):
    return [2.0 ** (-8.0 * i / n) for i in range(1, n + 1)]


def _cparams(sem):
    return pltpu.CompilerParams(dimension_semantics=sem, vmem_limit_bytes=VMEM_LIMIT)


def _dot(a, b):
    return jnp.dot(a, b, preferred_element_type=F32)


def _dot_nt(a, b):
    return lax.dot_general(a, b, (((1,), (1,)), ((), ())), preferred_element_type=F32)


def _dot_hi(a, b):
    return jnp.dot(a, b, preferred_element_type=F32, precision=lax.Precision.HIGHEST)


def _mm_kernel(x_ref, w_ref, *refs, nk):
    o_refs, acc_ref = refs[:-1], refs[-1]
    k = pl.program_id(2)

    @pl.when(k == 0)
    def _():
        acc_ref[...] = jnp.zeros_like(acc_ref)

    acc_ref[...] += _dot(x_ref[...].astype(BF16), w_ref[...].astype(BF16))

    @pl.when(k == nk - 1)
    def _():
        for o_ref in o_refs:
            o_ref[...] = acc_ref[...].astype(o_ref.dtype)


def _pick(n, cands):
    for c in cands:
        if n % c == 0:
            return c
    return n


def _matmul(x, w, n0=0, n=None, bf16_copy=False):
    m, kd = x.shape
    n = w.shape[1] if n is None else n
    tm = _pick(m, (512, 256, 128))
    tn = _pick(n, (1024, 512, 256, 128))
    assert n0 % tn == 0
    tk = _pick(kd, (1024, 512, 256, 128))
    nk = kd // tk
    j0 = n0 // tn
    dtypes = (F32, BF16) if bf16_copy else (F32,)
    out = pl.pallas_call(
        functools.partial(_mm_kernel, nk=nk),
        grid=(m // tm, n // tn, nk),
        in_specs=[pl.BlockSpec((tm, tk), lambda i, j, k: (i, k)),
                  pl.BlockSpec((tk, tn), lambda i, j, k: (k, j + j0))],
        out_specs=[pl.BlockSpec((tm, tn), lambda i, j, k: (i, j))] * len(dtypes),
        out_shape=[jax.ShapeDtypeStruct((m, n), dt) for dt in dtypes],
        scratch_shapes=[pltpu.VMEM((tm, tn), F32)],
        compiler_params=_cparams(("parallel", "parallel", "arbitrary")),
        name="matmul",
    )(x, w)
    return out if bf16_copy else out[0]


def _slab(ref, n_tok, g, c):
    return ref[pl.ds(g * 2 + c, n_tok, stride=KV_SLABS), :]


def _page_specs(n_pages, rows, width):
    return [pl.BlockSpec((rows, width), functools.partial(lambda n, pt, p: (pt[n, p], 0), p=p))
            for p in range(n_pages)]


def _sort_key(x):
    b = pltpu.bitcast(x, I32)
    return jnp.where(b < 0, b ^ jnp.int32(0x7FFFFFFF), b)


def _topk_mask(sc, idx, k, idx_bits):
    key = _sort_key(sc)

    def count(m):
        return jnp.sum(m.astype(I32), axis=1, keepdims=True)

    t0 = jnp.where(count(key >= 0) >= k, jnp.int32(0), jnp.int32(INT_MIN))

    def vstep(i, t):
        cand = t | (jnp.int32(1) << (30 - i))
        return jnp.where(count(key >= cand) >= k, cand, t)

    thr = lax.fori_loop(0, 31, vstep, t0)
    gt = key > thr
    eq = key == thr
    need = k - count(gt)

    def istep(i, c):
        cand = c | (jnp.int32(1) << (idx_bits - 1 - i))
        return jnp.where(count(eq & (idx < cand)) < need, cand, c)

    cut = lax.fori_loop(0, idx_bits, istep, jnp.zeros_like(thr))
    return gt | (eq & (idx <= cut))


def _softmax_pv(s, valid, v):
    s = jnp.where(valid, s, NEG_INF)
    m = jnp.max(s, axis=1, keepdims=True)
    p = jnp.where(valid, jnp.exp(s - m), 0.0)
    l = jnp.sum(p, axis=1, keepdims=True)
    inv = 1.0 / jnp.where(l > 0.0, l, 1.0)
    return _dot(p.astype(BF16), v) * inv


def _dsa_prompt_kernel(q_ref, qi_ref, cq_ref, ck_ref, kv_ref, o_ref, *, seq, k_keep):
    c = pl.program_id(1)
    t = c * Q_TILE + lax.broadcasted_iota(I32, (Q_TILE, 1), 0)
    kpos = lax.broadcasted_iota(I32, (Q_TILE, seq), 1)
    causal = kpos <= t
    ki = ck_ref[:, 0:IDX_DIM].astype(BF16)
    wi = cq_ref[:, IDX_DIM:IDX_DIM + IDX_HEADS] * (IDX_HEADS ** -0.5)
    sc = jnp.zeros((Q_TILE, seq), F32)
    for h in range(IDX_HEADS):
        a = _dot_nt(qi_ref[:, h * IDX_DIM:(h + 1) * IDX_DIM].astype(BF16), ki)
        sc = sc + jnp.maximum(a, 0.0) * wi[:, h:h + 1]
    sc = jnp.where(causal, sc, NEG_INF)
    keep = _topk_mask(sc, kpos, k_keep, max(1, (seq - 1).bit_length()))
    valid = keep & causal
    dist = (t - kpos).astype(F32)
    slopes = _slopes(DSA_HEADS)
    rep = DSA_HEADS // DSA_KV_HEADS
    for g in range(DSA_KV_HEADS):
        kg = kv_ref[:, g * 256:g * 256 + 128].astype(BF16)
        vg = kv_ref[:, g * 256 + 128:g * 256 + 256].astype(BF16)
        for r in range(rep):
            h = g * rep + r
            qh = (q_ref[:, h * HEAD_DIM:(h + 1) * HEAD_DIM] * SCALE).astype(BF16)
            s = _dot_nt(qh, kg) - slopes[h] * dist
            o_ref[:, h * HEAD_DIM:(h + 1) * HEAD_DIM] = _softmax_pv(s, valid, vg)


def _dsa_prompt(h_a, h_c, kv, batch, seq):
    nc = seq // Q_TILE
    k_keep = min(IDX_TOPK, seq // 4)
    return pl.pallas_call(
        functools.partial(_dsa_prompt_kernel, seq=seq, k_keep=k_keep),
        grid=(batch, nc),
        in_specs=[pl.BlockSpec((Q_TILE, Q_ROW), lambda b, c: (b * nc + c, 0)),
                  pl.BlockSpec((Q_TILE, IDX_HEADS * IDX_DIM), lambda b, c: (b * nc + c, 2 * Q_ROW // (IDX_HEADS * IDX_DIM))),
                  pl.BlockSpec((Q_TILE, LANES), lambda b, c: (b * nc + c, 0)),
                  pl.BlockSpec((seq, LANES), lambda b, c: (b, 0)),
                  pl.BlockSpec((seq, KV_ROW), lambda b, c: (b, 0))],
        out_specs=pl.BlockSpec((Q_TILE, Q_ROW), lambda b, c: (b * nc + c, 0)),
        out_shape=jax.ShapeDtypeStruct((batch * seq, Q_ROW), F32),
        compiler_params=_cparams(("parallel", "arbitrary")),
        name="dsa_prompt",
    )(h_a, h_a, h_c, h_c, kv)


def _peb_kernel(pe_ref, w1_ref, o_ref):
    for c in range(2):
        o_ref[c] = _dot_hi(pe_ref[c], w1_ref[c])


def _cmp_pe_bias(cmp_pe, cmp_w1):
    return pl.pallas_call(
        _peb_kernel,
        out_shape=jax.ShapeDtypeStruct((2, 1, HEAD_DIM), F32),
        name="cmp_pe_bias",
    )(cmp_pe.reshape(2, 1, CMP_BLOCK * HEAD_DIM), cmp_w1.reshape(2, CMP_BLOCK * HEAD_DIM, HEAD_DIM))


def _compress_kernel(pt_ref, *refs, n_pages, page):
    pages = refs[:n_pages]
    w1_ref, w2_ref, peb_ref, o_ref = refs[n_pages:]
    sub_per_page = page // CMP_STRIDE
    n_sub = n_pages * sub_per_page
    for c in range(2):
        acc = [jnp.zeros((n_sub, 2 * HEAD_DIM), F32) for _ in range(NSA_KV_GROUPS)]
        for j in range(CMP_STRIDE):
            w = w1_ref[c, j].astype(BF16)
            for g in range(NSA_KV_GROUPS):
                first = j * KV_SLABS + g * 2 + c
                lhs = jnp.concatenate(
                    [p[pl.ds(first, sub_per_page, stride=CMP_STRIDE * KV_SLABS), :] for p in pages], axis=0)
                acc[g] = acc[g] + _dot(lhs.astype(BF16), w)
        w2 = w2_ref[c].astype(BF16)
        for g in range(NSA_KV_GROUPS):
            col = g * 256 + c * 128
            nxt = pltpu.roll(acc[g][:, HEAD_DIM:], n_sub - 1, 0)
            hid = acc[g][:, :HEAD_DIM] + nxt + peb_ref[c]
            o_ref[0, :, col:col + HEAD_DIM] = _dot(jax.nn.gelu(hid).astype(BF16), w2)


def _compress(pool, page_table, page, w1r, w2, peb):
    n_seq, n_pages = page_table.shape
    n_sub = n_pages * page // CMP_STRIDE
    page_specs = _page_specs(n_pages, page * KV_SLABS, HEAD_DIM)
    return pl.pallas_call(
        functools.partial(_compress_kernel, n_pages=n_pages, page=page),
        grid_spec=pltpu.PrefetchScalarGridSpec(
            num_scalar_prefetch=1,
            grid=(n_seq,),
            in_specs=page_specs + [
                pl.BlockSpec((2, CMP_STRIDE, HEAD_DIM, 2 * HEAD_DIM), lambda n, pt: (0, 0, 0, 0)),
                pl.BlockSpec((2, HEAD_DIM, HEAD_DIM), lambda n, pt: (0, 0, 0)),
                pl.BlockSpec((2, 1, HEAD_DIM), lambda n, pt: (0, 0, 0))],
            out_specs=pl.BlockSpec((1, n_sub, KV_ROW), lambda n, pt: (n, 0, 0))),
        out_shape=jax.ShapeDtypeStruct((n_seq, n_sub, KV_ROW), F32),
        compiler_params=_cparams(("arbitrary",)),
        name="nsa_compress",
    )(page_table, *([pool] * n_pages), w1r, w2, peb)


def _cover(n_cmp_pad, n_slc_pad):
    cs = lax.broadcasted_iota(I32, (n_cmp_pad, n_slc_pad), 0) * CMP_STRIDE
    bs = lax.broadcasted_iota(I32, (n_cmp_pad, n_slc_pad), 1) * SLC_BLOCK
    return ((cs < bs + SLC_BLOCK) & (cs + CMP_BLOCK > bs)).astype(F32)


def _select_blocks(score, t, n_slc, n_keep):
    rows, width = score.shape
    j = lax.broadcasted_iota(I32, (rows, width), 1)
    cur = t // SLC_BLOCK
    forced = (j == 0) | (j == cur) | (j == cur - 1)
    admissible = j * SLC_BLOCK <= t
    score = jnp.where(forced, FORCE_SCORE, jnp.where(admissible, score, NEG_INF))
    rank = jnp.zeros((rows, width), I32)
    for k in range(n_slc):
        sk = score[:, k:k + 1]
        ahead = (sk > score) | ((sk == score) & (j > k))
        rank = rank + ahead.astype(I32)
    return ((rank < n_keep) & (j < n_slc)).astype(F32)


def _nsa_prompt_kernel(q_ref, cq_ref, ckv_ref, slc_ref, win_ref, o_ref, *, seq, n_cmp, win_keys):
    c = pl.program_id(1)
    t = c * Q_TILE + lax.broadcasted_iota(I32, (Q_TILE, 1), 0)
    slopes = _slopes(NSA_HEADS)
    rep = NSA_HEADS // NSA_KV_GROUPS
    n_cmp_pad = ckv_ref.shape[1]
    n_slc = seq // SLC_BLOCK
    n_keep = min(SLC_TOPN, n_slc)

    kc = lax.broadcasted_iota(I32, (Q_TILE, n_cmp_pad), 1)
    dist_c_i = t - (kc * CMP_STRIDE + (CMP_BLOCK - 1))
    valid_c = (dist_c_i >= 0) & (kc < n_cmp)
    dist_c = dist_c_i.astype(F32)
    cover = _cover(n_cmp_pad, n_slc)
    expand = (lax.broadcasted_iota(I32, (n_slc, seq), 1) // SLC_BLOCK
              == lax.broadcasted_iota(I32, (n_slc, seq), 0)).astype(BF16)

    kpos = lax.broadcasted_iota(I32, (Q_TILE, seq), 1)
    causal = kpos <= t
    dist_s = (t - kpos).astype(F32)

    w0 = pl.multiple_of(jnp.maximum(c * Q_TILE + Q_TILE - win_keys, 0), Q_TILE)
    wpos = w0 + lax.broadcasted_iota(I32, (Q_TILE, win_keys), 1)
    dist_w_i = t - wpos
    valid_w = (dist_w_i >= 0) & (dist_w_i <= WINDOW)
    dist_w = dist_w_i.astype(F32)

    gates = jax.nn.sigmoid(cq_ref[:, IDX_DIM + IDX_HEADS:LANES])

    for g in range(NSA_KV_GROUPS):
        ck = ckv_ref[0, :, g * 256:g * 256 + 128].astype(BF16)
        cv = ckv_ref[0, :, g * 256 + 128:g * 256 + 256].astype(BF16)
        qs, o_cs = [], []
        imp = jnp.zeros((Q_TILE, n_cmp_pad), F32)
        for r in range(rep):
            h = g * rep + r
            qh = (q_ref[:, h * HEAD_DIM:(h + 1) * HEAD_DIM] * SCALE).astype(BF16)
            qs.append(qh)
            s = jnp.where(valid_c, _dot_nt(qh, ck) - slopes[h] * dist_c, NEG_INF)
            m = jnp.max(s, axis=1, keepdims=True)
            p = jnp.where(valid_c, jnp.exp(s - m), 0.0)
            l = jnp.sum(p, axis=1, keepdims=True)
            p = p * (1.0 / jnp.where(l > 0.0, l, 1.0))
            imp = imp + p
            o_cs.append(_dot(p.astype(BF16), cv))
        sel = _select_blocks(_dot_hi(imp, cover), t, n_slc, n_keep)
        valid_s = (_dot(sel.astype(BF16), expand) > 0.5) & causal
        sk = slc_ref[:, g * 256:g * 256 + 128].astype(BF16)
        sv = slc_ref[:, g * 256 + 128:g * 256 + 256].astype(BF16)
        wk = win_ref[pl.ds(w0, win_keys), g * 256:g * 256 + 128].astype(BF16)
        wv = win_ref[pl.ds(w0, win_keys), g * 256 + 128:g * 256 + 256].astype(BF16)
        for r in range(rep):
            h = g * rep + r
            o_s = _softmax_pv(_dot_nt(qs[r], sk) - slopes[h] * dist_s, valid_s, sv)
            o_w = _softmax_pv(_dot_nt(qs[r], wk) - slopes[h] * dist_w, valid_w, wv)
            o_ref[:, h * HEAD_DIM:(h + 1) * HEAD_DIM] = (
                gates[:, 3 * h:3 * h + 1] * o_cs[r] + gates[:, 3 * h + 1:3 * h + 2] * o_s
                + gates[:, 3 * h + 2:3 * h + 3] * o_w)


def _nsa_prompt(h_a, h_c, ckv, slc, win, batch, seq):
    nc = seq // Q_TILE
    n_cmp_pad = ckv.shape[1]
    n_cmp = seq // CMP_STRIDE - CMP_BLOCK // CMP_STRIDE + 1
    win_keys = min(seq, WINDOW + Q_TILE)
    return pl.pallas_call(
        functools.partial(_nsa_prompt_kernel, seq=seq, n_cmp=n_cmp, win_keys=win_keys),
        grid=(batch, nc),
        in_specs=[pl.BlockSpec((Q_TILE, Q_ROW), lambda b, c: (b * nc + c, 1)),
                  pl.BlockSpec((Q_TILE, LANES), lambda b, c: (b * nc + c, 0)),
                  pl.BlockSpec((1, n_cmp_pad, KV_ROW), lambda b, c: (b, 0, 0)),
                  pl.BlockSpec((seq, KV_ROW), lambda b, c: (b, 0)),
                  pl.BlockSpec((seq, KV_ROW), lambda b, c: (b, 0))],
        out_specs=pl.BlockSpec((Q_TILE, Q_ROW), lambda b, c: (b * nc + c, 0)),
        out_shape=jax.ShapeDtypeStruct((batch * seq, Q_ROW), F32),
        compiler_params=_cparams(("parallel", "arbitrary")),
        name="nsa_prompt",
    )(h_a, h_c, ckv, slc, win)


def _group_rows(per_group, rep):
    row = lax.broadcasted_iota(I32, per_group[0].shape, 0) // rep
    out = per_group[0]
    for g in range(1, len(per_group)):
        out = jnp.where(row == g, per_group[g], out)
    return out


def _dsa_sample_kernel(pt_ref, *refs, n_pages, page, k_keep):
    ki_pages = refs[:n_pages]
    kv_pages = refs[n_pages:2 * n_pages]
    q_ref, qi_ref, wi_ref, cnew_ref, kvnew_ref, o_ref = refs[2 * n_pages:]
    n = pl.program_id(0)
    past = n_pages * page
    width = past + LANES
    rep = DSA_HEADS // DSA_KV_HEADS
    slopes = _slopes(DSA_HEADS)

    qi = qi_ref[0].astype(BF16)
    wi = wi_ref[0] * (IDX_HEADS ** -0.5)
    cnew = cnew_ref[pl.ds(n, 1), :]
    a_past = jnp.concatenate([_dot_nt(qi, p[0].astype(BF16)) for p in ki_pages], axis=1)
    a_new = _dot_nt(qi, jnp.broadcast_to(cnew[:, 0:IDX_DIM], (8, IDX_DIM)).astype(BF16))[:, 0:1]
    sc_past = jnp.sum(jnp.maximum(a_past, 0.0) * wi, axis=0, keepdims=True)
    sc_new = jnp.sum(jnp.maximum(a_new, 0.0) * wi, axis=0, keepdims=True)
    col = lax.broadcasted_iota(I32, (1, width), 1)
    sc = jnp.concatenate([sc_past, jnp.broadcast_to(sc_new, (1, LANES))], axis=1)
    sc = jnp.where(col <= past, sc, NEG_INF)
    sc8 = jnp.broadcast_to(sc, (8, width))
    col8 = jnp.broadcast_to(col, (8, width))
    keep = _topk_mask(sc8, col8, k_keep, max(1, (width - 1).bit_length()))[0:1] & (col <= past)
    keep_past = jnp.broadcast_to(keep[:, 0:past], (DSA_HEADS, past))
    keep_new = jnp.broadcast_to(keep[:, past:past + 1], (DSA_HEADS, 1))

    q = (q_ref[0] * SCALE).astype(BF16)
    q32 = q.astype(F32)
    kvnew = kvnew_ref[pl.ds(n, 1), :]
    slope_col = jnp.concatenate([jnp.full((1, 1), s, F32) for s in slopes], axis=0)
    dist = (past - lax.broadcasted_iota(I32, (DSA_HEADS, past), 1)).astype(F32)
    s_g, s_new_g = [], []
    for g in range(DSA_KV_HEADS):
        s_g.append(jnp.concatenate(
            [_dot_nt(q, _slab(p, page, g, 0).astype(BF16)) for p in kv_pages], axis=1))
        knew = kvnew[:, g * 256:g * 256 + 128].astype(BF16).astype(F32)
        s_new_g.append(jnp.sum(q32 * knew, axis=1, keepdims=True))
    s = jnp.where(keep_past, _group_rows(s_g, rep) - slope_col * dist, NEG_INF)
    s_new = jnp.where(keep_new, _group_rows(s_new_g, rep), NEG_INF)
    m = jnp.maximum(jnp.max(s, axis=1, keepdims=True), s_new)
    p = jnp.where(keep_past, jnp.exp(s - m), 0.0)
    p_new = jnp.where(keep_new, jnp.exp(s_new - m), 0.0)
    l = jnp.sum(p, axis=1, keepdims=True) + p_new
    inv = 1.0 / jnp.where(l > 0.0, l, 1.0)
    pb = p.astype(BF16)
    p_new = p_new.astype(BF16).astype(F32)
    o_g = []
    for g in range(DSA_KV_HEADS):
        acc = p_new * kvnew[:, g * 256 + 128:g * 256 + 256].astype(BF16).astype(F32)
        for i, pg in enumerate(kv_pages):
            acc = acc + _dot(pb[:, i * page:(i + 1) * page], _slab(pg, page, g, 1).astype(BF16))
        o_g.append(acc)
    o_ref[0] = _group_rows(o_g, rep) * inv


def _dsa_sample(kidx_pool, kv_pool, page_table, q, qi, wi, h_c, kv_new):
    n_seq, n_pages = page_table.shape
    page = kidx_pool.shape[1]
    total = n_pages * page + 1
    k_keep = min(IDX_TOPK, total // 4)

    ki_specs = [pl.BlockSpec((1, page, IDX_DIM), functools.partial(lambda n, pt, p: (pt[n, p], 0, 0), p=p))
                for p in range(n_pages)]
    kv_specs = _page_specs(n_pages, page * KV_SLABS, HEAD_DIM)
    return pl.pallas_call(
        functools.partial(_dsa_sample_kernel, n_pages=n_pages, page=page, k_keep=k_keep),
        grid_spec=pltpu.PrefetchScalarGridSpec(
            num_scalar_prefetch=1,
            grid=(n_seq,),
            in_specs=ki_specs + kv_specs + [
                pl.BlockSpec((1, DSA_HEADS, HEAD_DIM), lambda n, pt: (n, 0, 0)),
                pl.BlockSpec((1, IDX_HEADS, IDX_DIM), lambda n, pt: (n, 0, 0)),
                pl.BlockSpec((1, IDX_HEADS, 1), lambda n, pt: (n, 0, 0)),
                pl.BlockSpec((n_seq, LANES), lambda n, pt: (0, 0)),
                pl.BlockSpec((n_seq, KV_ROW), lambda n, pt: (0, 0))],
            out_specs=pl.BlockSpec((1, DSA_HEADS, HEAD_DIM), lambda n, pt: (n, 0, 0))),
        out_shape=jax.ShapeDtypeStruct((n_seq, DSA_HEADS, HEAD_DIM), F32),
        compiler_params=_cparams(("arbitrary",)),
        name="dsa_sample",
    )(page_table, *([kidx_pool] * n_pages), *([kv_pool] * n_pages), q, qi, wi, h_c, kv_new)


def _nsa_sample_kernel(pt_ref, *refs, n_pages, page, n_cmp):
    slc_pages = refs[:n_pages]
    q_ref, gate_ref, ckv_ref, win_ref, slcnew_ref, winnew_ref, o_ref = refs[n_pages:]
    n = pl.program_id(0)
    past = n_pages * page
    heads = NSA_HEADS
    rep = NSA_HEADS // NSA_KV_GROUPS
    slopes = _slopes(NSA_HEADS)
    slope_col = jnp.concatenate([jnp.full((1, 1), s, F32) for s in slopes], axis=0)
    q = (q_ref[0] * SCALE).astype(BF16)
    q32 = q.astype(F32)
    gates = jax.nn.sigmoid(gate_ref[0])
    n_cmp_pad = ckv_ref.shape[1]
    total = past + 1
    n_slc = -(-total // SLC_BLOCK)
    n_keep = min(SLC_TOPN, n_slc)
    slc_pad = -(-n_slc // LANES) * LANES

    kc = lax.broadcasted_iota(I32, (heads, n_cmp_pad), 1)
    dist_c_i = past - (kc * CMP_STRIDE + (CMP_BLOCK - 1))
    valid_c = (dist_c_i >= 0) & (kc < n_cmp)
    s_g = [_dot_nt(q, ckv_ref[0, :, g * 256:g * 256 + 128].astype(BF16)) for g in range(NSA_KV_GROUPS)]
    s = jnp.where(valid_c, _group_rows(s_g, rep) - slope_col * dist_c_i.astype(F32), NEG_INF)
    m = jnp.max(s, axis=1, keepdims=True)
    p = jnp.where(valid_c, jnp.exp(s - m), 0.0)
    l = jnp.sum(p, axis=1, keepdims=True)
    p = p * (1.0 / jnp.where(l > 0.0, l, 1.0))
    pb = p.astype(BF16)
    o_c = _group_rows([_dot(pb, ckv_ref[0, :, g * 256 + 128:g * 256 + 256].astype(BF16))
                       for g in range(NSA_KV_GROUPS)], rep)
    same_group = (lax.broadcasted_iota(I32, (heads, heads), 0) // rep
                  == lax.broadcasted_iota(I32, (heads, heads), 1) // rep).astype(F32)
    imp = _dot_hi(same_group, p)
    score = _dot_hi(imp, _cover(n_cmp_pad, slc_pad))
    sel = _select_blocks(score, jnp.full((heads, 1), past, I32), n_slc, n_keep)
    expand = (lax.broadcasted_iota(I32, (slc_pad, past), 1) // SLC_BLOCK
              == lax.broadcasted_iota(I32, (slc_pad, past), 0)).astype(BF16)
    valid_s = _dot(sel.astype(BF16), expand) > 0.5
    new_blk = past // SLC_BLOCK
    valid_s_new = sel[:, new_blk:new_blk + 1] > 0.5

    def attend_with_new(s_past, valid_past, s_new, valid_new, dist_past, v_of, v_new_of):
        s_p = jnp.where(valid_past, s_past - slope_col * dist_past, NEG_INF)
        s_n = jnp.where(valid_new, s_new, NEG_INF)
        mm = jnp.maximum(jnp.max(s_p, axis=1, keepdims=True), s_n)
        pp = jnp.where(valid_past, jnp.exp(s_p - mm), 0.0)
        pn = jnp.where(valid_new, jnp.exp(s_n - mm), 0.0)
        ll = jnp.sum(pp, axis=1, keepdims=True) + pn
        inv = 1.0 / jnp.where(ll > 0.0, ll, 1.0)
        ppb = pp.astype(BF16)
        pn = pn.astype(BF16).astype(F32)
        outs = [v_of(ppb, g) + pn * v_new_of(g) for g in range(NSA_KV_GROUPS)]
        return _group_rows(outs, rep) * inv

    slcnew = slcnew_ref[pl.ds(n, 1), :]
    s_g, s_new_g = [], []
    for g in range(NSA_KV_GROUPS):
        s_g.append(jnp.concatenate(
            [_dot_nt(q, _slab(pg, page, g, 0).astype(BF16)) for pg in slc_pages], axis=1))
        s_new_g.append(jnp.sum(q32 * slcnew[:, g * 256:g * 256 + 128].astype(BF16).astype(F32), axis=1, keepdims=True))
    dist_s = (past - lax.broadcasted_iota(I32, (heads, past), 1)).astype(F32)

    def slc_v(ppb, g):
        acc = jnp.zeros((heads, HEAD_DIM), F32)
        for i, pg in enumerate(slc_pages):
            acc = acc + _dot(ppb[:, i * page:(i + 1) * page], _slab(pg, page, g, 1).astype(BF16))
        return acc

    o_s = attend_with_new(_group_rows(s_g, rep), valid_s, _group_rows(s_new_g, rep), valid_s_new, dist_s, slc_v,
                          lambda g: slcnew[:, g * 256 + 128:g * 256 + 256].astype(BF16).astype(F32))

    n_buf = win_ref.shape[0] // KV_SLABS
    winnew = winnew_ref[pl.ds(n, 1), :]
    dist_w_i = n_buf - lax.broadcasted_iota(I32, (heads, n_buf), 1)
    valid_w = dist_w_i <= WINDOW
    s_g = [_dot_nt(q, _slab(win_ref, n_buf, g, 0).astype(BF16)) for g in range(NSA_KV_GROUPS)]
    s_new_g = [jnp.sum(q32 * winnew[:, g * 256:g * 256 + 128].astype(BF16).astype(F32), axis=1, keepdims=True)
               for g in range(NSA_KV_GROUPS)]
    o_w = attend_with_new(_group_rows(s_g, rep), valid_w, _group_rows(s_new_g, rep),
                          jnp.full((heads, 1), True), dist_w_i.astype(F32),
                          lambda ppb, g: _dot(ppb, _slab(win_ref, n_buf, g, 1).astype(BF16)),
                          lambda g: winnew[:, g * 256 + 128:g * 256 + 256].astype(BF16).astype(F32))

    o_ref[0] = gates[:, 0:1] * o_c + gates[:, 1:2] * o_s + gates[:, 2:3] * o_w


def _nsa_sample(slc_pool, page_table, page, q, gate, ckv, win_state, slc_new, win_new):
    n_seq, n_pages = page_table.shape
    n_cmp_pad = ckv.shape[1]
    n_cmp = n_pages * page // CMP_STRIDE - CMP_BLOCK // CMP_STRIDE + 1
    n_buf = win_state.shape[0] // (n_seq * KV_SLABS)
    slc_specs = _page_specs(n_pages, page * KV_SLABS, HEAD_DIM)
    return pl.pallas_call(
        functools.partial(_nsa_sample_kernel, n_pages=n_pages, page=page, n_cmp=n_cmp),
        grid_spec=pltpu.PrefetchScalarGridSpec(
            num_scalar_prefetch=1,
            grid=(n_seq,),
            in_specs=slc_specs + [
                pl.BlockSpec((1, NSA_HEADS, HEAD_DIM), lambda n, pt: (n, 0, 0)),
                pl.BlockSpec((1, NSA_HEADS, 3), lambda n, pt: (n, 0, 0)),
                pl.BlockSpec((1, n_cmp_pad, KV_ROW), lambda n, pt: (n, 0, 0)),
                pl.BlockSpec((n_buf * KV_SLABS, HEAD_DIM), lambda n, pt: (n, 0)),
                pl.BlockSpec((n_seq, KV_ROW), lambda n, pt: (0, 0)),
                pl.BlockSpec((n_seq, KV_ROW), lambda n, pt: (0, 0))],
            out_specs=pl.BlockSpec((1, NSA_HEADS, HEAD_DIM), lambda n, pt: (n, 0, 0))),
        out_shape=jax.ShapeDtypeStruct((n_seq, NSA_HEADS, HEAD_DIM), F32),
        compiler_params=_cparams(("arbitrary",)),
        name="nsa_sample",
    )(page_table, *([slc_pool] * n_pages), q, gate, ckv, win_state, slc_new, win_new)


def _ln(v, g, b):
    mu = jnp.mean(v, axis=1, keepdims=True)
    d = v - mu
    var = jnp.mean(d * d, axis=1, keepdims=True)
    return d * lax.rsqrt(var + LN_EPS) * g + b


def _ln_router_kernel(x_ref, mix_ref, g_ref, b_ref, wr_ref, br_ref, x1_ref, e_ref, gate_ref, *, alpha):
    x1 = _ln(alpha * x_ref[...] + mix_ref[...], g_ref[...], b_ref[...])
    x1_ref[...] = x1
    logits = _dot_hi(x1, wr_ref[...]) + br_ref[...]
    rows = logits.shape[0]
    lane = lax.broadcasted_iota(I32, (rows, LANES), 1)
    big = jnp.int32(LANES)
    is_grp = lane < N_GROUPS
    lg = jnp.where(is_grp, logits, -jnp.inf)
    mg = jnp.max(lg, axis=1, keepdims=True)
    grp = jnp.min(jnp.where(lg == mg, lane, big), axis=1, keepdims=True)
    p_grp = 1.0 / jnp.sum(jnp.where(is_grp, jnp.exp(lg - mg), 0.0), axis=1, keepdims=True)
    ex = lane - N_GROUPS
    in_grp = (ex >= 0) & (ex < N_EXPERTS) & (ex // EXPERTS_PER_GROUP == grp)
    le = jnp.where(in_grp, logits, -jnp.inf)
    m1 = jnp.max(le, axis=1, keepdims=True)
    e1 = jnp.min(jnp.where(le == m1, ex, big), axis=1, keepdims=True)
    le2 = jnp.where(ex == e1, -jnp.inf, le)
    m2 = jnp.max(le2, axis=1, keepdims=True)
    e2 = jnp.min(jnp.where(le2 == m2, ex, big), axis=1, keepdims=True)
    z = jnp.sum(jnp.where(in_grp, jnp.exp(le - m1), 0.0), axis=1, keepdims=True)
    p1 = 1.0 / z
    p2 = jnp.exp(m2 - m1) / z
    g1 = p_grp * p1 / (p1 + p2)
    g2 = p_grp * p2 / (p1 + p2)
    e_ref[...] = jnp.where(lane == 0, e1, jnp.where(lane == 1, e2, 0))
    gate_ref[...] = jnp.where(lane == 0, g1, jnp.where(lane == 1, g2, 0.0))


def _ln_router(x, mix, g, b, wr, br, alpha):
    m, d = x.shape
    tm = _pick(m, (256, 128))
    row = lambda i: (i, 0)
    fixed = lambda i: (0, 0)
    return pl.pallas_call(
        functools.partial(_ln_router_kernel, alpha=alpha),
        grid=(m // tm,),
        in_specs=[pl.BlockSpec((tm, d), row), pl.BlockSpec((tm, d), row),
                  pl.BlockSpec((1, d), fixed), pl.BlockSpec((1, d), fixed),
                  pl.BlockSpec((d, LANES), fixed), pl.BlockSpec((1, LANES), fixed)],
        out_specs=[pl.BlockSpec((tm, d), row), pl.BlockSpec((tm, LANES), row), pl.BlockSpec((tm, LANES), row)],
        out_shape=[jax.ShapeDtypeStruct((m, d), F32), jax.ShapeDtypeStruct((m, LANES), I32),
                   jax.ShapeDtypeStruct((m, LANES), F32)],
        compiler_params=_cparams(("parallel",)),
        name="ln_router",
    )(x, mix, g, b, wr, br)


def _dispatch_kernel(src_ref, nused_ref, xp_ref, xs_ref, o_ref, buf_ref, sem, *, m_prompt):
    i = pl.program_id(0)

    def row_copy(r):
        tok = src_ref[i * MOE_ROWS + r]
        from_prompt = pltpu.make_async_copy(xp_ref.at[pl.ds(jnp.minimum(tok, m_prompt - 1), 1)],
                                            buf_ref.at[pl.ds(r, 1)], sem)
        from_sample = pltpu.make_async_copy(xs_ref.at[pl.ds(jnp.maximum(tok - m_prompt, 0), 1)],
                                            buf_ref.at[pl.ds(r, 1)], sem)
        return tok < m_prompt, from_prompt, from_sample

    @pl.when(i < nused_ref[0])
    def _():
        def start(r, carry):
            is_p, cp, cs = row_copy(r)

            @pl.when(is_p)
            def _():
                cp.start()

            @pl.when(jnp.logical_not(is_p))
            def _():
                cs.start()

            return carry

        lax.fori_loop(0, MOE_ROWS, start, 0)

        def wait(r, carry):
            row_copy(r)[1].wait()
            return carry

        lax.fori_loop(0, MOE_ROWS, wait, 0)
        o_ref[...] = buf_ref[...].astype(BF16)

    @pl.when(i >= nused_ref[0])
    def _():
        o_ref[...] = jnp.zeros_like(o_ref)


def _dispatch(src_tok, n_used, x1_p, x1_s, nb):
    d = x1_p.shape[1]
    return pl.pallas_call(
        functools.partial(_dispatch_kernel, m_prompt=x1_p.shape[0]),
        grid_spec=pltpu.PrefetchScalarGridSpec(
            num_scalar_prefetch=2,
            grid=(nb,),
            in_specs=[pl.BlockSpec(memory_space=pl.ANY), pl.BlockSpec(memory_space=pl.ANY)],
            out_specs=pl.BlockSpec((MOE_ROWS, d), lambda i, s, u: (i, 0)),
            scratch_shapes=[pltpu.VMEM((MOE_ROWS, d), F32), pltpu.SemaphoreType.DMA(())]),
        out_shape=jax.ShapeDtypeStruct((nb * MOE_ROWS, d), BF16),
        compiler_params=_cparams(("arbitrary",)),
        name="moe_dispatch",
    )(src_tok, n_used, x1_p, x1_s)


def _gate_up_kernel(be_ref, nused_ref, x_ref, wg_ref, wu_ref, o_ref):
    i = pl.program_id(1)

    @pl.when(i < nused_ref[0])
    def _():
        x = x_ref[...]
        a = _dot(x, wg_ref[0].astype(BF16))
        u = _dot(x, wu_ref[0].astype(BF16))
        o_ref[...] = (a * jax.nn.sigmoid(a) * u).astype(BF16)

    @pl.when(i >= nused_ref[0])
    def _():
        o_ref[...] = jnp.zeros_like(o_ref)


def _gate_up(block_e, n_used, xbuf, w_gate, w_up, nb):
    d, f = w_gate.shape[1], w_gate.shape[2]
    tn = _pick(f, (512, 256, 128))
    wmap = lambda j, i, be, u: (be[i], 0, j)
    return pl.pallas_call(
        _gate_up_kernel,
        grid_spec=pltpu.PrefetchScalarGridSpec(
            num_scalar_prefetch=2,
            grid=(f // tn, nb),
            in_specs=[pl.BlockSpec((MOE_ROWS, d), lambda j, i, be, u: (i, 0)),
                      pl.BlockSpec((1, d, tn), wmap), pl.BlockSpec((1, d, tn), wmap)],
            out_specs=pl.BlockSpec((MOE_ROWS, tn), lambda j, i, be, u: (i, j))),
        out_shape=jax.ShapeDtypeStruct((nb * MOE_ROWS, f), BF16),
        compiler_params=_cparams(("arbitrary", "arbitrary")),
        name="moe_gate_up",
    )(block_e, n_used, xbuf, w_gate, w_up)


def _down_kernel(be_ref, nused_ref, h_ref, wd_ref, o_ref):
    i = pl.program_id(1)

    @pl.when(i < nused_ref[0])
    def _():
        o_ref[...] = _dot(h_ref[...], wd_ref[0].astype(BF16))

    @pl.when(i >= nused_ref[0])
    def _():
        o_ref[...] = jnp.zeros_like(o_ref)


def _down(block_e, n_used, hid, w_down, nb):
    f, d = w_down.shape[1], w_down.shape[2]
    tn = _pick(d, (1024, 512, 256, 128))
    return pl.pallas_call(
        _down_kernel,
        grid_spec=pltpu.PrefetchScalarGridSpec(
            num_scalar_prefetch=2,
            grid=(d // tn, nb),
            in_specs=[pl.BlockSpec((MOE_ROWS, f), lambda j, i, be, u: (i, 0)),
                      pl.BlockSpec((1, f, tn), lambda j, i, be, u: (be[i], 0, j))],
            out_specs=pl.BlockSpec((MOE_ROWS, tn), lambda j, i, be, u: (i, j))),
        out_shape=jax.ShapeDtypeStruct((nb * MOE_ROWS, d), F32),
        compiler_params=_cparams(("arbitrary", "arbitrary")),
        name="moe_down",
    )(block_e, n_used, hid, w_down)


def _combine_kernel(dest_ref, y_ref, x1_ref, gate_ref, g_ref, b_ref, o_ref, buf_ref, sem, *, tm, alpha):
    i = pl.program_id(0)

    def row_copy(r, k):
        slot = dest_ref[(i * tm + r) * EXPERT_TOPK + k]
        return pltpu.make_async_copy(y_ref.at[pl.ds(slot, 1)], buf_ref.at[k, pl.ds(r, 1)], sem)

    def start(r, carry):
        for k in range(EXPERT_TOPK):
            row_copy(r, k).start()
        return carry

    lax.fori_loop(0, tm, start, 0)

    def wait(r, carry):
        for k in range(EXPERT_TOPK):
            row_copy(r, k).wait()
        return carry

    lax.fori_loop(0, tm, wait, 0)
    gate = gate_ref[...]
    f = gate[:, 0:1] * buf_ref[0] + gate[:, 1:2] * buf_ref[1]
    o_ref[...] = _ln(alpha * x1_ref[...] + f, g_ref[...], b_ref[...])


def _combine(dest, ybuf, x1, gate, g, b, alpha):
    m, d = x1.shape
    tm = _pick(m, (128,))
    return pl.pallas_call(
        functools.partial(_combine_kernel, tm=tm, alpha=alpha),
        grid_spec=pltpu.PrefetchScalarGridSpec(
            num_scalar_prefetch=1,
            grid=(m // tm,),
            in_specs=[pl.BlockSpec(memory_space=pl.ANY),
                      pl.BlockSpec((tm, d), lambda i, dd: (i, 0)),
                      pl.BlockSpec((tm, LANES), lambda i, dd: (i, 0)),
                      pl.BlockSpec((1, d), lambda i, dd: (0, 0)),
                      pl.BlockSpec((1, d), lambda i, dd: (0, 0))],
            out_specs=pl.BlockSpec((tm, d), lambda i, dd: (i, 0)),
            scratch_shapes=[pltpu.VMEM((EXPERT_TOPK, tm, d), F32), pltpu.SemaphoreType.DMA(())]),
        out_shape=jax.ShapeDtypeStruct((m, d), F32),
        compiler_params=_cparams(("arbitrary",)),
        name="moe_combine",
    )(dest, ybuf, x1, gate, g, b)


def _route_plan(e_all, nb):
    a_n = e_all.shape[0]
    onehot = (e_all[:, None] == jnp.arange(N_EXPERTS, dtype=I32)[None, :]).astype(I32)
    before = jnp.cumsum(onehot, axis=0) - onehot
    rank = jnp.sum(before * onehot, axis=1)
    counts = jnp.sum(onehot, axis=0)
    pad_counts = (counts + MOE_ROWS - 1) // MOE_ROWS * MOE_ROWS
    pad_ends = jnp.cumsum(pad_counts)
    pad_starts = pad_ends - pad_counts
    dest = (pad_starts[e_all] + rank).astype(I32)
    block_e = jnp.minimum(
        jnp.searchsorted(pad_ends, jnp.arange(nb, dtype=I32) * MOE_ROWS, side="right"), N_EXPERTS - 1).astype(I32)
    n_used = (pad_ends[-1] // MOE_ROWS).astype(I32).reshape(1)
    src_tok = jnp.zeros((nb * MOE_ROWS,), I32).at[dest].set(jnp.arange(a_n, dtype=I32) // EXPERT_TOPK)
    return dest, block_e, n_used, src_tok


def _kv_cols(w, k_off, v_off):
    d = w.shape[0]
    k = w[:, k_off:k_off + NSA_KV_GROUPS * HEAD_DIM].reshape(d, NSA_KV_GROUPS, 1, HEAD_DIM)
    v = w[:, v_off:v_off + NSA_KV_GROUPS * HEAD_DIM].reshape(d, NSA_KV_GROUPS, 1, HEAD_DIM)
    return jnp.concatenate([k, v], axis=2).reshape(d, KV_ROW)


def kernel(x_prompt, x_sample, cache_dsa_kv, cache_dsa_kidx, cache_nsa_cmp_kv, cache_nsa_slc_kv, state_nsa_win_kv, page_table, w_in, w_out, cmp_w1, cmp_w2, cmp_pe, ln1_g, ln1_b, w_router_group, b_router_group, w_router_expert, b_router_expert, w_gate, w_up, w_down, ln2_g, ln2_b):
    depth = w_in.shape[0]
    assert depth == 1 and x_sample.shape[1] == 1
    batch, seq, d_model = x_prompt.shape
    n_seq = x_sample.shape[0]
    n_pool, page = cache_dsa_kv.shape[1], cache_dsa_kv.shape[2]
    n_pages = page_table.shape[1]
    alpha = (2.0 * depth) ** 0.25
    m_p = batch * seq
    xp = x_prompt.reshape(m_p, d_model)
    xs = x_sample.reshape(n_seq, d_model)

    w = w_in[0]
    o_dq, o_dk, o_dv, o_iq, o_ik, o_iw = 0, 2048, 2560, 3072, 4096, 4160
    o_nq, o_ck, o_cv, o_sk, o_sv, o_wk, o_wv, o_ng = 4176, 6224, 6736, 7248, 7760, 8272, 8784, 9296
    w_a = jnp.concatenate([w[:, o_dq:o_dq + Q_ROW], w[:, o_nq:o_nq + Q_ROW], w[:, o_iq:o_iq + 1024]], axis=1).astype(BF16)
    w_kv = jnp.concatenate([_kv_cols(w, o_dk, o_dv), _kv_cols(w, o_ck, o_cv), _kv_cols(w, o_sk, o_sv),
                            _kv_cols(w, o_wk, o_wv)], axis=1).astype(BF16)
    w_c = jnp.concatenate([w[:, o_ik:o_ik + IDX_DIM + IDX_HEADS], w[:, o_ng:o_ng + 3 * NSA_HEADS]], axis=1).astype(BF16)

    ha_p, ha_s = _matmul(xp, w_a), _matmul(xs, w_a)
    hc_p, hc_s = _matmul(xp, w_c), _matmul(xs, w_c)
    kv_pairs = [_matmul(xp, w_kv, n0=i * KV_ROW, n=KV_ROW, bf16_copy=True) for i in range(4)]
    kv_p = [pair[0] for pair in kv_pairs]
    kv_p16 = [pair[1] for pair in kv_pairs]
    kv_s = [_matmul(xs, w_kv, n0=i * KV_ROW, n=KV_ROW) for i in range(4)]

    w1r = cmp_w1[0].reshape(2, CMP_BLOCK // CMP_STRIDE, CMP_STRIDE, HEAD_DIM, HEAD_DIM)
    w1r = jnp.transpose(w1r, (0, 2, 3, 1, 4)).reshape(2, CMP_STRIDE, HEAD_DIM, 2 * HEAD_DIM)
    peb = _cmp_pe_bias(cmp_pe[0], cmp_w1[0])
    pages_per_seq = seq // page
    ident = jnp.arange(batch * pages_per_seq, dtype=I32).reshape(batch, pages_per_seq)
    ckv_p = _compress(kv_p[1].reshape(m_p * KV_SLABS, HEAD_DIM), ident, page, w1r, cmp_w2[0], peb)
    ckv_s = _compress(cache_nsa_cmp_kv.reshape(-1, HEAD_DIM), page_table, page, w1r, cmp_w2[0], peb)

    oa_p = _dsa_prompt(ha_p, hc_p, kv_p16[0], batch, seq)
    ob_p = _nsa_prompt(ha_p, hc_p, ckv_p, kv_p16[2], kv_p16[3], batch, seq)

    q_a = ha_s[:, 0:Q_ROW].reshape(n_seq, DSA_HEADS, HEAD_DIM)
    q_b = ha_s[:, Q_ROW:2 * Q_ROW].reshape(n_seq, NSA_HEADS, HEAD_DIM)
    q_i = ha_s[:, 2 * Q_ROW:].reshape(n_seq, IDX_HEADS, IDX_DIM)
    w_i = hc_s[:, IDX_DIM:IDX_DIM + IDX_HEADS].reshape(n_seq, IDX_HEADS, 1)
    gate_s = hc_s[:, IDX_DIM + IDX_HEADS:].reshape(n_seq, NSA_HEADS, 3)
    oa_s = _dsa_sample(cache_dsa_kidx.reshape(n_pool, page, IDX_DIM), cache_dsa_kv.reshape(-1, HEAD_DIM),
                       page_table, q_a, q_i, w_i, hc_s, kv_s[0])
    ob_s = _nsa_sample(cache_nsa_slc_kv.reshape(-1, HEAD_DIM), page_table, page, q_b, gate_s, ckv_s,
                       state_nsa_win_kv.reshape(-1, HEAD_DIM), kv_s[2], kv_s[3])

    w_o = w_out[0].astype(BF16)
    attn_p = jnp.concatenate([oa_p, ob_p], axis=1)
    attn_s = jnp.concatenate([oa_s.reshape(n_seq, Q_ROW), ob_s.reshape(n_seq, Q_ROW)], axis=1)
    wr = jnp.concatenate([w_router_group[0], w_router_expert[0],
                          jnp.zeros((d_model, LANES - N_GROUPS - N_EXPERTS), F32)], axis=1)
    br = jnp.concatenate([b_router_group[0], b_router_expert[0],
                          jnp.zeros((LANES - N_GROUPS - N_EXPERTS,), F32)]).reshape(1, LANES)
    g1, b1 = ln1_g[0].reshape(1, d_model), ln1_b[0].reshape(1, d_model)
    g2, b2 = ln2_g[0].reshape(1, d_model), ln2_b[0].reshape(1, d_model)
    x1_p, e_p, gt_p = _ln_router(xp, _matmul(attn_p, w_o), g1, b1, wr, br, alpha)
    x1_s, e_s, gt_s = _ln_router(xs, _matmul(attn_s, w_o), g1, b1, wr, br, alpha)

    e_all = jnp.concatenate([e_p[:, :EXPERT_TOPK], e_s[:, :EXPERT_TOPK]], axis=0).reshape(-1)
    a_n = e_all.shape[0]
    nb = -(-(a_n + N_EXPERTS * (MOE_ROWS - 1)) // MOE_ROWS)
    dest, block_e, n_used, src_tok = _route_plan(e_all, nb)
    xbuf = _dispatch(src_tok, n_used, x1_p, x1_s, nb)
    hid = _gate_up(block_e, n_used, xbuf, w_gate[0], w_up[0], nb)
    ybuf = _down(block_e, n_used, hid, w_down[0], nb)
    y_p = _combine(dest[:m_p * EXPERT_TOPK], ybuf, x1_p, gt_p, g2, b2, alpha)
    y_s = _combine(dest[m_p * EXPERT_TOPK:], ybuf, x1_s, gt_s, g2, b2, alpha)

    def state(a, lead):
        return a.reshape((1,) + lead + (NSA_KV_GROUPS, 2, HEAD_DIM))

    n_win = min(WINDOW, seq)
    win_p = kv_p[3].reshape(batch, seq, KV_ROW)[:, seq - n_win:]
    win_state = state_nsa_win_kv[0].reshape(n_seq, -1, KV_ROW)
    win_s = jnp.concatenate([win_state[:, 1:], kv_s[3].reshape(n_seq, 1, KV_ROW)], axis=1)
    return (y_p.reshape(batch, seq, d_model), y_s.reshape(n_seq, 1, d_model),
            state(kv_p[0], (batch, seq)), state(kv_s[0], (n_seq, 1)),
            hc_p[:, :IDX_DIM].reshape(1, batch, seq, IDX_DIM), hc_s[:, :IDX_DIM].reshape(1, n_seq, 1, IDX_DIM),
            state(kv_p[1], (batch, seq)), state(kv_s[1], (n_seq, 1)),
            state(kv_p[2], (batch, seq)), state(kv_s[2], (n_seq, 1)),
            state(win_p, (batch, n_win)), state(win_s, (n_seq, win_state.shape[1])))
```

```python
import functools

import jax
import jax.numpy as jnp
from jax import lax
from jax.experimental import pallas as pl
from jax.experimental.pallas import tpu as pltpu

F32 = jnp.float32
BF16 = jnp.bfloat16
I32 = jnp.int32

HEAD_DIM = 128
DSA_HEADS = 16
DSA_KV_HEADS = 4
IDX_HEADS = 16
IDX_DIM = 64
IDX_TOPK = 256
NSA_HEADS = 16
NSA_KV_GROUPS = 4
CMP_STRIDE = 16
CMP_BLOCK = 32
SLC_BLOCK = 64
SLC_TOPN = 16
WINDOW = 512
N_GROUPS = 8
EXPERTS_PER_GROUP = 8
N_EXPERTS = N_GROUPS * EXPERTS_PER_GROUP
EXPERT_TOPK = 2
LN_EPS = 1e-5
NEG_INF = -1e30
FORCE_SCORE = 1e9

KV_ROW = 2 * DSA_KV_HEADS * HEAD_DIM
KV_SLABS = 2 * DSA_KV_HEADS
KV_HALF = DSA_KV_HEADS * HEAD_DIM
Q_ROW = DSA_HEADS * HEAD_DIM
Q_TILE = 128
KEY_SPAN = 512
SEQ_GROUP = 8
LANES = 128
MOE_ROWS = 128
MOE_RING = 4
MOE_CHUNK = 256
VMEM_LIMIT = 56 * 1024 * 1024
SCALE = HEAD_DIM ** -0.5
LOG2E = 1.4426950408889634
MASKED_DIST = 1e32
INT_MIN = -(2 ** 31)


def _slopes(n):
    return [2.0 ** (-8.0 * i / n) for i in range(1, n + 1)]


def _cparams(sem):
    return pltpu.CompilerParams(dimension_semantics=sem, vmem_limit_bytes=VMEM_LIMIT)


def _dot(a, b):
    return jnp.dot(a, b, preferred_element_type=F32)


def _dot_nt(a, b):
    return lax.dot_general(a, b, (((1,), (1,)), ((), ())), preferred_element_type=F32)


def _dot_hi(a, b):
    return jnp.dot(a, b, preferred_element_type=F32, precision=lax.Precision.HIGHEST)


def _pick(n, cands):
    for c in cands:
        if n % c == 0:
            return c
    return n


def _mm_kernel(*refs, n_lhs):
    x_refs, w_refs, o_ref = refs[:n_lhs], refs[n_lhs:2 * n_lhs], refs[2 * n_lhs]
    acc = _dot(x_refs[0][...], w_refs[0][...])
    for x_ref, w_ref in zip(x_refs[1:], w_refs[1:]):
        acc = acc + _dot(x_ref[...], w_ref[...])
    o_ref[...] = acc.astype(o_ref.dtype)


def _matmul(xs, w, n0=0, n=None):
    m = xs[0].shape[0]
    n = w.shape[1] if n is None else n
    kds = [x.shape[1] for x in xs]
    assert len(set(kds)) == 1 and sum(kds) == w.shape[0]
    kd = kds[0]
    tm = _pick(m, (512, 256, 128))
    tn = _pick(n, (1024, 512, 256, 128))
    assert n0 % tn == 0
    j0 = n0 // tn
    x_specs = [pl.BlockSpec((tm, kd), lambda j, i: (i, 0)) for _ in xs]
    w_specs = [pl.BlockSpec((kd, tn), functools.partial(lambda j, i, a: (a, j + j0), a=a)) for a in range(len(xs))]
    return pl.pallas_call(
        functools.partial(_mm_kernel, n_lhs=len(xs)),
        grid=(n // tn, m // tm),
        in_specs=x_specs + w_specs,
        out_specs=pl.BlockSpec((tm, tn), lambda j, i: (i, j)),
        out_shape=jax.ShapeDtypeStruct((m, n), F32),
        compiler_params=_cparams(("parallel", "parallel")),
        name="matmul",
    )(*xs, *([w] * len(xs)))


def _kv_proj_kernel(x_ref, wk_ref, wv_ref, slab_ref, tile_ref):
    x = x_ref[...]
    rows = x.shape[0]
    for c, w_ref in enumerate((wk_ref, wv_ref)):
        res = _dot(x, w_ref[...])
        for g in range(DSA_KV_HEADS):
            piece = res[:, g * HEAD_DIM:(g + 1) * HEAD_DIM]
            slab_ref[pl.ds(g * 2 + c, rows, stride=KV_SLABS), :] = piece
            tile_ref[:, (g * 2 + c) * HEAD_DIM:(g * 2 + c + 1) * HEAD_DIM] = piece.astype(BF16)


def _kv_proj(x, w, k_col, v_col):
    m, kd = x.shape
    tm = _pick(m, (512, 256, 128))
    assert k_col % KV_HALF == 0 and v_col % KV_HALF == 0
    kb, vb = k_col // KV_HALF, v_col // KV_HALF
    return pl.pallas_call(
        _kv_proj_kernel,
        grid=(m // tm,),
        in_specs=[pl.BlockSpec((tm, kd), lambda i: (i, 0)),
                  pl.BlockSpec((kd, KV_HALF), lambda i: (0, kb)),
                  pl.BlockSpec((kd, KV_HALF), lambda i: (0, vb))],
        out_specs=[pl.BlockSpec((tm * KV_SLABS, HEAD_DIM), lambda i: (i, 0)),
                   pl.BlockSpec((tm, KV_ROW), lambda i: (i, 0))],
        out_shape=[jax.ShapeDtypeStruct((m * KV_SLABS, HEAD_DIM), F32), jax.ShapeDtypeStruct((m, KV_ROW), BF16)],
        compiler_params=_cparams(("parallel",)),
        name="kv_proj",
    )(x, w, w)


def _slab(ref, n_tok, g, c):
    return ref[pl.ds(g * 2 + c, n_tok, stride=KV_SLABS), :]


def _page_specs(n_pages, rows, width, index):
    return [pl.BlockSpec((rows, width), functools.partial(lambda *a, p: (index(*a, p), 0), p=p))
            for p in range(n_pages)]


def _sort_key(x):
    b = pltpu.bitcast(x, I32)
    return jnp.where(b < 0, b ^ jnp.int32(0x7FFFFFFF), b)


def _topk_mask(sc, idx, k, idx_bits):
    key = _sort_key(sc)

    def count(m):
        return jnp.sum(m.astype(I32), axis=1, keepdims=True)

    t0 = jnp.where(count(key >= 0) >= k, jnp.int32(0), jnp.int32(INT_MIN))

    def vstep(i, t):
        cand = t | (jnp.int32(1) << (30 - i))
        return jnp.where(count(key >= cand) >= k, cand, t)

    thr = lax.fori_loop(0, 31, vstep, t0)
    gt = key > thr
    eq = key == thr
    need = k - count(gt)
    crowded = count(eq) > need
    n_steps = jnp.where(jnp.max(crowded.astype(I32)) > 0, idx_bits, 0)

    def istep(i, c):
        cand = c | (jnp.int32(1) << (idx_bits - 1 - i))
        return jnp.where(count(eq & (idx < cand)) < need, cand, c)

    cut = lax.fori_loop(0, n_steps, istep, jnp.zeros_like(thr))
    return gt | (eq & (jnp.logical_not(crowded) | (idx <= cut)))


def _attend(q2, k, v, dm, slope2):
    s = _dot_nt(q2, k) - slope2 * dm
    m = jnp.max(s, axis=1, keepdims=True)
    p = jnp.exp2(s - m)
    l = jnp.sum(p, axis=1, keepdims=True)
    return _dot(p.astype(BF16), v) * (1.0 / l)


def _spans(seq):
    span = KEY_SPAN if seq % KEY_SPAN == 0 else seq
    return span, seq // span


def _dsa_prompt_body(c, q_ref, qi_ref, cq_ref, ck_ref, kv_ref, o_ref, kw, k_keep):
    t = c * Q_TILE + lax.broadcasted_iota(I32, (Q_TILE, 1), 0)
    kpos = lax.broadcasted_iota(I32, (Q_TILE, kw), 1)
    causal = kpos <= t
    ki = ck_ref[0:kw, 0:IDX_DIM].astype(BF16)
    wi = cq_ref[:, IDX_DIM:IDX_DIM + IDX_HEADS] * (IDX_HEADS ** -0.5)
    sc = jnp.zeros((Q_TILE, kw), F32)
    for h in range(IDX_HEADS):
        a = _dot_nt(qi_ref[:, h * IDX_DIM:(h + 1) * IDX_DIM].astype(BF16), ki)
        sc = sc + jnp.maximum(a, 0.0) * wi[:, h:h + 1]
    sc = jnp.where(causal, sc, NEG_INF)
    keep = _topk_mask(sc, kpos, k_keep, max(1, (kw - 1).bit_length()))
    dm = jnp.where(keep & causal, (t - kpos).astype(F32), MASKED_DIST)
    slopes = _slopes(DSA_HEADS)
    rep = DSA_HEADS // DSA_KV_HEADS
    for g in range(DSA_KV_HEADS):
        kg = kv_ref[0:kw, g * 256:g * 256 + 128]
        vg = kv_ref[0:kw, g * 256 + 128:g * 256 + 256]
        for r in range(rep):
            h = g * rep + r
            q2 = (q_ref[:, h * HEAD_DIM:(h + 1) * HEAD_DIM] * (SCALE * LOG2E)).astype(BF16)
            o_ref[:, h * HEAD_DIM:(h + 1) * HEAD_DIM] = _attend(q2, kg, vg, dm, slopes[h] * LOG2E).astype(BF16)


def _dsa_prompt_kernel(q_ref, qi_ref, cq_ref, ck_ref, kv_ref, o_ref, *, seq, k_keep):
    c = pl.program_id(1)
    span, n_var = _spans(seq)
    for v in range(n_var):
        @pl.when((c * Q_TILE) // span == v)
        def _(kw=(v + 1) * span):
            _dsa_prompt_body(c, q_ref, qi_ref, cq_ref, ck_ref, kv_ref, o_ref, kw, k_keep)


def _dsa_prompt(q, qi, h_c, kv, batch, seq):
    nc = seq // Q_TILE
    k_keep = min(IDX_TOPK, seq // 4)
    tile = lambda b, c: (b * nc + c, 0)
    return pl.pallas_call(
        functools.partial(_dsa_prompt_kernel, seq=seq, k_keep=k_keep),
        grid=(batch, nc),
        in_specs=[pl.BlockSpec((Q_TILE, Q_ROW), tile),
                  pl.BlockSpec((Q_TILE, IDX_HEADS * IDX_DIM), tile),
                  pl.BlockSpec((Q_TILE, LANES), tile),
                  pl.BlockSpec((seq, LANES), lambda b, c: (b, 0)),
                  pl.BlockSpec((seq, KV_ROW), lambda b, c: (b, 0))],
        out_specs=pl.BlockSpec((Q_TILE, Q_ROW), tile),
        out_shape=jax.ShapeDtypeStruct((batch * seq, Q_ROW), BF16),
        compiler_params=_cparams(("parallel", "arbitrary")),
        name="dsa_prompt",
    )(q, qi, h_c, h_c, kv)


def _peb_kernel(pe_ref, w1_ref, o_ref):
    for c in range(2):
        o_ref[c] = _dot_hi(pe_ref[c], w1_ref[c])


def _cmp_pe_bias(cmp_pe, cmp_w1):
    return pl.pallas_call(
        _peb_kernel,
        out_shape=jax.ShapeDtypeStruct((2, 1, HEAD_DIM), F32),
        name="cmp_pe_bias",
    )(cmp_pe.reshape(2, 1, CMP_BLOCK * HEAD_DIM), cmp_w1.reshape(2, CMP_BLOCK * HEAD_DIM, HEAD_DIM))


def _compress_kernel(pt_ref, *refs, n_pages, page):
    pages = refs[:n_pages]
    w1_ref, w2_ref, peb_ref, o_ref = refs[n_pages:]
    sub_per_page = page // CMP_STRIDE
    n_sub = n_pages * sub_per_page
    for c in range(2):
        lhs = jnp.concatenate([
            jnp.concatenate([
                jnp.concatenate([p[pl.ds(j * KV_SLABS + g * 2 + c, sub_per_page, stride=CMP_STRIDE * KV_SLABS), :]
                                 for p in pages], axis=0).astype(BF16)
                for j in range(CMP_STRIDE)], axis=1)
            for g in range(NSA_KV_GROUPS)], axis=0)
        part = _dot(lhs, w1_ref[c].astype(BF16))
        w2 = w2_ref[c].astype(BF16)
        for g in range(NSA_KV_GROUPS):
            col = g * 256 + c * 128
            pg = part[g * n_sub:(g + 1) * n_sub]
            nxt = pltpu.roll(pg[:, HEAD_DIM:], n_sub - 1, 0)
            hid = pg[:, :HEAD_DIM] + nxt + peb_ref[c]
            o_ref[0, :, col:col + HEAD_DIM] = _dot(jax.nn.gelu(hid).astype(BF16), w2)


def _compress(pool, page_table, page, w1r, w2, peb):
    n_seq, n_pages = page_table.shape
    n_sub = n_pages * page // CMP_STRIDE
    page_specs = _page_specs(n_pages, page * KV_SLABS, HEAD_DIM, lambda n, pt, p: pt[n, p])
    return pl.pallas_call(
        functools.partial(_compress_kernel, n_pages=n_pages, page=page),
        grid_spec=pltpu.PrefetchScalarGridSpec(
            num_scalar_prefetch=1,
            grid=(n_seq,),
            in_specs=page_specs + [
                pl.BlockSpec((2, CMP_STRIDE * HEAD_DIM, 2 * HEAD_DIM), lambda n, pt: (0, 0, 0)),
                pl.BlockSpec((2, HEAD_DIM, HEAD_DIM), lambda n, pt: (0, 0, 0)),
                pl.BlockSpec((2, 1, HEAD_DIM), lambda n, pt: (0, 0, 0))],
            out_specs=pl.BlockSpec((1, n_sub, KV_ROW), lambda n, pt: (n, 0, 0))),
        out_shape=jax.ShapeDtypeStruct((n_seq, n_sub, KV_ROW), F32),
        compiler_params=_cparams(("arbitrary",)),
        name="nsa_compress",
    )(page_table, *([pool] * n_pages), w1r, w2, peb)


def _cover(n_cmp_pad, n_slc_pad):
    cs = lax.broadcasted_iota(I32, (n_cmp_pad, n_slc_pad), 0) * CMP_STRIDE
    bs = lax.broadcasted_iota(I32, (n_cmp_pad, n_slc_pad), 1) * SLC_BLOCK
    return ((cs < bs + SLC_BLOCK) & (cs + CMP_BLOCK > bs)).astype(F32)


def _select_blocks(score, t, n_slc, n_keep):
    rows, width = score.shape
    j = lax.broadcasted_iota(I32, (rows, width), 1)
    cur = t // SLC_BLOCK
    forced = (j == 0) | (j == cur) | (j == cur - 1)
    admissible = j * SLC_BLOCK <= t
    score = jnp.where(forced, FORCE_SCORE, jnp.where(admissible, score, NEG_INF))
    rank = jnp.zeros((rows, width), I32)
    for k in range(n_slc):
        sk = score[:, k:k + 1]
        ahead = (sk > score) | ((sk == score) & (j > k))
        rank = rank + ahead.astype(I32)
    return ((rank < n_keep) & (j < n_slc)).astype(F32)


def _nsa_prompt_body(c, q_ref, gate_ref, ckv_ref, slc_ref, win_ref, o_ref, kw, seq, n_cmp, win_keys):
    t = c * Q_TILE + lax.broadcasted_iota(I32, (Q_TILE, 1), 0)
    slopes = _slopes(NSA_HEADS)
    rep = NSA_HEADS // NSA_KV_GROUPS
    n_cmp_pad = ckv_ref.shape[1]
    n_slc = seq // SLC_BLOCK
    n_keep = min(SLC_TOPN, n_slc)

    kc = lax.broadcasted_iota(I32, (Q_TILE, n_cmp_pad), 1)
    dist_c_i = t - (kc * CMP_STRIDE + (CMP_BLOCK - 1))
    valid_c = (dist_c_i >= 0) & (kc < n_cmp)
    dist_c = dist_c_i.astype(F32)
    cover = _cover(n_cmp_pad, n_slc)
    expand = (lax.broadcasted_iota(I32, (n_slc, kw), 1) // SLC_BLOCK
              == lax.broadcasted_iota(I32, (n_slc, kw), 0)).astype(BF16)

    kpos = lax.broadcasted_iota(I32, (Q_TILE, kw), 1)
    causal = kpos <= t
    dist_s = (t - kpos).astype(F32)

    w0 = pl.multiple_of(jnp.maximum(c * Q_TILE + Q_TILE - win_keys, 0), Q_TILE)
    wpos = w0 + lax.broadcasted_iota(I32, (Q_TILE, win_keys), 1)
    dist_w_i = t - wpos
    dm_w = jnp.where((dist_w_i >= 0) & (dist_w_i <= WINDOW), dist_w_i.astype(F32), MASKED_DIST)

    gates = jax.nn.sigmoid(gate_ref[:, 0:3 * NSA_HEADS])

    for g in range(NSA_KV_GROUPS):
        ck = ckv_ref[0, :, g * 256:g * 256 + 128].astype(BF16)
        cv = ckv_ref[0, :, g * 256 + 128:g * 256 + 256].astype(BF16)
        q2s, o_cs = [], []
        imp = jnp.zeros((Q_TILE, n_cmp_pad), F32)
        for r in range(rep):
            h = g * rep + r
            q2 = (q_ref[:, h * HEAD_DIM:(h + 1) * HEAD_DIM] * (SCALE * LOG2E)).astype(BF16)
            q2s.append(q2)
            s = jnp.where(valid_c, _dot_nt(q2, ck) - (slopes[h] * LOG2E) * dist_c, NEG_INF)
            m = jnp.max(s, axis=1, keepdims=True)
            p = jnp.where(valid_c, jnp.exp2(s - m), 0.0)
            l = jnp.sum(p, axis=1, keepdims=True)
            p = p * (1.0 / jnp.where(l > 0.0, l, 1.0))
            imp = imp + p
            o_cs.append(_dot(p.astype(BF16), cv))
        sel = _select_blocks(_dot_hi(imp, cover), t, n_slc, n_keep)
        picked = _dot(sel.astype(BF16), expand) > 0.5
        dm_s = jnp.where(picked & causal, dist_s, MASKED_DIST)
        sk = slc_ref[0:kw, g * 256:g * 256 + 128]
        sv = slc_ref[0:kw, g * 256 + 128:g * 256 + 256]
        wk = win_ref[pl.ds(w0, win_keys), g * 256:g * 256 + 128]
        wv = win_ref[pl.ds(w0, win_keys), g * 256 + 128:g * 256 + 256]
        for r in range(rep):
            h = g * rep + r
            o_s = _attend(q2s[r], sk, sv, dm_s, slopes[h] * LOG2E)
            o_w = _attend(q2s[r], wk, wv, dm_w, slopes[h] * LOG2E)
            o_ref[:, h * HEAD_DIM:(h + 1) * HEAD_DIM] = (
                gates[:, 3 * h:3 * h + 1] * o_cs[r] + gates[:, 3 * h + 1:3 * h + 2] * o_s
                + gates[:, 3 * h + 2:3 * h + 3] * o_w).astype(BF16)


def _nsa_prompt_kernel(q_ref, gate_ref, ckv_ref, slc_ref, win_ref, o_ref, *, seq, n_cmp, win_keys):
    c = pl.program_id(1)
    span, n_var = _spans(seq)
    for v in range(n_var):
        @pl.when((c * Q_TILE) // span == v)
        def _(kw=(v + 1) * span):
            _nsa_prompt_body(c, q_ref, gate_ref, ckv_ref, slc_ref, win_ref, o_ref, kw, seq, n_cmp, win_keys)


def _nsa_prompt(q, gate, ckv, slc, win, batch, seq):
    nc = seq // Q_TILE
    n_cmp_pad = ckv.shape[1]
    n_cmp = seq // CMP_STRIDE - CMP_BLOCK // CMP_STRIDE + 1
    win_keys = min(seq, WINDOW + Q_TILE)
    tile = lambda b, c: (b * nc + c, 0)
    return pl.pallas_call(
        functools.partial(_nsa_prompt_kernel, seq=seq, n_cmp=n_cmp, win_keys=win_keys),
        grid=(batch, nc),
        in_specs=[pl.BlockSpec((Q_TILE, Q_ROW), tile),
                  pl.BlockSpec((Q_TILE, LANES), tile),
                  pl.BlockSpec((1, n_cmp_pad, KV_ROW), lambda b, c: (b, 0, 0)),
                  pl.BlockSpec((seq, KV_ROW), lambda b, c: (b, 0)),
                  pl.BlockSpec((seq, KV_ROW), lambda b, c: (b, 0))],
        out_specs=pl.BlockSpec((Q_TILE, Q_ROW), tile),
        out_shape=jax.ShapeDtypeStruct((batch * seq, Q_ROW), BF16),
        compiler_params=_cparams(("parallel", "arbitrary")),
        name="nsa_prompt",
    )(q, gate, ckv, slc, win)


def _group_rows(per_group, rep):
    row = lax.broadcasted_iota(I32, per_group[0].shape, 0) // rep
    out = per_group[0]
    for g in range(1, len(per_group)):
        out = jnp.where(row == g, per_group[g], out)
    return out


def _dsa_sample_kernel(pt_ref, *refs, n_pages, page, k_keep):
    n_ki = SEQ_GROUP * n_pages
    ki_pages = refs[:n_ki]
    kv_pages = refs[n_ki:n_ki + n_pages]
    q_ref, qi_ref, wi_ref, cnew_ref, kvnew_ref, o_ref, keep_ref = refs[n_ki + n_pages:]
    r = pl.program_id(1)
    past = n_pages * page
    width = past + LANES
    rep = DSA_HEADS // DSA_KV_HEADS
    slopes = _slopes(DSA_HEADS)

    @pl.when(r == 0)
    def _():
        rows = []
        for s in range(SEQ_GROUP):
            qi = qi_ref[s].astype(BF16)
            wi = wi_ref[s] * (IDX_HEADS ** -0.5)
            knew = cnew_ref[pl.ds(s, 1), 0:IDX_DIM]
            a_past = jnp.concatenate([_dot_nt(qi, ki_pages[s * n_pages + p][0].astype(BF16))
                                      for p in range(n_pages)], axis=1)
            a_new = _dot_nt(qi, jnp.broadcast_to(knew, (8, IDX_DIM)).astype(BF16))[:, 0:1]
            sc_past = jnp.sum(jnp.maximum(a_past, 0.0) * wi, axis=0, keepdims=True)
            sc_new = jnp.sum(jnp.maximum(a_new, 0.0) * wi, axis=0, keepdims=True)
            rows.append(jnp.concatenate([sc_past, jnp.broadcast_to(sc_new, (1, LANES))], axis=1))
        col = lax.broadcasted_iota(I32, (SEQ_GROUP, width), 1)
        sc = jnp.where(col <= past, jnp.concatenate(rows, axis=0), NEG_INF)
        keep = _topk_mask(sc, col, k_keep, max(1, (width - 1).bit_length())) & (col <= past)
        keep_ref[...] = keep.astype(F32)

    keep = keep_ref[pl.ds(r, 1), :] > 0.5
    keep_past = jnp.broadcast_to(keep[:, 0:past], (DSA_HEADS, past))
    keep_new = jnp.broadcast_to(keep[:, past:past + 1], (DSA_HEADS, 1))

    q = (q_ref[0] * SCALE).astype(BF16)
    q32 = q.astype(F32)
    kvnew = kvnew_ref[pl.ds(r, 1), :]
    slope_col = jnp.concatenate([jnp.full((1, 1), s, F32) for s in slopes], axis=0)
    dist = (past - lax.broadcasted_iota(I32, (DSA_HEADS, past), 1)).astype(F32)
    s_g, s_new_g = [], []
    for g in range(DSA_KV_HEADS):
        s_g.append(jnp.concatenate(
            [_dot_nt(q, _slab(p, page, g, 0).astype(BF16)) for p in kv_pages], axis=1))
        s_new_g.append(jnp.sum(q32 * kvnew[:, g * 256:g * 256 + 128], axis=1, keepdims=True))
    s = jnp.where(keep_past, _group_rows(s_g, rep) - slope_col * dist, NEG_INF)
    s_new = jnp.where(keep_new, _group_rows(s_new_g, rep), NEG_INF)
    m = jnp.maximum(jnp.max(s, axis=1, keepdims=True), s_new)
    p = jnp.where(keep_past, jnp.exp(s - m), 0.0)
    p_new = jnp.where(keep_new, jnp.exp(s_new - m), 0.0)
    l = jnp.sum(p, axis=1, keepdims=True) + p_new
    inv = 1.0 / jnp.where(l > 0.0, l, 1.0)
    pb = p.astype(BF16)
    p_new = p_new.astype(BF16).astype(F32)
    o_g = []
    for g in range(DSA_KV_HEADS):
        acc = p_new * kvnew[:, g * 256 + 128:g * 256 + 256]
        for i, pg in enumerate(kv_pages):
            acc = acc + _dot(pb[:, i * page:(i + 1) * page], _slab(pg, page, g, 1).astype(BF16))
        o_g.append(acc)
    o_ref[0] = (_group_rows(o_g, rep) * inv).astype(BF16)


def _dsa_sample(kidx_pool, kv_pool, page_table, q, qi, wi, h_c, kv_new):
    n_seq, n_pages = page_table.shape
    assert n_seq % SEQ_GROUP == 0
    page = kidx_pool.shape[1]
    total = n_pages * page + 1
    k_keep = min(IDX_TOPK, total // 4)
    width = n_pages * page + LANES

    ki_specs = [pl.BlockSpec((1, page, IDX_DIM),
                             functools.partial(lambda i, r, pt, s, p: (pt[i * SEQ_GROUP + s, p], 0, 0), s=s, p=p))
                for s in range(SEQ_GROUP) for p in range(n_pages)]
    kv_specs = _page_specs(n_pages, page * KV_SLABS, HEAD_DIM, lambda i, r, pt, p: pt[i * SEQ_GROUP + r, p])
    one = lambda i, r, pt: (i * SEQ_GROUP + r, 0, 0)
    grp3 = lambda i, r, pt: (i, 0, 0)
    grp2 = lambda i, r, pt: (i, 0)
    return pl.pallas_call(
        functools.partial(_dsa_sample_kernel, n_pages=n_pages, page=page, k_keep=k_keep),
        grid_spec=pltpu.PrefetchScalarGridSpec(
            num_scalar_prefetch=1,
            grid=(n_seq // SEQ_GROUP, SEQ_GROUP),
            in_specs=ki_specs + kv_specs + [
                pl.BlockSpec((1, DSA_HEADS, HEAD_DIM), one),
                pl.BlockSpec((SEQ_GROUP, IDX_HEADS, IDX_DIM), grp3),
                pl.BlockSpec((SEQ_GROUP, IDX_HEADS, 1), grp3),
                pl.BlockSpec((SEQ_GROUP, LANES), grp2),
                pl.BlockSpec((SEQ_GROUP, KV_ROW), grp2)],
            out_specs=pl.BlockSpec((1, DSA_HEADS, HEAD_DIM), one),
            scratch_shapes=[pltpu.VMEM((SEQ_GROUP, width), F32)]),
        out_shape=jax.ShapeDtypeStruct((n_seq, DSA_HEADS, HEAD_DIM), BF16),
        compiler_params=_cparams(("arbitrary", "arbitrary")),
        name="dsa_sample",
    )(page_table, *([kidx_pool] * (SEQ_GROUP * n_pages)), *([kv_pool] * n_pages), q, qi, wi, h_c, kv_new)


def _nsa_sample_kernel(pt_ref, *refs, n_pages, page, n_cmp):
    slc_pages = refs[:n_pages]
    q_ref, gate_ref, ckv_ref, win_ref, slcnew_ref, winnew_ref, winslab_ref, o_ref, wout_ref = refs[n_pages:]
    n = pl.program_id(0)
    past = n_pages * page
    heads = NSA_HEADS
    rep = NSA_HEADS // NSA_KV_GROUPS
    slopes = _slopes(NSA_HEADS)
    slope_col = jnp.concatenate([jnp.full((1, 1), s, F32) for s in slopes], axis=0)
    q = (q_ref[0] * SCALE).astype(BF16)
    q32 = q.astype(F32)
    gates = jax.nn.sigmoid(gate_ref[0])
    n_cmp_pad = ckv_ref.shape[1]
    total = past + 1
    n_slc = -(-total // SLC_BLOCK)
    n_keep = min(SLC_TOPN, n_slc)
    slc_pad = -(-n_slc // LANES) * LANES

    kc = lax.broadcasted_iota(I32, (heads, n_cmp_pad), 1)
    dist_c_i = past - (kc * CMP_STRIDE + (CMP_BLOCK - 1))
    valid_c = (dist_c_i >= 0) & (kc < n_cmp)
    s_g = [_dot_nt(q, ckv_ref[0, :, g * 256:g * 256 + 128].astype(BF16)) for g in range(NSA_KV_GROUPS)]
    s = jnp.where(valid_c, _group_rows(s_g, rep) - slope_col * dist_c_i.astype(F32), NEG_INF)
    m = jnp.max(s, axis=1, keepdims=True)
    p = jnp.where(valid_c, jnp.exp(s - m), 0.0)
    l = jnp.sum(p, axis=1, keepdims=True)
    p = p * (1.0 / jnp.where(l > 0.0, l, 1.0))
    pb = p.astype(BF16)
    o_c = _group_rows([_dot(pb, ckv_ref[0, :, g * 256 + 128:g * 256 + 256].astype(BF16))
                       for g in range(NSA_KV_GROUPS)], rep)
    same_group = (lax.broadcasted_iota(I32, (heads, heads), 0) // rep
                  == lax.broadcasted_iota(I32, (heads, heads), 1) // rep).astype(F32)
    imp = _dot_hi(same_group, p)
    score = _dot_hi(imp, _cover(n_cmp_pad, slc_pad))
    sel = _select_blocks(score, jnp.full((heads, 1), past, I32), n_slc, n_keep)
    expand = (lax.broadcasted_iota(I32, (slc_pad, past), 1) // SLC_BLOCK
              == lax.broadcasted_iota(I32, (slc_pad, past), 0)).astype(BF16)
    valid_s = _dot(sel.astype(BF16), expand) > 0.5
    new_blk = past // SLC_BLOCK
    valid_s_new = sel[:, new_blk:new_blk + 1] > 0.5

    def attend_with_new(s_past, valid_past, s_new, valid_new, dist_past, v_of, v_new_of):
        s_p = jnp.where(valid_past, s_past - slope_col * dist_past, NEG_INF)
        s_n = jnp.where(valid_new, s_new, NEG_INF)
        mm = jnp.maximum(jnp.max(s_p, axis=1, keepdims=True), s_n)
        pp = jnp.where(valid_past, jnp.exp(s_p - mm), 0.0)
        pn = jnp.where(valid_new, jnp.exp(s_n - mm), 0.0)
        ll = jnp.sum(pp, axis=1, keepdims=True) + pn
        inv = 1.0 / jnp.where(ll > 0.0, ll, 1.0)
        ppb = pp.astype(BF16)
        pn = pn.astype(BF16).astype(F32)
        outs = [v_of(ppb, g) + pn * v_new_of(g) for g in range(NSA_KV_GROUPS)]
        return _group_rows(outs, rep) * inv

    slcnew = slcnew_ref[pl.ds(n, 1), :]
    s_g, s_new_g = [], []
    for g in range(NSA_KV_GROUPS):
        s_g.append(jnp.concatenate(
            [_dot_nt(q, _slab(pg, page, g, 0).astype(BF16)) for pg in slc_pages], axis=1))
        s_new_g.append(jnp.sum(q32 * slcnew[:, g * 256:g * 256 + 128], axis=1, keepdims=True))
    dist_s = (past - lax.broadcasted_iota(I32, (heads, past), 1)).astype(F32)

    def slc_v(ppb, g):
        acc = jnp.zeros((heads, HEAD_DIM), F32)
        for i, pg in enumerate(slc_pages):
            acc = acc + _dot(ppb[:, i * page:(i + 1) * page], _slab(pg, page, g, 1).astype(BF16))
        return acc

    o_s = attend_with_new(_group_rows(s_g, rep), valid_s, _group_rows(s_new_g, rep), valid_s_new, dist_s, slc_v,
                          lambda g: slcnew[:, g * 256 + 128:g * 256 + 256])

    n_buf = win_ref.shape[0] // KV_SLABS
    winnew = winnew_ref[pl.ds(n, 1), :]
    dist_w_i = n_buf - lax.broadcasted_iota(I32, (heads, n_buf), 1)
    valid_w = dist_w_i <= WINDOW
    s_g = [_dot_nt(q, _slab(win_ref, n_buf, g, 0).astype(BF16)) for g in range(NSA_KV_GROUPS)]
    s_new_g = [jnp.sum(q32 * winnew[:, g * 256:g * 256 + 128], axis=1, keepdims=True)
               for g in range(NSA_KV_GROUPS)]
    o_w = attend_with_new(_group_rows(s_g, rep), valid_w, _group_rows(s_new_g, rep),
                          jnp.full((heads, 1), True), dist_w_i.astype(F32),
                          lambda ppb, g: _dot(ppb, _slab(win_ref, n_buf, g, 1).astype(BF16)),
                          lambda g: winnew[:, g * 256 + 128:g * 256 + 256])

    o_ref[0] = (gates[:, 0:1] * o_c + gates[:, 1:2] * o_s + gates[:, 2:3] * o_w).astype(BF16)

    keep_rows = (n_buf - 1) * KV_SLABS
    wout_ref[0:keep_rows, :] = win_ref[KV_SLABS:n_buf * KV_SLABS, :]
    wout_ref[keep_rows:keep_rows + KV_SLABS, :] = winslab_ref[pl.ds(pl.multiple_of(n * KV_SLABS, KV_SLABS), KV_SLABS), :]


def _nsa_sample(slc_pool, page_table, page, q, gate, ckv, win_state, slc_new, win_new, win_new_slab):
    n_seq, n_pages = page_table.shape
    n_cmp_pad = ckv.shape[1]
    n_cmp = n_pages * page // CMP_STRIDE - CMP_BLOCK // CMP_STRIDE + 1
    n_buf = win_state.shape[0] // (n_seq * KV_SLABS)
    slc_specs = _page_specs(n_pages, page * KV_SLABS, HEAD_DIM, lambda n, pt, p: pt[n, p])
    return pl.pallas_call(
        functools.partial(_nsa_sample_kernel, n_pages=n_pages, page=page, n_cmp=n_cmp),
        grid_spec=pltpu.PrefetchScalarGridSpec(
            num_scalar_prefetch=1,
            grid=(n_seq,),
            in_specs=slc_specs + [
                pl.BlockSpec((1, NSA_HEADS, HEAD_DIM), lambda n, pt: (n, 0, 0)),
                pl.BlockSpec((1, NSA_HEADS, 3), lambda n, pt: (n, 0, 0)),
                pl.BlockSpec((1, n_cmp_pad, KV_ROW), lambda n, pt: (n, 0, 0)),
                pl.BlockSpec((n_buf * KV_SLABS, HEAD_DIM), lambda n, pt: (n, 0)),
                pl.BlockSpec((n_seq, KV_ROW), lambda n, pt: (0, 0)),
                pl.BlockSpec((n_seq, KV_ROW), lambda n, pt: (0, 0)),
                pl.BlockSpec((n_seq * KV_SLABS, HEAD_DIM), lambda n, pt: (0, 0))],
            out_specs=[pl.BlockSpec((1, NSA_HEADS, HEAD_DIM), lambda n, pt: (n, 0, 0)),
                       pl.BlockSpec((n_buf * KV_SLABS, HEAD_DIM), lambda n, pt: (n, 0))]),
        out_shape=[jax.ShapeDtypeStruct((n_seq, NSA_HEADS, HEAD_DIM), BF16),
                   jax.ShapeDtypeStruct(win_state.shape, F32)],
        compiler_params=_cparams(("arbitrary",)),
        name="nsa_sample",
    )(page_table, *([slc_pool] * n_pages), q, gate, ckv, win_state, slc_new, win_new, win_new_slab)


def _ln(v, g, b):
    mu = jnp.mean(v, axis=1, keepdims=True)
    d = v - mu
    var = jnp.mean(d * d, axis=1, keepdims=True)
    return d * lax.rsqrt(var + LN_EPS) * g + b


def _ln_router_kernel(x_ref, mix_ref, g_ref, b_ref, wr_ref, br_ref, x1_ref, e_ref, gate_ref, *, alpha):
    x1 = _ln(alpha * x_ref[...] + mix_ref[...], g_ref[...], b_ref[...])
    x1_ref[...] = x1
    logits = _dot_hi(x1, wr_ref[...]) + br_ref[...]
    rows = logits.shape[0]
    lane = lax.broadcasted_iota(I32, (rows, LANES), 1)
    big = jnp.int32(LANES)
    is_grp = lane < N_GROUPS
    lg = jnp.where(is_grp, logits, -jnp.inf)
    mg = jnp.max(lg, axis=1, keepdims=True)
    grp = jnp.min(jnp.where(lg == mg, lane, big), axis=1, keepdims=True)
    p_grp = 1.0 / jnp.sum(jnp.where(is_grp, jnp.exp(lg - mg), 0.0), axis=1, keepdims=True)
    ex = lane - N_GROUPS
    in_grp = (ex >= 0) & (ex < N_EXPERTS) & (ex // EXPERTS_PER_GROUP == grp)
    le = jnp.where(in_grp, logits, -jnp.inf)
    m1 = jnp.max(le, axis=1, keepdims=True)
    e1 = jnp.min(jnp.where(le == m1, ex, big), axis=1, keepdims=True)
    le2 = jnp.where(ex == e1, -jnp.inf, le)
    m2 = jnp.max(le2, axis=1, keepdims=True)
    e2 = jnp.min(jnp.where(le2 == m2, ex, big), axis=1, keepdims=True)
    z = jnp.sum(jnp.where(in_grp, jnp.exp(le - m1), 0.0), axis=1, keepdims=True)
    p1 = 1.0 / z
    p2 = jnp.exp(m2 - m1) / z
    g1 = p_grp * p1 / (p1 + p2)
    g2 = p_grp * p2 / (p1 + p2)
    e_ref[...] = jnp.where(lane == 0, e1, jnp.where(lane == 1, e2, 0))
    gate_ref[...] = jnp.where(lane == 0, g1, jnp.where(lane == 1, g2, 0.0))


def _ln_router(x, mix, g, b, wr, br, alpha):
    m, d = x.shape
    tm = _pick(m, (256, 128))
    row = lambda i: (i, 0)
    fixed = lambda i: (0, 0)
    return pl.pallas_call(
        functools.partial(_ln_router_kernel, alpha=alpha),
        grid=(m // tm,),
        in_specs=[pl.BlockSpec((tm, d), row), pl.BlockSpec((tm, d), row),
                  pl.BlockSpec((1, d), fixed), pl.BlockSpec((1, d), fixed),
                  pl.BlockSpec((d, LANES), fixed), pl.BlockSpec((1, LANES), fixed)],
        out_specs=[pl.BlockSpec((tm, d), row), pl.BlockSpec((tm, LANES), row), pl.BlockSpec((tm, LANES), row)],
        out_shape=[jax.ShapeDtypeStruct((m, d), F32), jax.ShapeDtypeStruct((m, LANES), I32),
                   jax.ShapeDtypeStruct((m, LANES), F32)],
        compiler_params=_cparams(("parallel",)),
        name="ln_router",
    )(x, mix, g, b, wr, br)


def _dispatch_kernel(src_ref, nused_ref, xp_ref, xs_ref, o_ref, buf_ref, sem, *, m_prompt):
    i = pl.program_id(0)

    def row_copy(r):
        tok = src_ref[i * MOE_ROWS + r]
        from_prompt = pltpu.make_async_copy(xp_ref.at[pl.ds(jnp.minimum(tok, m_prompt - 1), 1)],
                                            buf_ref.at[pl.ds(r, 1)], sem)
        from_sample = pltpu.make_async_copy(xs_ref.at[pl.ds(jnp.maximum(tok - m_prompt, 0), 1)],
                                            buf_ref.at[pl.ds(r, 1)], sem)
        return tok < m_prompt, from_prompt, from_sample

    @pl.when(i < nused_ref[0])
    def _():
        def start(r, carry):
            is_p, cp, cs = row_copy(r)

            @pl.when(is_p)
            def _():
                cp.start()

            @pl.when(jnp.logical_not(is_p))
            def _():
                cs.start()

            return carry

        lax.fori_loop(0, MOE_ROWS, start, 0, unroll=4)

        def wait(r, carry):
            row_copy(r)[1].wait()
            return carry

        lax.fori_loop(0, MOE_ROWS, wait, 0, unroll=4)
        o_ref[...] = buf_ref[...].astype(BF16)

    @pl.when(i >= nused_ref[0])
    def _():
        o_ref[...] = jnp.zeros_like(o_ref)


def _dispatch(src_tok, n_used, x1_p, x1_s, nb):
    d = x1_p.shape[1]
    return pl.pallas_call(
        functools.partial(_dispatch_kernel, m_prompt=x1_p.shape[0]),
        grid_spec=pltpu.PrefetchScalarGridSpec(
            num_scalar_prefetch=2,
            grid=(nb,),
            in_specs=[pl.BlockSpec(memory_space=pl.ANY), pl.BlockSpec(memory_space=pl.ANY)],
            out_specs=pl.BlockSpec((MOE_ROWS, d), lambda i, s, u: (i, 0)),
            scratch_shapes=[pltpu.VMEM((MOE_ROWS, d), F32), pltpu.SemaphoreType.DMA(())]),
        out_shape=jax.ShapeDtypeStruct((nb * MOE_ROWS, d), BF16),
        compiler_params=_cparams(("arbitrary",)),
        name="moe_dispatch",
    )(src_tok, n_used, x1_p, x1_s)


def _expert_mm_kernel(run_ref, rune_ref, nruns_ref, nused_ref, x_ref, *refs, n_w, tn, epilogue):
    w_hbm = refs[:n_w]
    o_ref = refs[n_w]
    w_buf = refs[n_w + 1:2 * n_w + 1]
    w16 = refs[2 * n_w + 1:3 * n_w + 1]
    sem, state = refs[3 * n_w + 1], refs[3 * n_w + 2]
    j, i = pl.program_id(0), pl.program_id(1)
    n_runs = nruns_ref[0]
    total = pl.num_programs(0) * n_runs
    cur = j * n_runs + run_ref[i]

    def item_copies(item):
        e = rune_ref[item % n_runs]
        col = pl.multiple_of((item // n_runs) * tn, tn)
        slot = item % MOE_RING
        return [pltpu.make_async_copy(w_hbm[a].at[e, :, pl.ds(col, tn)], w_buf[a].at[slot], sem.at[a, slot])
                for a in range(n_w)]

    @pl.when((j == 0) & (i == 0))
    def _():
        state[0] = 0
        state[1] = -1

    def request(item, carry):
        for cp in item_copies(item):
            cp.start()
        return carry

    limit = jnp.minimum(cur + MOE_RING, total)
    lax.fori_loop(state[0], limit, request, 0)
    state[0] = jnp.maximum(state[0], limit)

    @pl.when(state[1] != cur)
    def _():
        for cp in item_copies(cur):
            cp.wait()
        slot = cur % MOE_RING
        for a in range(n_w):
            w16[a][...] = w_buf[a][slot].astype(BF16)
        state[1] = cur

    @pl.when(i < nused_ref[0])
    def _():
        x = x_ref[...]
        o_ref[...] = epilogue(*[_dot(x, w16[a][...]) for a in range(n_w)]).astype(o_ref.dtype)

    @pl.when(i >= nused_ref[0])
    def _():
        o_ref[...] = jnp.zeros_like(o_ref)


def _expert_mm(plan, x, weights, tn, epilogue, out_dtype, name):
    run_of_block, run_e, n_runs, n_used = plan
    nb = run_of_block.shape[0]
    kd, n = weights[0].shape[1], weights[0].shape[2]
    n_w = len(weights)
    return pl.pallas_call(
        functools.partial(_expert_mm_kernel, n_w=n_w, tn=tn, epilogue=epilogue),
        grid_spec=pltpu.PrefetchScalarGridSpec(
            num_scalar_prefetch=4,
            grid=(n // tn, nb),
            in_specs=[pl.BlockSpec((MOE_ROWS, kd), lambda j, i, *_: (i, 0))]
            + [pl.BlockSpec(memory_space=pl.ANY)] * n_w,
            out_specs=pl.BlockSpec((MOE_ROWS, tn), lambda j, i, *_: (i, j)),
            scratch_shapes=[pltpu.VMEM((MOE_RING, kd, tn), F32) for _ in range(n_w)]
            + [pltpu.VMEM((kd, tn), BF16) for _ in range(n_w)]
            + [pltpu.SemaphoreType.DMA((n_w, MOE_RING)), pltpu.SMEM((2,), I32)]),
        out_shape=jax.ShapeDtypeStruct((nb * MOE_ROWS, n), out_dtype),
        compiler_params=_cparams(("arbitrary", "arbitrary")),
        name=name,
    )(run_of_block, run_e, n_runs, n_used, x, *weights)


def _swiglu(a, u):
    return a * jax.nn.sigmoid(a) * u


def _combine_kernel(dest_ref, y_ref, x1_ref, gate_ref, g_ref, b_ref, o_ref, buf_ref, sem, *, tm, alpha):
    i = pl.program_id(0)

    def row_copy(r, k):
        slot = dest_ref[(i * tm + r) * EXPERT_TOPK + k]
        return pltpu.make_async_copy(y_ref.at[pl.ds(slot, 1)], buf_ref.at[k, pl.ds(r, 1)], sem)

    def start(r, carry):
        for k in range(EXPERT_TOPK):
            row_copy(r, k).start()
        return carry

    lax.fori_loop(0, tm, start, 0, unroll=4)

    def wait(r, carry):
        for k in range(EXPERT_TOPK):
            row_copy(r, k).wait()
        return carry

    lax.fori_loop(0, tm, wait, 0, unroll=4)
    gate = gate_ref[...]
    f = gate[:, 0:1] * buf_ref[0] + gate[:, 1:2] * buf_ref[1]
    o_ref[...] = _ln(alpha * x1_ref[...] + f, g_ref[...], b_ref[...])


def _combine(dest, ybuf, x1, gate, g, b, alpha):
    m, d = x1.shape
    tm = _pick(m, (128,))
    return pl.pallas_call(
        functools.partial(_combine_kernel, tm=tm, alpha=alpha),
        grid_spec=pltpu.PrefetchScalarGridSpec(
            num_scalar_prefetch=1,
            grid=(m // tm,),
            in_specs=[pl.BlockSpec(memory_space=pl.ANY),
                      pl.BlockSpec((tm, d), lambda i, dd: (i, 0)),
                      pl.BlockSpec((tm, LANES), lambda i, dd: (i, 0)),
                      pl.BlockSpec((1, d), lambda i, dd: (0, 0)),
                      pl.BlockSpec((1, d), lambda i, dd: (0, 0))],
            out_specs=pl.BlockSpec((tm, d), lambda i, dd: (i, 0)),
            scratch_shapes=[pltpu.VMEM((EXPERT_TOPK, tm, d), F32), pltpu.SemaphoreType.DMA(())]),
        out_shape=jax.ShapeDtypeStruct((m, d), F32),
        compiler_params=_cparams(("arbitrary",)),
        name="moe_combine",
    )(dest, ybuf, x1, gate, g, b)


def _route_plan(e_all, nb):
    a_n = e_all.shape[0]
    onehot = (e_all[:, None] == jnp.arange(N_EXPERTS, dtype=I32)[None, :]).astype(I32)
    before = jnp.cumsum(onehot, axis=0) - onehot
    rank = jnp.sum(before * onehot, axis=1)
    counts = jnp.sum(onehot, axis=0)
    pad_counts = (counts + MOE_ROWS - 1) // MOE_ROWS * MOE_ROWS
    pad_ends = jnp.cumsum(pad_counts)
    pad_starts = pad_ends - pad_counts
    dest = (pad_starts[e_all] + rank).astype(I32)
    n_used = (pad_ends[-1] // MOE_ROWS).astype(I32)
    block_e = jnp.searchsorted(pad_ends, jnp.arange(nb, dtype=I32) * MOE_ROWS, side="right").astype(I32)
    owns = (counts > 0).astype(I32)
    run_id = jnp.cumsum(owns) - owns
    n_runs = jnp.sum(owns).astype(I32)
    last_e = jnp.max(jnp.where(counts > 0, jnp.arange(N_EXPERTS, dtype=I32), 0)).astype(I32)
    block_e = jnp.where(jnp.arange(nb, dtype=I32) < n_used, jnp.minimum(block_e, N_EXPERTS - 1), last_e)
    run_of_block = run_id[block_e].astype(I32)
    run_e = jnp.full((N_EXPERTS,), last_e, I32).at[jnp.where(counts > 0, run_id, N_EXPERTS)].set(
        jnp.arange(N_EXPERTS, dtype=I32), mode="drop")
    src_tok = jnp.zeros((nb * MOE_ROWS,), I32).at[dest].set(jnp.arange(a_n, dtype=I32) // EXPERT_TOPK)
    return dest, (run_of_block, run_e, n_runs.reshape(1), n_used.reshape(1)), src_tok


def kernel(x_prompt, x_sample, cache_dsa_kv, cache_dsa_kidx, cache_nsa_cmp_kv, cache_nsa_slc_kv, state_nsa_win_kv, page_table, w_in, w_out, cmp_w1, cmp_w2, cmp_pe, ln1_g, ln1_b, w_router_group, b_router_group, w_router_expert, b_router_expert, w_gate, w_up, w_down, ln2_g, ln2_b):
    depth = w_in.shape[0]
    assert depth == 1 and x_sample.shape[1] == 1
    batch, seq, d_model = x_prompt.shape
    n_seq = x_sample.shape[0]
    n_pool, page = cache_dsa_kv.shape[1], cache_dsa_kv.shape[2]
    alpha = (2.0 * depth) ** 0.25
    m_p = batch * seq
    xp = x_prompt.reshape(m_p, d_model)
    xs = x_sample.reshape(n_seq, d_model)
    xp16, xs16 = xp.astype(BF16), xs.astype(BF16)

    w = w_in[0]
    t0 = 4176
    w_head = w[:, :t0 + 48].astype(BF16)
    w_tail = jnp.pad(w[:, t0:], ((0, 0), (0, 80))).astype(BF16)
    c_dq, c_dk, c_dv, c_iq, c_ik = 0, 2048, 2560, 3072, 4096
    c_nq, c_ck, c_cv, c_sk, c_sv, c_wk, c_wv, c_ng = 0, 2048, 2560, 3072, 3584, 4096, 4608, 5120

    def project(x16):
        qa = _matmul([x16], w_head, c_dq, Q_ROW)
        qi = _matmul([x16], w_head, c_iq, IDX_HEADS * IDX_DIM)
        hc = _matmul([x16], w_head, c_ik, LANES)
        qb = _matmul([x16], w_tail, c_nq, Q_ROW)
        gate = _matmul([x16], w_tail, c_ng, LANES)
        kv = [_kv_proj(x16, w_head, c_dk, c_dv), _kv_proj(x16, w_tail, c_ck, c_cv),
              _kv_proj(x16, w_tail, c_sk, c_sv), _kv_proj(x16, w_tail, c_wk, c_wv)]
        return qa, qi, hc, qb, gate, kv

    qa_p, qi_p, hc_p, qb_p, gate_p, kv_p = project(xp16)
    qa_s, qi_s, hc_s, qb_s, gate_s, kv_s = project(xs16)

    w1r = cmp_w1[0].reshape(2, CMP_BLOCK // CMP_STRIDE, CMP_STRIDE, HEAD_DIM, HEAD_DIM)
    w1r = jnp.transpose(w1r, (0, 2, 3, 1, 4)).reshape(2, CMP_STRIDE * HEAD_DIM, 2 * HEAD_DIM)
    peb = _cmp_pe_bias(cmp_pe[0], cmp_w1[0])
    pages_per_seq = seq // page
    ident = jnp.arange(batch * pages_per_seq, dtype=I32).reshape(batch, pages_per_seq)
    ckv_p = _compress(kv_p[1][0], ident, page, w1r, cmp_w2[0], peb)
    ckv_s = _compress(cache_nsa_cmp_kv.reshape(-1, HEAD_DIM), page_table, page, w1r, cmp_w2[0], peb)

    oa_p = _dsa_prompt(qa_p, qi_p, hc_p, kv_p[0][1], batch, seq)
    ob_p = _nsa_prompt(qb_p, gate_p, ckv_p, kv_p[2][1], kv_p[3][1], batch, seq)

    oa_s = _dsa_sample(cache_dsa_kidx.reshape(n_pool, page, IDX_DIM), cache_dsa_kv.reshape(-1, HEAD_DIM), page_table,
                       qa_s.reshape(n_seq, DSA_HEADS, HEAD_DIM), qi_s.reshape(n_seq, IDX_HEADS, IDX_DIM),
                       hc_s[:, IDX_DIM:IDX_DIM + IDX_HEADS].reshape(n_seq, IDX_HEADS, 1), hc_s,
                       kv_s[0][1].astype(F32))
    ob_s, win_s = _nsa_sample(cache_nsa_slc_kv.reshape(-1, HEAD_DIM), page_table, page,
                              qb_s.reshape(n_seq, NSA_HEADS, HEAD_DIM),
                              gate_s[:, :3 * NSA_HEADS].reshape(n_seq, NSA_HEADS, 3), ckv_s,
                              state_nsa_win_kv.reshape(-1, HEAD_DIM), kv_s[2][1].astype(F32),
                              kv_s[3][1].astype(F32), kv_s[3][0])

    w_o = w_out[0].astype(BF16)
    wr = jnp.concatenate([w_router_group[0], w_router_expert[0],
                          jnp.zeros((d_model, LANES - N_GROUPS - N_EXPERTS), F32)], axis=1)
    br = jnp.concatenate([b_router_group[0], b_router_expert[0],
                          jnp.zeros((LANES - N_GROUPS - N_EXPERTS,), F32)]).reshape(1, LANES)
    g1, b1 = ln1_g[0].reshape(1, d_model), ln1_b[0].reshape(1, d_model)
    g2, b2 = ln2_g[0].reshape(1, d_model), ln2_b[0].reshape(1, d_model)
    x1_p, e_p, gt_p = _ln_router(xp, _matmul([oa_p, ob_p], w_o), g1, b1, wr, br, alpha)
    x1_s, e_s, gt_s = _ln_router(xs, _matmul([oa_s.reshape(n_seq, Q_ROW), ob_s.reshape(n_seq, Q_ROW)], w_o),
                                 g1, b1, wr, br, alpha)

    e_all = jnp.concatenate([e_p[:, :EXPERT_TOPK], e_s[:, :EXPERT_TOPK]], axis=0).reshape(-1)
    a_n = e_all.shape[0]
    nb = -(-(a_n + N_EXPERTS * (MOE_ROWS - 1)) // MOE_ROWS)
    dest, plan, src_tok = _route_plan(e_all, nb)
    xbuf = _dispatch(src_tok, plan[3], x1_p, x1_s, nb)
    d_expert = w_gate.shape[3]
    hid = _expert_mm(plan, xbuf, [w_gate[0], w_up[0]], _pick(d_expert, (MOE_CHUNK, LANES)), _swiglu, BF16,
                     "moe_gate_up")
    ybuf = _expert_mm(plan, hid, [w_down[0]], _pick(d_model, (1024, 512, 256, LANES)), lambda y: y, F32, "moe_down")
    y_p = _combine(dest[:m_p * EXPERT_TOPK], ybuf, x1_p, gt_p, g2, b2, alpha)
    y_s = _combine(dest[m_p * EXPERT_TOPK:], ybuf, x1_s, gt_s, g2, b2, alpha)

    def state(slab, lead):
        return slab.reshape((1,) + lead + (NSA_KV_GROUPS, 2, HEAD_DIM))

    n_win = min(WINDOW, seq)
    win_p = state(kv_p[3][0], (batch, seq))[:, :, seq - n_win:]
    return (y_p.reshape(batch, seq, d_model), y_s.reshape(n_seq, 1, d_model),
            state(kv_p[0][0], (batch, seq)), state(kv_s[0][0], (n_seq, 1)),
            hc_p[:, :IDX_DIM].reshape(1, batch, seq, IDX_DIM), hc_s[:, :IDX_DIM].reshape(1, n_seq, 1, IDX_DIM),
            state(kv_p[1][0], (batch, seq)), state(kv_s[1][0], (n_seq, 1)),
            state(kv_p[2][0], (batch, seq)), state(kv_s[2][0], (n_seq, 1)),
            win_p, state(win_s, (n_seq, state_nsa_win_kv.shape[2])))
```

```python
import functools

import jax
import jax.numpy as jnp
from jax import lax
from jax.experimental import pallas as pl
from jax.experimental.pallas import tpu as pltpu

F32 = jnp.float32
BF16 = jnp.bfloat16
I32 = jnp.int32

HEAD_DIM = 128
DSA_HEADS = 16
DSA_KV_HEADS = 4
IDX_HEADS = 16
IDX_DIM = 64
IDX_TOPK = 256
NSA_HEADS = 16
NSA_KV_GROUPS = 4
CMP_STRIDE = 16
CMP_BLOCK = 32
SLC_BLOCK = 64
SLC_TOPN = 16
WINDOW = 512
N_GROUPS = 8
EXPERTS_PER_GROUP = 8
N_EXPERTS = N_GROUPS * EXPERTS_PER_GROUP
EXPERT_TOPK = 2
LN_EPS = 1e-5
NEG_INF = -1e30
FORCE_SCORE = 1e9

KV_ROW = 2 * DSA_KV_HEADS * HEAD_DIM
KV_SLABS = 2 * DSA_KV_HEADS
KV_HALF = DSA_KV_HEADS * HEAD_DIM
Q_ROW = DSA_HEADS * HEAD_DIM
Q_TILE = 128
KEY_SPAN = 512
SEQ_GROUP = 8
LANES = 128
MOE_ROWS = 256
MOE_RING = 4
MOE_CHUNK = 256
VMEM_LIMIT = 56 * 1024 * 1024
SCALE = HEAD_DIM ** -0.5
LOG2E = 1.4426950408889634
MASKED_DIST = 1e32
INT_MIN = -(2 ** 31)


def _slopes(n):
    return [2.0 ** (-8.0 * i / n) for i in range(1, n + 1)]


def _cparams(sem):
    return pltpu.CompilerParams(dimension_semantics=sem, vmem_limit_bytes=VMEM_LIMIT)


def _dot(a, b):
    return jnp.dot(a, b, preferred_element_type=F32)


def _dot_nt(a, b):
    return lax.dot_general(a, b, (((1,), (1,)), ((), ())), preferred_element_type=F32)


def _dot_hi(a, b):
    return jnp.dot(a, b, preferred_element_type=F32, precision=lax.Precision.HIGHEST)


def _pick(n, cands):
    for c in cands:
        if n % c == 0:
            return c
    return n


def _mm_kernel(*refs, n_lhs):
    x_refs, w_refs, o_ref = refs[:n_lhs], refs[n_lhs:2 * n_lhs], refs[2 * n_lhs]
    acc = _dot(x_refs[0][...], w_refs[0][...])
    for x_ref, w_ref in zip(x_refs[1:], w_refs[1:]):
        acc = acc + _dot(x_ref[...], w_ref[...])
    o_ref[...] = acc.astype(o_ref.dtype)


def _matmul(xs, w, n0=0, n=None):
    m = xs[0].shape[0]
    n = w.shape[1] if n is None else n
    kds = [x.shape[1] for x in xs]
    assert len(set(kds)) == 1 and sum(kds) == w.shape[0]
    kd = kds[0]
    tm = _pick(m, (512, 256, 128))
    tn = _pick(n, (1024, 512, 256, 128))
    assert n0 % tn == 0
    j0 = n0 // tn
    x_specs = [pl.BlockSpec((tm, kd), lambda j, i: (i, 0)) for _ in xs]
    w_specs = [pl.BlockSpec((kd, tn), functools.partial(lambda j, i, a: (a, j + j0), a=a)) for a in range(len(xs))]
    return pl.pallas_call(
        functools.partial(_mm_kernel, n_lhs=len(xs)),
        grid=(n // tn, m // tm),
        in_specs=x_specs + w_specs,
        out_specs=pl.BlockSpec((tm, tn), lambda j, i: (i, j)),
        out_shape=jax.ShapeDtypeStruct((m, n), F32),
        compiler_params=_cparams(("parallel", "parallel")),
        name="matmul",
    )(*xs, *([w] * len(xs)))


def _kv_proj_kernel(x_ref, wk_ref, wv_ref, slab_ref, tile_ref):
    x = x_ref[...]
    rows = x.shape[0]
    for c, w_ref in enumerate((wk_ref, wv_ref)):
        res = _dot(x, w_ref[...])
        for g in range(DSA_KV_HEADS):
            piece = res[:, g * HEAD_DIM:(g + 1) * HEAD_DIM]
            slab_ref[pl.ds(g * 2 + c, rows, stride=KV_SLABS), :] = piece
            tile_ref[:, (g * 2 + c) * HEAD_DIM:(g * 2 + c + 1) * HEAD_DIM] = piece.astype(BF16)


def _kv_proj(x, w, k_col, v_col):
    m, kd = x.shape
    tm = _pick(m, (512, 256, 128))
    assert k_col % KV_HALF == 0 and v_col % KV_HALF == 0
    kb, vb = k_col // KV_HALF, v_col // KV_HALF
    return pl.pallas_call(
        _kv_proj_kernel,
        grid=(m // tm,),
        in_specs=[pl.BlockSpec((tm, kd), lambda i: (i, 0)),
                  pl.BlockSpec((kd, KV_HALF), lambda i: (0, kb)),
                  pl.BlockSpec((kd, KV_HALF), lambda i: (0, vb))],
        out_specs=[pl.BlockSpec((tm * KV_SLABS, HEAD_DIM), lambda i: (i, 0)),
                   pl.BlockSpec((tm, KV_ROW), lambda i: (i, 0))],
        out_shape=[jax.ShapeDtypeStruct((m * KV_SLABS, HEAD_DIM), F32), jax.ShapeDtypeStruct((m, KV_ROW), BF16)],
        compiler_params=_cparams(("parallel",)),
        name="kv_proj",
    )(x, w, w)


def _slab(ref, n_tok, g, c):
    return ref[pl.ds(g * 2 + c, n_tok, stride=KV_SLABS), :]


def _page_specs(n_pages, rows, width, index):
    return [pl.BlockSpec((rows, width), functools.partial(lambda *a, p: (index(*a, p), 0), p=p))
            for p in range(n_pages)]


def _sort_key(x):
    b = pltpu.bitcast(x, I32)
    return jnp.where(b < 0, b ^ jnp.int32(0x7FFFFFFF), b)


def _topk_mask(sc, idx, k, idx_bits):
    key = _sort_key(sc)

    def count(m):
        return jnp.sum(m.astype(I32), axis=1, keepdims=True)

    t0 = jnp.where(count(key >= 0) >= k, jnp.int32(0), jnp.int32(INT_MIN))

    def vstep(i, t):
        cand = t | (jnp.int32(1) << (30 - i))
        return jnp.where(count(key >= cand) >= k, cand, t)

    thr = lax.fori_loop(0, 31, vstep, t0)
    gt = key > thr
    eq = key == thr
    need = k - count(gt)
    crowded = count(eq) > need
    n_steps = jnp.where(jnp.max(crowded.astype(I32)) > 0, idx_bits, 0)

    def istep(i, c):
        cand = c | (jnp.int32(1) << (idx_bits - 1 - i))
        return jnp.where(count(eq & (idx < cand)) < need, cand, c)

    cut = lax.fori_loop(0, n_steps, istep, jnp.zeros_like(thr))
    return gt | (eq & (jnp.logical_not(crowded) | (idx <= cut)))


def _attend(q2, k, v, dm, slope2):
    s = _dot_nt(q2, k) - slope2 * dm
    m = jnp.max(s, axis=1, keepdims=True)
    p = jnp.exp2(s - m)
    l = jnp.sum(p, axis=1, keepdims=True)
    return _dot(p.astype(BF16), v) * (1.0 / l)


def _spans(seq):
    span = KEY_SPAN if seq % KEY_SPAN == 0 else seq
    return span, seq // span


def _group_slope(g, r, n_heads, n_groups):
    rep = n_heads // n_groups
    table = _slopes(n_heads)
    out = jnp.float32(table[r])
    for gg in range(1, n_groups):
        out = jnp.where(g == gg, jnp.float32(table[gg * rep + r]), out)
    return out


def _dsa_prompt_mask(c, qi_ref, cq_ref, ck_ref, dm_ref, kw, k_keep):
    t = c * Q_TILE + lax.broadcasted_iota(I32, (Q_TILE, 1), 0)
    kpos = lax.broadcasted_iota(I32, (Q_TILE, kw), 1)
    causal = kpos <= t
    ki = ck_ref[0:kw, 0:IDX_DIM].astype(BF16)
    wi = cq_ref[:, IDX_DIM:IDX_DIM + IDX_HEADS] * (IDX_HEADS ** -0.5)
    sc = jnp.zeros((Q_TILE, kw), F32)
    for h in range(IDX_HEADS):
        a = _dot_nt(qi_ref[:, h * IDX_DIM:(h + 1) * IDX_DIM].astype(BF16), ki)
        sc = sc + jnp.maximum(a, 0.0) * wi[:, h:h + 1]
    sc = jnp.where(causal, sc, NEG_INF)
    keep = _topk_mask(sc, kpos, k_keep, max(1, (kw - 1).bit_length()))
    dm_ref[:, 0:kw] = jnp.where(keep & causal, (t - kpos).astype(F32), MASKED_DIST)


def _dsa_prompt_kernel(q_ref, qi_ref, cq_ref, ck_ref, kv_ref, o_ref, dm_ref, *, seq, k_keep):
    c, g = pl.program_id(1), pl.program_id(2)
    rep = DSA_HEADS // DSA_KV_HEADS
    span, n_var = _spans(seq)
    for v in range(n_var):
        @pl.when((c * Q_TILE) // span == v)
        def _(kw=(v + 1) * span):
            @pl.when(g == 0)
            def _():
                _dsa_prompt_mask(c, qi_ref, cq_ref, ck_ref, dm_ref, kw, k_keep)

            dm = dm_ref[:, 0:kw]
            kg = kv_ref[0:kw, 0:HEAD_DIM]
            vg = kv_ref[0:kw, HEAD_DIM:2 * HEAD_DIM]
            for r in range(rep):
                q2 = (q_ref[:, r * HEAD_DIM:(r + 1) * HEAD_DIM] * (SCALE * LOG2E)).astype(BF16)
                slope2 = _group_slope(g, r, DSA_HEADS, DSA_KV_HEADS) * LOG2E
                o_ref[:, r * HEAD_DIM:(r + 1) * HEAD_DIM] = _attend(q2, kg, vg, dm, slope2).astype(BF16)


def _dsa_prompt(q, qi, h_c, kv, batch, seq):
    nc = seq // Q_TILE
    k_keep = min(IDX_TOPK, seq // 4)
    rep = DSA_HEADS // DSA_KV_HEADS
    tile = lambda b, c, g: (b * nc + c, 0)
    tile_g = lambda b, c, g: (b * nc + c, g)
    return pl.pallas_call(
        functools.partial(_dsa_prompt_kernel, seq=seq, k_keep=k_keep),
        grid=(batch, nc, DSA_KV_HEADS),
        in_specs=[pl.BlockSpec((Q_TILE, rep * HEAD_DIM), tile_g),
                  pl.BlockSpec((Q_TILE, IDX_HEADS * IDX_DIM), tile),
                  pl.BlockSpec((Q_TILE, LANES), tile),
                  pl.BlockSpec((seq, LANES), lambda b, c, g: (b, 0)),
                  pl.BlockSpec((seq, 2 * HEAD_DIM), lambda b, c, g: (b, g))],
        out_specs=pl.BlockSpec((Q_TILE, rep * HEAD_DIM), tile_g),
        out_shape=jax.ShapeDtypeStruct((batch * seq, Q_ROW), BF16),
        scratch_shapes=[pltpu.VMEM((Q_TILE, seq), F32)],
        compiler_params=_cparams(("parallel", "arbitrary", "arbitrary")),
        name="dsa_prompt",
    )(q, qi, h_c, h_c, kv)


def _peb_kernel(pe_ref, w1_ref, o_ref):
    for c in range(2):
        o_ref[c] = _dot_hi(pe_ref[c], w1_ref[c])


def _cmp_pe_bias(cmp_pe, cmp_w1):
    return pl.pallas_call(
        _peb_kernel,
        out_shape=jax.ShapeDtypeStruct((2, 1, HEAD_DIM), F32),
        name="cmp_pe_bias",
    )(cmp_pe.reshape(2, 1, CMP_BLOCK * HEAD_DIM), cmp_w1.reshape(2, CMP_BLOCK * HEAD_DIM, HEAD_DIM))


def _compress_kernel(pt_ref, *refs, n_pages, page):
    pages = refs[:n_pages]
    w1_ref, w2_ref, peb_ref, o_ref = refs[n_pages:]
    sub_per_page = page // CMP_STRIDE
    n_sub = n_pages * sub_per_page
    for c in range(2):
        lhs = jnp.concatenate([
            jnp.concatenate([
                jnp.concatenate([p[pl.ds(j * KV_SLABS + g * 2 + c, sub_per_page, stride=CMP_STRIDE * KV_SLABS), :]
                                 for p in pages], axis=0).astype(BF16)
                for j in range(CMP_STRIDE)], axis=1)
            for g in range(NSA_KV_GROUPS)], axis=0)
        part = _dot(lhs, w1_ref[c].astype(BF16))
        w2 = w2_ref[c].astype(BF16)
        for g in range(NSA_KV_GROUPS):
            col = g * 256 + c * 128
            pg = part[g * n_sub:(g + 1) * n_sub]
            nxt = pltpu.roll(pg[:, HEAD_DIM:], n_sub - 1, 0)
            hid = pg[:, :HEAD_DIM] + nxt + peb_ref[c]
            o_ref[0, :, col:col + HEAD_DIM] = _dot(jax.nn.gelu(hid).astype(BF16), w2)


def _compress(pool, page_table, page, w1r, w2, peb):
    n_seq, n_pages = page_table.shape
    n_sub = n_pages * page // CMP_STRIDE
    page_specs = _page_specs(n_pages, page * KV_SLABS, HEAD_DIM, lambda n, pt, p: pt[n, p])
    return pl.pallas_call(
        functools.partial(_compress_kernel, n_pages=n_pages, page=page),
        grid_spec=pltpu.PrefetchScalarGridSpec(
            num_scalar_prefetch=1,
            grid=(n_seq,),
            in_specs=page_specs + [
                pl.BlockSpec((2, CMP_STRIDE * HEAD_DIM, 2 * HEAD_DIM), lambda n, pt: (0, 0, 0)),
                pl.BlockSpec((2, HEAD_DIM, HEAD_DIM), lambda n, pt: (0, 0, 0)),
                pl.BlockSpec((2, 1, HEAD_DIM), lambda n, pt: (0, 0, 0))],
            out_specs=pl.BlockSpec((1, n_sub, KV_ROW), lambda n, pt: (n, 0, 0))),
        out_shape=jax.ShapeDtypeStruct((n_seq, n_sub, KV_ROW), F32),
        compiler_params=_cparams(("arbitrary",)),
        name="nsa_compress",
    )(page_table, *([pool] * n_pages), w1r, w2, peb)


def _cover(n_cmp_pad, n_slc_pad):
    cs = lax.broadcasted_iota(I32, (n_cmp_pad, n_slc_pad), 0) * CMP_STRIDE
    bs = lax.broadcasted_iota(I32, (n_cmp_pad, n_slc_pad), 1) * SLC_BLOCK
    return ((cs < bs + SLC_BLOCK) & (cs + CMP_BLOCK > bs)).astype(F32)


def _select_blocks(score, t, n_slc, n_keep):
    rows, width = score.shape
    j = lax.broadcasted_iota(I32, (rows, width), 1)
    cur = t // SLC_BLOCK
    forced = (j == 0) | (j == cur) | (j == cur - 1)
    admissible = j * SLC_BLOCK <= t
    score = jnp.where(forced, FORCE_SCORE, jnp.where(admissible, score, NEG_INF))
    rank = jnp.zeros((rows, width), I32)
    for k in range(n_slc):
        sk = score[:, k:k + 1]
        ahead = (sk > score) | ((sk == score) & (j > k))
        rank = rank + ahead.astype(I32)
    return ((rank < n_keep) & (j < n_slc)).astype(F32)


def _select_blocks_t(score_t, t_row, n_keep):
    n_slc, width = score_t.shape
    j = lax.broadcasted_iota(I32, (n_slc, width), 0)
    cur = t_row // SLC_BLOCK
    forced = (j == 0) | (j == cur) | (j == cur - 1)
    admissible = j * SLC_BLOCK <= t_row
    score_t = jnp.where(forced, FORCE_SCORE, jnp.where(admissible, score_t, NEG_INF))
    rank = jnp.zeros((n_slc, width), I32)
    for k in range(n_slc):
        sk = score_t[k:k + 1, :]
        ahead = (sk > score_t) | ((sk == score_t) & (j > k))
        rank = rank + ahead.astype(I32)
    return (rank < n_keep).astype(F32)


def _nsa_prompt_body(c, g, q_ref, gate_ref, ckv_ref, slc_ref, win_ref, o_ref, kw, seq, n_cmp, win_keys):
    t = c * Q_TILE + lax.broadcasted_iota(I32, (Q_TILE, 1), 0)
    t_row = c * Q_TILE + lax.broadcasted_iota(I32, (1, Q_TILE), 1)
    rep = NSA_HEADS // NSA_KV_GROUPS
    slope2 = [_group_slope(g, r, NSA_HEADS, NSA_KV_GROUPS) * LOG2E for r in range(rep)]
    n_cmp_pad = ckv_ref.shape[1]
    n_slc = seq // SLC_BLOCK
    n_keep = min(SLC_TOPN, n_slc)

    kc = lax.broadcasted_iota(I32, (Q_TILE, n_cmp_pad), 1)
    dist_c_i = t - (kc * CMP_STRIDE + (CMP_BLOCK - 1))
    valid_c = (dist_c_i >= 0) & (kc < n_cmp)
    dist_c = dist_c_i.astype(F32)
    cover_bs = lax.broadcasted_iota(I32, (n_slc, n_cmp_pad), 0) * SLC_BLOCK
    cover_cs = lax.broadcasted_iota(I32, (n_slc, n_cmp_pad), 1) * CMP_STRIDE
    cover_t = ((cover_cs < cover_bs + SLC_BLOCK) & (cover_cs + CMP_BLOCK > cover_bs)).astype(F32)
    slc_rows = -(-n_slc // LANES) * LANES
    expand = (lax.broadcasted_iota(I32, (slc_rows, kw), 1) // SLC_BLOCK
              == lax.broadcasted_iota(I32, (slc_rows, kw), 0)).astype(BF16)

    kpos = lax.broadcasted_iota(I32, (Q_TILE, kw), 1)
    causal = kpos <= t
    dist_s = (t - kpos).astype(F32)

    w0 = pl.multiple_of(jnp.maximum(c * Q_TILE + Q_TILE - win_keys, 0), Q_TILE)
    wpos = w0 + lax.broadcasted_iota(I32, (Q_TILE, win_keys), 1)
    dist_w_i = t - wpos
    dm_w = jnp.where((dist_w_i >= 0) & (dist_w_i <= WINDOW), dist_w_i.astype(F32), MASKED_DIST)

    gates = jax.nn.sigmoid(gate_ref[:, 0:3 * rep])

    ck = ckv_ref[0, :, 0:HEAD_DIM].astype(BF16)
    cv = ckv_ref[0, :, HEAD_DIM:2 * HEAD_DIM].astype(BF16)
    q2s, o_cs = [], []
    imp = jnp.zeros((Q_TILE, n_cmp_pad), F32)
    for r in range(rep):
        q2 = (q_ref[:, r * HEAD_DIM:(r + 1) * HEAD_DIM] * (SCALE * LOG2E)).astype(BF16)
        q2s.append(q2)
        s = jnp.where(valid_c, _dot_nt(q2, ck) - slope2[r] * dist_c, NEG_INF)
        m = jnp.max(s, axis=1, keepdims=True)
        p = jnp.where(valid_c, jnp.exp2(s - m), 0.0)
        l = jnp.sum(p, axis=1, keepdims=True)
        p = p * (1.0 / jnp.where(l > 0.0, l, 1.0))
        imp = imp + p
        o_cs.append(_dot(p.astype(BF16), cv))
    sel_t = _select_blocks_t(_dot_hi(cover_t, imp.T), t_row, n_keep)
    sel_t = jnp.concatenate([sel_t, jnp.zeros((slc_rows - n_slc, Q_TILE), F32)], axis=0)
    picked = _dot(sel_t.T.astype(BF16), expand) > 0.5
    dm_s = jnp.where(picked & causal, dist_s, MASKED_DIST)
    sk = slc_ref[0:kw, 0:HEAD_DIM]
    sv = slc_ref[0:kw, HEAD_DIM:2 * HEAD_DIM]
    wk = win_ref[pl.ds(w0, win_keys), 0:HEAD_DIM]
    wv = win_ref[pl.ds(w0, win_keys), HEAD_DIM:2 * HEAD_DIM]
    for r in range(rep):
        o_s = _attend(q2s[r], sk, sv, dm_s, slope2[r])
        o_w = _attend(q2s[r], wk, wv, dm_w, slope2[r])
        o_ref[:, r * HEAD_DIM:(r + 1) * HEAD_DIM] = (
            gates[:, 3 * r:3 * r + 1] * o_cs[r] + gates[:, 3 * r + 1:3 * r + 2] * o_s
            + gates[:, 3 * r + 2:3 * r + 3] * o_w).astype(BF16)


def _nsa_prompt_kernel(q_ref, gate_ref, ckv_ref, slc_ref, win_ref, o_ref, *, seq, n_cmp, win_keys):
    c, g = pl.program_id(1), pl.program_id(2)
    span, n_var = _spans(seq)
    for v in range(n_var):
        @pl.when((c * Q_TILE) // span == v)
        def _(kw=(v + 1) * span):
            _nsa_prompt_body(c, g, q_ref, gate_ref, ckv_ref, slc_ref, win_ref, o_ref, kw, seq, n_cmp, win_keys)


def _nsa_prompt(q, gate, ckv, slc, win, batch, seq):
    nc = seq // Q_TILE
    n_cmp_pad = ckv.shape[1]
    n_cmp = seq // CMP_STRIDE - CMP_BLOCK // CMP_STRIDE + 1
    win_keys = min(seq, WINDOW + Q_TILE)
    rep = NSA_HEADS // NSA_KV_GROUPS
    tile_g = lambda b, c, g: (b * nc + c, g)
    return pl.pallas_call(
        functools.partial(_nsa_prompt_kernel, seq=seq, n_cmp=n_cmp, win_keys=win_keys),
        grid=(batch, nc, NSA_KV_GROUPS),
        in_specs=[pl.BlockSpec((Q_TILE, rep * HEAD_DIM), tile_g),
                  pl.BlockSpec((Q_TILE, LANES), tile_g),
                  pl.BlockSpec((1, n_cmp_pad, 2 * HEAD_DIM), lambda b, c, g: (b, 0, g)),
                  pl.BlockSpec((seq, 2 * HEAD_DIM), lambda b, c, g: (b, g)),
                  pl.BlockSpec((seq, 2 * HEAD_DIM), lambda b, c, g: (b, g))],
        out_specs=pl.BlockSpec((Q_TILE, rep * HEAD_DIM), tile_g),
        out_shape=jax.ShapeDtypeStruct((batch * seq, Q_ROW), BF16),
        compiler_params=_cparams(("parallel", "arbitrary", "arbitrary")),
        name="nsa_prompt",
    )(q, gate, ckv, slc, win)


def _group_rows(per_group, rep):
    row = lax.broadcasted_iota(I32, per_group[0].shape, 0) // rep
    out = per_group[0]
    for g in range(1, len(per_group)):
        out = jnp.where(row == g, per_group[g], out)
    return out


def _dsa_sample_kernel(pt_ref, *refs, n_pages, page, k_keep):
    n_ki = SEQ_GROUP * n_pages
    ki_pages = refs[:n_ki]
    kv_pages = refs[n_ki:n_ki + n_pages]
    q_ref, qi_ref, wi_ref, cnew_ref, kvnew_ref, o_ref, keep_ref = refs[n_ki + n_pages:]
    r = pl.program_id(1)
    past = n_pages * page
    width = past + LANES
    rep = DSA_HEADS // DSA_KV_HEADS
    slopes = _slopes(DSA_HEADS)

    @pl.when(r == 0)
    def _():
        rows = []
        for s in range(SEQ_GROUP):
            qi = qi_ref[s].astype(BF16)
            wi = wi_ref[s] * (IDX_HEADS ** -0.5)
            knew = cnew_ref[pl.ds(s, 1), 0:IDX_DIM]
            a_past = jnp.concatenate([_dot_nt(qi, ki_pages[s * n_pages + p][0].astype(BF16))
                                      for p in range(n_pages)], axis=1)
            a_new = _dot_nt(qi, jnp.broadcast_to(knew, (8, IDX_DIM)).astype(BF16))[:, 0:1]
            sc_past = jnp.sum(jnp.maximum(a_past, 0.0) * wi, axis=0, keepdims=True)
            sc_new = jnp.sum(jnp.maximum(a_new, 0.0) * wi, axis=0, keepdims=True)
            rows.append(jnp.concatenate([sc_past, jnp.broadcast_to(sc_new, (1, LANES))], axis=1))
        col = lax.broadcasted_iota(I32, (SEQ_GROUP, width), 1)
        sc = jnp.where(col <= past, jnp.concatenate(rows, axis=0), NEG_INF)
        keep = _topk_mask(sc, col, k_keep, max(1, (width - 1).bit_length())) & (col <= past)
        keep_ref[...] = keep.astype(F32)

    keep = keep_ref[pl.ds(r, 1), :] > 0.5
    keep_past = jnp.broadcast_to(keep[:, 0:past], (DSA_HEADS, past))
    keep_new = jnp.broadcast_to(keep[:, past:past + 1], (DSA_HEADS, 1))

    q = (q_ref[0] * SCALE).astype(BF16)
    q32 = q.astype(F32)
    kvnew = kvnew_ref[pl.ds(r, 1), :]
    slope_col = jnp.concatenate([jnp.full((1, 1), s, F32) for s in slopes], axis=0)
    dist = (past - lax.broadcasted_iota(I32, (DSA_HEADS, past), 1)).astype(F32)
    s_g, s_new_g = [], []
    for g in range(DSA_KV_HEADS):
        s_g.append(jnp.concatenate(
            [_dot_nt(q, _slab(p, page, g, 0).astype(BF16)) for p in kv_pages], axis=1))
        s_new_g.append(jnp.sum(q32 * kvnew[:, g * 256:g * 256 + 128], axis=1, keepdims=True))
    s = jnp.where(keep_past, _group_rows(s_g, rep) - slope_col * dist, NEG_INF)
    s_new = jnp.where(keep_new, _group_rows(s_new_g, rep), NEG_INF)
    m = jnp.maximum(jnp.max(s, axis=1, keepdims=True), s_new)
    p = jnp.where(keep_past, jnp.exp(s - m), 0.0)
    p_new = jnp.where(keep_new, jnp.exp(s_new - m), 0.0)
    l = jnp.sum(p, axis=1, keepdims=True) + p_new
    inv = 1.0 / jnp.where(l > 0.0, l, 1.0)
    pb = p.astype(BF16)
    p_new = p_new.astype(BF16).astype(F32)
    o_g = []
    for g in range(DSA_KV_HEADS):
        acc = p_new * kvnew[:, g * 256 + 128:g * 256 + 256]
        for i, pg in enumerate(kv_pages):
            acc = acc + _dot(pb[:, i * page:(i + 1) * page], _slab(pg, page, g, 1).astype(BF16))
        o_g.append(acc)
    o_ref[0] = (_group_rows(o_g, rep) * inv).astype(BF16)


def _dsa_sample(kidx_pool, kv_pool, page_table, q, qi, wi, h_c, kv_new):
    n_seq, n_pages = page_table.shape
    assert n_seq % SEQ_GROUP == 0
    page = kidx_pool.shape[1]
    total = n_pages * page + 1
    k_keep = min(IDX_TOPK, total // 4)
    width = n_pages * page + LANES

    ki_specs = [pl.BlockSpec((1, page, IDX_DIM),
                             functools.partial(lambda i, r, pt, s, p: (pt[i * SEQ_GROUP + s, p], 0, 0), s=s, p=p))
                for s in range(SEQ_GROUP) for p in range(n_pages)]
    kv_specs = _page_specs(n_pages, page * KV_SLABS, HEAD_DIM, lambda i, r, pt, p: pt[i * SEQ_GROUP + r, p])
    one = lambda i, r, pt: (i * SEQ_GROUP + r, 0, 0)
    grp3 = lambda i, r, pt: (i, 0, 0)
    grp2 = lambda i, r, pt: (i, 0)
    return pl.pallas_call(
        functools.partial(_dsa_sample_kernel, n_pages=n_pages, page=page, k_keep=k_keep),
        grid_spec=pltpu.PrefetchScalarGridSpec(
            num_scalar_prefetch=1,
            grid=(n_seq // SEQ_GROUP, SEQ_GROUP),
            in_specs=ki_specs + kv_specs + [
                pl.BlockSpec((1, DSA_HEADS, HEAD_DIM), one),
                pl.BlockSpec((SEQ_GROUP, IDX_HEADS, IDX_DIM), grp3),
                pl.BlockSpec((SEQ_GROUP, IDX_HEADS, 1), grp3),
                pl.BlockSpec((SEQ_GROUP, LANES), grp2),
                pl.BlockSpec((SEQ_GROUP, KV_ROW), grp2)],
            out_specs=pl.BlockSpec((1, DSA_HEADS, HEAD_DIM), one),
            scratch_shapes=[pltpu.VMEM((SEQ_GROUP, width), F32)]),
        out_shape=jax.ShapeDtypeStruct((n_seq, DSA_HEADS, HEAD_DIM), BF16),
        compiler_params=_cparams(("arbitrary", "arbitrary")),
        name="dsa_sample",
    )(page_table, *([kidx_pool] * (SEQ_GROUP * n_pages)), *([kv_pool] * n_pages), q, qi, wi, h_c, kv_new)


def _nsa_sample_kernel(pt_ref, *refs, n_pages, page, n_cmp):
    slc_pages = refs[:n_pages]
    q_ref, gate_ref, ckv_ref, win_ref, slcnew_ref, winnew_ref, winslab_ref, o_ref, wout_ref = refs[n_pages:]
    n = pl.program_id(0)
    past = n_pages * page
    heads = NSA_HEADS
    rep = NSA_HEADS // NSA_KV_GROUPS
    slopes = _slopes(NSA_HEADS)
    slope_col = jnp.concatenate([jnp.full((1, 1), s, F32) for s in slopes], axis=0)
    q = (q_ref[0] * SCALE).astype(BF16)
    q32 = q.astype(F32)
    gates = jax.nn.sigmoid(gate_ref[0])
    n_cmp_pad = ckv_ref.shape[1]
    total = past + 1
    n_slc = -(-total // SLC_BLOCK)
    n_keep = min(SLC_TOPN, n_slc)
    slc_pad = -(-n_slc // LANES) * LANES

    kc = lax.broadcasted_iota(I32, (heads, n_cmp_pad), 1)
    dist_c_i = past - (kc * CMP_STRIDE + (CMP_BLOCK - 1))
    valid_c = (dist_c_i >= 0) & (kc < n_cmp)
    s_g = [_dot_nt(q, ckv_ref[0, :, g * 256:g * 256 + 128].astype(BF16)) for g in range(NSA_KV_GROUPS)]
    s = jnp.where(valid_c, _group_rows(s_g, rep) - slope_col * dist_c_i.astype(F32), NEG_INF)
    m = jnp.max(s, axis=1, keepdims=True)
    p = jnp.where(valid_c, jnp.exp(s - m), 0.0)
    l = jnp.sum(p, axis=1, keepdims=True)
    p = p * (1.0 / jnp.where(l > 0.0, l, 1.0))
    pb = p.astype(BF16)
    o_c = _group_rows([_dot(pb, ckv_ref[0, :, g * 256 + 128:g * 256 + 256].astype(BF16))
                       for g in range(NSA_KV_GROUPS)], rep)
    same_group = (lax.broadcasted_iota(I32, (heads, heads), 0) // rep
                  == lax.broadcasted_iota(I32, (heads, heads), 1) // rep).astype(F32)
    imp = _dot_hi(same_group, p)
    score = _dot_hi(imp, _cover(n_cmp_pad, slc_pad))
    sel = _select_blocks(score, jnp.full((heads, 1), past, I32), n_slc, n_keep)
    expand = (lax.broadcasted_iota(I32, (slc_pad, past), 1) // SLC_BLOCK
              == lax.broadcasted_iota(I32, (slc_pad, past), 0)).astype(BF16)
    valid_s = _dot(sel.astype(BF16), expand) > 0.5
    new_blk = past // SLC_BLOCK
    valid_s_new = sel[:, new_blk:new_blk + 1] > 0.5

    def attend_with_new(s_past, valid_past, s_new, valid_new, dist_past, v_of, v_new_of):
        s_p = jnp.where(valid_past, s_past - slope_col * dist_past, NEG_INF)
        s_n = jnp.where(valid_new, s_new, NEG_INF)
        mm = jnp.maximum(jnp.max(s_p, axis=1, keepdims=True), s_n)
        pp = jnp.where(valid_past, jnp.exp(s_p - mm), 0.0)
        pn = jnp.where(valid_new, jnp.exp(s_n - mm), 0.0)
        ll = jnp.sum(pp, axis=1, keepdims=True) + pn
        inv = 1.0 / jnp.where(ll > 0.0, ll, 1.0)
        ppb = pp.astype(BF16)
        pn = pn.astype(BF16).astype(F32)
        outs = [v_of(ppb, g) + pn * v_new_of(g) for g in range(NSA_KV_GROUPS)]
        return _group_rows(outs, rep) * inv

    slcnew = slcnew_ref[pl.ds(n, 1), :]
    s_g, s_new_g = [], []
    for g in range(NSA_KV_GROUPS):
        s_g.append(jnp.concatenate(
            [_dot_nt(q, _slab(pg, page, g, 0).astype(BF16)) for pg in slc_pages], axis=1))
        s_new_g.append(jnp.sum(q32 * slcnew[:, g * 256:g * 256 + 128], axis=1, keepdims=True))
    dist_s = (past - lax.broadcasted_iota(I32, (heads, past), 1)).astype(F32)

    def slc_v(ppb, g):
        acc = jnp.zeros((heads, HEAD_DIM), F32)
        for i, pg in enumerate(slc_pages):
            acc = acc + _dot(ppb[:, i * page:(i + 1) * page], _slab(pg, page, g, 1).astype(BF16))
        return acc

    o_s = attend_with_new(_group_rows(s_g, rep), valid_s, _group_rows(s_new_g, rep), valid_s_new, dist_s, slc_v,
                          lambda g: slcnew[:, g * 256 + 128:g * 256 + 256])

    n_buf = win_ref.shape[0] // KV_SLABS
    winnew = winnew_ref[pl.ds(n, 1), :]
    dist_w_i = n_buf - lax.broadcasted_iota(I32, (heads, n_buf), 1)
    valid_w = dist_w_i <= WINDOW
    s_g = [_dot_nt(q, _slab(win_ref, n_buf, g, 0).astype(BF16)) for g in range(NSA_KV_GROUPS)]
    s_new_g = [jnp.sum(q32 * winnew[:, g * 256:g * 256 + 128], axis=1, keepdims=True)
               for g in range(NSA_KV_GROUPS)]
    o_w = attend_with_new(_group_rows(s_g, rep), valid_w, _group_rows(s_new_g, rep),
                          jnp.full((heads, 1), True), dist_w_i.astype(F32),
                          lambda ppb, g: _dot(ppb, _slab(win_ref, n_buf, g, 1).astype(BF16)),
                          lambda g: winnew[:, g * 256 + 128:g * 256 + 256])

    o_ref[0] = (gates[:, 0:1] * o_c + gates[:, 1:2] * o_s + gates[:, 2:3] * o_w).astype(BF16)

    keep_rows = (n_buf - 1) * KV_SLABS
    wout_ref[0:keep_rows, :] = win_ref[KV_SLABS:n_buf * KV_SLABS, :]
    wout_ref[keep_rows:keep_rows + KV_SLABS, :] = winslab_ref[pl.ds(pl.multiple_of(n * KV_SLABS, KV_SLABS), KV_SLABS), :]


def _nsa_sample(slc_pool, page_table, page, q, gate, ckv, win_state, slc_new, win_new, win_new_slab):
    n_seq, n_pages = page_table.shape
    n_cmp_pad = ckv.shape[1]
    n_cmp = n_pages * page // CMP_STRIDE - CMP_BLOCK // CMP_STRIDE + 1
    n_buf = win_state.shape[0] // (n_seq * KV_SLABS)
    slc_specs = _page_specs(n_pages, page * KV_SLABS, HEAD_DIM, lambda n, pt, p: pt[n, p])
    return pl.pallas_call(
        functools.partial(_nsa_sample_kernel, n_pages=n_pages, page=page, n_cmp=n_cmp),
        grid_spec=pltpu.PrefetchScalarGridSpec(
            num_scalar_prefetch=1,
            grid=(n_seq,),
            in_specs=slc_specs + [
                pl.BlockSpec((1, NSA_HEADS, HEAD_DIM), lambda n, pt: (n, 0, 0)),
                pl.BlockSpec((1, NSA_HEADS, 3), lambda n, pt: (n, 0, 0)),
                pl.BlockSpec((1, n_cmp_pad, KV_ROW), lambda n, pt: (n, 0, 0)),
                pl.BlockSpec((n_buf * KV_SLABS, HEAD_DIM), lambda n, pt: (n, 0)),
                pl.BlockSpec((n_seq, KV_ROW), lambda n, pt: (0, 0)),
                pl.BlockSpec((n_seq, KV_ROW), lambda n, pt: (0, 0)),
                pl.BlockSpec((n_seq * KV_SLABS, HEAD_DIM), lambda n, pt: (0, 0))],
            out_specs=[pl.BlockSpec((1, NSA_HEADS, HEAD_DIM), lambda n, pt: (n, 0, 0)),
                       pl.BlockSpec((n_buf * KV_SLABS, HEAD_DIM), lambda n, pt: (n, 0))]),
        out_shape=[jax.ShapeDtypeStruct((n_seq, NSA_HEADS, HEAD_DIM), BF16),
                   jax.ShapeDtypeStruct(win_state.shape, F32)],
        compiler_params=_cparams(("arbitrary",)),
        name="nsa_sample",
    )(page_table, *([slc_pool] * n_pages), q, gate, ckv, win_state, slc_new, win_new, win_new_slab)


def _ln(v, g, b):
    mu = jnp.mean(v, axis=1, keepdims=True)
    d = v - mu
    var = jnp.mean(d * d, axis=1, keepdims=True)
    return d * lax.rsqrt(var + LN_EPS) * g + b


def _ln_router_kernel(x_ref, mix_ref, g_ref, b_ref, wr_ref, br_ref, x1_ref, e_ref, gate_ref, *, alpha):
    x1 = _ln(alpha * x_ref[...] + mix_ref[...], g_ref[...], b_ref[...])
    x1_ref[...] = x1
    logits = _dot_hi(x1, wr_ref[...]) + br_ref[...]
    rows = logits.shape[0]
    lane = lax.broadcasted_iota(I32, (rows, LANES), 1)
    big = jnp.int32(LANES)
    is_grp = lane < N_GROUPS
    lg = jnp.where(is_grp, logits, -jnp.inf)
    mg = jnp.max(lg, axis=1, keepdims=True)
    grp = jnp.min(jnp.where(lg == mg, lane, big), axis=1, keepdims=True)
    p_grp = 1.0 / jnp.sum(jnp.where(is_grp, jnp.exp(lg - mg), 0.0), axis=1, keepdims=True)
    ex = lane - N_GROUPS
    in_grp = (ex >= 0) & (ex < N_EXPERTS) & (ex // EXPERTS_PER_GROUP == grp)
    le = jnp.where(in_grp, logits, -jnp.inf)
    m1 = jnp.max(le, axis=1, keepdims=True)
    e1 = jnp.min(jnp.where(le == m1, ex, big), axis=1, keepdims=True)
    le2 = jnp.where(ex == e1, -jnp.inf, le)
    m2 = jnp.max(le2, axis=1, keepdims=True)
    e2 = jnp.min(jnp.where(le2 == m2, ex, big), axis=1, keepdims=True)
    z = jnp.sum(jnp.where(in_grp, jnp.exp(le - m1), 0.0), axis=1, keepdims=True)
    p1 = 1.0 / z
    p2 = jnp.exp(m2 - m1) / z
    g1 = p_grp * p1 / (p1 + p2)
    g2 = p_grp * p2 / (p1 + p2)
    e_ref[...] = jnp.where(lane == 0, e1, jnp.where(lane == 1, e2, 0))
    gate_ref[...] = jnp.where(lane == 0, g1, jnp.where(lane == 1, g2, 0.0))


def _ln_router(x, mix, g, b, wr, br, alpha):
    m, d = x.shape
    tm = _pick(m, (256, 128))
    row = lambda i: (i, 0)
    fixed = lambda i: (0, 0)
    return pl.pallas_call(
        functools.partial(_ln_router_kernel, alpha=alpha),
        grid=(m // tm,),
        in_specs=[pl.BlockSpec((tm, d), row), pl.BlockSpec((tm, d), row),
                  pl.BlockSpec((1, d), fixed), pl.BlockSpec((1, d), fixed),
                  pl.BlockSpec((d, LANES), fixed), pl.BlockSpec((1, LANES), fixed)],
        out_specs=[pl.BlockSpec((tm, d), row), pl.BlockSpec((tm, LANES), row), pl.BlockSpec((tm, LANES), row)],
        out_shape=[jax.ShapeDtypeStruct((m, d), F32), jax.ShapeDtypeStruct((m, LANES), I32),
                   jax.ShapeDtypeStruct((m, LANES), F32)],
        compiler_params=_cparams(("parallel",)),
        name="ln_router",
    )(x, mix, g, b, wr, br)


def _dispatch_kernel(src_ref, rows_ref, xp_ref, xs_ref, o_ref, buf_ref, sem, *, m_prompt):
    i = pl.program_id(0)
    n_rows = rows_ref[i]

    @pl.when(i == 0)
    def _():
        buf_ref[...] = jnp.zeros_like(buf_ref)

    def row_copy(r):
        tok = src_ref[i * MOE_ROWS + r]
        from_prompt = pltpu.make_async_copy(xp_ref.at[pl.ds(jnp.minimum(tok, m_prompt - 1), 1)],
                                            buf_ref.at[pl.ds(r, 1)], sem)
        from_sample = pltpu.make_async_copy(xs_ref.at[pl.ds(jnp.maximum(tok - m_prompt, 0), 1)],
                                            buf_ref.at[pl.ds(r, 1)], sem)
        return tok < m_prompt, from_prompt, from_sample

    def start(r, carry):
        is_p, cp, cs = row_copy(r)

        @pl.when(is_p)
        def _():
            cp.start()

        @pl.when(jnp.logical_not(is_p))
        def _():
            cs.start()

        return carry

    lax.fori_loop(0, n_rows, start, 0)

    def wait(r, carry):
        row_copy(r)[1].wait()
        return carry

    lax.fori_loop(0, n_rows, wait, 0)
    o_ref[...] = buf_ref[...].astype(BF16)


def _dispatch(src_tok, rows_in_block, x1_p, x1_s, nb):
    d = x1_p.shape[1]
    return pl.pallas_call(
        functools.partial(_dispatch_kernel, m_prompt=x1_p.shape[0]),
        grid_spec=pltpu.PrefetchScalarGridSpec(
            num_scalar_prefetch=2,
            grid=(nb,),
            in_specs=[pl.BlockSpec(memory_space=pl.ANY), pl.BlockSpec(memory_space=pl.ANY)],
            out_specs=pl.BlockSpec((MOE_ROWS, d), lambda i, s, u: (i, 0)),
            scratch_shapes=[pltpu.VMEM((MOE_ROWS, d), F32), pltpu.SemaphoreType.DMA(())]),
        out_shape=jax.ShapeDtypeStruct((nb * MOE_ROWS, d), BF16),
        compiler_params=_cparams(("arbitrary",)),
        name="moe_dispatch",
    )(src_tok, rows_in_block, x1_p, x1_s)


def _expert_mm_kernel(run_ref, rune_ref, nruns_ref, nused_ref, x_ref, *refs, n_w, tn, epilogue):
    w_hbm = refs[:n_w]
    o_ref = refs[n_w]
    w_buf = refs[n_w + 1:2 * n_w + 1]
    w16 = refs[2 * n_w + 1:3 * n_w + 1]
    sem, state = refs[3 * n_w + 1], refs[3 * n_w + 2]
    j, i = pl.program_id(0), pl.program_id(1)
    n_runs = nruns_ref[0]
    total = pl.num_programs(0) * n_runs
    cur = j * n_runs + run_ref[i]

    def item_copies(item):
        e = rune_ref[item % n_runs]
        col = pl.multiple_of((item // n_runs) * tn, tn)
        slot = item % MOE_RING
        return [pltpu.make_async_copy(w_hbm[a].at[e, :, pl.ds(col, tn)], w_buf[a].at[slot], sem.at[a, slot])
                for a in range(n_w)]

    @pl.when((j == 0) & (i == 0))
    def _():
        state[0] = 0
        state[1] = -1

    def request(item, carry):
        for cp in item_copies(item):
            cp.start()
        return carry

    limit = jnp.minimum(cur + MOE_RING, total)
    lax.fori_loop(state[0], limit, request, 0)
    state[0] = jnp.maximum(state[0], limit)

    @pl.when(state[1] != cur)
    def _():
        for cp in item_copies(cur):
            cp.wait()
        slot = cur % MOE_RING
        for a in range(n_w):
            w16[a][...] = w_buf[a][slot].astype(BF16)
        state[1] = cur

    @pl.when(i < nused_ref[0])
    def _():
        x = x_ref[...]
        o_ref[...] = epilogue(*[_dot(x, w16[a][...]) for a in range(n_w)]).astype(o_ref.dtype)

    @pl.when(i >= nused_ref[0])
    def _():
        o_ref[...] = jnp.zeros_like(o_ref)


def _expert_mm(plan, x, weights, tn, epilogue, out_dtype, name):
    run_of_block, run_e, n_runs, n_used = plan
    nb = run_of_block.shape[0]
    kd, n = weights[0].shape[1], weights[0].shape[2]
    n_w = len(weights)
    return pl.pallas_call(
        functools.partial(_expert_mm_kernel, n_w=n_w, tn=tn, epilogue=epilogue),
        grid_spec=pltpu.PrefetchScalarGridSpec(
            num_scalar_prefetch=4,
            grid=(n // tn, nb),
            in_specs=[pl.BlockSpec((MOE_ROWS, kd), lambda j, i, *_: (i, 0))]
            + [pl.BlockSpec(memory_space=pl.ANY)] * n_w,
            out_specs=pl.BlockSpec((MOE_ROWS, tn), lambda j, i, *_: (i, j)),
            scratch_shapes=[pltpu.VMEM((MOE_RING, kd, tn), F32) for _ in range(n_w)]
            + [pltpu.VMEM((kd, tn), BF16) for _ in range(n_w)]
            + [pltpu.SemaphoreType.DMA((n_w, MOE_RING)), pltpu.SMEM((2,), I32)]),
        out_shape=jax.ShapeDtypeStruct((nb * MOE_ROWS, n), out_dtype),
        compiler_params=_cparams(("arbitrary", "arbitrary")),
        name=name,
    )(run_of_block, run_e, n_runs, n_used, x, *weights)


def _swiglu(a, u):
    return a * jax.nn.sigmoid(a) * u


def _combine_kernel(dest_ref, y_ref, x1_ref, gate_ref, g_ref, b_ref, o_ref, buf_ref, sem, *, tm, alpha):
    i = pl.program_id(0)

    def row_copy(r, k):
        slot = dest_ref[(i * tm + r) * EXPERT_TOPK + k]
        return pltpu.make_async_copy(y_ref.at[pl.ds(slot, 1)], buf_ref.at[k, pl.ds(r, 1)], sem)

    def start(r, carry):
        for k in range(EXPERT_TOPK):
            row_copy(r, k).start()
        return carry

    lax.fori_loop(0, tm, start, 0, unroll=4)

    def wait(r, carry):
        for k in range(EXPERT_TOPK):
            row_copy(r, k).wait()
        return carry

    lax.fori_loop(0, tm, wait, 0, unroll=4)
    gate = gate_ref[...]
    f = gate[:, 0:1] * buf_ref[0] + gate[:, 1:2] * buf_ref[1]
    o_ref[...] = _ln(alpha * x1_ref[...] + f, g_ref[...], b_ref[...])


def _combine(dest, ybuf, x1, gate, g, b, alpha):
    m, d = x1.shape
    tm = _pick(m, (128,))
    return pl.pallas_call(
        functools.partial(_combine_kernel, tm=tm, alpha=alpha),
        grid_spec=pltpu.PrefetchScalarGridSpec(
            num_scalar_prefetch=1,
            grid=(m // tm,),
            in_specs=[pl.BlockSpec(memory_space=pl.ANY),
                      pl.BlockSpec((tm, d), lambda i, dd: (i, 0)),
                      pl.BlockSpec((tm, LANES), lambda i, dd: (i, 0)),
                      pl.BlockSpec((1, d), lambda i, dd: (0, 0)),
                      pl.BlockSpec((1, d), lambda i, dd: (0, 0))],
            out_specs=pl.BlockSpec((tm, d), lambda i, dd: (i, 0)),
            scratch_shapes=[pltpu.VMEM((EXPERT_TOPK, tm, d), F32), pltpu.SemaphoreType.DMA(())]),
        out_shape=jax.ShapeDtypeStruct((m, d), F32),
        compiler_params=_cparams(("arbitrary",)),
        name="moe_combine",
    )(dest, ybuf, x1, gate, g, b)


def _route_plan(e_all, nb):
    a_n = e_all.shape[0]
    onehot = (e_all[:, None] == jnp.arange(N_EXPERTS, dtype=I32)[None, :]).astype(I32)
    before = jnp.cumsum(onehot, axis=0) - onehot
    rank = jnp.sum(before * onehot, axis=1)
    counts = jnp.sum(onehot, axis=0)
    pad_counts = (counts + MOE_ROWS - 1) // MOE_ROWS * MOE_ROWS
    pad_ends = jnp.cumsum(pad_counts)
    pad_starts = pad_ends - pad_counts
    dest = (pad_starts[e_all] + rank).astype(I32)
    n_used = (pad_ends[-1] // MOE_ROWS).astype(I32)
    block_e = jnp.searchsorted(pad_ends, jnp.arange(nb, dtype=I32) * MOE_ROWS, side="right").astype(I32)
    owns = (counts > 0).astype(I32)
    run_id = jnp.cumsum(owns) - owns
    n_runs = jnp.sum(owns).astype(I32)
    last_e = jnp.max(jnp.where(counts > 0, jnp.arange(N_EXPERTS, dtype=I32), 0)).astype(I32)
    block_e = jnp.where(jnp.arange(nb, dtype=I32) < n_used, jnp.minimum(block_e, N_EXPERTS - 1), last_e)
    run_of_block = run_id[block_e].astype(I32)
    run_e = jnp.full((N_EXPERTS,), last_e, I32).at[jnp.where(counts > 0, run_id, N_EXPERTS)].set(
        jnp.arange(N_EXPERTS, dtype=I32), mode="drop")
    src_tok = jnp.zeros((nb * MOE_ROWS,), I32).at[dest].set(jnp.arange(a_n, dtype=I32) // EXPERT_TOPK)
    blk = jnp.arange(nb, dtype=I32)
    rows_in_block = jnp.where(
        blk < n_used, jnp.clip(counts[block_e] - (blk * MOE_ROWS - pad_starts[block_e]), 0, MOE_ROWS), 0).astype(I32)
    return dest, (run_of_block, run_e, n_runs.reshape(1), n_used.reshape(1)), src_tok, rows_in_block


def kernel(x_prompt, x_sample, cache_dsa_kv, cache_dsa_kidx, cache_nsa_cmp_kv, cache_nsa_slc_kv, state_nsa_win_kv, page_table, w_in, w_out, cmp_w1, cmp_w2, cmp_pe, ln1_g, ln1_b, w_router_group, b_router_group, w_router_expert, b_router_expert, w_gate, w_up, w_down, ln2_g, ln2_b):
    depth = w_in.shape[0]
    assert depth == 1 and x_sample.shape[1] == 1
    batch, seq, d_model = x_prompt.shape
    n_seq = x_sample.shape[0]
    n_pool, page = cache_dsa_kv.shape[1], cache_dsa_kv.shape[2]
    alpha = (2.0 * depth) ** 0.25
    m_p = batch * seq
    xp = x_prompt.reshape(m_p, d_model)
    xs = x_sample.reshape(n_seq, d_model)
    xp16, xs16 = xp.astype(BF16), xs.astype(BF16)

    w = w_in[0]
    t0, t1 = 4176, 9296
    rep_b = NSA_HEADS // NSA_KV_GROUPS
    w_head = w[:, :t0 + 48].astype(BF16)
    w_tail = w[:, t0:t1].astype(BF16)
    w_gates = jnp.pad(w[:, t1:].reshape(d_model, NSA_KV_GROUPS, 3 * rep_b),
                      ((0, 0), (0, 0), (0, LANES - 3 * rep_b))).reshape(d_model, NSA_KV_GROUPS * LANES).astype(BF16)
    c_dq, c_dk, c_dv, c_iq, c_ik = 0, 2048, 2560, 3072, 4096
    c_nq, c_ck, c_cv, c_sk, c_sv, c_wk, c_wv = 0, 2048, 2560, 3072, 3584, 4096, 4608

    def project(x16):
        qa = _matmul([x16], w_head, c_dq, Q_ROW)
        qi = _matmul([x16], w_head, c_iq, IDX_HEADS * IDX_DIM)
        hc = _matmul([x16], w_head, c_ik, LANES)
        qb = _matmul([x16], w_tail, c_nq, Q_ROW)
        gate = _matmul([x16], w_gates)
        kv = [_kv_proj(x16, w_head, c_dk, c_dv), _kv_proj(x16, w_tail, c_ck, c_cv),
              _kv_proj(x16, w_tail, c_sk, c_sv), _kv_proj(x16, w_tail, c_wk, c_wv)]
        return qa, qi, hc, qb, gate, kv

    qa_p, qi_p, hc_p, qb_p, gate_p, kv_p = project(xp16)
    qa_s, qi_s, hc_s, qb_s, gate_s, kv_s = project(xs16)

    w1r = cmp_w1[0].reshape(2, CMP_BLOCK // CMP_STRIDE, CMP_STRIDE, HEAD_DIM, HEAD_DIM)
    w1r = jnp.transpose(w1r, (0, 2, 3, 1, 4)).reshape(2, CMP_STRIDE * HEAD_DIM, 2 * HEAD_DIM)
    peb = _cmp_pe_bias(cmp_pe[0], cmp_w1[0])
    pages_per_seq = seq // page
    ident = jnp.arange(batch * pages_per_seq, dtype=I32).reshape(batch, pages_per_seq)
    ckv_p = _compress(kv_p[1][0], ident, page, w1r, cmp_w2[0], peb)
    ckv_s = _compress(cache_nsa_cmp_kv.reshape(-1, HEAD_DIM), page_table, page, w1r, cmp_w2[0], peb)

    oa_p = _dsa_prompt(qa_p, qi_p, hc_p, kv_p[0][1], batch, seq)
    ob_p = _nsa_prompt(qb_p, gate_p, ckv_p, kv_p[2][1], kv_p[3][1], batch, seq)

    oa_s = _dsa_sample(cache_dsa_kidx.reshape(n_pool, page, IDX_DIM), cache_dsa_kv.reshape(-1, HEAD_DIM), page_table,
                       qa_s.reshape(n_seq, DSA_HEADS, HEAD_DIM), qi_s.reshape(n_seq, IDX_HEADS, IDX_DIM),
                       hc_s[:, IDX_DIM:IDX_DIM + IDX_HEADS].reshape(n_seq, IDX_HEADS, 1), hc_s,
                       kv_s[0][1].astype(F32))
    ob_s, win_s = _nsa_sample(cache_nsa_slc_kv.reshape(-1, HEAD_DIM), page_table, page,
                              qb_s.reshape(n_seq, NSA_HEADS, HEAD_DIM),
                              gate_s.reshape(n_seq, NSA_KV_GROUPS, LANES)[:, :, :3 * rep_b].reshape(n_seq, NSA_HEADS, 3),
                              ckv_s,
                              state_nsa_win_kv.reshape(-1, HEAD_DIM), kv_s[2][1].astype(F32),
                              kv_s[3][1].astype(F32), kv_s[3][0])

    w_o = w_out[0].astype(BF16)
    wr = jnp.concatenate([w_router_group[0], w_router_expert[0],
                          jnp.zeros((d_model, LANES - N_GROUPS - N_EXPERTS), F32)], axis=1)
    br = jnp.concatenate([b_router_group[0], b_router_expert[0],
                          jnp.zeros((LANES - N_GROUPS - N_EXPERTS,), F32)]).reshape(1, LANES)
    g1, b1 = ln1_g[0].reshape(1, d_model), ln1_b[0].reshape(1, d_model)
    g2, b2 = ln2_g[0].reshape(1, d_model), ln2_b[0].reshape(1, d_model)
    x1_p, e_p, gt_p = _ln_router(xp, _matmul([oa_p, ob_p], w_o), g1, b1, wr, br, alpha)
    x1_s, e_s, gt_s = _ln_router(xs, _matmul([oa_s.reshape(n_seq, Q_ROW), ob_s.reshape(n_seq, Q_ROW)], w_o),
                                 g1, b1, wr, br, alpha)

    e_all = jnp.concatenate([e_p[:, :EXPERT_TOPK], e_s[:, :EXPERT_TOPK]], axis=0).reshape(-1)
    a_n = e_all.shape[0]
    nb = -(-(a_n + N_EXPERTS * (MOE_ROWS - 1)) // MOE_ROWS)
    dest, plan, src_tok, rows_in_block = _route_plan(e_all, nb)
    xbuf = _dispatch(src_tok, rows_in_block, x1_p, x1_s, nb)
    d_expert = w_gate.shape[3]
    hid = _expert_mm(plan, xbuf, [w_gate[0], w_up[0]], _pick(d_expert, (MOE_CHUNK, LANES)), _swiglu, BF16,
                     "moe_gate_up")
    ybuf = _expert_mm(plan, hid, [w_down[0]], _pick(d_model, (1024, 512, 256, LANES)), lambda y: y, F32, "moe_down")
    y_p = _combine(dest[:m_p * EXPERT_TOPK], ybuf, x1_p, gt_p, g2, b2, alpha)
    y_s = _combine(dest[m_p * EXPERT_TOPK:], ybuf, x1_s, gt_s, g2, b2, alpha)

    def state(slab, lead):
        return slab.reshape((1,) + lead + (NSA_KV_GROUPS, 2, HEAD_DIM))

    n_win = min(WINDOW, seq)
    win_p = state(kv_p[3][0], (batch, seq))[:, :, seq - n_win:]
    return (y_p.reshape(batch, seq, d_model), y_s.reshape(n_seq, 1, d_model),
            state(kv_p[0][0], (batch, seq)), state(kv_s[0][0], (n_seq, 1)),
            hc_p[:, :IDX_DIM].reshape(1, batch, seq, IDX_DIM), hc_s[:, :IDX_DIM].reshape(1, n_seq, 1, IDX_DIM),
            state(kv_p[1][0], (batch, seq)), state(kv_s[1][0], (n_seq, 1)),
            state(kv_p[2][0], (batch, seq)), state(kv_s[2][0], (n_seq, 1)),
            win_p, state(win_s, (n_seq, state_nsa_win_kv.shape[2])))
```

```python
import functools

import jax
import jax.numpy as jnp
from jax import lax
from jax.experimental import pallas as pl
from jax.experimental.pallas import tpu as pltpu

F32 = jnp.float32
BF16 = jnp.bfloat16
I32 = jnp.int32

HEAD_DIM = 128
DSA_HEADS = 16
DSA_KV_HEADS = 4
IDX_HEADS = 16
IDX_DIM = 64
IDX_TOPK = 256
NSA_HEADS = 16
NSA_KV_GROUPS = 4
CMP_STRIDE = 16
CMP_BLOCK = 32
SLC_BLOCK = 64
SLC_TOPN = 16
WINDOW = 512
N_GROUPS = 8
EXPERTS_PER_GROUP = 8
N_EXPERTS = N_GROUPS * EXPERTS_PER_GROUP
EXPERT_TOPK = 2
LN_EPS = 1e-5
NEG_INF = -1e30
FORCE_SCORE = 1e9

KV_ROW = 2 * DSA_KV_HEADS * HEAD_DIM
KV_SLABS = 2 * DSA_KV_HEADS
KV_HALF = DSA_KV_HEADS * HEAD_DIM
Q_ROW = DSA_HEADS * HEAD_DIM
Q_TILE = 128
KEY_SPAN = 512
SEQ_GROUP = 8
LANES = 128
MOE_ROWS = 256
MOE_RING = 4
MOE_CHUNK = 256
VMEM_LIMIT = 56 * 1024 * 1024
SCALE = HEAD_DIM ** -0.5
LOG2E = 1.4426950408889634
MASKED_DIST = 1e32
INT_MIN = -(2 ** 31)


def _slopes(n):
    return [2.0 ** (-8.0 * i / n) for i in range(1, n + 1)]


def _cparams(sem):
    return pltpu.CompilerParams(dimension_semantics=sem, vmem_limit_bytes=VMEM_LIMIT)


def _dot(a, b):
    return jnp.dot(a, b, preferred_element_type=F32)


def _dot_nt(a, b):
    return lax.dot_general(a, b, (((1,), (1,)), ((), ())), preferred_element_type=F32)


def _dot_hi(a, b):
    return jnp.dot(a, b, preferred_element_type=F32, precision=lax.Precision.HIGHEST)


def _pick(n, cands):
    for c in cands:
        if n % c == 0:
            return c
    return n


def _mm_kernel(*refs, n_lhs):
    x_refs, w_refs, o_ref = refs[:n_lhs], refs[n_lhs:2 * n_lhs], refs[2 * n_lhs]
    acc = _dot(x_refs[0][...], w_refs[0][...])
    for x_ref, w_ref in zip(x_refs[1:], w_refs[1:]):
        acc = acc + _dot(x_ref[...], w_ref[...])
    o_ref[...] = acc.astype(o_ref.dtype)


def _matmul(xs, w, n0=0, n=None):
    m = xs[0].shape[0]
    n = w.shape[1] if n is None else n
    kds = [x.shape[1] for x in xs]
    assert len(set(kds)) == 1 and sum(kds) == w.shape[0]
    kd = kds[0]
    tm = _pick(m, (512, 256, 128))
    tn = _pick(n, (1024, 512, 256, 128))
    assert n0 % tn == 0
    j0 = n0 // tn
    x_specs = [pl.BlockSpec((tm, kd), lambda j, i: (i, 0)) for _ in xs]
    w_specs = [pl.BlockSpec((kd, tn), functools.partial(lambda j, i, a: (a, j + j0), a=a)) for a in range(len(xs))]
    return pl.pallas_call(
        functools.partial(_mm_kernel, n_lhs=len(xs)),
        grid=(n // tn, m // tm),
        in_specs=x_specs + w_specs,
        out_specs=pl.BlockSpec((tm, tn), lambda j, i: (i, j)),
        out_shape=jax.ShapeDtypeStruct((m, n), F32),
        compiler_params=_cparams(("parallel", "parallel")),
        name="matmul",
    )(*xs, *([w] * len(xs)))


def _kv_proj_kernel(x_ref, wk_ref, wv_ref, slab_ref, tile_ref):
    x = x_ref[...]
    rows = x.shape[0]
    for c, w_ref in enumerate((wk_ref, wv_ref)):
        res = _dot(x, w_ref[...])
        for g in range(DSA_KV_HEADS):
            piece = res[:, g * HEAD_DIM:(g + 1) * HEAD_DIM]
            slab_ref[pl.ds(g * 2 + c, rows, stride=KV_SLABS), :] = piece
            tile_ref[:, (g * 2 + c) * HEAD_DIM:(g * 2 + c + 1) * HEAD_DIM] = piece.astype(BF16)


def _kv_proj(x, w, k_col, v_col):
    m, kd = x.shape
    tm = _pick(m, (512, 256, 128))
    assert k_col % KV_HALF == 0 and v_col % KV_HALF == 0
    kb, vb = k_col // KV_HALF, v_col // KV_HALF
    return pl.pallas_call(
        _kv_proj_kernel,
        grid=(m // tm,),
        in_specs=[pl.BlockSpec((tm, kd), lambda i: (i, 0)),
                  pl.BlockSpec((kd, KV_HALF), lambda i: (0, kb)),
                  pl.BlockSpec((kd, KV_HALF), lambda i: (0, vb))],
        out_specs=[pl.BlockSpec((tm * KV_SLABS, HEAD_DIM), lambda i: (i, 0)),
                   pl.BlockSpec((tm, KV_ROW), lambda i: (i, 0))],
        out_shape=[jax.ShapeDtypeStruct((m * KV_SLABS, HEAD_DIM), F32), jax.ShapeDtypeStruct((m, KV_ROW), BF16)],
        compiler_params=_cparams(("parallel",)),
        name="kv_proj",
    )(x, w, w)


def _slab(ref, n_tok, g, c):
    return ref[pl.ds(g * 2 + c, n_tok, stride=KV_SLABS), :]


def _page_specs(n_pages, rows, width, index):
    return [pl.BlockSpec((rows, width), functools.partial(lambda *a, p: (index(*a, p), 0), p=p))
            for p in range(n_pages)]


def _sort_key(x):
    b = pltpu.bitcast(x, I32)
    return jnp.where(b < 0, b ^ jnp.int32(0x7FFFFFFF), b)


def _topk_mask(sc, idx, k, idx_bits):
    key = _sort_key(sc)

    def count(m):
        return jnp.sum(m.astype(I32), axis=1, keepdims=True)

    t0 = jnp.where(count(key >= 0) >= k, jnp.int32(0), jnp.int32(INT_MIN))

    def vstep(i, t):
        cand = t | (jnp.int32(1) << (30 - i))
        return jnp.where(count(key >= cand) >= k, cand, t)

    thr = lax.fori_loop(0, 31, vstep, t0)
    gt = key > thr
    eq = key == thr
    need = k - count(gt)
    crowded = count(eq) > need
    n_steps = jnp.where(jnp.max(crowded.astype(I32)) > 0, idx_bits, 0)

    def istep(i, c):
        cand = c | (jnp.int32(1) << (idx_bits - 1 - i))
        return jnp.where(count(eq & (idx < cand)) < need, cand, c)

    cut = lax.fori_loop(0, n_steps, istep, jnp.zeros_like(thr))
    return gt | (eq & (jnp.logical_not(crowded) | (idx <= cut)))


def _attend(q2, k, v, dm, slope2):
    s = _dot_nt(q2, k) - slope2 * dm
    m = jnp.max(s, axis=1, keepdims=True)
    p = jnp.exp2(s - m)
    l = jnp.sum(p, axis=1, keepdims=True)
    return _dot(p.astype(BF16), v) * (1.0 / l)


def _spans(seq):
    span = KEY_SPAN if seq % KEY_SPAN == 0 else seq
    return span, seq // span


def _group_slope(g, r, n_heads, n_groups):
    rep = n_heads // n_groups
    table = _slopes(n_heads)
    out = jnp.float32(table[r])
    for gg in range(1, n_groups):
        out = jnp.where(g == gg, jnp.float32(table[gg * rep + r]), out)
    return out


def _dsa_prompt_mask(c, qi_ref, cq_ref, ck_ref, dm_ref, kw, k_keep):
    t = c * Q_TILE + lax.broadcasted_iota(I32, (Q_TILE, 1), 0)
    kpos = lax.broadcasted_iota(I32, (Q_TILE, kw), 1)
    causal = kpos <= t
    ki = ck_ref[0:kw, 0:IDX_DIM].astype(BF16)
    wi = cq_ref[:, IDX_DIM:IDX_DIM + IDX_HEADS] * (IDX_HEADS ** -0.5)
    sc = jnp.zeros((Q_TILE, kw), F32)
    for h in range(IDX_HEADS):
        a = _dot_nt(qi_ref[:, h * IDX_DIM:(h + 1) * IDX_DIM].astype(BF16), ki)
        sc = sc + jnp.maximum(a, 0.0) * wi[:, h:h + 1]
    sc = jnp.where(causal, sc, NEG_INF)
    keep = _topk_mask(sc, kpos, k_keep, max(1, (kw - 1).bit_length()))
    dm_ref[:, 0:kw] = jnp.where(keep & causal, (t - kpos).astype(F32), MASKED_DIST)


def _dsa_prompt_kernel(q_ref, qi_ref, cq_ref, ck_ref, kv_ref, o_ref, dm_ref, *, seq, k_keep):
    c, g = pl.program_id(1), pl.program_id(2)
    rep = DSA_HEADS // DSA_KV_HEADS
    span, n_var = _spans(seq)
    for v in range(n_var):
        @pl.when((c * Q_TILE) // span == v)
        def _(kw=(v + 1) * span):
            @pl.when(g == 0)
            def _():
                _dsa_prompt_mask(c, qi_ref, cq_ref, ck_ref, dm_ref, kw, k_keep)

            dm = dm_ref[:, 0:kw]
            kg = kv_ref[0:kw, 0:HEAD_DIM]
            vg = kv_ref[0:kw, HEAD_DIM:2 * HEAD_DIM]
            for r in range(rep):
                q2 = (q_ref[:, r * HEAD_DIM:(r + 1) * HEAD_DIM] * (SCALE * LOG2E)).astype(BF16)
                slope2 = _group_slope(g, r, DSA_HEADS, DSA_KV_HEADS) * LOG2E
                o_ref[:, r * HEAD_DIM:(r + 1) * HEAD_DIM] = _attend(q2, kg, vg, dm, slope2).astype(BF16)


def _dsa_prompt(q, qi, h_c, kv, batch, seq):
    nc = seq // Q_TILE
    k_keep = min(IDX_TOPK, seq // 4)
    rep = DSA_HEADS // DSA_KV_HEADS
    tile = lambda b, c, g: (b * nc + c, 0)
    tile_g = lambda b, c, g: (b * nc + c, g)
    return pl.pallas_call(
        functools.partial(_dsa_prompt_kernel, seq=seq, k_keep=k_keep),
        grid=(batch, nc, DSA_KV_HEADS),
        in_specs=[pl.BlockSpec((Q_TILE, rep * HEAD_DIM), tile_g),
                  pl.BlockSpec((Q_TILE, IDX_HEADS * IDX_DIM), tile),
                  pl.BlockSpec((Q_TILE, LANES), tile),
                  pl.BlockSpec((seq, LANES), lambda b, c, g: (b, 0)),
                  pl.BlockSpec((seq, 2 * HEAD_DIM), lambda b, c, g: (b, g))],
        out_specs=pl.BlockSpec((Q_TILE, rep * HEAD_DIM), tile_g),
        out_shape=jax.ShapeDtypeStruct((batch * seq, Q_ROW), BF16),
        scratch_shapes=[pltpu.VMEM((Q_TILE, seq), F32)],
        compiler_params=_cparams(("parallel", "arbitrary", "arbitrary")),
        name="dsa_prompt",
    )(q, qi, h_c, h_c, kv)


def _peb_kernel(pe_ref, w1_ref, o_ref):
    for c in range(2):
        o_ref[c] = _dot_hi(pe_ref[c], w1_ref[c])


def _cmp_pe_bias(cmp_pe, cmp_w1):
    return pl.pallas_call(
        _peb_kernel,
        out_shape=jax.ShapeDtypeStruct((2, 1, HEAD_DIM), F32),
        name="cmp_pe_bias",
    )(cmp_pe.reshape(2, 1, CMP_BLOCK * HEAD_DIM), cmp_w1.reshape(2, CMP_BLOCK * HEAD_DIM, HEAD_DIM))


def _compress_kernel(pt_ref, *refs, n_pages, page):
    pages = refs[:n_pages]
    w1_ref, w2_ref, peb_ref, o_ref = refs[n_pages:]
    sub_per_page = page // CMP_STRIDE
    n_sub = n_pages * sub_per_page
    for c in range(2):
        lhs = jnp.concatenate([
            jnp.concatenate([
                jnp.concatenate([p[pl.ds(j * KV_SLABS + g * 2 + c, sub_per_page, stride=CMP_STRIDE * KV_SLABS), :]
                                 for p in pages], axis=0).astype(BF16)
                for j in range(CMP_STRIDE)], axis=1)
            for g in range(NSA_KV_GROUPS)], axis=0)
        part = _dot(lhs, w1_ref[c].astype(BF16))
        w2 = w2_ref[c].astype(BF16)
        for g in range(NSA_KV_GROUPS):
            col = g * 256 + c * 128
            pg = part[g * n_sub:(g + 1) * n_sub]
            nxt = pltpu.roll(pg[:, HEAD_DIM:], n_sub - 1, 0)
            hid = pg[:, :HEAD_DIM] + nxt + peb_ref[c]
            o_ref[0, :, col:col + HEAD_DIM] = _dot(jax.nn.gelu(hid).astype(BF16), w2)


def _compress(pool, page_table, page, w1r, w2, peb):
    n_seq, n_pages = page_table.shape
    n_sub = n_pages * page // CMP_STRIDE
    page_specs = _page_specs(n_pages, page * KV_SLABS, HEAD_DIM, lambda n, pt, p: pt[n, p])
    return pl.pallas_call(
        functools.partial(_compress_kernel, n_pages=n_pages, page=page),
        grid_spec=pltpu.PrefetchScalarGridSpec(
            num_scalar_prefetch=1,
            grid=(n_seq,),
            in_specs=page_specs + [
                pl.BlockSpec((2, CMP_STRIDE * HEAD_DIM, 2 * HEAD_DIM), lambda n, pt: (0, 0, 0)),
                pl.BlockSpec((2, HEAD_DIM, HEAD_DIM), lambda n, pt: (0, 0, 0)),
                pl.BlockSpec((2, 1, HEAD_DIM), lambda n, pt: (0, 0, 0))],
            out_specs=pl.BlockSpec((1, n_sub, KV_ROW), lambda n, pt: (n, 0, 0))),
        out_shape=jax.ShapeDtypeStruct((n_seq, n_sub, KV_ROW), F32),
        compiler_params=_cparams(("arbitrary",)),
        name="nsa_compress",
    )(page_table, *([pool] * n_pages), w1r, w2, peb)


def _cover(n_cmp_pad, n_slc_pad):
    cs = lax.broadcasted_iota(I32, (n_cmp_pad, n_slc_pad), 0) * CMP_STRIDE
    bs = lax.broadcasted_iota(I32, (n_cmp_pad, n_slc_pad), 1) * SLC_BLOCK
    return ((cs < bs + SLC_BLOCK) & (cs + CMP_BLOCK > bs)).astype(F32)


def _select_blocks(score, t, n_slc, n_keep):
    rows, width = score.shape
    j = lax.broadcasted_iota(I32, (rows, width), 1)
    cur = t // SLC_BLOCK
    forced = (j == 0) | (j == cur) | (j == cur - 1)
    admissible = j * SLC_BLOCK <= t
    score = jnp.where(forced, FORCE_SCORE, jnp.where(admissible, score, NEG_INF))
    rank = jnp.zeros((rows, width), I32)
    for k in range(n_slc):
        sk = score[:, k:k + 1]
        ahead = (sk > score) | ((sk == score) & (j > k))
        rank = rank + ahead.astype(I32)
    return ((rank < n_keep) & (j < n_slc)).astype(F32)


def _select_blocks_t(score_t, t_row, n_keep):
    n_slc, width = score_t.shape
    j = lax.broadcasted_iota(I32, (n_slc, width), 0)
    cur = t_row // SLC_BLOCK
    forced = (j == 0) | (j == cur) | (j == cur - 1)
    admissible = j * SLC_BLOCK <= t_row
    score_t = jnp.where(forced, FORCE_SCORE, jnp.where(admissible, score_t, NEG_INF))
    rank = jnp.zeros((n_slc, width), I32)
    for k in range(n_slc):
        sk = score_t[k:k + 1, :]
        ahead = (sk > score_t) | ((sk == score_t) & (j > k))
        rank = rank + ahead.astype(I32)
    return (rank < n_keep).astype(F32)


def _nsa_prompt_body(c, g, q_ref, gate_ref, ckv_ref, slc_ref, win_ref, o_ref, kw, seq, n_cmp, win_keys):
    t = c * Q_TILE + lax.broadcasted_iota(I32, (Q_TILE, 1), 0)
    t_row = c * Q_TILE + lax.broadcasted_iota(I32, (1, Q_TILE), 1)
    rep = NSA_HEADS // NSA_KV_GROUPS
    slope2 = [_group_slope(g, r, NSA_HEADS, NSA_KV_GROUPS) * LOG2E for r in range(rep)]
    n_cmp_pad = ckv_ref.shape[1]
    n_slc = seq // SLC_BLOCK
    n_keep = min(SLC_TOPN, n_slc)

    kc = lax.broadcasted_iota(I32, (Q_TILE, n_cmp_pad), 1)
    dist_c_i = t - (kc * CMP_STRIDE + (CMP_BLOCK - 1))
    valid_c = (dist_c_i >= 0) & (kc < n_cmp)
    dist_c = dist_c_i.astype(F32)
    cover_bs = lax.broadcasted_iota(I32, (n_slc, n_cmp_pad), 0) * SLC_BLOCK
    cover_cs = lax.broadcasted_iota(I32, (n_slc, n_cmp_pad), 1) * CMP_STRIDE
    cover_t = ((cover_cs < cover_bs + SLC_BLOCK) & (cover_cs + CMP_BLOCK > cover_bs)).astype(F32)
    slc_rows = -(-n_slc // LANES) * LANES
    expand = (lax.broadcasted_iota(I32, (slc_rows, kw), 1) // SLC_BLOCK
              == lax.broadcasted_iota(I32, (slc_rows, kw), 0)).astype(BF16)

    kpos = lax.broadcasted_iota(I32, (Q_TILE, kw), 1)
    causal = kpos <= t
    dist_s = (t - kpos).astype(F32)

    w0 = pl.multiple_of(jnp.maximum(c * Q_TILE + Q_TILE - win_keys, 0), Q_TILE)
    wpos = w0 + lax.broadcasted_iota(I32, (Q_TILE, win_keys), 1)
    dist_w_i = t - wpos
    dm_w = jnp.where((dist_w_i >= 0) & (dist_w_i <= WINDOW), dist_w_i.astype(F32), MASKED_DIST)

    gates = jax.nn.sigmoid(gate_ref[:, 0:3 * rep])

    ck = ckv_ref[0, :, 0:HEAD_DIM].astype(BF16)
    cv = ckv_ref[0, :, HEAD_DIM:2 * HEAD_DIM].astype(BF16)
    q2s, o_cs = [], []
    imp = jnp.zeros((Q_TILE, n_cmp_pad), F32)
    for r in range(rep):
        q2 = (q_ref[:, r * HEAD_DIM:(r + 1) * HEAD_DIM] * (SCALE * LOG2E)).astype(BF16)
        q2s.append(q2)
        s = jnp.where(valid_c, _dot_nt(q2, ck) - slope2[r] * dist_c, NEG_INF)
        m = jnp.max(s, axis=1, keepdims=True)
        p = jnp.where(valid_c, jnp.exp2(s - m), 0.0)
        l = jnp.sum(p, axis=1, keepdims=True)
        p = p * (1.0 / jnp.where(l > 0.0, l, 1.0))
        imp = imp + p
        o_cs.append(_dot(p.astype(BF16), cv))
    sel_t = _select_blocks_t(_dot_hi(cover_t, imp.T), t_row, n_keep)
    sel_t = jnp.concatenate([sel_t, jnp.zeros((slc_rows - n_slc, Q_TILE), F32)], axis=0)
    picked = _dot(sel_t.T.astype(BF16), expand) > 0.5
    dm_s = jnp.where(picked & causal, dist_s, MASKED_DIST)
    sk = slc_ref[0:kw, 0:HEAD_DIM]
    sv = slc_ref[0:kw, HEAD_DIM:2 * HEAD_DIM]
    wk = win_ref[pl.ds(w0, win_keys), 0:HEAD_DIM]
    wv = win_ref[pl.ds(w0, win_keys), HEAD_DIM:2 * HEAD_DIM]
    for r in range(rep):
        o_s = _attend(q2s[r], sk, sv, dm_s, slope2[r])
        o_w = _attend(q2s[r], wk, wv, dm_w, slope2[r])
        o_ref[:, r * HEAD_DIM:(r + 1) * HEAD_DIM] = (
            gates[:, 3 * r:3 * r + 1] * o_cs[r] + gates[:, 3 * r + 1:3 * r + 2] * o_s
            + gates[:, 3 * r + 2:3 * r + 3] * o_w).astype(BF16)


def _nsa_prompt_kernel(q_ref, gate_ref, ckv_ref, slc_ref, win_ref, o_ref, *, seq, n_cmp, win_keys):
    c, g = pl.program_id(1), pl.program_id(2)
    span, n_var = _spans(seq)
    for v in range(n_var):
        @pl.when((c * Q_TILE) // span == v)
        def _(kw=(v + 1) * span):
            _nsa_prompt_body(c, g, q_ref, gate_ref, ckv_ref, slc_ref, win_ref, o_ref, kw, seq, n_cmp, win_keys)


def _nsa_prompt(q, gate, ckv, slc, win, batch, seq):
    nc = seq // Q_TILE
    n_cmp_pad = ckv.shape[1]
    n_cmp = seq // CMP_STRIDE - CMP_BLOCK // CMP_STRIDE + 1
    win_keys = min(seq, WINDOW + Q_TILE)
    rep = NSA_HEADS // NSA_KV_GROUPS
    tile_g = lambda b, c, g: (b * nc + c, g)
    return pl.pallas_call(
        functools.partial(_nsa_prompt_kernel, seq=seq, n_cmp=n_cmp, win_keys=win_keys),
        grid=(batch, nc, NSA_KV_GROUPS),
        in_specs=[pl.BlockSpec((Q_TILE, rep * HEAD_DIM), tile_g),
                  pl.BlockSpec((Q_TILE, LANES), tile_g),
                  pl.BlockSpec((1, n_cmp_pad, 2 * HEAD_DIM), lambda b, c, g: (b, 0, g)),
                  pl.BlockSpec((seq, 2 * HEAD_DIM), lambda b, c, g: (b, g)),
                  pl.BlockSpec((seq, 2 * HEAD_DIM), lambda b, c, g: (b, g))],
        out_specs=pl.BlockSpec((Q_TILE, rep * HEAD_DIM), tile_g),
        out_shape=jax.ShapeDtypeStruct((batch * seq, Q_ROW), BF16),
        compiler_params=_cparams(("parallel", "arbitrary", "arbitrary")),
        name="nsa_prompt",
    )(q, gate, ckv, slc, win)


def _group_rows(per_group, rep):
    row = lax.broadcasted_iota(I32, per_group[0].shape, 0) // rep
    out = per_group[0]
    for g in range(1, len(per_group)):
        out = jnp.where(row == g, per_group[g], out)
    return out


def _dsa_sample_mask_kernel(pt_ref, *refs, n_pages, page, k_keep):
    n_ki = SEQ_GROUP * n_pages
    ki_pages = refs[:n_ki]
    qi_ref, wi_ref, cnew_ref, keep_ref = refs[n_ki:]
    past = n_pages * page
    width = past + LANES
    rows = []
    for s in range(SEQ_GROUP):
        qi = qi_ref[s].astype(BF16)
        wi = wi_ref[s] * (IDX_HEADS ** -0.5)
        knew = cnew_ref[pl.ds(s, 1), 0:IDX_DIM]
        a_past = jnp.concatenate([_dot(qi, ki_pages[s * n_pages + p][0].astype(BF16))
                                  for p in range(n_pages)], axis=1)
        a_new = _dot_nt(qi, jnp.broadcast_to(knew, (8, IDX_DIM)).astype(BF16))[:, 0:1]
        sc_past = jnp.sum(jnp.maximum(a_past, 0.0) * wi, axis=0, keepdims=True)
        sc_new = jnp.sum(jnp.maximum(a_new, 0.0) * wi, axis=0, keepdims=True)
        rows.append(jnp.concatenate([sc_past, jnp.broadcast_to(sc_new, (1, LANES))], axis=1))
    col = lax.broadcasted_iota(I32, (SEQ_GROUP, width), 1)
    sc = jnp.where(col <= past, jnp.concatenate(rows, axis=0), NEG_INF)
    keep = _topk_mask(sc, col, k_keep, max(1, (width - 1).bit_length())) & (col <= past)
    keep_ref[...] = keep.astype(F32)


def _dsa_sample_mask(kidx_pool_t, page_table, qi, wi, h_c):
    n_seq, n_pages = page_table.shape
    assert n_seq % SEQ_GROUP == 0
    page = kidx_pool_t.shape[2]
    total = n_pages * page + 1
    k_keep = min(IDX_TOPK, total // 4)
    width = n_pages * page + LANES
    ki_specs = [pl.BlockSpec((1, IDX_DIM, page),
                             functools.partial(lambda i, pt, s, p: (pt[i * SEQ_GROUP + s, p], 0, 0), s=s, p=p))
                for s in range(SEQ_GROUP) for p in range(n_pages)]
    return pl.pallas_call(
        functools.partial(_dsa_sample_mask_kernel, n_pages=n_pages, page=page, k_keep=k_keep),
        grid_spec=pltpu.PrefetchScalarGridSpec(
            num_scalar_prefetch=1,
            grid=(n_seq // SEQ_GROUP,),
            in_specs=ki_specs + [
                pl.BlockSpec((SEQ_GROUP, IDX_HEADS, IDX_DIM), lambda i, pt: (i, 0, 0)),
                pl.BlockSpec((SEQ_GROUP, IDX_HEADS, 1), lambda i, pt: (i, 0, 0)),
                pl.BlockSpec((SEQ_GROUP, LANES), lambda i, pt: (i, 0))],
            out_specs=pl.BlockSpec((SEQ_GROUP, width), lambda i, pt: (i, 0))),
        out_shape=jax.ShapeDtypeStruct((n_seq, width), F32),
        compiler_params=_cparams(("arbitrary",)),
        name="dsa_sample_mask",
    )(page_table, *([kidx_pool_t] * (SEQ_GROUP * n_pages)), qi, wi, h_c)


def _dsa_sample_kernel(pt_ref, *refs, n_pages, page):
    kv_pages = refs[:n_pages]
    q_ref, keep_ref, kvnew_ref, o_ref = refs[n_pages:]
    r = pl.program_id(0) % SEQ_GROUP
    past = n_pages * page
    rep = DSA_HEADS // DSA_KV_HEADS
    slopes = _slopes(DSA_HEADS)

    keep = keep_ref[pl.ds(r, 1), :] > 0.5
    keep_past = jnp.broadcast_to(keep[:, 0:past], (DSA_HEADS, past))
    keep_new = jnp.broadcast_to(keep[:, past:past + 1], (DSA_HEADS, 1))

    q = (q_ref[0] * SCALE).astype(BF16)
    q32 = q.astype(F32)
    kvnew = kvnew_ref[pl.ds(r, 1), :]
    slope_col = jnp.concatenate([jnp.full((1, 1), s, F32) for s in slopes], axis=0)
    dist = (past - lax.broadcasted_iota(I32, (DSA_HEADS, past), 1)).astype(F32)
    s_g, s_new_g = [], []
    for g in range(DSA_KV_HEADS):
        s_g.append(jnp.concatenate(
            [_dot_nt(q, _slab(p, page, g, 0).astype(BF16)) for p in kv_pages], axis=1))
        s_new_g.append(jnp.sum(q32 * kvnew[:, g * 256:g * 256 + 128], axis=1, keepdims=True))
    s = jnp.where(keep_past, _group_rows(s_g, rep) - slope_col * dist, NEG_INF)
    s_new = jnp.where(keep_new, _group_rows(s_new_g, rep), NEG_INF)
    m = jnp.maximum(jnp.max(s, axis=1, keepdims=True), s_new)
    p = jnp.where(keep_past, jnp.exp(s - m), 0.0)
    p_new = jnp.where(keep_new, jnp.exp(s_new - m), 0.0)
    l = jnp.sum(p, axis=1, keepdims=True) + p_new
    inv = 1.0 / jnp.where(l > 0.0, l, 1.0)
    pb = p.astype(BF16)
    p_new = p_new.astype(BF16).astype(F32)
    o_g = []
    for g in range(DSA_KV_HEADS):
        acc = p_new * kvnew[:, g * 256 + 128:g * 256 + 256]
        for i, pg in enumerate(kv_pages):
            acc = acc + _dot(pb[:, i * page:(i + 1) * page], _slab(pg, page, g, 1).astype(BF16))
        o_g.append(acc)
    o_ref[0] = (_group_rows(o_g, rep) * inv).astype(BF16)


def _dsa_sample(kv_pool, page_table, page, q, keep, kv_new):
    n_seq, n_pages = page_table.shape
    width = keep.shape[1]
    kv_specs = _page_specs(n_pages, page * KV_SLABS, HEAD_DIM, lambda n, pt, p: pt[n, p])
    return pl.pallas_call(
        functools.partial(_dsa_sample_kernel, n_pages=n_pages, page=page),
        grid_spec=pltpu.PrefetchScalarGridSpec(
            num_scalar_prefetch=1,
            grid=(n_seq,),
            in_specs=kv_specs + [
                pl.BlockSpec((1, DSA_HEADS, HEAD_DIM), lambda n, pt: (n, 0, 0)),
                pl.BlockSpec((SEQ_GROUP, width), lambda n, pt: (n // SEQ_GROUP, 0)),
                pl.BlockSpec((SEQ_GROUP, KV_ROW), lambda n, pt: (n // SEQ_GROUP, 0))],
            out_specs=pl.BlockSpec((1, DSA_HEADS, HEAD_DIM), lambda n, pt: (n, 0, 0))),
        out_shape=jax.ShapeDtypeStruct((n_seq, DSA_HEADS, HEAD_DIM), BF16),
        compiler_params=_cparams(("arbitrary",)),
        name="dsa_sample",
    )(page_table, *([kv_pool] * n_pages), q, keep, kv_new)


def _nsa_sample_kernel(pt_ref, *refs, n_pages, page, n_cmp):
    slc_pages = refs[:n_pages]
    q_ref, gate_ref, ckv_ref, win_ref, slcnew_ref, winnew_ref, winslab_ref, o_ref, wout_ref = refs[n_pages:]
    n = pl.program_id(0)
    past = n_pages * page
    heads = NSA_HEADS
    rep = NSA_HEADS // NSA_KV_GROUPS
    slopes = _slopes(NSA_HEADS)
    slope_col = jnp.concatenate([jnp.full((1, 1), s, F32) for s in slopes], axis=0)
    q = (q_ref[0] * SCALE).astype(BF16)
    q32 = q.astype(F32)
    gates = jax.nn.sigmoid(gate_ref[0])
    n_cmp_pad = ckv_ref.shape[1]
    total = past + 1
    n_slc = -(-total // SLC_BLOCK)
    n_keep = min(SLC_TOPN, n_slc)
    slc_pad = -(-n_slc // LANES) * LANES

    kc = lax.broadcasted_iota(I32, (heads, n_cmp_pad), 1)
    dist_c_i = past - (kc * CMP_STRIDE + (CMP_BLOCK - 1))
    valid_c = (dist_c_i >= 0) & (kc < n_cmp)
    s_g = [_dot_nt(q, ckv_ref[0, :, g * 256:g * 256 + 128].astype(BF16)) for g in range(NSA_KV_GROUPS)]
    s = jnp.where(valid_c, _group_rows(s_g, rep) - slope_col * dist_c_i.astype(F32), NEG_INF)
    m = jnp.max(s, axis=1, keepdims=True)
    p = jnp.where(valid_c, jnp.exp(s - m), 0.0)
    l = jnp.sum(p, axis=1, keepdims=True)
    p = p * (1.0 / jnp.where(l > 0.0, l, 1.0))
    pb = p.astype(BF16)
    o_c = _group_rows([_dot(pb, ckv_ref[0, :, g * 256 + 128:g * 256 + 256].astype(BF16))
                       for g in range(NSA_KV_GROUPS)], rep)
    same_group = (lax.broadcasted_iota(I32, (heads, heads), 0) // rep
                  == lax.broadcasted_iota(I32, (heads, heads), 1) // rep).astype(F32)
    imp = _dot_hi(same_group, p)
    score = _dot_hi(imp, _cover(n_cmp_pad, slc_pad))
    sel = _select_blocks(score, jnp.full((heads, 1), past, I32), n_slc, n_keep)
    expand = (lax.broadcasted_iota(I32, (slc_pad, past), 1) // SLC_BLOCK
              == lax.broadcasted_iota(I32, (slc_pad, past), 0)).astype(BF16)
    valid_s = _dot(sel.astype(BF16), expand) > 0.5
    new_blk = past // SLC_BLOCK
    valid_s_new = sel[:, new_blk:new_blk + 1] > 0.5

    def attend_with_new(s_past, valid_past, s_new, valid_new, dist_past, v_of, v_new_of):
        s_p = jnp.where(valid_past, s_past - slope_col * dist_past, NEG_INF)
        s_n = jnp.where(valid_new, s_new, NEG_INF)
        mm = jnp.maximum(jnp.max(s_p, axis=1, keepdims=True), s_n)
        pp = jnp.where(valid_past, jnp.exp(s_p - mm), 0.0)
        pn = jnp.where(valid_new, jnp.exp(s_n - mm), 0.0)
        ll = jnp.sum(pp, axis=1, keepdims=True) + pn
        inv = 1.0 / jnp.where(ll > 0.0, ll, 1.0)
        ppb = pp.astype(BF16)
        pn = pn.astype(BF16).astype(F32)
        outs = [v_of(ppb, g) + pn * v_new_of(g) for g in range(NSA_KV_GROUPS)]
        return _group_rows(outs, rep) * inv

    slcnew = slcnew_ref[pl.ds(n, 1), :]
    s_g, s_new_g = [], []
    for g in range(NSA_KV_GROUPS):
        s_g.append(jnp.concatenate(
            [_dot_nt(q, _slab(pg, page, g, 0).astype(BF16)) for pg in slc_pages], axis=1))
        s_new_g.append(jnp.sum(q32 * slcnew[:, g * 256:g * 256 + 128], axis=1, keepdims=True))
    dist_s = (past - lax.broadcasted_iota(I32, (heads, past), 1)).astype(F32)

    def slc_v(ppb, g):
        acc = jnp.zeros((heads, HEAD_DIM), F32)
        for i, pg in enumerate(slc_pages):
            acc = acc + _dot(ppb[:, i * page:(i + 1) * page], _slab(pg, page, g, 1).astype(BF16))
        return acc

    o_s = attend_with_new(_group_rows(s_g, rep), valid_s, _group_rows(s_new_g, rep), valid_s_new, dist_s, slc_v,
                          lambda g: slcnew[:, g * 256 + 128:g * 256 + 256])

    n_buf = win_ref.shape[0] // KV_SLABS
    winnew = winnew_ref[pl.ds(n, 1), :]
    dist_w_i = n_buf - lax.broadcasted_iota(I32, (heads, n_buf), 1)
    valid_w = dist_w_i <= WINDOW
    s_g = [_dot_nt(q, _slab(win_ref, n_buf, g, 0).astype(BF16)) for g in range(NSA_KV_GROUPS)]
    s_new_g = [jnp.sum(q32 * winnew[:, g * 256:g * 256 + 128], axis=1, keepdims=True)
               for g in range(NSA_KV_GROUPS)]
    o_w = attend_with_new(_group_rows(s_g, rep), valid_w, _group_rows(s_new_g, rep),
                          jnp.full((heads, 1), True), dist_w_i.astype(F32),
                          lambda ppb, g: _dot(ppb, _slab(win_ref, n_buf, g, 1).astype(BF16)),
                          lambda g: winnew[:, g * 256 + 128:g * 256 + 256])

    o_ref[0] = (gates[:, 0:1] * o_c + gates[:, 1:2] * o_s + gates[:, 2:3] * o_w).astype(BF16)

    keep_rows = (n_buf - 1) * KV_SLABS
    wout_ref[0:keep_rows, :] = win_ref[KV_SLABS:n_buf * KV_SLABS, :]
    wout_ref[keep_rows:keep_rows + KV_SLABS, :] = winslab_ref[pl.ds(pl.multiple_of(n * KV_SLABS, KV_SLABS), KV_SLABS), :]


def _nsa_sample(slc_pool, page_table, page, q, gate, ckv, win_state, slc_new, win_new, win_new_slab):
    n_seq, n_pages = page_table.shape
    n_cmp_pad = ckv.shape[1]
    n_cmp = n_pages * page // CMP_STRIDE - CMP_BLOCK // CMP_STRIDE + 1
    n_buf = win_state.shape[0] // (n_seq * KV_SLABS)
    slc_specs = _page_specs(n_pages, page * KV_SLABS, HEAD_DIM, lambda n, pt, p: pt[n, p])
    return pl.pallas_call(
        functools.partial(_nsa_sample_kernel, n_pages=n_pages, page=page, n_cmp=n_cmp),
        grid_spec=pltpu.PrefetchScalarGridSpec(
            num_scalar_prefetch=1,
            grid=(n_seq,),
            in_specs=slc_specs + [
                pl.BlockSpec((1, NSA_HEADS, HEAD_DIM), lambda n, pt: (n, 0, 0)),
                pl.BlockSpec((1, NSA_HEADS, 3), lambda n, pt: (n, 0, 0)),
                pl.BlockSpec((1, n_cmp_pad, KV_ROW), lambda n, pt: (n, 0, 0)),
                pl.BlockSpec((n_buf * KV_SLABS, HEAD_DIM), lambda n, pt: (n, 0)),
                pl.BlockSpec((n_seq, KV_ROW), lambda n, pt: (0, 0)),
                pl.BlockSpec((n_seq, KV_ROW), lambda n, pt: (0, 0)),
                pl.BlockSpec((n_seq * KV_SLABS, HEAD_DIM), lambda n, pt: (0, 0))],
            out_specs=[pl.BlockSpec((1, NSA_HEADS, HEAD_DIM), lambda n, pt: (n, 0, 0)),
                       pl.BlockSpec((n_buf * KV_SLABS, HEAD_DIM), lambda n, pt: (n, 0))]),
        out_shape=[jax.ShapeDtypeStruct((n_seq, NSA_HEADS, HEAD_DIM), BF16),
                   jax.ShapeDtypeStruct(win_state.shape, F32)],
        compiler_params=_cparams(("arbitrary",)),
        name="nsa_sample",
    )(page_table, *([slc_pool] * n_pages), q, gate, ckv, win_state, slc_new, win_new, win_new_slab)


def _ln(v, g, b):
    mu = jnp.mean(v, axis=1, keepdims=True)
    d = v - mu
    var = jnp.mean(d * d, axis=1, keepdims=True)
    return d * lax.rsqrt(var + LN_EPS) * g + b


def _ln_router_kernel(xa_ref, mixa_ref, xb_ref, mixb_ref, g_ref, b_ref, wr_ref, br_ref, x1_ref, e_ref, gate_ref, *,
                      alpha, blocks_a):
    from_a = pl.program_id(0) < blocks_a
    pre = jnp.where(from_a, alpha * xa_ref[...] + mixa_ref[...], alpha * xb_ref[...] + mixb_ref[...])
    x1 = _ln(pre, g_ref[...], b_ref[...])
    x1_ref[...] = x1
    logits = _dot_hi(x1, wr_ref[...]) + br_ref[...]
    rows = logits.shape[0]
    lane = lax.broadcasted_iota(I32, (rows, LANES), 1)
    big = jnp.int32(LANES)
    is_grp = lane < N_GROUPS
    lg = jnp.where(is_grp, logits, -jnp.inf)
    mg = jnp.max(lg, axis=1, keepdims=True)
    grp = jnp.min(jnp.where(lg == mg, lane, big), axis=1, keepdims=True)
    p_grp = 1.0 / jnp.sum(jnp.where(is_grp, jnp.exp(lg - mg), 0.0), axis=1, keepdims=True)
    ex = lane - N_GROUPS
    in_grp = (ex >= 0) & (ex < N_EXPERTS) & (ex // EXPERTS_PER_GROUP == grp)
    le = jnp.where(in_grp, logits, -jnp.inf)
    m1 = jnp.max(le, axis=1, keepdims=True)
    e1 = jnp.min(jnp.where(le == m1, ex, big), axis=1, keepdims=True)
    le2 = jnp.where(ex == e1, -jnp.inf, le)
    m2 = jnp.max(le2, axis=1, keepdims=True)
    e2 = jnp.min(jnp.where(le2 == m2, ex, big), axis=1, keepdims=True)
    z = jnp.sum(jnp.where(in_grp, jnp.exp(le - m1), 0.0), axis=1, keepdims=True)
    p1 = 1.0 / z
    p2 = jnp.exp(m2 - m1) / z
    g1 = p_grp * p1 / (p1 + p2)
    g2 = p_grp * p2 / (p1 + p2)
    e_ref[...] = jnp.where(lane == 0, e1, jnp.where(lane == 1, e2, 0))
    gate_ref[...] = jnp.where(lane == 0, g1, jnp.where(lane == 1, g2, 0.0))


def _ln_router(xa, mixa, xb, mixb, g, b, wr, br, alpha):
    (ma, d), mb = xa.shape, xb.shape[0]
    tm = _pick(mb, (128, 64, 32, 16, 8))
    assert ma % tm == 0 and mb % tm == 0
    blocks_a, blocks_b = ma // tm, mb // tm
    seg_a = lambda i: (jnp.minimum(i, blocks_a - 1), 0)
    seg_b = lambda i: (jnp.maximum(i - blocks_a, 0), 0)
    row = lambda i: (i, 0)
    fixed = lambda i: (0, 0)
    m = ma + mb
    return pl.pallas_call(
        functools.partial(_ln_router_kernel, alpha=alpha, blocks_a=blocks_a),
        grid=(blocks_a + blocks_b,),
        in_specs=[pl.BlockSpec((tm, d), seg_a), pl.BlockSpec((tm, d), seg_a),
                  pl.BlockSpec((tm, d), seg_b), pl.BlockSpec((tm, d), seg_b),
                  pl.BlockSpec((1, d), fixed), pl.BlockSpec((1, d), fixed),
                  pl.BlockSpec((d, LANES), fixed), pl.BlockSpec((1, LANES), fixed)],
        out_specs=[pl.BlockSpec((tm, d), row), pl.BlockSpec((tm, LANES), row), pl.BlockSpec((tm, LANES), row)],
        out_shape=[jax.ShapeDtypeStruct((m, d), F32), jax.ShapeDtypeStruct((m, LANES), I32),
                   jax.ShapeDtypeStruct((m, LANES), F32)],
        compiler_params=_cparams(("arbitrary",)),
        name="ln_router",
    )(xa, mixa, xb, mixb, g, b, wr, br)


DISPATCH_UNROLL = 8


def _dispatch_kernel(src_ref, rows_ref, x_ref, o_ref, buf_ref, sem):
    i = pl.program_id(0)
    n_rows = rows_ref[i]

    @pl.when(i == 0)
    def _():
        buf_ref[...] = jnp.zeros_like(buf_ref)

    def row_copy(r):
        tok = src_ref[i * MOE_ROWS + r]
        return pltpu.make_async_copy(x_ref.at[pl.ds(tok, 1)], buf_ref.at[pl.ds(r, 1)], sem)

    def trips(fn):
        def trip(t, carry):
            for u in range(DISPATCH_UNROLL):
                r = t * DISPATCH_UNROLL + u

                @pl.when(r < n_rows)
                def _():
                    fn(row_copy(r))

            return carry

        lax.fori_loop(0, (n_rows + DISPATCH_UNROLL - 1) // DISPATCH_UNROLL, trip, 0)

    trips(lambda cp: cp.start())
    trips(lambda cp: cp.wait())
    o_ref[...] = buf_ref[...].astype(BF16)


def _dispatch(src_tok, rows_in_block, x1, nb):
    d = x1.shape[1]
    return pl.pallas_call(
        _dispatch_kernel,
        grid_spec=pltpu.PrefetchScalarGridSpec(
            num_scalar_prefetch=2,
            grid=(nb,),
            in_specs=[pl.BlockSpec(memory_space=pl.ANY)],
            out_specs=pl.BlockSpec((MOE_ROWS, d), lambda i, s, u: (i, 0)),
            scratch_shapes=[pltpu.VMEM((MOE_ROWS, d), F32), pltpu.SemaphoreType.DMA(())]),
        out_shape=jax.ShapeDtypeStruct((nb * MOE_ROWS, d), BF16),
        compiler_params=_cparams(("arbitrary",)),
        name="moe_dispatch",
    )(src_tok, rows_in_block, x1)


def _expert_mm_kernel(run_ref, rune_ref, nruns_ref, nused_ref, x_ref, *refs, n_w, tn, epilogue):
    w_hbm = refs[:n_w]
    o_ref = refs[n_w]
    w_buf = refs[n_w + 1:2 * n_w + 1]
    w16 = refs[2 * n_w + 1:3 * n_w + 1]
    sem, state = refs[3 * n_w + 1], refs[3 * n_w + 2]
    j, i = pl.program_id(0), pl.program_id(1)
    n_runs = nruns_ref[0]
    total = pl.num_programs(0) * n_runs
    cur = j * n_runs + run_ref[i]

    def item_copies(item):
        e = rune_ref[item % n_runs]
        col = pl.multiple_of((item // n_runs) * tn, tn)
        slot = item % MOE_RING
        return [pltpu.make_async_copy(w_hbm[a].at[e, :, pl.ds(col, tn)], w_buf[a].at[slot], sem.at[a, slot])
                for a in range(n_w)]

    @pl.when((j == 0) & (i == 0))
    def _():
        state[0] = 0
        state[1] = -1

    def request(item, carry):
        for cp in item_copies(item):
            cp.start()
        return carry

    limit = jnp.minimum(cur + MOE_RING, total)
    lax.fori_loop(state[0], limit, request, 0)
    state[0] = jnp.maximum(state[0], limit)

    @pl.when(state[1] != cur)
    def _():
        for cp in item_copies(cur):
            cp.wait()
        slot = cur % MOE_RING
        for a in range(n_w):
            w16[a][...] = w_buf[a][slot].astype(BF16)
        state[1] = cur

    @pl.when(i < nused_ref[0])
    def _():
        x = x_ref[...]
        o_ref[...] = epilogue(*[_dot(x, w16[a][...]) for a in range(n_w)]).astype(o_ref.dtype)

    @pl.when(i >= nused_ref[0])
    def _():
        o_ref[...] = jnp.zeros_like(o_ref)


def _expert_mm(plan, x, weights, tn, epilogue, out_dtype, name):
    run_of_block, run_e, n_runs, n_used = plan
    nb = run_of_block.shape[0]
    kd, n = weights[0].shape[1], weights[0].shape[2]
    n_w = len(weights)
    return pl.pallas_call(
        functools.partial(_expert_mm_kernel, n_w=n_w, tn=tn, epilogue=epilogue),
        grid_spec=pltpu.PrefetchScalarGridSpec(
            num_scalar_prefetch=4,
            grid=(n // tn, nb),
            in_specs=[pl.BlockSpec((MOE_ROWS, kd), lambda j, i, *_: (i, 0))]
            + [pl.BlockSpec(memory_space=pl.ANY)] * n_w,
            out_specs=pl.BlockSpec((MOE_ROWS, tn), lambda j, i, *_: (i, j)),
            scratch_shapes=[pltpu.VMEM((MOE_RING, kd, tn), F32) for _ in range(n_w)]
            + [pltpu.VMEM((kd, tn), BF16) for _ in range(n_w)]
            + [pltpu.SemaphoreType.DMA((n_w, MOE_RING)), pltpu.SMEM((2,), I32)]),
        out_shape=jax.ShapeDtypeStruct((nb * MOE_ROWS, n), out_dtype),
        compiler_params=_cparams(("arbitrary", "arbitrary")),
        name=name,
    )(run_of_block, run_e, n_runs, n_used, x, *weights)


def _swiglu(a, u):
    return a * jax.nn.sigmoid(a) * u


def _combine_kernel(dest_ref, y_ref, x1_ref, gate_ref, g_ref, b_ref, o_ref, buf_ref, sem, *, tm, alpha, off):
    i = pl.program_id(0) + off

    def row_copy(r, k):
        slot = dest_ref[(i * tm + r) * EXPERT_TOPK + k]
        return pltpu.make_async_copy(y_ref.at[pl.ds(slot, 1)], buf_ref.at[k, pl.ds(r, 1)], sem)

    def start(r, carry):
        for k in range(EXPERT_TOPK):
            row_copy(r, k).start()
        return carry

    lax.fori_loop(0, tm, start, 0, unroll=4)

    def wait(r, carry):
        for k in range(EXPERT_TOPK):
            row_copy(r, k).wait()
        return carry

    lax.fori_loop(0, tm, wait, 0, unroll=4)
    gate = gate_ref[...]
    f = gate[:, 0:1] * buf_ref[0] + gate[:, 1:2] * buf_ref[1]
    o_ref[...] = _ln(alpha * x1_ref[...] + f, g_ref[...], b_ref[...])


def _combine(dest, ybuf, x1, gate, row_offset, m, g, b, alpha):
    d = x1.shape[1]
    tm = _pick(m, (128,))
    assert row_offset % tm == 0
    off = row_offset // tm
    return pl.pallas_call(
        functools.partial(_combine_kernel, tm=tm, alpha=alpha, off=off),
        grid_spec=pltpu.PrefetchScalarGridSpec(
            num_scalar_prefetch=1,
            grid=(m // tm,),
            in_specs=[pl.BlockSpec(memory_space=pl.ANY),
                      pl.BlockSpec((tm, d), lambda i, dd: (i + off, 0)),
                      pl.BlockSpec((tm, LANES), lambda i, dd: (i + off, 0)),
                      pl.BlockSpec((1, d), lambda i, dd: (0, 0)),
                      pl.BlockSpec((1, d), lambda i, dd: (0, 0))],
            out_specs=pl.BlockSpec((tm, d), lambda i, dd: (i, 0)),
            scratch_shapes=[pltpu.VMEM((EXPERT_TOPK, tm, d), F32), pltpu.SemaphoreType.DMA(())]),
        out_shape=jax.ShapeDtypeStruct((m, d), F32),
        compiler_params=_cparams(("arbitrary",)),
        name="moe_combine",
    )(dest, ybuf, x1, gate, g, b)


def _route_plan(e_all, nb):
    a_n = e_all.shape[0]
    onehot = (e_all[:, None] == jnp.arange(N_EXPERTS, dtype=I32)[None, :]).astype(I32)
    before = jnp.cumsum(onehot, axis=0) - onehot
    rank = jnp.sum(before * onehot, axis=1)
    counts = jnp.sum(onehot, axis=0)
    pad_counts = (counts + MOE_ROWS - 1) // MOE_ROWS * MOE_ROWS
    pad_ends = jnp.cumsum(pad_counts)
    pad_starts = pad_ends - pad_counts
    dest = (pad_starts[e_all] + rank).astype(I32)
    n_used = (pad_ends[-1] // MOE_ROWS).astype(I32)
    block_e = jnp.searchsorted(pad_ends, jnp.arange(nb, dtype=I32) * MOE_ROWS, side="right").astype(I32)
    owns = (counts > 0).astype(I32)
    run_id = jnp.cumsum(owns) - owns
    n_runs = jnp.sum(owns).astype(I32)
    last_e = jnp.max(jnp.where(counts > 0, jnp.arange(N_EXPERTS, dtype=I32), 0)).astype(I32)
    block_e = jnp.where(jnp.arange(nb, dtype=I32) < n_used, jnp.minimum(block_e, N_EXPERTS - 1), last_e)
    run_of_block = run_id[block_e].astype(I32)
    run_e = jnp.full((N_EXPERTS,), last_e, I32).at[jnp.where(counts > 0, run_id, N_EXPERTS)].set(
        jnp.arange(N_EXPERTS, dtype=I32), mode="drop")
    src_tok = jnp.zeros((nb * MOE_ROWS,), I32).at[dest].set(jnp.arange(a_n, dtype=I32) // EXPERT_TOPK)
    blk = jnp.arange(nb, dtype=I32)
    rows_in_block = jnp.where(
        blk < n_used, jnp.clip(counts[block_e] - (blk * MOE_ROWS - pad_starts[block_e]), 0, MOE_ROWS), 0).astype(I32)
    return dest, (run_of_block, run_e, n_runs.reshape(1), n_used.reshape(1)), src_tok, rows_in_block


def kernel(x_prompt, x_sample, cache_dsa_kv, cache_dsa_kidx, cache_nsa_cmp_kv, cache_nsa_slc_kv, state_nsa_win_kv, page_table, w_in, w_out, cmp_w1, cmp_w2, cmp_pe, ln1_g, ln1_b, w_router_group, b_router_group, w_router_expert, b_router_expert, w_gate, w_up, w_down, ln2_g, ln2_b):
    depth = w_in.shape[0]
    assert depth == 1 and x_sample.shape[1] == 1
    batch, seq, d_model = x_prompt.shape
    n_seq = x_sample.shape[0]
    n_pool, page = cache_dsa_kv.shape[1], cache_dsa_kv.shape[2]
    alpha = (2.0 * depth) ** 0.25
    m_p = batch * seq
    xp = x_prompt.reshape(m_p, d_model)
    xs = x_sample.reshape(n_seq, d_model)
    xp16, xs16 = xp.astype(BF16), xs.astype(BF16)

    w = w_in[0]
    t0, t1 = 4176, 9296
    rep_b = NSA_HEADS // NSA_KV_GROUPS
    w_head = w[:, :t0 + 48].astype(BF16)
    w_tail = w[:, t0:t1].astype(BF16)
    w_gates = jnp.pad(w[:, t1:].reshape(d_model, NSA_KV_GROUPS, 3 * rep_b),
                      ((0, 0), (0, 0), (0, LANES - 3 * rep_b))).reshape(d_model, NSA_KV_GROUPS * LANES).astype(BF16)
    c_dq, c_dk, c_dv, c_iq, c_ik = 0, 2048, 2560, 3072, 4096
    c_nq, c_ck, c_cv, c_sk, c_sv, c_wk, c_wv = 0, 2048, 2560, 3072, 3584, 4096, 4608

    def project(x16):
        qa = _matmul([x16], w_head, c_dq, Q_ROW)
        qi = _matmul([x16], w_head, c_iq, IDX_HEADS * IDX_DIM)
        hc = _matmul([x16], w_head, c_ik, LANES)
        qb = _matmul([x16], w_tail, c_nq, Q_ROW)
        gate = _matmul([x16], w_gates)
        kv = [_kv_proj(x16, w_head, c_dk, c_dv), _kv_proj(x16, w_tail, c_ck, c_cv),
              _kv_proj(x16, w_tail, c_sk, c_sv), _kv_proj(x16, w_tail, c_wk, c_wv)]
        return qa, qi, hc, qb, gate, kv

    qa_p, qi_p, hc_p, qb_p, gate_p, kv_p = project(xp16)
    qa_s, qi_s, hc_s, qb_s, gate_s, kv_s = project(xs16)

    w1r = cmp_w1[0].reshape(2, CMP_BLOCK // CMP_STRIDE, CMP_STRIDE, HEAD_DIM, HEAD_DIM)
    w1r = jnp.transpose(w1r, (0, 2, 3, 1, 4)).reshape(2, CMP_STRIDE * HEAD_DIM, 2 * HEAD_DIM)
    peb = _cmp_pe_bias(cmp_pe[0], cmp_w1[0])
    pages_per_seq = seq // page
    ident = jnp.arange(batch * pages_per_seq, dtype=I32).reshape(batch, pages_per_seq)
    ckv_p = _compress(kv_p[1][0], ident, page, w1r, cmp_w2[0], peb)
    ckv_s = _compress(cache_nsa_cmp_kv.reshape(-1, HEAD_DIM), page_table, page, w1r, cmp_w2[0], peb)

    oa_p = _dsa_prompt(qa_p, qi_p, hc_p, kv_p[0][1], batch, seq)
    ob_p = _nsa_prompt(qb_p, gate_p, ckv_p, kv_p[2][1], kv_p[3][1], batch, seq)

    keep_s = _dsa_sample_mask(jnp.swapaxes(cache_dsa_kidx[0], 1, 2), page_table,
                              qi_s.reshape(n_seq, IDX_HEADS, IDX_DIM),
                              hc_s[:, IDX_DIM:IDX_DIM + IDX_HEADS].reshape(n_seq, IDX_HEADS, 1), hc_s)
    oa_s = _dsa_sample(cache_dsa_kv.reshape(-1, HEAD_DIM), page_table, page,
                       qa_s.reshape(n_seq, DSA_HEADS, HEAD_DIM), keep_s, kv_s[0][1].astype(F32))
    ob_s, win_s = _nsa_sample(cache_nsa_slc_kv.reshape(-1, HEAD_DIM), page_table, page,
                              qb_s.reshape(n_seq, NSA_HEADS, HEAD_DIM),
                              gate_s.reshape(n_seq, NSA_KV_GROUPS, LANES)[:, :, :3 * rep_b].reshape(n_seq, NSA_HEADS, 3),
                              ckv_s,
                              state_nsa_win_kv.reshape(-1, HEAD_DIM), kv_s[2][1].astype(F32),
                              kv_s[3][1].astype(F32), kv_s[3][0])

    w_o = w_out[0].astype(BF16)
    wr = jnp.concatenate([w_router_group[0], w_router_expert[0],
                          jnp.zeros((d_model, LANES - N_GROUPS - N_EXPERTS), F32)], axis=1)
    br = jnp.concatenate([b_router_group[0], b_router_expert[0],
                          jnp.zeros((LANES - N_GROUPS - N_EXPERTS,), F32)]).reshape(1, LANES)
    g1, b1 = ln1_g[0].reshape(1, d_model), ln1_b[0].reshape(1, d_model)
    g2, b2 = ln2_g[0].reshape(1, d_model), ln2_b[0].reshape(1, d_model)
    x1, e_tok, gt_tok = _ln_router(
        xp, _matmul([oa_p, ob_p], w_o),
        xs, _matmul([oa_s.reshape(n_seq, Q_ROW), ob_s.reshape(n_seq, Q_ROW)], w_o), g1, b1, wr, br, alpha)

    e_all = e_tok[:, :EXPERT_TOPK].reshape(-1)
    a_n = e_all.shape[0]
    nb = -(-(a_n + N_EXPERTS * (MOE_ROWS - 1)) // MOE_ROWS)
    dest, plan, src_tok, rows_in_block = _route_plan(e_all, nb)
    xbuf = _dispatch(src_tok, rows_in_block, x1, nb)
    d_expert = w_gate.shape[3]
    hid = _expert_mm(plan, xbuf, [w_gate[0], w_up[0]], _pick(d_expert, (MOE_CHUNK, LANES)), _swiglu, BF16,
                     "moe_gate_up")
    ybuf = _expert_mm(plan, hid, [w_down[0]], _pick(d_model, (2048, 1024, 512, 256, LANES)), lambda y: y, F32,
                      "moe_down")
    y_p = _combine(dest, ybuf, x1, gt_tok, 0, m_p, g2, b2, alpha)
    y_s = _combine(dest, ybuf, x1, gt_tok, m_p, n_seq, g2, b2, alpha)

    def state(slab, lead):
        return slab.reshape((1,) + lead + (NSA_KV_GROUPS, 2, HEAD_DIM))

    n_win = min(WINDOW, seq)
    win_p = state(kv_p[3][0], (batch, seq))[:, :, seq - n_win:]
    return (y_p.reshape(batch, seq, d_model), y_s.reshape(n_seq, 1, d_model),
            state(kv_p[0][0], (batch, seq)), state(kv_s[0][0], (n_seq, 1)),
            hc_p[:, :IDX_DIM].reshape(1, batch, seq, IDX_DIM), hc_s[:, :IDX_DIM].reshape(1, n_seq, 1, IDX_DIM),
            state(kv_p[1][0], (batch, seq)), state(kv_s[1][0], (n_seq, 1)),
            state(kv_p[2][0], (batch, seq)), state(kv_s[2][0], (n_seq, 1)),
            win_p, state(win_s, (n_seq, state_nsa_win_kv.shape[2])))
```

```python
import functools

import jax
import jax.numpy as jnp
from jax import lax
from jax.experimental import pallas as pl
from jax.experimental.pallas import tpu as pltpu

F32 = jnp.float32
BF16 = jnp.bfloat16
I32 = jnp.int32

HEAD_DIM = 128
DSA_HEADS = 16
DSA_KV_HEADS = 4
IDX_HEADS = 16
IDX_DIM = 64
IDX_TOPK = 256
NSA_HEADS = 16
NSA_KV_GROUPS = 4
CMP_STRIDE = 16
CMP_BLOCK = 32
SLC_BLOCK = 64
SLC_TOPN = 16
WINDOW = 512
N_GROUPS = 8
EXPERTS_PER_GROUP = 8
N_EXPERTS = N_GROUPS * EXPERTS_PER_GROUP
EXPERT_TOPK = 2
LN_EPS = 1e-5
NEG_INF = -1e30
FORCE_SCORE = 1e9

KV_ROW = 2 * DSA_KV_HEADS * HEAD_DIM
KV_SLABS = 2 * DSA_KV_HEADS
KV_HALF = DSA_KV_HEADS * HEAD_DIM
Q_ROW = DSA_HEADS * HEAD_DIM
Q_TILE = 128
KEY_SPAN = 512
SEQ_GROUP = 8
LANES = 128
MOE_ROWS = 256
MOE_RING = 4
MOE_CHUNK = 256
VMEM_LIMIT = 56 * 1024 * 1024
SCALE = HEAD_DIM ** -0.5
LOG2E = 1.4426950408889634
MASKED_DIST = 1e32
INT_MIN = -(2 ** 31)


def _slopes(n):
    return [2.0 ** (-8.0 * i / n) for i in range(1, n + 1)]


def _cparams(sem):
    return pltpu.CompilerParams(dimension_semantics=sem, vmem_limit_bytes=VMEM_LIMIT)


def _dot(a, b):
    return jnp.dot(a, b, preferred_element_type=F32)


def _dot_nt(a, b):
    return lax.dot_general(a, b, (((1,), (1,)), ((), ())), preferred_element_type=F32)


def _dot_hi(a, b):
    return jnp.dot(a, b, preferred_element_type=F32, precision=lax.Precision.HIGHEST)


def _pick(n, cands):
    for c in cands:
        if n % c == 0:
            return c
    return n


def _mm_kernel(*refs, n_lhs):
    x_refs, w_refs, o_ref = refs[:n_lhs], refs[n_lhs:2 * n_lhs], refs[2 * n_lhs]
    acc = _dot(x_refs[0][...], w_refs[0][...])
    for x_ref, w_ref in zip(x_refs[1:], w_refs[1:]):
        acc = acc + _dot(x_ref[...], w_ref[...])
    o_ref[...] = acc.astype(o_ref.dtype)


def _matmul(xs, w, n0=0, n=None):
    m = xs[0].shape[0]
    n = w.shape[1] if n is None else n
    kds = [x.shape[1] for x in xs]
    assert len(set(kds)) == 1 and sum(kds) == w.shape[0]
    kd = kds[0]
    tm = _pick(m, (512, 256, 128))
    tn = _pick(n, (1024, 512, 256, 128))
    assert n0 % tn == 0
    j0 = n0 // tn
    x_specs = [pl.BlockSpec((tm, kd), lambda j, i: (i, 0)) for _ in xs]
    w_specs = [pl.BlockSpec((kd, tn), functools.partial(lambda j, i, a: (a, j + j0), a=a)) for a in range(len(xs))]
    return pl.pallas_call(
        functools.partial(_mm_kernel, n_lhs=len(xs)),
        grid=(n // tn, m // tm),
        in_specs=x_specs + w_specs,
        out_specs=pl.BlockSpec((tm, tn), lambda j, i: (i, j)),
        out_shape=jax.ShapeDtypeStruct((m, n), F32),
        compiler_params=_cparams(("parallel", "parallel")),
        name="matmul",
    )(*xs, *([w] * len(xs)))


def _kv_proj_kernel(x_ref, wk_ref, wv_ref, slab_ref, tile_ref):
    x = x_ref[...]
    rows = x.shape[0]
    for c, w_ref in enumerate((wk_ref, wv_ref)):
        res = _dot(x, w_ref[...])
        for g in range(DSA_KV_HEADS):
            piece = res[:, g * HEAD_DIM:(g + 1) * HEAD_DIM]
            slab_ref[pl.ds(g * 2 + c, rows, stride=KV_SLABS), :] = piece
            tile_ref[:, (g * 2 + c) * HEAD_DIM:(g * 2 + c + 1) * HEAD_DIM] = piece.astype(BF16)


def _kv_proj(x, w, k_col, v_col):
    m, kd = x.shape
    tm = _pick(m, (512, 256, 128))
    assert k_col % KV_HALF == 0 and v_col % KV_HALF == 0
    kb, vb = k_col // KV_HALF, v_col // KV_HALF
    return pl.pallas_call(
        _kv_proj_kernel,
        grid=(m // tm,),
        in_specs=[pl.BlockSpec((tm, kd), lambda i: (i, 0)),
                  pl.BlockSpec((kd, KV_HALF), lambda i: (0, kb)),
                  pl.BlockSpec((kd, KV_HALF), lambda i: (0, vb))],
        out_specs=[pl.BlockSpec((tm * KV_SLABS, HEAD_DIM), lambda i: (i, 0)),
                   pl.BlockSpec((tm, KV_ROW), lambda i: (i, 0))],
        out_shape=[jax.ShapeDtypeStruct((m * KV_SLABS, HEAD_DIM), F32), jax.ShapeDtypeStruct((m, KV_ROW), BF16)],
        compiler_params=_cparams(("parallel",)),
        name="kv_proj",
    )(x, w, w)


def _slab(ref, n_tok, g, c):
    return ref[pl.ds(g * 2 + c, n_tok, stride=KV_SLABS), :]


def _page_specs(n_pages, rows, width, index):
    return [pl.BlockSpec((rows, width), functools.partial(lambda *a, p: (index(*a, p), 0), p=p))
            for p in range(n_pages)]


def _sort_key(x):
    b = pltpu.bitcast(x, I32)
    return jnp.where(b < 0, b ^ jnp.int32(0x7FFFFFFF), b)


def _topk_mask(sc, idx, k, idx_bits):
    key = _sort_key(sc)

    def count(m):
        return jnp.sum(m.astype(I32), axis=1, keepdims=True)

    t0 = jnp.where(count(key >= 0) >= k, jnp.int32(0), jnp.int32(INT_MIN))

    def vstep(i, t):
        cand = t | (jnp.int32(1) << (30 - i))
        return jnp.where(count(key >= cand) >= k, cand, t)

    thr = lax.fori_loop(0, 31, vstep, t0)
    gt = key > thr
    eq = key == thr
    need = k - count(gt)
    crowded = count(eq) > need
    n_steps = jnp.where(jnp.max(crowded.astype(I32)) > 0, idx_bits, 0)

    def istep(i, c):
        cand = c | (jnp.int32(1) << (idx_bits - 1 - i))
        return jnp.where(count(eq & (idx < cand)) < need, cand, c)

    cut = lax.fori_loop(0, n_steps, istep, jnp.zeros_like(thr))
    return gt | (eq & (jnp.logical_not(crowded) | (idx <= cut)))


def _stack_heads(ref, n_heads, scale):
    return jnp.concatenate([(ref[:, r * HEAD_DIM:(r + 1) * HEAD_DIM] * scale).astype(BF16) for r in range(n_heads)],
                           axis=0)


def _slope_stack(slopes):
    return jnp.concatenate([jnp.full((1, 1, 1), s, F32) for s in slopes], axis=0)


def _attend(q2, k, v, dm, slope2):
    n_heads = slope2.shape[0]
    rows, keys = dm.shape
    s = _dot_nt(q2, k).reshape(n_heads, rows, keys) - slope2 * dm[None]
    m = jnp.max(s, axis=2, keepdims=True)
    p = jnp.exp2(s - m)
    l = jnp.sum(p, axis=2, keepdims=True)
    o = _dot(p.reshape(n_heads * rows, keys).astype(BF16), v)
    return o * (1.0 / l).reshape(n_heads * rows, 1)


def _spans(seq):
    span = KEY_SPAN if seq % KEY_SPAN == 0 else seq
    return span, seq // span


def _group_slope(g, r, n_heads, n_groups):
    rep = n_heads // n_groups
    table = _slopes(n_heads)
    out = jnp.float32(table[r])
    for gg in range(1, n_groups):
        out = jnp.where(g == gg, jnp.float32(table[gg * rep + r]), out)
    return out


def _dsa_prompt_mask(c, qi_ref, cq_ref, ck_ref, dm_ref, kw, k_keep):
    t = c * Q_TILE + lax.broadcasted_iota(I32, (Q_TILE, 1), 0)
    kpos = lax.broadcasted_iota(I32, (Q_TILE, kw), 1)
    causal = kpos <= t
    ki = ck_ref[0:kw, 0:IDX_DIM].astype(BF16)
    wi = cq_ref[:, IDX_DIM:IDX_DIM + IDX_HEADS] * (IDX_HEADS ** -0.5)
    sc = jnp.zeros((Q_TILE, kw), F32)
    for h in range(IDX_HEADS):
        a = _dot_nt(qi_ref[:, h * IDX_DIM:(h + 1) * IDX_DIM].astype(BF16), ki)
        sc = sc + jnp.maximum(a, 0.0) * wi[:, h:h + 1]
    sc = jnp.where(causal, sc, NEG_INF)
    keep = _topk_mask(sc, kpos, k_keep, max(1, (kw - 1).bit_length()))
    dm_ref[:, 0:kw] = jnp.where(keep & causal, (t - kpos).astype(F32), MASKED_DIST)


def _dsa_prompt_kernel(q_ref, qi_ref, cq_ref, ck_ref, kv_ref, o_ref, dm_ref, *, seq, k_keep):
    c, g = pl.program_id(1), pl.program_id(2)
    rep = DSA_HEADS // DSA_KV_HEADS
    span, n_var = _spans(seq)
    for v in range(n_var):
        @pl.when((c * Q_TILE) // span == v)
        def _(kw=(v + 1) * span):
            @pl.when(g == 0)
            def _():
                _dsa_prompt_mask(c, qi_ref, cq_ref, ck_ref, dm_ref, kw, k_keep)

            dm = dm_ref[:, 0:kw]
            kg = kv_ref[0:kw, 0:HEAD_DIM]
            vg = kv_ref[0:kw, HEAD_DIM:2 * HEAD_DIM]
            slope2 = _slope_stack([_group_slope(g, r, DSA_HEADS, DSA_KV_HEADS) * LOG2E for r in range(rep)])
            o = _attend(_stack_heads(q_ref, rep, SCALE * LOG2E), kg, vg, dm, slope2)
            for r in range(rep):
                o_ref[:, r * HEAD_DIM:(r + 1) * HEAD_DIM] = o[r * Q_TILE:(r + 1) * Q_TILE].astype(BF16)


def _dsa_prompt(q, qi, h_c, kv, batch, seq):
    nc = seq // Q_TILE
    k_keep = min(IDX_TOPK, seq // 4)
    rep = DSA_HEADS // DSA_KV_HEADS
    tile = lambda b, c, g: (b * nc + c, 0)
    tile_g = lambda b, c, g: (b * nc + c, g)
    return pl.pallas_call(
        functools.partial(_dsa_prompt_kernel, seq=seq, k_keep=k_keep),
        grid=(batch, nc, DSA_KV_HEADS),
        in_specs=[pl.BlockSpec((Q_TILE, rep * HEAD_DIM), tile_g),
                  pl.BlockSpec((Q_TILE, IDX_HEADS * IDX_DIM), tile),
                  pl.BlockSpec((Q_TILE, LANES), tile),
                  pl.BlockSpec((seq, LANES), lambda b, c, g: (b, 0)),
                  pl.BlockSpec((seq, 2 * HEAD_DIM), lambda b, c, g: (b, g))],
        out_specs=pl.BlockSpec((Q_TILE, rep * HEAD_DIM), tile_g),
        out_shape=jax.ShapeDtypeStruct((batch * seq, Q_ROW), BF16),
        scratch_shapes=[pltpu.VMEM((Q_TILE, seq), F32)],
        compiler_params=_cparams(("parallel", "arbitrary", "arbitrary")),
        name="dsa_prompt",
    )(q, qi, h_c, h_c, kv)


def _peb_kernel(pe_ref, w1_ref, o_ref):
    for c in range(2):
        o_ref[c] = _dot_hi(pe_ref[c], w1_ref[c])


def _cmp_pe_bias(cmp_pe, cmp_w1):
    return pl.pallas_call(
        _peb_kernel,
        out_shape=jax.ShapeDtypeStruct((2, 1, HEAD_DIM), F32),
        name="cmp_pe_bias",
    )(cmp_pe.reshape(2, 1, CMP_BLOCK * HEAD_DIM), cmp_w1.reshape(2, CMP_BLOCK * HEAD_DIM, HEAD_DIM))


def _compress_kernel(pt_ref, *refs, n_pages, page):
    pages = refs[:n_pages]
    w1_ref, w2_ref, peb_ref, o_ref = refs[n_pages:]
    sub_per_page = page // CMP_STRIDE
    n_sub = n_pages * sub_per_page
    prow = lax.broadcasted_iota(I32, (page, page), 0)
    pcol = lax.broadcasted_iota(I32, (page, page), 1)
    regroup = (pcol == (prow % sub_per_page) * CMP_STRIDE + prow // sub_per_page).astype(BF16)
    for c in range(2):
        by_pos = [[_dot(regroup, _slab(p, page, g, c).astype(BF16)) for p in pages] for g in range(NSA_KV_GROUPS)]
        lhs = jnp.concatenate([
            jnp.concatenate([
                jnp.concatenate([by_pos[g][p][j * sub_per_page:(j + 1) * sub_per_page]
                                 for p in range(n_pages)], axis=0).astype(BF16)
                for j in range(CMP_STRIDE)], axis=1)
            for g in range(NSA_KV_GROUPS)], axis=0)
        part = _dot(lhs, w1_ref[c].astype(BF16))
        w2 = w2_ref[c].astype(BF16)
        for g in range(NSA_KV_GROUPS):
            col = g * 256 + c * 128
            pg = part[g * n_sub:(g + 1) * n_sub]
            nxt = pltpu.roll(pg[:, HEAD_DIM:], n_sub - 1, 0)
            hid = pg[:, :HEAD_DIM] + nxt + peb_ref[c]
            o_ref[0, :, col:col + HEAD_DIM] = _dot(jax.nn.gelu(hid).astype(BF16), w2)


def _compress(pool, page_table, page, w1r, w2, peb):
    n_seq, n_pages = page_table.shape
    n_sub = n_pages * page // CMP_STRIDE
    page_specs = _page_specs(n_pages, page * KV_SLABS, HEAD_DIM, lambda n, pt, p: pt[n, p])
    return pl.pallas_call(
        functools.partial(_compress_kernel, n_pages=n_pages, page=page),
        grid_spec=pltpu.PrefetchScalarGridSpec(
            num_scalar_prefetch=1,
            grid=(n_seq,),
            in_specs=page_specs + [
                pl.BlockSpec((2, CMP_STRIDE * HEAD_DIM, 2 * HEAD_DIM), lambda n, pt: (0, 0, 0)),
                pl.BlockSpec((2, HEAD_DIM, HEAD_DIM), lambda n, pt: (0, 0, 0)),
                pl.BlockSpec((2, 1, HEAD_DIM), lambda n, pt: (0, 0, 0))],
            out_specs=pl.BlockSpec((1, n_sub, KV_ROW), lambda n, pt: (n, 0, 0))),
        out_shape=jax.ShapeDtypeStruct((n_seq, n_sub, KV_ROW), F32),
        compiler_params=_cparams(("arbitrary",)),
        name="nsa_compress",
    )(page_table, *([pool] * n_pages), w1r, w2, peb)


def _cover(n_cmp_pad, n_slc_pad):
    cs = lax.broadcasted_iota(I32, (n_cmp_pad, n_slc_pad), 0) * CMP_STRIDE
    bs = lax.broadcasted_iota(I32, (n_cmp_pad, n_slc_pad), 1) * SLC_BLOCK
    return ((cs < bs + SLC_BLOCK) & (cs + CMP_BLOCK > bs)).astype(F32)


def _select_blocks(score, t, n_slc, n_keep):
    rows, width = score.shape
    j = lax.broadcasted_iota(I32, (rows, width), 1)
    cur = t // SLC_BLOCK
    forced = (j == 0) | (j == cur) | (j == cur - 1)
    admissible = j * SLC_BLOCK <= t
    score = jnp.where(forced, FORCE_SCORE, jnp.where(admissible, score, NEG_INF))
    rank = jnp.zeros((rows, width), I32)
    for k in range(n_slc):
        sk = score[:, k:k + 1]
        ahead = (sk > score) | ((sk == score) & (j > k))
        rank = rank + ahead.astype(I32)
    return ((rank < n_keep) & (j < n_slc)).astype(F32)


def _select_blocks_t(score_t, t_row, n_keep):
    n_slc, width = score_t.shape
    j = lax.broadcasted_iota(I32, (n_slc, width), 0)
    cur = t_row // SLC_BLOCK
    forced = (j == 0) | (j == cur) | (j == cur - 1)
    admissible = j * SLC_BLOCK <= t_row
    score_t = jnp.where(forced, FORCE_SCORE, jnp.where(admissible, score_t, NEG_INF))
    rank = jnp.zeros((n_slc, width), I32)
    for k in range(n_slc):
        sk = score_t[k:k + 1, :]
        ahead = (sk > score_t) | ((sk == score_t) & (j > k))
        rank = rank + ahead.astype(I32)
    return (rank < n_keep).astype(F32)


def _nsa_prompt_body(c, g, q_ref, gate_ref, ckv_ref, slc_ref, win_ref, o_ref, kw, seq, n_cmp, win_keys):
    t = c * Q_TILE + lax.broadcasted_iota(I32, (Q_TILE, 1), 0)
    t_row = c * Q_TILE + lax.broadcasted_iota(I32, (1, Q_TILE), 1)
    rep = NSA_HEADS // NSA_KV_GROUPS
    slope2 = _slope_stack([_group_slope(g, r, NSA_HEADS, NSA_KV_GROUPS) * LOG2E for r in range(rep)])
    n_cmp_pad = ckv_ref.shape[1]
    n_slc = seq // SLC_BLOCK
    n_keep = min(SLC_TOPN, n_slc)

    kc = lax.broadcasted_iota(I32, (Q_TILE, n_cmp_pad), 1)
    dist_c_i = t - (kc * CMP_STRIDE + (CMP_BLOCK - 1))
    valid_c = (dist_c_i >= 0) & (kc < n_cmp)
    dist_c = dist_c_i.astype(F32)
    cover_bs = lax.broadcasted_iota(I32, (n_slc, n_cmp_pad), 0) * SLC_BLOCK
    cover_cs = lax.broadcasted_iota(I32, (n_slc, n_cmp_pad), 1) * CMP_STRIDE
    cover_t = ((cover_cs < cover_bs + SLC_BLOCK) & (cover_cs + CMP_BLOCK > cover_bs)).astype(F32)
    slc_rows = -(-n_slc // LANES) * LANES
    expand = (lax.broadcasted_iota(I32, (slc_rows, kw), 1) // SLC_BLOCK
              == lax.broadcasted_iota(I32, (slc_rows, kw), 0)).astype(BF16)

    kpos = lax.broadcasted_iota(I32, (Q_TILE, kw), 1)
    causal = kpos <= t
    dist_s = (t - kpos).astype(F32)

    w0 = pl.multiple_of(jnp.maximum(c * Q_TILE + Q_TILE - win_keys, 0), Q_TILE)
    wpos = w0 + lax.broadcasted_iota(I32, (Q_TILE, win_keys), 1)
    dist_w_i = t - wpos
    dm_w = jnp.where((dist_w_i >= 0) & (dist_w_i <= WINDOW), dist_w_i.astype(F32), MASKED_DIST)

    gates = jax.nn.sigmoid(gate_ref[:, 0:3 * rep])

    ck = ckv_ref[0, :, 0:HEAD_DIM].astype(BF16)
    cv = ckv_ref[0, :, HEAD_DIM:2 * HEAD_DIM].astype(BF16)
    q2 = _stack_heads(q_ref, rep, SCALE * LOG2E)
    s = jnp.where(valid_c[None], _dot_nt(q2, ck).reshape(rep, Q_TILE, n_cmp_pad) - slope2 * dist_c[None], NEG_INF)
    m = jnp.max(s, axis=2, keepdims=True)
    p = jnp.where(valid_c[None], jnp.exp2(s - m), 0.0)
    l = jnp.sum(p, axis=2, keepdims=True)
    p = p * (1.0 / jnp.where(l > 0.0, l, 1.0))
    imp = jnp.sum(p, axis=0)
    o_c = _dot(p.reshape(rep * Q_TILE, n_cmp_pad).astype(BF16), cv)
    sel_t = _select_blocks_t(_dot_hi(cover_t, imp.T), t_row, n_keep)
    sel_t = jnp.concatenate([sel_t, jnp.zeros((slc_rows - n_slc, Q_TILE), F32)], axis=0)
    picked = _dot(sel_t.T.astype(BF16), expand) > 0.5
    dm_s = jnp.where(picked & causal, dist_s, MASKED_DIST)
    sk = slc_ref[0:kw, 0:HEAD_DIM]
    sv = slc_ref[0:kw, HEAD_DIM:2 * HEAD_DIM]
    wk = win_ref[pl.ds(w0, win_keys), 0:HEAD_DIM]
    wv = win_ref[pl.ds(w0, win_keys), HEAD_DIM:2 * HEAD_DIM]
    o_s = _attend(q2, sk, sv, dm_s, slope2)
    o_w = _attend(q2, wk, wv, dm_w, slope2)
    gate_col = [jnp.concatenate([gates[:, 3 * r + k:3 * r + k + 1] for r in range(rep)], axis=0) for k in range(3)]
    o = gate_col[0] * o_c + gate_col[1] * o_s + gate_col[2] * o_w
    for r in range(rep):
        o_ref[:, r * HEAD_DIM:(r + 1) * HEAD_DIM] = o[r * Q_TILE:(r + 1) * Q_TILE].astype(BF16)


def _nsa_prompt_kernel(q_ref, gate_ref, ckv_ref, slc_ref, win_ref, o_ref, *, seq, n_cmp, win_keys):
    c, g = pl.program_id(1), pl.program_id(2)
    span, n_var = _spans(seq)
    for v in range(n_var):
        @pl.when((c * Q_TILE) // span == v)
        def _(kw=(v + 1) * span):
            _nsa_prompt_body(c, g, q_ref, gate_ref, ckv_ref, slc_ref, win_ref, o_ref, kw, seq, n_cmp, win_keys)


def _nsa_prompt(q, gate, ckv, slc, win, batch, seq):
    nc = seq // Q_TILE
    n_cmp_pad = ckv.shape[1]
    n_cmp = seq // CMP_STRIDE - CMP_BLOCK // CMP_STRIDE + 1
    win_keys = min(seq, WINDOW + Q_TILE)
    rep = NSA_HEADS // NSA_KV_GROUPS
    tile_g = lambda b, c, g: (b * nc + c, g)
    return pl.pallas_call(
        functools.partial(_nsa_prompt_kernel, seq=seq, n_cmp=n_cmp, win_keys=win_keys),
        grid=(batch, nc, NSA_KV_GROUPS),
        in_specs=[pl.BlockSpec((Q_TILE, rep * HEAD_DIM), tile_g),
                  pl.BlockSpec((Q_TILE, LANES), tile_g),
                  pl.BlockSpec((1, n_cmp_pad, 2 * HEAD_DIM), lambda b, c, g: (b, 0, g)),
                  pl.BlockSpec((seq, 2 * HEAD_DIM), lambda b, c, g: (b, g)),
                  pl.BlockSpec((seq, 2 * HEAD_DIM), lambda b, c, g: (b, g))],
        out_specs=pl.BlockSpec((Q_TILE, rep * HEAD_DIM), tile_g),
        out_shape=jax.ShapeDtypeStruct((batch * seq, Q_ROW), BF16),
        compiler_params=_cparams(("parallel", "arbitrary", "arbitrary")),
        name="nsa_prompt",
    )(q, gate, ckv, slc, win)


def _group_rows(per_group, rep):
    row = lax.broadcasted_iota(I32, per_group[0].shape, 0) // rep
    out = per_group[0]
    for g in range(1, len(per_group)):
        out = jnp.where(row == g, per_group[g], out)
    return out


def _dsa_sample_mask_kernel(pt_ref, *refs, n_pages, page, k_keep):
    n_ki = SEQ_GROUP * n_pages
    ki_pages = refs[:n_ki]
    qi_ref, wi_ref, cnew_ref, keep_ref = refs[n_ki:]
    past = n_pages * page
    width = past + LANES
    rows = []
    for s in range(SEQ_GROUP):
        qi = qi_ref[s].astype(BF16)
        wi = wi_ref[s] * (IDX_HEADS ** -0.5)
        knew = cnew_ref[pl.ds(s, 1), 0:IDX_DIM]
        a_past = jnp.concatenate([_dot(qi, ki_pages[s * n_pages + p][0].astype(BF16))
                                  for p in range(n_pages)], axis=1)
        a_new = _dot_nt(qi, jnp.broadcast_to(knew, (8, IDX_DIM)).astype(BF16))[:, 0:1]
        sc_past = jnp.sum(jnp.maximum(a_past, 0.0) * wi, axis=0, keepdims=True)
        sc_new = jnp.sum(jnp.maximum(a_new, 0.0) * wi, axis=0, keepdims=True)
        rows.append(jnp.concatenate([sc_past, jnp.broadcast_to(sc_new, (1, LANES))], axis=1))
    col = lax.broadcasted_iota(I32, (SEQ_GROUP, width), 1)
    sc = jnp.where(col <= past, jnp.concatenate(rows, axis=0), NEG_INF)
    keep = _topk_mask(sc, col, k_keep, max(1, (width - 1).bit_length())) & (col <= past)
    keep_ref[...] = keep.astype(F32)


def _dsa_sample_mask(kidx_pool_t, page_table, qi, wi, h_c):
    n_seq, n_pages = page_table.shape
    assert n_seq % SEQ_GROUP == 0
    page = kidx_pool_t.shape[2]
    total = n_pages * page + 1
    k_keep = min(IDX_TOPK, total // 4)
    width = n_pages * page + LANES
    ki_specs = [pl.BlockSpec((1, IDX_DIM, page),
                             functools.partial(lambda i, pt, s, p: (pt[i * SEQ_GROUP + s, p], 0, 0), s=s, p=p))
                for s in range(SEQ_GROUP) for p in range(n_pages)]
    return pl.pallas_call(
        functools.partial(_dsa_sample_mask_kernel, n_pages=n_pages, page=page, k_keep=k_keep),
        grid_spec=pltpu.PrefetchScalarGridSpec(
            num_scalar_prefetch=1,
            grid=(n_seq // SEQ_GROUP,),
            in_specs=ki_specs + [
                pl.BlockSpec((SEQ_GROUP, IDX_HEADS, IDX_DIM), lambda i, pt: (i, 0, 0)),
                pl.BlockSpec((SEQ_GROUP, IDX_HEADS, 1), lambda i, pt: (i, 0, 0)),
                pl.BlockSpec((SEQ_GROUP, LANES), lambda i, pt: (i, 0))],
            out_specs=pl.BlockSpec((SEQ_GROUP, width), lambda i, pt: (i, 0))),
        out_shape=jax.ShapeDtypeStruct((n_seq, width), F32),
        compiler_params=_cparams(("arbitrary",)),
        name="dsa_sample_mask",
    )(page_table, *([kidx_pool_t] * (SEQ_GROUP * n_pages)), qi, wi, h_c)


def _dsa_sample_kernel(pt_ref, *refs, n_pages, page):
    kv_pages = refs[:n_pages]
    q_ref, keep_ref, kvnew_ref, o_ref = refs[n_pages:]
    r = pl.program_id(0) % SEQ_GROUP
    past = n_pages * page
    rep = DSA_HEADS // DSA_KV_HEADS
    slopes = _slopes(DSA_HEADS)

    keep = keep_ref[pl.ds(r, 1), :] > 0.5
    keep_past = jnp.broadcast_to(keep[:, 0:past], (DSA_HEADS, past))
    keep_new = jnp.broadcast_to(keep[:, past:past + 1], (DSA_HEADS, 1))

    q = (q_ref[0] * SCALE).astype(BF16)
    q32 = q.astype(F32)
    kvnew = kvnew_ref[pl.ds(r, 1), :]
    slope_col = jnp.concatenate([jnp.full((1, 1), s, F32) for s in slopes], axis=0)
    dist = (past - lax.broadcasted_iota(I32, (DSA_HEADS, past), 1)).astype(F32)
    s_g, s_new_g = [], []
    for g in range(DSA_KV_HEADS):
        s_g.append(jnp.concatenate(
            [_dot_nt(q, _slab(p, page, g, 0).astype(BF16)) for p in kv_pages], axis=1))
        s_new_g.append(jnp.sum(q32 * kvnew[:, g * 256:g * 256 + 128], axis=1, keepdims=True))
    s = jnp.where(keep_past, _group_rows(s_g, rep) - slope_col * dist, NEG_INF)
    s_new = jnp.where(keep_new, _group_rows(s_new_g, rep), NEG_INF)
    m = jnp.maximum(jnp.max(s, axis=1, keepdims=True), s_new)
    p = jnp.where(keep_past, jnp.exp(s - m), 0.0)
    p_new = jnp.where(keep_new, jnp.exp(s_new - m), 0.0)
    l = jnp.sum(p, axis=1, keepdims=True) + p_new
    inv = 1.0 / jnp.where(l > 0.0, l, 1.0)
    pb = p.astype(BF16)
    p_new = p_new.astype(BF16).astype(F32)
    o_g = []
    for g in range(DSA_KV_HEADS):
        acc = p_new * kvnew[:, g * 256 + 128:g * 256 + 256]
        for i, pg in enumerate(kv_pages):
            acc = acc + _dot(pb[:, i * page:(i + 1) * page], _slab(pg, page, g, 1).astype(BF16))
        o_g.append(acc)
    o_ref[0] = (_group_rows(o_g, rep) * inv).astype(BF16)


def _dsa_sample(kv_pool, page_table, page, q, keep, kv_new):
    n_seq, n_pages = page_table.shape
    width = keep.shape[1]
    kv_specs = _page_specs(n_pages, page * KV_SLABS, HEAD_DIM, lambda n, pt, p: pt[n, p])
    return pl.pallas_call(
        functools.partial(_dsa_sample_kernel, n_pages=n_pages, page=page),
        grid_spec=pltpu.PrefetchScalarGridSpec(
            num_scalar_prefetch=1,
            grid=(n_seq,),
            in_specs=kv_specs + [
                pl.BlockSpec((1, DSA_HEADS, HEAD_DIM), lambda n, pt: (n, 0, 0)),
                pl.BlockSpec((SEQ_GROUP, width), lambda n, pt: (n // SEQ_GROUP, 0)),
                pl.BlockSpec((SEQ_GROUP, KV_ROW), lambda n, pt: (n // SEQ_GROUP, 0))],
            out_specs=pl.BlockSpec((1, DSA_HEADS, HEAD_DIM), lambda n, pt: (n, 0, 0))),
        out_shape=jax.ShapeDtypeStruct((n_seq, DSA_HEADS, HEAD_DIM), BF16),
        compiler_params=_cparams(("arbitrary",)),
        name="dsa_sample",
    )(page_table, *([kv_pool] * n_pages), q, keep, kv_new)


def _nsa_sample_kernel(pt_ref, *refs, n_pages, page, n_cmp):
    slc_pages = refs[:n_pages]
    q_ref, gate_ref, ckv_ref, win_ref, slcnew_ref, winnew_ref, winslab_ref, o_ref, wout_ref = refs[n_pages:]
    n = pl.program_id(0)
    past = n_pages * page
    heads = NSA_HEADS
    rep = NSA_HEADS // NSA_KV_GROUPS
    slopes = _slopes(NSA_HEADS)
    slope_col = jnp.concatenate([jnp.full((1, 1), s, F32) for s in slopes], axis=0)
    q = (q_ref[0] * SCALE).astype(BF16)
    q32 = q.astype(F32)
    gates = jax.nn.sigmoid(gate_ref[0])
    n_cmp_pad = ckv_ref.shape[1]
    total = past + 1
    n_slc = -(-total // SLC_BLOCK)
    n_keep = min(SLC_TOPN, n_slc)
    slc_pad = -(-n_slc // LANES) * LANES

    kc = lax.broadcasted_iota(I32, (heads, n_cmp_pad), 1)
    dist_c_i = past - (kc * CMP_STRIDE + (CMP_BLOCK - 1))
    valid_c = (dist_c_i >= 0) & (kc < n_cmp)
    s_g = [_dot_nt(q, ckv_ref[0, :, g * 256:g * 256 + 128].astype(BF16)) for g in range(NSA_KV_GROUPS)]
    s = jnp.where(valid_c, _group_rows(s_g, rep) - slope_col * dist_c_i.astype(F32), NEG_INF)
    m = jnp.max(s, axis=1, keepdims=True)
    p = jnp.where(valid_c, jnp.exp(s - m), 0.0)
    l = jnp.sum(p, axis=1, keepdims=True)
    p = p * (1.0 / jnp.where(l > 0.0, l, 1.0))
    pb = p.astype(BF16)
    o_c = _group_rows([_dot(pb, ckv_ref[0, :, g * 256 + 128:g * 256 + 256].astype(BF16))
                       for g in range(NSA_KV_GROUPS)], rep)
    same_group = (lax.broadcasted_iota(I32, (heads, heads), 0) // rep
                  == lax.broadcasted_iota(I32, (heads, heads), 1) // rep).astype(F32)
    imp = _dot_hi(same_group, p)
    score = _dot_hi(imp, _cover(n_cmp_pad, slc_pad))
    sel = _select_blocks(score, jnp.full((heads, 1), past, I32), n_slc, n_keep)
    expand = (lax.broadcasted_iota(I32, (slc_pad, past), 1) // SLC_BLOCK
              == lax.broadcasted_iota(I32, (slc_pad, past), 0)).astype(BF16)
    valid_s = _dot(sel.astype(BF16), expand) > 0.5
    new_blk = past // SLC_BLOCK
    valid_s_new = sel[:, new_blk:new_blk + 1] > 0.5

    def attend_with_new(s_past, valid_past, s_new, valid_new, dist_past, v_of, v_new_of):
        s_p = jnp.where(valid_past, s_past - slope_col * dist_past, NEG_INF)
        s_n = jnp.where(valid_new, s_new, NEG_INF)
        mm = jnp.maximum(jnp.max(s_p, axis=1, keepdims=True), s_n)
        pp = jnp.where(valid_past, jnp.exp(s_p - mm), 0.0)
        pn = jnp.where(valid_new, jnp.exp(s_n - mm), 0.0)
        ll = jnp.sum(pp, axis=1, keepdims=True) + pn
        inv = 1.0 / jnp.where(ll > 0.0, ll, 1.0)
        ppb = pp.astype(BF16)
        pn = pn.astype(BF16).astype(F32)
        outs = [v_of(ppb, g) + pn * v_new_of(g) for g in range(NSA_KV_GROUPS)]
        return _group_rows(outs, rep) * inv

    slcnew = slcnew_ref[pl.ds(n, 1), :]
    s_g, s_new_g = [], []
    for g in range(NSA_KV_GROUPS):
        s_g.append(jnp.concatenate(
            [_dot_nt(q, _slab(pg, page, g, 0).astype(BF16)) for pg in slc_pages], axis=1))
        s_new_g.append(jnp.sum(q32 * slcnew[:, g * 256:g * 256 + 128], axis=1, keepdims=True))
    dist_s = (past - lax.broadcasted_iota(I32, (heads, past), 1)).astype(F32)

    def slc_v(ppb, g):
        acc = jnp.zeros((heads, HEAD_DIM), F32)
        for i, pg in enumerate(slc_pages):
            acc = acc + _dot(ppb[:, i * page:(i + 1) * page], _slab(pg, page, g, 1).astype(BF16))
        return acc

    o_s = attend_with_new(_group_rows(s_g, rep), valid_s, _group_rows(s_new_g, rep), valid_s_new, dist_s, slc_v,
                          lambda g: slcnew[:, g * 256 + 128:g * 256 + 256])

    n_buf = win_ref.shape[0] // KV_SLABS
    winnew = winnew_ref[pl.ds(n, 1), :]
    dist_w_i = n_buf - lax.broadcasted_iota(I32, (heads, n_buf), 1)
    valid_w = dist_w_i <= WINDOW
    s_g = [_dot_nt(q, _slab(win_ref, n_buf, g, 0).astype(BF16)) for g in range(NSA_KV_GROUPS)]
    s_new_g = [jnp.sum(q32 * winnew[:, g * 256:g * 256 + 128], axis=1, keepdims=True)
               for g in range(NSA_KV_GROUPS)]
    o_w = attend_with_new(_group_rows(s_g, rep), valid_w, _group_rows(s_new_g, rep),
                          jnp.full((heads, 1), True), dist_w_i.astype(F32),
                          lambda ppb, g: _dot(ppb, _slab(win_ref, n_buf, g, 1).astype(BF16)),
                          lambda g: winnew[:, g * 256 + 128:g * 256 + 256])

    o_ref[0] = (gates[:, 0:1] * o_c + gates[:, 1:2] * o_s + gates[:, 2:3] * o_w).astype(BF16)

    keep_rows = (n_buf - 1) * KV_SLABS
    wout_ref[0:keep_rows, :] = win_ref[KV_SLABS:n_buf * KV_SLABS, :]
    wout_ref[keep_rows:keep_rows + KV_SLABS, :] = winslab_ref[pl.ds(pl.multiple_of(n * KV_SLABS, KV_SLABS), KV_SLABS), :]


def _nsa_sample(slc_pool, page_table, page, q, gate, ckv, win_state, slc_new, win_new, win_new_slab):
    n_seq, n_pages = page_table.shape
    n_cmp_pad = ckv.shape[1]
    n_cmp = n_pages * page // CMP_STRIDE - CMP_BLOCK // CMP_STRIDE + 1
    n_buf = win_state.shape[0] // (n_seq * KV_SLABS)
    slc_specs = _page_specs(n_pages, page * KV_SLABS, HEAD_DIM, lambda n, pt, p: pt[n, p])
    return pl.pallas_call(
        functools.partial(_nsa_sample_kernel, n_pages=n_pages, page=page, n_cmp=n_cmp),
        grid_spec=pltpu.PrefetchScalarGridSpec(
            num_scalar_prefetch=1,
            grid=(n_seq,),
            in_specs=slc_specs + [
                pl.BlockSpec((1, NSA_HEADS, HEAD_DIM), lambda n, pt: (n, 0, 0)),
                pl.BlockSpec((1, NSA_HEADS, 3), lambda n, pt: (n, 0, 0)),
                pl.BlockSpec((1, n_cmp_pad, KV_ROW), lambda n, pt: (n, 0, 0)),
                pl.BlockSpec((n_buf * KV_SLABS, HEAD_DIM), lambda n, pt: (n, 0)),
                pl.BlockSpec((n_seq, KV_ROW), lambda n, pt: (0, 0)),
                pl.BlockSpec((n_seq, KV_ROW), lambda n, pt: (0, 0)),
                pl.BlockSpec((n_seq * KV_SLABS, HEAD_DIM), lambda n, pt: (0, 0))],
            out_specs=[pl.BlockSpec((1, NSA_HEADS, HEAD_DIM), lambda n, pt: (n, 0, 0)),
                       pl.BlockSpec((n_buf * KV_SLABS, HEAD_DIM), lambda n, pt: (n, 0))]),
        out_shape=[jax.ShapeDtypeStruct((n_seq, NSA_HEADS, HEAD_DIM), BF16),
                   jax.ShapeDtypeStruct(win_state.shape, F32)],
        compiler_params=_cparams(("arbitrary",)),
        name="nsa_sample",
    )(page_table, *([slc_pool] * n_pages), q, gate, ckv, win_state, slc_new, win_new, win_new_slab)


def _ln(v, g, b):
    mu = jnp.mean(v, axis=1, keepdims=True)
    d = v - mu
    var = jnp.mean(d * d, axis=1, keepdims=True)
    return d * lax.rsqrt(var + LN_EPS) * g + b


def _ln_router_kernel(xa_ref, mixa_ref, xb_ref, mixb_ref, g_ref, b_ref, wr_ref, br_ref, x1_ref, e_ref, gate_ref, *,
                      alpha, blocks_a):
    from_a = pl.program_id(0) < blocks_a
    pre = jnp.where(from_a, alpha * xa_ref[...] + mixa_ref[...], alpha * xb_ref[...] + mixb_ref[...])
    x1 = _ln(pre, g_ref[...], b_ref[...])
    x1_ref[...] = x1
    logits = _dot_hi(x1, wr_ref[...]) + br_ref[...]
    rows = logits.shape[0]
    lane = lax.broadcasted_iota(I32, (rows, LANES), 1)
    big = jnp.int32(LANES)
    is_grp = lane < N_GROUPS
    lg = jnp.where(is_grp, logits, -jnp.inf)
    mg = jnp.max(lg, axis=1, keepdims=True)
    grp = jnp.min(jnp.where(lg == mg, lane, big), axis=1, keepdims=True)
    p_grp = 1.0 / jnp.sum(jnp.where(is_grp, jnp.exp(lg - mg), 0.0), axis=1, keepdims=True)
    ex = lane - N_GROUPS
    in_grp = (ex >= 0) & (ex < N_EXPERTS) & (ex // EXPERTS_PER_GROUP == grp)
    le = jnp.where(in_grp, logits, -jnp.inf)
    m1 = jnp.max(le, axis=1, keepdims=True)
    e1 = jnp.min(jnp.where(le == m1, ex, big), axis=1, keepdims=True)
    le2 = jnp.where(ex == e1, -jnp.inf, le)
    m2 = jnp.max(le2, axis=1, keepdims=True)
    e2 = jnp.min(jnp.where(le2 == m2, ex, big), axis=1, keepdims=True)
    z = jnp.sum(jnp.where(in_grp, jnp.exp(le - m1), 0.0), axis=1, keepdims=True)
    p1 = 1.0 / z
    p2 = jnp.exp(m2 - m1) / z
    g1 = p_grp * p1 / (p1 + p2)
    g2 = p_grp * p2 / (p1 + p2)
    e_ref[...] = jnp.where(lane == 0, e1, jnp.where(lane == 1, e2, 0))
    gate_ref[...] = jnp.where(lane == 0, g1, jnp.where(lane == 1, g2, 0.0))


def _ln_router(xa, mixa, xb, mixb, g, b, wr, br, alpha):
    (ma, d), mb = xa.shape, xb.shape[0]
    tm = _pick(mb, (128, 64, 32, 16, 8))
    assert ma % tm == 0 and mb % tm == 0
    blocks_a, blocks_b = ma // tm, mb // tm
    seg_a = lambda i: (jnp.minimum(i, blocks_a - 1), 0)
    seg_b = lambda i: (jnp.maximum(i - blocks_a, 0), 0)
    row = lambda i: (i, 0)
    fixed = lambda i: (0, 0)
    m = ma + mb
    return pl.pallas_call(
        functools.partial(_ln_router_kernel, alpha=alpha, blocks_a=blocks_a),
        grid=(blocks_a + blocks_b,),
        in_specs=[pl.BlockSpec((tm, d), seg_a), pl.BlockSpec((tm, d), seg_a),
                  pl.BlockSpec((tm, d), seg_b), pl.BlockSpec((tm, d), seg_b),
                  pl.BlockSpec((1, d), fixed), pl.BlockSpec((1, d), fixed),
                  pl.BlockSpec((d, LANES), fixed), pl.BlockSpec((1, LANES), fixed)],
        out_specs=[pl.BlockSpec((tm, d), row), pl.BlockSpec((tm, LANES), row), pl.BlockSpec((tm, LANES), row)],
        out_shape=[jax.ShapeDtypeStruct((m, d), F32), jax.ShapeDtypeStruct((m, LANES), I32),
                   jax.ShapeDtypeStruct((m, LANES), F32)],
        compiler_params=_cparams(("arbitrary",)),
        name="ln_router",
    )(xa, mixa, xb, mixb, g, b, wr, br)


DISPATCH_UNROLL = 8


def _dispatch_kernel(src_ref, rows_ref, x_ref, o_ref, buf_ref, sem):
    i = pl.program_id(0)
    n_rows = rows_ref[i]

    @pl.when(i == 0)
    def _():
        buf_ref[...] = jnp.zeros_like(buf_ref)

    def row_copy(r):
        tok = src_ref[i * MOE_ROWS + r]
        return pltpu.make_async_copy(x_ref.at[pl.ds(tok, 1)], buf_ref.at[pl.ds(r, 1)], sem)

    def trips(fn):
        def trip(t, carry):
            for u in range(DISPATCH_UNROLL):
                r = t * DISPATCH_UNROLL + u

                @pl.when(r < n_rows)
                def _():
                    fn(row_copy(r))

            return carry

        lax.fori_loop(0, (n_rows + DISPATCH_UNROLL - 1) // DISPATCH_UNROLL, trip, 0)

    trips(lambda cp: cp.start())
    trips(lambda cp: cp.wait())
    o_ref[...] = buf_ref[...].astype(BF16)


def _dispatch(src_tok, rows_in_block, x1, nb):
    d = x1.shape[1]
    return pl.pallas_call(
        _dispatch_kernel,
        grid_spec=pltpu.PrefetchScalarGridSpec(
            num_scalar_prefetch=2,
            grid=(nb,),
            in_specs=[pl.BlockSpec(memory_space=pl.ANY)],
            out_specs=pl.BlockSpec((MOE_ROWS, d), lambda i, s, u: (i, 0)),
            scratch_shapes=[pltpu.VMEM((MOE_ROWS, d), F32), pltpu.SemaphoreType.DMA(())]),
        out_shape=jax.ShapeDtypeStruct((nb * MOE_ROWS, d), BF16),
        compiler_params=_cparams(("arbitrary",)),
        name="moe_dispatch",
    )(src_tok, rows_in_block, x1)


def _expert_mm_kernel(run_ref, rune_ref, nruns_ref, nused_ref, x_ref, *refs, n_w, tn, epilogue):
    w_hbm = refs[:n_w]
    o_ref = refs[n_w]
    w_buf = refs[n_w + 1:2 * n_w + 1]
    w16 = refs[2 * n_w + 1:3 * n_w + 1]
    sem, state = refs[3 * n_w + 1], refs[3 * n_w + 2]
    j, i = pl.program_id(0), pl.program_id(1)
    n_runs = nruns_ref[0]
    total = pl.num_programs(0) * n_runs
    cur = j * n_runs + run_ref[i]

    def item_copies(item):
        e = rune_ref[item % n_runs]
        col = pl.multiple_of((item // n_runs) * tn, tn)
        slot = item % MOE_RING
        return [pltpu.make_async_copy(w_hbm[a].at[e, :, pl.ds(col, tn)], w_buf[a].at[slot], sem.at[a, slot])
                for a in range(n_w)]

    @pl.when((j == 0) & (i == 0))
    def _():
        state[0] = 0
        state[1] = -1

    def request(item, carry):
        for cp in item_copies(item):
            cp.start()
        return carry

    limit = jnp.minimum(cur + MOE_RING, total)
    lax.fori_loop(state[0], limit, request, 0)
    state[0] = jnp.maximum(state[0], limit)

    @pl.when(state[1] != cur)
    def _():
        for cp in item_copies(cur):
            cp.wait()
        slot = cur % MOE_RING
        for a in range(n_w):
            w16[a][...] = w_buf[a][slot].astype(BF16)
        state[1] = cur

    @pl.when(i < nused_ref[0])
    def _():
        x = x_ref[...]
        o_ref[...] = epilogue(*[_dot(x, w16[a][...]) for a in range(n_w)]).astype(o_ref.dtype)

    @pl.when(i >= nused_ref[0])
    def _():
        o_ref[...] = jnp.zeros_like(o_ref)


def _expert_mm(plan, x, weights, tn, epilogue, out_dtype, name):
    run_of_block, run_e, n_runs, n_used = plan
    nb = run_of_block.shape[0]
    kd, n = weights[0].shape[1], weights[0].shape[2]
    n_w = len(weights)
    return pl.pallas_call(
        functools.partial(_expert_mm_kernel, n_w=n_w, tn=tn, epilogue=epilogue),
        grid_spec=pltpu.PrefetchScalarGridSpec(
            num_scalar_prefetch=4,
            grid=(n // tn, nb),
            in_specs=[pl.BlockSpec((MOE_ROWS, kd), lambda j, i, *_: (i, 0))]
            + [pl.BlockSpec(memory_space=pl.ANY)] * n_w,
            out_specs=pl.BlockSpec((MOE_ROWS, tn), lambda j, i, *_: (i, j)),
            scratch_shapes=[pltpu.VMEM((MOE_RING, kd, tn), F32) for _ in range(n_w)]
            + [pltpu.VMEM((kd, tn), BF16) for _ in range(n_w)]
            + [pltpu.SemaphoreType.DMA((n_w, MOE_RING)), pltpu.SMEM((2,), I32)]),
        out_shape=jax.ShapeDtypeStruct((nb * MOE_ROWS, n), out_dtype),
        compiler_params=_cparams(("arbitrary", "arbitrary")),
        name=name,
    )(run_of_block, run_e, n_runs, n_used, x, *weights)


def _swiglu(a, u):
    return a * jax.nn.sigmoid(a) * u


def _combine_kernel(dest_ref, y_ref, x1_ref, gate_ref, g_ref, b_ref, o_ref, buf_ref, sem, *, tm, alpha, off):
    i = pl.program_id(0) + off

    def row_copy(r, k):
        slot = dest_ref[(i * tm + r) * EXPERT_TOPK + k]
        return pltpu.make_async_copy(y_ref.at[pl.ds(slot, 1)], buf_ref.at[k, pl.ds(r, 1)], sem)

    def start(r, carry):
        for k in range(EXPERT_TOPK):
            row_copy(r, k).start()
        return carry

    lax.fori_loop(0, tm, start, 0, unroll=4)

    def wait(r, carry):
        for k in range(EXPERT_TOPK):
            row_copy(r, k).wait()
        return carry

    lax.fori_loop(0, tm, wait, 0, unroll=4)
    gate = gate_ref[...]
    f = gate[:, 0:1] * buf_ref[0] + gate[:, 1:2] * buf_ref[1]
    o_ref[...] = _ln(alpha * x1_ref[...] + f, g_ref[...], b_ref[...])


def _combine(dest, ybuf, x1, gate, row_offset, m, g, b, alpha):
    d = x1.shape[1]
    tm = _pick(m, (128,))
    assert row_offset % tm == 0
    off = row_offset // tm
    return pl.pallas_call(
        functools.partial(_combine_kernel, tm=tm, alpha=alpha, off=off),
        grid_spec=pltpu.PrefetchScalarGridSpec(
            num_scalar_prefetch=1,
            grid=(m // tm,),
            in_specs=[pl.BlockSpec(memory_space=pl.ANY),
                      pl.BlockSpec((tm, d), lambda i, dd: (i + off, 0)),
                      pl.BlockSpec((tm, LANES), lambda i, dd: (i + off, 0)),
                      pl.BlockSpec((1, d), lambda i, dd: (0, 0)),
                      pl.BlockSpec((1, d), lambda i, dd: (0, 0))],
            out_specs=pl.BlockSpec((tm, d), lambda i, dd: (i, 0)),
            scratch_shapes=[pltpu.VMEM((EXPERT_TOPK, tm, d), F32), pltpu.SemaphoreType.DMA(())]),
        out_shape=jax.ShapeDtypeStruct((m, d), F32),
        compiler_params=_cparams(("arbitrary",)),
        name="moe_combine",
    )(dest, ybuf, x1, gate, g, b)


def _route_plan(e_all, nb):
    a_n = e_all.shape[0]
    onehot = (e_all[:, None] == jnp.arange(N_EXPERTS, dtype=I32)[None, :]).astype(I32)
    before = jnp.cumsum(onehot, axis=0) - onehot
    rank = jnp.sum(before * onehot, axis=1)
    counts = jnp.sum(onehot, axis=0)
    pad_counts = (counts + MOE_ROWS - 1) // MOE_ROWS * MOE_ROWS
    pad_ends = jnp.cumsum(pad_counts)
    pad_starts = pad_ends - pad_counts
    dest = (pad_starts[e_all] + rank).astype(I32)
    n_used = (pad_ends[-1] // MOE_ROWS).astype(I32)
    block_e = jnp.searchsorted(pad_ends, jnp.arange(nb, dtype=I32) * MOE_ROWS, side="right").astype(I32)
    owns = (counts > 0).astype(I32)
    run_id = jnp.cumsum(owns) - owns
    n_runs = jnp.sum(owns).astype(I32)
    last_e = jnp.max(jnp.where(counts > 0, jnp.arange(N_EXPERTS, dtype=I32), 0)).astype(I32)
    block_e = jnp.where(jnp.arange(nb, dtype=I32) < n_used, jnp.minimum(block_e, N_EXPERTS - 1), last_e)
    run_of_block = run_id[block_e].astype(I32)
    run_e = jnp.full((N_EXPERTS,), last_e, I32).at[jnp.where(counts > 0, run_id, N_EXPERTS)].set(
        jnp.arange(N_EXPERTS, dtype=I32), mode="drop")
    src_tok = jnp.zeros((nb * MOE_ROWS,), I32).at[dest].set(jnp.arange(a_n, dtype=I32) // EXPERT_TOPK)
    blk = jnp.arange(nb, dtype=I32)
    rows_in_block = jnp.where(
        blk < n_used, jnp.clip(counts[block_e] - (blk * MOE_ROWS - pad_starts[block_e]), 0, MOE_ROWS), 0).astype(I32)
    return dest, (run_of_block, run_e, n_runs.reshape(1), n_used.reshape(1)), src_tok, rows_in_block


def kernel(x_prompt, x_sample, cache_dsa_kv, cache_dsa_kidx, cache_nsa_cmp_kv, cache_nsa_slc_kv, state_nsa_win_kv, page_table, w_in, w_out, cmp_w1, cmp_w2, cmp_pe, ln1_g, ln1_b, w_router_group, b_router_group, w_router_expert, b_router_expert, w_gate, w_up, w_down, ln2_g, ln2_b):
    depth = w_in.shape[0]
    assert depth == 1 and x_sample.shape[1] == 1
    batch, seq, d_model = x_prompt.shape
    n_seq = x_sample.shape[0]
    n_pool, page = cache_dsa_kv.shape[1], cache_dsa_kv.shape[2]
    alpha = (2.0 * depth) ** 0.25
    m_p = batch * seq
    xp = x_prompt.reshape(m_p, d_model)
    xs = x_sample.reshape(n_seq, d_model)
    xp16, xs16 = xp.astype(BF16), xs.astype(BF16)

    w = w_in[0]
    t0, t1 = 4176, 9296
    rep_b = NSA_HEADS // NSA_KV_GROUPS
    w_head = w[:, :t0 + 48].astype(BF16)
    w_tail = w[:, t0:t1].astype(BF16)
    w_gates = jnp.pad(w[:, t1:].reshape(d_model, NSA_KV_GROUPS, 3 * rep_b),
                      ((0, 0), (0, 0), (0, LANES - 3 * rep_b))).reshape(d_model, NSA_KV_GROUPS * LANES).astype(BF16)
    c_dq, c_dk, c_dv, c_iq, c_ik = 0, 2048, 2560, 3072, 4096
    c_nq, c_ck, c_cv, c_sk, c_sv, c_wk, c_wv = 0, 2048, 2560, 3072, 3584, 4096, 4608

    def project(x16):
        qa = _matmul([x16], w_head, c_dq, Q_ROW)
        qi = _matmul([x16], w_head, c_iq, IDX_HEADS * IDX_DIM)
        hc = _matmul([x16], w_head, c_ik, LANES)
        qb = _matmul([x16], w_tail, c_nq, Q_ROW)
        gate = _matmul([x16], w_gates)
        kv = [_kv_proj(x16, w_head, c_dk, c_dv), _kv_proj(x16, w_tail, c_ck, c_cv),
              _kv_proj(x16, w_tail, c_sk, c_sv), _kv_proj(x16, w_tail, c_wk, c_wv)]
        return qa, qi, hc, qb, gate, kv

    qa_p, qi_p, hc_p, qb_p, gate_p, kv_p = project(xp16)
    qa_s, qi_s, hc_s, qb_s, gate_s, kv_s = project(xs16)

    w1r = cmp_w1[0].reshape(2, CMP_BLOCK // CMP_STRIDE, CMP_STRIDE, HEAD_DIM, HEAD_DIM)
    w1r = jnp.transpose(w1r, (0, 2, 3, 1, 4)).reshape(2, CMP_STRIDE * HEAD_DIM, 2 * HEAD_DIM)
    peb = _cmp_pe_bias(cmp_pe[0], cmp_w1[0])
    pages_per_seq = seq // page
    ident = jnp.arange(batch * pages_per_seq, dtype=I32).reshape(batch, pages_per_seq)
    ckv_p = _compress(kv_p[1][0], ident, page, w1r, cmp_w2[0], peb)
    ckv_s = _compress(cache_nsa_cmp_kv.reshape(-1, HEAD_DIM), page_table, page, w1r, cmp_w2[0], peb)

    oa_p = _dsa_prompt(qa_p, qi_p, hc_p, kv_p[0][1], batch, seq)
    ob_p = _nsa_prompt(qb_p, gate_p, ckv_p, kv_p[2][1], kv_p[3][1], batch, seq)

    keep_s = _dsa_sample_mask(jnp.swapaxes(cache_dsa_kidx[0], 1, 2), page_table,
                              qi_s.reshape(n_seq, IDX_HEADS, IDX_DIM),
                              hc_s[:, IDX_DIM:IDX_DIM + IDX_HEADS].reshape(n_seq, IDX_HEADS, 1), hc_s)
    oa_s = _dsa_sample(cache_dsa_kv.reshape(-1, HEAD_DIM), page_table, page,
                       qa_s.reshape(n_seq, DSA_HEADS, HEAD_DIM), keep_s, kv_s[0][1].astype(F32))
    ob_s, win_s = _nsa_sample(cache_nsa_slc_kv.reshape(-1, HEAD_DIM), page_table, page,
                              qb_s.reshape(n_seq, NSA_HEADS, HEAD_DIM),
                              gate_s.reshape(n_seq, NSA_KV_GROUPS, LANES)[:, :, :3 * rep_b].reshape(n_seq, NSA_HEADS, 3),
                              ckv_s,
                              state_nsa_win_kv.reshape(-1, HEAD_DIM), kv_s[2][1].astype(F32),
                              kv_s[3][1].astype(F32), kv_s[3][0])

    w_o = w_out[0].astype(BF16)
    wr = jnp.concatenate([w_router_group[0], w_router_expert[0],
                          jnp.zeros((d_model, LANES - N_GROUPS - N_EXPERTS), F32)], axis=1)
    br = jnp.concatenate([b_router_group[0], b_router_expert[0],
                          jnp.zeros((LANES - N_GROUPS - N_EXPERTS,), F32)]).reshape(1, LANES)
    g1, b1 = ln1_g[0].reshape(1, d_model), ln1_b[0].reshape(1, d_model)
    g2, b2 = ln2_g[0].reshape(1, d_model), ln2_b[0].reshape(1, d_model)
    x1, e_tok, gt_tok = _ln_router(
        xp, _matmul([oa_p, ob_p], w_o),
        xs, _matmul([oa_s.reshape(n_seq, Q_ROW), ob_s.reshape(n_seq, Q_ROW)], w_o), g1, b1, wr, br, alpha)

    e_all = e_tok[:, :EXPERT_TOPK].reshape(-1)
    a_n = e_all.shape[0]
    nb = -(-(a_n + N_EXPERTS * (MOE_ROWS - 1)) // MOE_ROWS)
    dest, plan, src_tok, rows_in_block = _route_plan(e_all, nb)
    xbuf = _dispatch(src_tok, rows_in_block, x1, nb)
    d_expert = w_gate.shape[3]
    hid = _expert_mm(plan, xbuf, [w_gate[0], w_up[0]], _pick(d_expert, (MOE_CHUNK, LANES)), _swiglu, BF16,
                     "moe_gate_up")
    ybuf = _expert_mm(plan, hid, [w_down[0]], _pick(d_model, (2048, 1024, 512, 256, LANES)), lambda y: y, F32,
                      "moe_down")
    y_p = _combine(dest, ybuf, x1, gt_tok, 0, m_p, g2, b2, alpha)
    y_s = _combine(dest, ybuf, x1, gt_tok, m_p, n_seq, g2, b2, alpha)

    def state(slab, lead):
        return slab.reshape((1,) + lead + (NSA_KV_GROUPS, 2, HEAD_DIM))

    n_win = min(WINDOW, seq)
    win_p = state(kv_p[3][0], (batch, seq))[:, :, seq - n_win:]
    return (y_p.reshape(batch, seq, d_model), y_s.reshape(n_seq, 1, d_model),
            state(kv_p[0][0], (batch, seq)), state(kv_s[0][0], (n_seq, 1)),
            hc_p[:, :IDX_DIM].reshape(1, batch, seq, IDX_DIM), hc_s[:, :IDX_DIM].reshape(1, n_seq, 1, IDX_DIM),
            state(kv_p[1][0], (batch, seq)), state(kv_s[1][0], (n_seq, 1)),
            state(kv_p[2][0], (batch, seq)), state(kv_s[2][0], (n_seq, 1)),
            win_p, state(win_s, (n_seq, state_nsa_win_kv.shape[2])))
```

```python
import functools

import jax
import jax.numpy as jnp
from jax import lax
from jax.experimental import pallas as pl
from jax.experimental.pallas import tpu as pltpu

F32 = jnp.float32
BF16 = jnp.bfloat16
I32 = jnp.int32

HEAD_DIM = 128
DSA_HEADS = 16
DSA_KV_HEADS = 4
IDX_HEADS = 16
IDX_DIM = 64
IDX_TOPK = 256
NSA_HEADS = 16
NSA_KV_GROUPS = 4
CMP_STRIDE = 16
CMP_BLOCK = 32
SLC_BLOCK = 64
SLC_TOPN = 16
WINDOW = 512
N_GROUPS = 8
EXPERTS_PER_GROUP = 8
N_EXPERTS = N_GROUPS * EXPERTS_PER_GROUP
EXPERT_TOPK = 2
LN_EPS = 1e-5
NEG_INF = -1e30
FORCE_SCORE = 1e9

KV_ROW = 2 * DSA_KV_HEADS * HEAD_DIM
KV_SLABS = 2 * DSA_KV_HEADS
KV_HALF = DSA_KV_HEADS * HEAD_DIM
Q_ROW = DSA_HEADS * HEAD_DIM
Q_TILE = 128
KEY_SPAN = 512
SEQ_GROUP = 8
LANES = 128
MOE_ROWS = 256
MOE_RING = 4
MOE_CHUNK = 256
VMEM_LIMIT = 56 * 1024 * 1024
SCALE = HEAD_DIM ** -0.5
LOG2E = 1.4426950408889634
MASKED_DIST = 1e32
INT_MIN = -(2 ** 31)


def _slopes(n):
    return [2.0 ** (-8.0 * i / n) for i in range(1, n + 1)]


def _cparams(sem):
    return pltpu.CompilerParams(dimension_semantics=sem, vmem_limit_bytes=VMEM_LIMIT)


def _dot(a, b):
    return jnp.dot(a, b, preferred_element_type=F32)


def _dot_nt(a, b):
    return lax.dot_general(a, b, (((1,), (1,)), ((), ())), preferred_element_type=F32)


def _dot_hi(a, b):
    return jnp.dot(a, b, preferred_element_type=F32, precision=lax.Precision.HIGHEST)


def _pick(n, cands):
    for c in cands:
        if n % c == 0:
            return c
    return n


def _mm_kernel(*refs, n_lhs):
    x_refs, w_refs, o_ref = refs[:n_lhs], refs[n_lhs:2 * n_lhs], refs[2 * n_lhs]
    acc = _dot(x_refs[0][...], w_refs[0][...])
    for x_ref, w_ref in zip(x_refs[1:], w_refs[1:]):
        acc = acc + _dot(x_ref[...], w_ref[...])
    o_ref[...] = acc.astype(o_ref.dtype)


def _matmul(xs, w, n0=0, n=None):
    m = xs[0].shape[0]
    n = w.shape[1] if n is None else n
    kds = [x.shape[1] for x in xs]
    assert len(set(kds)) == 1 and sum(kds) == w.shape[0]
    kd = kds[0]
    tm = _pick(m, (512, 256, 128))
    tn = _pick(n, (1024, 512, 256, 128))
    assert n0 % tn == 0
    j0 = n0 // tn
    x_specs = [pl.BlockSpec((tm, kd), lambda j, i: (i, 0)) for _ in xs]
    w_specs = [pl.BlockSpec((kd, tn), functools.partial(lambda j, i, a: (a, j + j0), a=a)) for a in range(len(xs))]
    return pl.pallas_call(
        functools.partial(_mm_kernel, n_lhs=len(xs)),
        grid=(n // tn, m // tm),
        in_specs=x_specs + w_specs,
        out_specs=pl.BlockSpec((tm, tn), lambda j, i: (i, j)),
        out_shape=jax.ShapeDtypeStruct((m, n), F32),
        compiler_params=_cparams(("parallel", "parallel")),
        name="matmul",
    )(*xs, *([w] * len(xs)))


def _kv_proj_kernel(x_ref, wk_ref, wv_ref, slab_ref, tile_ref):
    x = x_ref[...]
    rows = x.shape[0]
    for c, w_ref in enumerate((wk_ref, wv_ref)):
        res = _dot(x, w_ref[...])
        for g in range(DSA_KV_HEADS):
            piece = res[:, g * HEAD_DIM:(g + 1) * HEAD_DIM]
            slab_ref[pl.ds(g * 2 + c, rows, stride=KV_SLABS), :] = piece
            tile_ref[:, (g * 2 + c) * HEAD_DIM:(g * 2 + c + 1) * HEAD_DIM] = piece.astype(BF16)


def _kv_proj(x, w, k_col, v_col):
    m, kd = x.shape
    tm = _pick(m, (512, 256, 128))
    assert k_col % KV_HALF == 0 and v_col % KV_HALF == 0
    kb, vb = k_col // KV_HALF, v_col // KV_HALF
    return pl.pallas_call(
        _kv_proj_kernel,
        grid=(m // tm,),
        in_specs=[pl.BlockSpec((tm, kd), lambda i: (i, 0)),
                  pl.BlockSpec((kd, KV_HALF), lambda i: (0, kb)),
                  pl.BlockSpec((kd, KV_HALF), lambda i: (0, vb))],
        out_specs=[pl.BlockSpec((tm * KV_SLABS, HEAD_DIM), lambda i: (i, 0)),
                   pl.BlockSpec((tm, KV_ROW), lambda i: (i, 0))],
        out_shape=[jax.ShapeDtypeStruct((m * KV_SLABS, HEAD_DIM), F32), jax.ShapeDtypeStruct((m, KV_ROW), BF16)],
        compiler_params=_cparams(("parallel",)),
        name="kv_proj",
    )(x, w, w)


def _slab(ref, n_tok, g, c):
    return ref[pl.ds(g * 2 + c, n_tok, stride=KV_SLABS), :]


def _slab_all(pages, n_tok, g, c):
    return jnp.concatenate([_slab(p, n_tok, g, c) for p in pages], axis=0).astype(BF16)


def _page_specs(n_pages, rows, width, index):
    return [pl.BlockSpec((rows, width), functools.partial(lambda *a, p: (index(*a, p), 0), p=p))
            for p in range(n_pages)]


def _sort_key(x):
    b = pltpu.bitcast(x, I32)
    return jnp.where(b < 0, b ^ jnp.int32(0x7FFFFFFF), b)


def _topk_mask(sc, idx, k, idx_bits, keep_all=False):
    key = _sort_key(sc)
    search = jnp.logical_not(keep_all)

    def count(m):
        return jnp.sum(m.astype(I32), axis=1, keepdims=True)

    t0 = jnp.where(count(key >= 0) >= k, jnp.int32(0), jnp.int32(INT_MIN))

    def vstep(i, t):
        cand = t | (jnp.int32(1) << (30 - i))
        return jnp.where(count(key >= cand) >= k, cand, t)

    thr = lax.fori_loop(0, jnp.where(search, 31, 0), vstep, t0)
    gt = key > thr
    eq = key == thr
    need = k - count(gt)
    crowded = count(eq) > need
    n_steps = jnp.where(search & (jnp.max(crowded.astype(I32)) > 0), idx_bits, 0)

    def istep(i, c):
        cand = c | (jnp.int32(1) << (idx_bits - 1 - i))
        return jnp.where(count(eq & (idx < cand)) < need, cand, c)

    cut = lax.fori_loop(0, n_steps, istep, jnp.zeros_like(thr))
    return keep_all | gt | (eq & (jnp.logical_not(crowded) | (idx <= cut)))


def _stack_heads(ref, n_heads, scale):
    return jnp.concatenate([(ref[:, r * HEAD_DIM:(r + 1) * HEAD_DIM] * scale).astype(BF16) for r in range(n_heads)],
                           axis=0)


def _slope_stack(slopes):
    return jnp.concatenate([jnp.full((1, 1, 1), s, F32) for s in slopes], axis=0)


def _attend(q2, k, v, dm, slope2):
    n_heads = slope2.shape[0]
    rows, keys = dm.shape
    s = _dot_nt(q2, k).reshape(n_heads, rows, keys) - slope2 * dm[None]
    m = jnp.max(s, axis=2, keepdims=True)
    p = jnp.exp2(s - m)
    l = jnp.sum(p, axis=2, keepdims=True)
    o = _dot(p.reshape(n_heads * rows, keys).astype(BF16), v)
    return o * (1.0 / l).reshape(n_heads * rows, 1)


def _spans(seq):
    span = KEY_SPAN if seq % KEY_SPAN == 0 else seq
    return span, seq // span


def _group_slope(g, r, n_heads, n_groups):
    rep = n_heads // n_groups
    table = _slopes(n_heads)
    out = jnp.float32(table[r])
    for gg in range(1, n_groups):
        out = jnp.where(g == gg, jnp.float32(table[gg * rep + r]), out)
    return out


def _dsa_prompt_mask(c, qi_ref, cq_ref, ck_ref, dm_ref, kw, k_keep):
    t = c * Q_TILE + lax.broadcasted_iota(I32, (Q_TILE, 1), 0)
    kpos = lax.broadcasted_iota(I32, (Q_TILE, kw), 1)
    causal = kpos <= t
    ki = ck_ref[0:kw, 0:IDX_DIM].astype(BF16)
    wi = cq_ref[:, IDX_DIM:IDX_DIM + IDX_HEADS] * (IDX_HEADS ** -0.5)
    sc = jnp.zeros((Q_TILE, kw), F32)
    for h in range(IDX_HEADS):
        a = _dot_nt(qi_ref[:, h * IDX_DIM:(h + 1) * IDX_DIM].astype(BF16), ki)
        sc = sc + jnp.maximum(a, 0.0) * wi[:, h:h + 1]
    sc = jnp.where(causal, sc, NEG_INF)
    keep = _topk_mask(sc, kpos, k_keep, max(1, (kw - 1).bit_length()), keep_all=(c + 1) * Q_TILE <= k_keep)
    dm_ref[:, 0:kw] = jnp.where(keep & causal, (t - kpos).astype(F32), MASKED_DIST)


def _dsa_prompt_kernel(q_ref, qi_ref, cq_ref, ck_ref, kv_ref, o_ref, dm_ref, *, seq, k_keep):
    c, g = pl.program_id(1), pl.program_id(2)
    rep = DSA_HEADS // DSA_KV_HEADS
    span, n_var = _spans(seq)
    for v in range(n_var):
        @pl.when((c * Q_TILE) // span == v)
        def _(kw=(v + 1) * span):
            @pl.when(g == 0)
            def _():
                _dsa_prompt_mask(c, qi_ref, cq_ref, ck_ref, dm_ref, kw, k_keep)

            dm = dm_ref[:, 0:kw]
            kg = kv_ref[0:kw, 0:HEAD_DIM]
            vg = kv_ref[0:kw, HEAD_DIM:2 * HEAD_DIM]
            slope2 = _slope_stack([_group_slope(g, r, DSA_HEADS, DSA_KV_HEADS) * LOG2E for r in range(rep)])
            o = _attend(_stack_heads(q_ref, rep, SCALE * LOG2E), kg, vg, dm, slope2)
            for r in range(rep):
                o_ref[:, r * HEAD_DIM:(r + 1) * HEAD_DIM] = o[r * Q_TILE:(r + 1) * Q_TILE].astype(BF16)


def _dsa_prompt(q, qi, h_c, kv, batch, seq):
    nc = seq // Q_TILE
    k_keep = min(IDX_TOPK, seq // 4)
    rep = DSA_HEADS // DSA_KV_HEADS
    tile = lambda b, c, g: (b * nc + c, 0)
    tile_g = lambda b, c, g: (b * nc + c, g)
    return pl.pallas_call(
        functools.partial(_dsa_prompt_kernel, seq=seq, k_keep=k_keep),
        grid=(batch, nc, DSA_KV_HEADS),
        in_specs=[pl.BlockSpec((Q_TILE, rep * HEAD_DIM), tile_g),
                  pl.BlockSpec((Q_TILE, IDX_HEADS * IDX_DIM), tile),
                  pl.BlockSpec((Q_TILE, LANES), tile),
                  pl.BlockSpec((seq, LANES), lambda b, c, g: (b, 0)),
                  pl.BlockSpec((seq, 2 * HEAD_DIM), lambda b, c, g: (b, g))],
        out_specs=pl.BlockSpec((Q_TILE, rep * HEAD_DIM), tile_g),
        out_shape=jax.ShapeDtypeStruct((batch * seq, Q_ROW), BF16),
        scratch_shapes=[pltpu.VMEM((Q_TILE, seq), F32)],
        compiler_params=_cparams(("parallel", "arbitrary", "arbitrary")),
        name="dsa_prompt",
    )(q, qi, h_c, h_c, kv)


def _peb_kernel(pe_ref, w1_ref, o_ref):
    for c in range(2):
        o_ref[c] = _dot_hi(pe_ref[c], w1_ref[c])


def _cmp_pe_bias(cmp_pe, cmp_w1):
    return pl.pallas_call(
        _peb_kernel,
        out_shape=jax.ShapeDtypeStruct((2, 1, HEAD_DIM), F32),
        name="cmp_pe_bias",
    )(cmp_pe.reshape(2, 1, CMP_BLOCK * HEAD_DIM), cmp_w1.reshape(2, CMP_BLOCK * HEAD_DIM, HEAD_DIM))


def _compress_kernel(pt_ref, *refs, n_pages, page):
    pages = refs[:n_pages]
    w1_ref, w2_ref, peb_ref, o_ref = refs[n_pages:]
    sub_per_page = page // CMP_STRIDE
    n_sub = n_pages * sub_per_page
    prow = lax.broadcasted_iota(I32, (page, page), 0)
    pcol = lax.broadcasted_iota(I32, (page, page), 1)
    regroup = (pcol == (prow % sub_per_page) * CMP_STRIDE + prow // sub_per_page).astype(BF16)
    by_pos = []
    for p in pages:
        x = jnp.concatenate([_slab(p, page, gc // 2, gc % 2) for gc in range(KV_SLABS)], axis=1).astype(BF16)
        by_pos.append(_dot(regroup, x))
    for c in range(2):
        lhs = jnp.concatenate([
            jnp.concatenate([
                jnp.concatenate([bp[j * sub_per_page:(j + 1) * sub_per_page,
                                    (g * 2 + c) * HEAD_DIM:(g * 2 + c + 1) * HEAD_DIM] for bp in by_pos],
                                axis=0).astype(BF16)
                for j in range(CMP_STRIDE)], axis=1)
            for g in range(NSA_KV_GROUPS)], axis=0)
        part = _dot(lhs, w1_ref[c].astype(BF16))
        w2 = w2_ref[c].astype(BF16)
        for g in range(NSA_KV_GROUPS):
            col = g * 256 + c * 128
            pg = part[g * n_sub:(g + 1) * n_sub]
            nxt = pltpu.roll(pg[:, HEAD_DIM:], n_sub - 1, 0)
            hid = pg[:, :HEAD_DIM] + nxt + peb_ref[c]
            o_ref[0, :, col:col + HEAD_DIM] = _dot(jax.nn.gelu(hid).astype(BF16), w2)


def _compress(pool, page_table, page, w1r, w2, peb):
    n_seq, n_pages = page_table.shape
    n_sub = n_pages * page // CMP_STRIDE
    page_specs = _page_specs(n_pages, page * KV_SLABS, HEAD_DIM, lambda n, pt, p: pt[n, p])
    return pl.pallas_call(
        functools.partial(_compress_kernel, n_pages=n_pages, page=page),
        grid_spec=pltpu.PrefetchScalarGridSpec(
            num_scalar_prefetch=1,
            grid=(n_seq,),
            in_specs=page_specs + [
                pl.BlockSpec((2, CMP_STRIDE * HEAD_DIM, 2 * HEAD_DIM), lambda n, pt: (0, 0, 0)),
                pl.BlockSpec((2, HEAD_DIM, HEAD_DIM), lambda n, pt: (0, 0, 0)),
                pl.BlockSpec((2, 1, HEAD_DIM), lambda n, pt: (0, 0, 0))],
            out_specs=pl.BlockSpec((1, n_sub, KV_ROW), lambda n, pt: (n, 0, 0))),
        out_shape=jax.ShapeDtypeStruct((n_seq, n_sub, KV_ROW), F32),
        compiler_params=_cparams(("arbitrary",)),
        name="nsa_compress",
    )(page_table, *([pool] * n_pages), w1r, w2, peb)


def _cover(n_cmp_pad, n_slc_pad):
    cs = lax.broadcasted_iota(I32, (n_cmp_pad, n_slc_pad), 0) * CMP_STRIDE
    bs = lax.broadcasted_iota(I32, (n_cmp_pad, n_slc_pad), 1) * SLC_BLOCK
    return ((cs < bs + SLC_BLOCK) & (cs + CMP_BLOCK > bs)).astype(F32)


def _select_blocks(score, t, n_slc, n_keep):
    rows, width = score.shape
    j = lax.broadcasted_iota(I32, (rows, width), 1)
    cur = t // SLC_BLOCK
    forced = (j == 0) | (j == cur) | (j == cur - 1)
    admissible = j * SLC_BLOCK <= t
    score = jnp.where(forced, FORCE_SCORE, jnp.where(admissible, score, NEG_INF))
    rank = jnp.zeros((rows, width), I32)
    for k in range(n_slc):
        sk = score[:, k:k + 1]
        ahead = (sk > score) | ((sk == score) & (j > k))
        rank = rank + ahead.astype(I32)
    return ((rank < n_keep) & (j < n_slc)).astype(F32)


def _select_blocks_t(score_t, t_row, n_keep):
    n_slc, width = score_t.shape
    j = lax.broadcasted_iota(I32, (n_slc, width), 0)
    cur = t_row // SLC_BLOCK
    forced = (j == 0) | (j == cur) | (j == cur - 1)
    admissible = j * SLC_BLOCK <= t_row
    score_t = jnp.where(forced, FORCE_SCORE, jnp.where(admissible, score_t, NEG_INF))
    rank = jnp.zeros((n_slc, width), I32)
    for k in range(n_slc):
        sk = score_t[k:k + 1, :]
        ahead = (sk > score_t) | ((sk == score_t) & (j > k))
        rank = rank + ahead.astype(I32)
    return (rank < n_keep).astype(F32)


def _nsa_prompt_body(c, g, q_ref, gate_ref, ckv_ref, slc_ref, win_ref, o_ref, kw, seq, n_cmp, win_keys):
    t = c * Q_TILE + lax.broadcasted_iota(I32, (Q_TILE, 1), 0)
    t_row = c * Q_TILE + lax.broadcasted_iota(I32, (1, Q_TILE), 1)
    rep = NSA_HEADS // NSA_KV_GROUPS
    slope2 = _slope_stack([_group_slope(g, r, NSA_HEADS, NSA_KV_GROUPS) * LOG2E for r in range(rep)])
    n_cmp_pad = ckv_ref.shape[1]
    n_slc = seq // SLC_BLOCK
    n_keep = min(SLC_TOPN, n_slc)

    kc = lax.broadcasted_iota(I32, (Q_TILE, n_cmp_pad), 1)
    dist_c_i = t - (kc * CMP_STRIDE + (CMP_BLOCK - 1))
    valid_c = (dist_c_i >= 0) & (kc < n_cmp)
    dist_c = dist_c_i.astype(F32)
    cover_bs = lax.broadcasted_iota(I32, (n_slc, n_cmp_pad), 0) * SLC_BLOCK
    cover_cs = lax.broadcasted_iota(I32, (n_slc, n_cmp_pad), 1) * CMP_STRIDE
    cover_t = ((cover_cs < cover_bs + SLC_BLOCK) & (cover_cs + CMP_BLOCK > cover_bs)).astype(F32)
    slc_rows = -(-n_slc // LANES) * LANES
    expand = (lax.broadcasted_iota(I32, (slc_rows, kw), 1) // SLC_BLOCK
              == lax.broadcasted_iota(I32, (slc_rows, kw), 0)).astype(BF16)

    kpos = lax.broadcasted_iota(I32, (Q_TILE, kw), 1)
    causal = kpos <= t
    dist_s = (t - kpos).astype(F32)

    w0 = pl.multiple_of(jnp.maximum(c * Q_TILE + Q_TILE - win_keys, 0), Q_TILE)
    wpos = w0 + lax.broadcasted_iota(I32, (Q_TILE, win_keys), 1)
    dist_w_i = t - wpos
    dm_w = jnp.where((dist_w_i >= 0) & (dist_w_i <= WINDOW), dist_w_i.astype(F32), MASKED_DIST)

    gates = jax.nn.sigmoid(gate_ref[:, 0:3 * rep])

    ck = ckv_ref[0, :, 0:HEAD_DIM].astype(BF16)
    cv = ckv_ref[0, :, HEAD_DIM:2 * HEAD_DIM].astype(BF16)
    q2 = _stack_heads(q_ref, rep, SCALE * LOG2E)
    s = jnp.where(valid_c[None], _dot_nt(q2, ck).reshape(rep, Q_TILE, n_cmp_pad) - slope2 * dist_c[None], NEG_INF)
    m = jnp.max(s, axis=2, keepdims=True)
    p = jnp.where(valid_c[None], jnp.exp2(s - m), 0.0)
    l = jnp.sum(p, axis=2, keepdims=True)
    p = p * (1.0 / jnp.where(l > 0.0, l, 1.0))
    imp = jnp.sum(p, axis=0)
    o_c = _dot(p.reshape(rep * Q_TILE, n_cmp_pad).astype(BF16), cv)
    sel_t = _select_blocks_t(_dot_hi(cover_t, imp.T), t_row, n_keep)
    sel_t = jnp.concatenate([sel_t, jnp.zeros((slc_rows - n_slc, Q_TILE), F32)], axis=0)
    picked = _dot(sel_t.T.astype(BF16), expand) > 0.5
    dm_s = jnp.where(picked & causal, dist_s, MASKED_DIST)
    sk = slc_ref[0:kw, 0:HEAD_DIM]
    sv = slc_ref[0:kw, HEAD_DIM:2 * HEAD_DIM]
    wk = win_ref[pl.ds(w0, win_keys), 0:HEAD_DIM]
    wv = win_ref[pl.ds(w0, win_keys), HEAD_DIM:2 * HEAD_DIM]
    o_s = _attend(q2, sk, sv, dm_s, slope2)
    o_w = _attend(q2, wk, wv, dm_w, slope2)
    gate_col = [jnp.concatenate([gates[:, 3 * r + k:3 * r + k + 1] for r in range(rep)], axis=0) for k in range(3)]
    o = gate_col[0] * o_c + gate_col[1] * o_s + gate_col[2] * o_w
    for r in range(rep):
        o_ref[:, r * HEAD_DIM:(r + 1) * HEAD_DIM] = o[r * Q_TILE:(r + 1) * Q_TILE].astype(BF16)


def _nsa_prompt_kernel(q_ref, gate_ref, ckv_ref, slc_ref, win_ref, o_ref, *, seq, n_cmp, win_keys):
    c, g = pl.program_id(1), pl.program_id(2)
    span, n_var = _spans(seq)
    for v in range(n_var):
        @pl.when((c * Q_TILE) // span == v)
        def _(kw=(v + 1) * span):
            _nsa_prompt_body(c, g, q_ref, gate_ref, ckv_ref, slc_ref, win_ref, o_ref, kw, seq, n_cmp, win_keys)


def _nsa_prompt(q, gate, ckv, slc, win, batch, seq):
    nc = seq // Q_TILE
    n_cmp_pad = ckv.shape[1]
    n_cmp = seq // CMP_STRIDE - CMP_BLOCK // CMP_STRIDE + 1
    win_keys = min(seq, WINDOW + Q_TILE)
    rep = NSA_HEADS // NSA_KV_GROUPS
    tile_g = lambda b, c, g: (b * nc + c, g)
    return pl.pallas_call(
        functools.partial(_nsa_prompt_kernel, seq=seq, n_cmp=n_cmp, win_keys=win_keys),
        grid=(batch, nc, NSA_KV_GROUPS),
        in_specs=[pl.BlockSpec((Q_TILE, rep * HEAD_DIM), tile_g),
                  pl.BlockSpec((Q_TILE, LANES), tile_g),
                  pl.BlockSpec((1, n_cmp_pad, 2 * HEAD_DIM), lambda b, c, g: (b, 0, g)),
                  pl.BlockSpec((seq, 2 * HEAD_DIM), lambda b, c, g: (b, g)),
                  pl.BlockSpec((seq, 2 * HEAD_DIM), lambda b, c, g: (b, g))],
        out_specs=pl.BlockSpec((Q_TILE, rep * HEAD_DIM), tile_g),
        out_shape=jax.ShapeDtypeStruct((batch * seq, Q_ROW), BF16),
        compiler_params=_cparams(("parallel", "arbitrary", "arbitrary")),
        name="nsa_prompt",
    )(q, gate, ckv, slc, win)


def _group_rows(per_group, rep):
    row = lax.broadcasted_iota(I32, per_group[0].shape, 0) // rep
    out = per_group[0]
    for g in range(1, len(per_group)):
        out = jnp.where(row == g, per_group[g], out)
    return out


def _dsa_sample_mask_kernel(pt_ref, *refs, n_pages, page, k_keep):
    n_ki = SEQ_GROUP * n_pages
    ki_pages = refs[:n_ki]
    qi_ref, wi_ref, cnew_ref, keep_ref = refs[n_ki:]
    past = n_pages * page
    width = past + LANES
    rows = []
    for s in range(SEQ_GROUP):
        qi = qi_ref[s].astype(BF16)
        wi = wi_ref[s] * (IDX_HEADS ** -0.5)
        knew = cnew_ref[pl.ds(s, 1), 0:IDX_DIM]
        a_past = jnp.concatenate([_dot(qi, ki_pages[s * n_pages + p][0].astype(BF16))
                                  for p in range(n_pages)], axis=1)
        a_new = _dot_nt(qi, jnp.broadcast_to(knew, (8, IDX_DIM)).astype(BF16))[:, 0:1]
        sc_past = jnp.sum(jnp.maximum(a_past, 0.0) * wi, axis=0, keepdims=True)
        sc_new = jnp.sum(jnp.maximum(a_new, 0.0) * wi, axis=0, keepdims=True)
        rows.append(jnp.concatenate([sc_past, jnp.broadcast_to(sc_new, (1, LANES))], axis=1))
    col = lax.broadcasted_iota(I32, (SEQ_GROUP, width), 1)
    sc = jnp.where(col <= past, jnp.concatenate(rows, axis=0), NEG_INF)
    keep = _topk_mask(sc, col, k_keep, max(1, (width - 1).bit_length())) & (col <= past)
    keep_ref[...] = keep.astype(F32)


def _dsa_sample_mask(kidx_pool_t, page_table, qi, wi, h_c):
    n_seq, n_pages = page_table.shape
    assert n_seq % SEQ_GROUP == 0
    page = kidx_pool_t.shape[2]
    total = n_pages * page + 1
    k_keep = min(IDX_TOPK, total // 4)
    width = n_pages * page + LANES
    ki_specs = [pl.BlockSpec((1, IDX_DIM, page),
                             functools.partial(lambda i, pt, s, p: (pt[i * SEQ_GROUP + s, p], 0, 0), s=s, p=p))
                for s in range(SEQ_GROUP) for p in range(n_pages)]
    return pl.pallas_call(
        functools.partial(_dsa_sample_mask_kernel, n_pages=n_pages, page=page, k_keep=k_keep),
        grid_spec=pltpu.PrefetchScalarGridSpec(
            num_scalar_prefetch=1,
            grid=(n_seq // SEQ_GROUP,),
            in_specs=ki_specs + [
                pl.BlockSpec((SEQ_GROUP, IDX_HEADS, IDX_DIM), lambda i, pt: (i, 0, 0)),
                pl.BlockSpec((SEQ_GROUP, IDX_HEADS, 1), lambda i, pt: (i, 0, 0)),
                pl.BlockSpec((SEQ_GROUP, LANES), lambda i, pt: (i, 0))],
            out_specs=pl.BlockSpec((SEQ_GROUP, width), lambda i, pt: (i, 0))),
        out_shape=jax.ShapeDtypeStruct((n_seq, width), F32),
        compiler_params=_cparams(("arbitrary",)),
        name="dsa_sample_mask",
    )(page_table, *([kidx_pool_t] * (SEQ_GROUP * n_pages)), qi, wi, h_c)


def _dsa_sample_kernel(pt_ref, *refs, n_pages, page):
    kv_pages = refs[:n_pages]
    q_ref, keep_ref, kvnew_ref, o_ref = refs[n_pages:]
    r = pl.program_id(0) % SEQ_GROUP
    past = n_pages * page
    rep = DSA_HEADS // DSA_KV_HEADS
    slopes = _slopes(DSA_HEADS)

    keep = keep_ref[pl.ds(r, 1), :] > 0.5
    keep_past = jnp.broadcast_to(keep[:, 0:past], (DSA_HEADS, past))
    keep_new = jnp.broadcast_to(keep[:, past:past + 1], (DSA_HEADS, 1))

    q = (q_ref[0] * SCALE).astype(BF16)
    q32 = q.astype(F32)
    kvnew = kvnew_ref[pl.ds(r, 1), :]
    slope_col = jnp.concatenate([jnp.full((1, 1), s, F32) for s in slopes], axis=0)
    dist = (past - lax.broadcasted_iota(I32, (DSA_HEADS, past), 1)).astype(F32)
    s_g, s_new_g = [], []
    for g in range(DSA_KV_HEADS):
        s_g.append(_dot_nt(q, _slab_all(kv_pages, page, g, 0)))
        s_new_g.append(jnp.sum(q32 * kvnew[:, g * 256:g * 256 + 128], axis=1, keepdims=True))
    s = jnp.where(keep_past, _group_rows(s_g, rep) - slope_col * dist, NEG_INF)
    s_new = jnp.where(keep_new, _group_rows(s_new_g, rep), NEG_INF)
    m = jnp.maximum(jnp.max(s, axis=1, keepdims=True), s_new)
    p = jnp.where(keep_past, jnp.exp(s - m), 0.0)
    p_new = jnp.where(keep_new, jnp.exp(s_new - m), 0.0)
    l = jnp.sum(p, axis=1, keepdims=True) + p_new
    inv = 1.0 / jnp.where(l > 0.0, l, 1.0)
    pb = p.astype(BF16)
    p_new = p_new.astype(BF16).astype(F32)
    o_g = []
    for g in range(DSA_KV_HEADS):
        o_g.append(p_new * kvnew[:, g * 256 + 128:g * 256 + 256] + _dot(pb, _slab_all(kv_pages, page, g, 1)))
    o_ref[0] = (_group_rows(o_g, rep) * inv).astype(BF16)


def _dsa_sample(kv_pool, page_table, page, q, keep, kv_new):
    n_seq, n_pages = page_table.shape
    width = keep.shape[1]
    kv_specs = _page_specs(n_pages, page * KV_SLABS, HEAD_DIM, lambda n, pt, p: pt[n, p])
    return pl.pallas_call(
        functools.partial(_dsa_sample_kernel, n_pages=n_pages, page=page),
        grid_spec=pltpu.PrefetchScalarGridSpec(
            num_scalar_prefetch=1,
            grid=(n_seq,),
            in_specs=kv_specs + [
                pl.BlockSpec((1, DSA_HEADS, HEAD_DIM), lambda n, pt: (n, 0, 0)),
                pl.BlockSpec((SEQ_GROUP, width), lambda n, pt: (n // SEQ_GROUP, 0)),
                pl.BlockSpec((SEQ_GROUP, KV_ROW), lambda n, pt: (n // SEQ_GROUP, 0))],
            out_specs=pl.BlockSpec((1, DSA_HEADS, HEAD_DIM), lambda n, pt: (n, 0, 0))),
        out_shape=jax.ShapeDtypeStruct((n_seq, DSA_HEADS, HEAD_DIM), BF16),
        compiler_params=_cparams(("arbitrary",)),
        name="dsa_sample",
    )(page_table, *([kv_pool] * n_pages), q, keep, kv_new)


def _nsa_sample_kernel(pt_ref, *refs, n_pages, page, n_cmp):
    slc_pages = refs[:n_pages]
    q_ref, gate_ref, ckv_ref, win_ref, slcnew_ref, winnew_ref, winslab_ref, o_ref, wout_ref = refs[n_pages:]
    n = pl.program_id(0)
    past = n_pages * page
    heads = NSA_HEADS
    rep = NSA_HEADS // NSA_KV_GROUPS
    slopes = _slopes(NSA_HEADS)
    slope_col = jnp.concatenate([jnp.full((1, 1), s, F32) for s in slopes], axis=0)
    q = (q_ref[0] * SCALE).astype(BF16)
    q32 = q.astype(F32)
    gates = jax.nn.sigmoid(gate_ref[0])
    n_cmp_pad = ckv_ref.shape[1]
    total = past + 1
    n_slc = -(-total // SLC_BLOCK)
    n_keep = min(SLC_TOPN, n_slc)
    slc_pad = -(-n_slc // LANES) * LANES

    kc = lax.broadcasted_iota(I32, (heads, n_cmp_pad), 1)
    dist_c_i = past - (kc * CMP_STRIDE + (CMP_BLOCK - 1))
    valid_c = (dist_c_i >= 0) & (kc < n_cmp)
    s_g = [_dot_nt(q, ckv_ref[0, :, g * 256:g * 256 + 128].astype(BF16)) for g in range(NSA_KV_GROUPS)]
    s = jnp.where(valid_c, _group_rows(s_g, rep) - slope_col * dist_c_i.astype(F32), NEG_INF)
    m = jnp.max(s, axis=1, keepdims=True)
    p = jnp.where(valid_c, jnp.exp(s - m), 0.0)
    l = jnp.sum(p, axis=1, keepdims=True)
    p = p * (1.0 / jnp.where(l > 0.0, l, 1.0))
    pb = p.astype(BF16)
    o_c = _group_rows([_dot(pb, ckv_ref[0, :, g * 256 + 128:g * 256 + 256].astype(BF16))
                       for g in range(NSA_KV_GROUPS)], rep)
    same_group = (lax.broadcasted_iota(I32, (heads, heads), 0) // rep
                  == lax.broadcasted_iota(I32, (heads, heads), 1) // rep).astype(F32)
    imp = _dot_hi(same_group, p)
    score = _dot_hi(imp, _cover(n_cmp_pad, slc_pad))
    sel = _select_blocks(score, jnp.full((heads, 1), past, I32), n_slc, n_keep)
    expand = (lax.broadcasted_iota(I32, (slc_pad, past), 1) // SLC_BLOCK
              == lax.broadcasted_iota(I32, (slc_pad, past), 0)).astype(BF16)
    valid_s = _dot(sel.astype(BF16), expand) > 0.5
    new_blk = past // SLC_BLOCK
    valid_s_new = sel[:, new_blk:new_blk + 1] > 0.5

    def attend_with_new(s_past, valid_past, s_new, valid_new, dist_past, v_of, v_new_of):
        s_p = jnp.where(valid_past, s_past - slope_col * dist_past, NEG_INF)
        s_n = jnp.where(valid_new, s_new, NEG_INF)
        mm = jnp.maximum(jnp.max(s_p, axis=1, keepdims=True), s_n)
        pp = jnp.where(valid_past, jnp.exp(s_p - mm), 0.0)
        pn = jnp.where(valid_new, jnp.exp(s_n - mm), 0.0)
        ll = jnp.sum(pp, axis=1, keepdims=True) + pn
        inv = 1.0 / jnp.where(ll > 0.0, ll, 1.0)
        ppb = pp.astype(BF16)
        pn = pn.astype(BF16).astype(F32)
        outs = [v_of(ppb, g) + pn * v_new_of(g) for g in range(NSA_KV_GROUPS)]
        return _group_rows(outs, rep) * inv

    slcnew = slcnew_ref[pl.ds(n, 1), :]
    s_g, s_new_g = [], []
    for g in range(NSA_KV_GROUPS):
        s_g.append(_dot_nt(q, _slab_all(slc_pages, page, g, 0)))
        s_new_g.append(jnp.sum(q32 * slcnew[:, g * 256:g * 256 + 128], axis=1, keepdims=True))
    dist_s = (past - lax.broadcasted_iota(I32, (heads, past), 1)).astype(F32)

    def slc_v(ppb, g):
        return _dot(ppb, _slab_all(slc_pages, page, g, 1))

    o_s = attend_with_new(_group_rows(s_g, rep), valid_s, _group_rows(s_new_g, rep), valid_s_new, dist_s, slc_v,
                          lambda g: slcnew[:, g * 256 + 128:g * 256 + 256])

    n_buf = win_ref.shape[0] // KV_SLABS
    winnew = winnew_ref[pl.ds(n, 1), :]
    dist_w_i = n_buf - lax.broadcasted_iota(I32, (heads, n_buf), 1)
    valid_w = dist_w_i <= WINDOW
    s_g = [_dot_nt(q, _slab(win_ref, n_buf, g, 0).astype(BF16)) for g in range(NSA_KV_GROUPS)]
    s_new_g = [jnp.sum(q32 * winnew[:, g * 256:g * 256 + 128], axis=1, keepdims=True)
               for g in range(NSA_KV_GROUPS)]
    o_w = attend_with_new(_group_rows(s_g, rep), valid_w, _group_rows(s_new_g, rep),
                          jnp.full((heads, 1), True), dist_w_i.astype(F32),
                          lambda ppb, g: _dot(ppb, _slab(win_ref, n_buf, g, 1).astype(BF16)),
                          lambda g: winnew[:, g * 256 + 128:g * 256 + 256])

    o_ref[0] = (gates[:, 0:1] * o_c + gates[:, 1:2] * o_s + gates[:, 2:3] * o_w).astype(BF16)

    keep_rows = (n_buf - 1) * KV_SLABS
    wout_ref[0:keep_rows, :] = win_ref[KV_SLABS:n_buf * KV_SLABS, :]
    wout_ref[keep_rows:keep_rows + KV_SLABS, :] = winslab_ref[pl.ds(pl.multiple_of(n * KV_SLABS, KV_SLABS), KV_SLABS), :]


def _nsa_sample(slc_pool, page_table, page, q, gate, ckv, win_state, slc_new, win_new, win_new_slab):
    n_seq, n_pages = page_table.shape
    n_cmp_pad = ckv.shape[1]
    n_cmp = n_pages * page // CMP_STRIDE - CMP_BLOCK // CMP_STRIDE + 1
    n_buf = win_state.shape[0] // (n_seq * KV_SLABS)
    slc_specs = _page_specs(n_pages, page * KV_SLABS, HEAD_DIM, lambda n, pt, p: pt[n, p])
    return pl.pallas_call(
        functools.partial(_nsa_sample_kernel, n_pages=n_pages, page=page, n_cmp=n_cmp),
        grid_spec=pltpu.PrefetchScalarGridSpec(
            num_scalar_prefetch=1,
            grid=(n_seq,),
            in_specs=slc_specs + [
                pl.BlockSpec((1, NSA_HEADS, HEAD_DIM), lambda n, pt: (n, 0, 0)),
                pl.BlockSpec((1, NSA_HEADS, 3), lambda n, pt: (n, 0, 0)),
                pl.BlockSpec((1, n_cmp_pad, KV_ROW), lambda n, pt: (n, 0, 0)),
                pl.BlockSpec((n_buf * KV_SLABS, HEAD_DIM), lambda n, pt: (n, 0)),
                pl.BlockSpec((n_seq, KV_ROW), lambda n, pt: (0, 0)),
                pl.BlockSpec((n_seq, KV_ROW), lambda n, pt: (0, 0)),
                pl.BlockSpec((n_seq * KV_SLABS, HEAD_DIM), lambda n, pt: (0, 0))],
            out_specs=[pl.BlockSpec((1, NSA_HEADS, HEAD_DIM), lambda n, pt: (n, 0, 0)),
                       pl.BlockSpec((n_buf * KV_SLABS, HEAD_DIM), lambda n, pt: (n, 0))]),
        out_shape=[jax.ShapeDtypeStruct((n_seq, NSA_HEADS, HEAD_DIM), BF16),
                   jax.ShapeDtypeStruct(win_state.shape, F32)],
        compiler_params=_cparams(("arbitrary",)),
        name="nsa_sample",
    )(page_table, *([slc_pool] * n_pages), q, gate, ckv, win_state, slc_new, win_new, win_new_slab)


def _ln(v, g, b):
    mu = jnp.mean(v, axis=1, keepdims=True)
    d = v - mu
    var = jnp.mean(d * d, axis=1, keepdims=True)
    return d * lax.rsqrt(var + LN_EPS) * g + b


def _ln_router_kernel(xa_ref, mixa_ref, xb_ref, mixb_ref, g_ref, b_ref, wr_ref, br_ref, x1_ref, e_ref, gate_ref, *,
                      alpha, blocks_a):
    from_a = pl.program_id(0) < blocks_a
    pre = jnp.where(from_a, alpha * xa_ref[...] + mixa_ref[...], alpha * xb_ref[...] + mixb_ref[...])
    x1 = _ln(pre, g_ref[...], b_ref[...])
    x1_ref[...] = x1
    logits = _dot_hi(x1, wr_ref[...]) + br_ref[...]
    rows = logits.shape[0]
    lane = lax.broadcasted_iota(I32, (rows, LANES), 1)
    big = jnp.int32(LANES)
    is_grp = lane < N_GROUPS
    lg = jnp.where(is_grp, logits, -jnp.inf)
    mg = jnp.max(lg, axis=1, keepdims=True)
    grp = jnp.min(jnp.where(lg == mg, lane, big), axis=1, keepdims=True)
    p_grp = 1.0 / jnp.sum(jnp.where(is_grp, jnp.exp(lg - mg), 0.0), axis=1, keepdims=True)
    ex = lane - N_GROUPS
    in_grp = (ex >= 0) & (ex < N_EXPERTS) & (ex // EXPERTS_PER_GROUP == grp)
    le = jnp.where(in_grp, logits, -jnp.inf)
    m1 = jnp.max(le, axis=1, keepdims=True)
    e1 = jnp.min(jnp.where(le == m1, ex, big), axis=1, keepdims=True)
    le2 = jnp.where(ex == e1, -jnp.inf, le)
    m2 = jnp.max(le2, axis=1, keepdims=True)
    e2 = jnp.min(jnp.where(le2 == m2, ex, big), axis=1, keepdims=True)
    z = jnp.sum(jnp.where(in_grp, jnp.exp(le - m1), 0.0), axis=1, keepdims=True)
    p1 = 1.0 / z
    p2 = jnp.exp(m2 - m1) / z
    g1 = p_grp * p1 / (p1 + p2)
    g2 = p_grp * p2 / (p1 + p2)
    e_ref[...] = jnp.where(lane == 0, e1, jnp.where(lane == 1, e2, 0))
    gate_ref[...] = jnp.where(lane == 0, g1, jnp.where(lane == 1, g2, 0.0))


def _ln_router(xa, mixa, xb, mixb, g, b, wr, br, alpha):
    (ma, d), mb = xa.shape, xb.shape[0]
    tm = _pick(mb, (128, 64, 32, 16, 8))
    assert ma % tm == 0 and mb % tm == 0
    blocks_a, blocks_b = ma // tm, mb // tm
    seg_a = lambda i: (jnp.minimum(i, blocks_a - 1), 0)
    seg_b = lambda i: (jnp.maximum(i - blocks_a, 0), 0)
    row = lambda i: (i, 0)
    fixed = lambda i: (0, 0)
    m = ma + mb
    return pl.pallas_call(
        functools.partial(_ln_router_kernel, alpha=alpha, blocks_a=blocks_a),
        grid=(blocks_a + blocks_b,),
        in_specs=[pl.BlockSpec((tm, d), seg_a), pl.BlockSpec((tm, d), seg_a),
                  pl.BlockSpec((tm, d), seg_b), pl.BlockSpec((tm, d), seg_b),
                  pl.BlockSpec((1, d), fixed), pl.BlockSpec((1, d), fixed),
                  pl.BlockSpec((d, LANES), fixed), pl.BlockSpec((1, LANES), fixed)],
        out_specs=[pl.BlockSpec((tm, d), row), pl.BlockSpec((tm, LANES), row), pl.BlockSpec((tm, LANES), row)],
        out_shape=[jax.ShapeDtypeStruct((m, d), F32), jax.ShapeDtypeStruct((m, LANES), I32),
                   jax.ShapeDtypeStruct((m, LANES), F32)],
        compiler_params=_cparams(("arbitrary",)),
        name="ln_router",
    )(xa, mixa, xb, mixb, g, b, wr, br)


DISPATCH_UNROLL = 8


def _dispatch_kernel(src_ref, rows_ref, x_ref, o_ref, buf_ref, sem):
    i = pl.program_id(0)
    n_rows = rows_ref[i]

    @pl.when(i == 0)
    def _():
        buf_ref[...] = jnp.zeros_like(buf_ref)

    def row_copy(r):
        tok = src_ref[i * MOE_ROWS + r]
        return pltpu.make_async_copy(x_ref.at[pl.ds(tok, 1)], buf_ref.at[pl.ds(r, 1)], sem)

    def trips(fn):
        def trip(t, carry):
            for u in range(DISPATCH_UNROLL):
                r = t * DISPATCH_UNROLL + u

                @pl.when(r < n_rows)
                def _():
                    fn(row_copy(r))

            return carry

        lax.fori_loop(0, (n_rows + DISPATCH_UNROLL - 1) // DISPATCH_UNROLL, trip, 0)

    trips(lambda cp: cp.start())
    trips(lambda cp: cp.wait())
    o_ref[...] = buf_ref[...].astype(BF16)


def _dispatch(src_tok, rows_in_block, x1, nb):
    d = x1.shape[1]
    return pl.pallas_call(
        _dispatch_kernel,
        grid_spec=pltpu.PrefetchScalarGridSpec(
            num_scalar_prefetch=2,
            grid=(nb,),
            in_specs=[pl.BlockSpec(memory_space=pl.ANY)],
            out_specs=pl.BlockSpec((MOE_ROWS, d), lambda i, s, u: (i, 0)),
            scratch_shapes=[pltpu.VMEM((MOE_ROWS, d), F32), pltpu.SemaphoreType.DMA(())]),
        out_shape=jax.ShapeDtypeStruct((nb * MOE_ROWS, d), BF16),
        compiler_params=_cparams(("arbitrary",)),
        name="moe_dispatch",
    )(src_tok, rows_in_block, x1)


def _expert_mm_kernel(run_ref, rune_ref, nruns_ref, nused_ref, x_ref, *refs, n_w, tn, epilogue):
    w_hbm = refs[:n_w]
    o_ref = refs[n_w]
    w_buf = refs[n_w + 1:2 * n_w + 1]
    w16 = refs[2 * n_w + 1:3 * n_w + 1]
    sem, state = refs[3 * n_w + 1], refs[3 * n_w + 2]
    j, i = pl.program_id(0), pl.program_id(1)
    n_runs = nruns_ref[0]
    total = pl.num_programs(0) * n_runs
    cur = j * n_runs + run_ref[i]

    def item_copies(item):
        e = rune_ref[item % n_runs]
        col = pl.multiple_of((item // n_runs) * tn, tn)
        slot = item % MOE_RING
        return [pltpu.make_async_copy(w_hbm[a].at[e, :, pl.ds(col, tn)], w_buf[a].at[slot], sem.at[a, slot])
                for a in range(n_w)]

    @pl.when((j == 0) & (i == 0))
    def _():
        state[0] = 0
        state[1] = -1

    def request(item, carry):
        for cp in item_copies(item):
            cp.start()
        return carry

    limit = jnp.minimum(cur + MOE_RING, total)
    lax.fori_loop(state[0], limit, request, 0)
    state[0] = jnp.maximum(state[0], limit)

    @pl.when(state[1] != cur)
    def _():
        for cp in item_copies(cur):
            cp.wait()
        slot = cur % MOE_RING
        for a in range(n_w):
            w16[a][...] = w_buf[a][slot].astype(BF16)
        state[1] = cur

    @pl.when(i < nused_ref[0])
    def _():
        x = x_ref[...]
        o_ref[...] = epilogue(*[_dot(x, w16[a][...]) for a in range(n_w)]).astype(o_ref.dtype)

    @pl.when(i >= nused_ref[0])
    def _():
        o_ref[...] = jnp.zeros_like(o_ref)


def _expert_mm(plan, x, weights, tn, epilogue, out_dtype, name):
    run_of_block, run_e, n_runs, n_used = plan
    nb = run_of_block.shape[0]
    kd, n = weights[0].shape[1], weights[0].shape[2]
    n_w = len(weights)
    return pl.pallas_call(
        functools.partial(_expert_mm_kernel, n_w=n_w, tn=tn, epilogue=epilogue),
        grid_spec=pltpu.PrefetchScalarGridSpec(
            num_scalar_prefetch=4,
            grid=(n // tn, nb),
            in_specs=[pl.BlockSpec((MOE_ROWS, kd), lambda j, i, *_: (i, 0))]
            + [pl.BlockSpec(memory_space=pl.ANY)] * n_w,
            out_specs=pl.BlockSpec((MOE_ROWS, tn), lambda j, i, *_: (i, j)),
            scratch_shapes=[pltpu.VMEM((MOE_RING, kd, tn), F32) for _ in range(n_w)]
            + [pltpu.VMEM((kd, tn), BF16) for _ in range(n_w)]
            + [pltpu.SemaphoreType.DMA((n_w, MOE_RING)), pltpu.SMEM((2,), I32)]),
        out_shape=jax.ShapeDtypeStruct((nb * MOE_ROWS, n), out_dtype),
        compiler_params=_cparams(("arbitrary", "arbitrary")),
        name=name,
    )(run_of_block, run_e, n_runs, n_used, x, *weights)


def _swiglu(a, u):
    return a * jax.nn.sigmoid(a) * u


def _combine_kernel(dest_ref, y_ref, x1_ref, gate_ref, g_ref, b_ref, o_ref, buf_ref, sem, *, tm, alpha, off):
    i = pl.program_id(0) + off

    def row_copy(r, k):
        slot = dest_ref[(i * tm + r) * EXPERT_TOPK + k]
        return pltpu.make_async_copy(y_ref.at[pl.ds(slot, 1)], buf_ref.at[k, pl.ds(r, 1)], sem)

    def start(r, carry):
        for k in range(EXPERT_TOPK):
            row_copy(r, k).start()
        return carry

    lax.fori_loop(0, tm, start, 0, unroll=4)

    def wait(r, carry):
        for k in range(EXPERT_TOPK):
            row_copy(r, k).wait()
        return carry

    lax.fori_loop(0, tm, wait, 0, unroll=4)
    gate = gate_ref[...]
    f = gate[:, 0:1] * buf_ref[0] + gate[:, 1:2] * buf_ref[1]
    o_ref[...] = _ln(alpha * x1_ref[...] + f, g_ref[...], b_ref[...])


def _combine(dest, ybuf, x1, gate, row_offset, m, g, b, alpha):
    d = x1.shape[1]
    tm = _pick(m, (128,))
    assert row_offset % tm == 0
    off = row_offset // tm
    return pl.pallas_call(
        functools.partial(_combine_kernel, tm=tm, alpha=alpha, off=off),
        grid_spec=pltpu.PrefetchScalarGridSpec(
            num_scalar_prefetch=1,
            grid=(m // tm,),
            in_specs=[pl.BlockSpec(memory_space=pl.ANY),
                      pl.BlockSpec((tm, d), lambda i, dd: (i + off, 0)),
                      pl.BlockSpec((tm, LANES), lambda i, dd: (i + off, 0)),
                      pl.BlockSpec((1, d), lambda i, dd: (0, 0)),
                      pl.BlockSpec((1, d), lambda i, dd: (0, 0))],
            out_specs=pl.BlockSpec((tm, d), lambda i, dd: (i, 0)),
            scratch_shapes=[pltpu.VMEM((EXPERT_TOPK, tm, d), F32), pltpu.SemaphoreType.DMA(())]),
        out_shape=jax.ShapeDtypeStruct((m, d), F32),
        compiler_params=_cparams(("arbitrary",)),
        name="moe_combine",
    )(dest, ybuf, x1, gate, g, b)


def _route_plan(e_all, nb):
    a_n = e_all.shape[0]
    onehot = (e_all[:, None] == jnp.arange(N_EXPERTS, dtype=I32)[None, :]).astype(I32)
    before = jnp.cumsum(onehot, axis=0) - onehot
    rank = jnp.sum(before * onehot, axis=1)
    counts = jnp.sum(onehot, axis=0)
    pad_counts = (counts + MOE_ROWS - 1) // MOE_ROWS * MOE_ROWS
    pad_ends = jnp.cumsum(pad_counts)
    pad_starts = pad_ends - pad_counts
    dest = (pad_starts[e_all] + rank).astype(I32)
    n_used = (pad_ends[-1] // MOE_ROWS).astype(I32)
    block_e = jnp.searchsorted(pad_ends, jnp.arange(nb, dtype=I32) * MOE_ROWS, side="right").astype(I32)
    owns = (counts > 0).astype(I32)
    run_id = jnp.cumsum(owns) - owns
    n_runs = jnp.sum(owns).astype(I32)
    last_e = jnp.max(jnp.where(counts > 0, jnp.arange(N_EXPERTS, dtype=I32), 0)).astype(I32)
    block_e = jnp.where(jnp.arange(nb, dtype=I32) < n_used, jnp.minimum(block_e, N_EXPERTS - 1), last_e)
    run_of_block = run_id[block_e].astype(I32)
    run_e = jnp.full((N_EXPERTS,), last_e, I32).at[jnp.where(counts > 0, run_id, N_EXPERTS)].set(
        jnp.arange(N_EXPERTS, dtype=I32), mode="drop")
    src_tok = jnp.zeros((nb * MOE_ROWS,), I32).at[dest].set(jnp.arange(a_n, dtype=I32) // EXPERT_TOPK)
    blk = jnp.arange(nb, dtype=I32)
    rows_in_block = jnp.where(
        blk < n_used, jnp.clip(counts[block_e] - (blk * MOE_ROWS - pad_starts[block_e]), 0, MOE_ROWS), 0).astype(I32)
    return dest, (run_of_block, run_e, n_runs.reshape(1), n_used.reshape(1)), src_tok, rows_in_block


def kernel(x_prompt, x_sample, cache_dsa_kv, cache_dsa_kidx, cache_nsa_cmp_kv, cache_nsa_slc_kv, state_nsa_win_kv, page_table, w_in, w_out, cmp_w1, cmp_w2, cmp_pe, ln1_g, ln1_b, w_router_group, b_router_group, w_router_expert, b_router_expert, w_gate, w_up, w_down, ln2_g, ln2_b):
    depth = w_in.shape[0]
    assert depth == 1 and x_sample.shape[1] == 1
    batch, seq, d_model = x_prompt.shape
    n_seq = x_sample.shape[0]
    n_pool, page = cache_dsa_kv.shape[1], cache_dsa_kv.shape[2]
    alpha = (2.0 * depth) ** 0.25
    m_p = batch * seq
    xp = x_prompt.reshape(m_p, d_model)
    xs = x_sample.reshape(n_seq, d_model)
    xp16, xs16 = xp.astype(BF16), xs.astype(BF16)

    w = w_in[0]
    t0, t1 = 4176, 9296
    rep_b = NSA_HEADS // NSA_KV_GROUPS
    w_head = w[:, :t0 + 48].astype(BF16)
    w_tail = w[:, t0:t1].astype(BF16)
    w_gates = jnp.pad(w[:, t1:].reshape(d_model, NSA_KV_GROUPS, 3 * rep_b),
                      ((0, 0), (0, 0), (0, LANES - 3 * rep_b))).reshape(d_model, NSA_KV_GROUPS * LANES).astype(BF16)
    c_dq, c_dk, c_dv, c_iq, c_ik = 0, 2048, 2560, 3072, 4096
    c_nq, c_ck, c_cv, c_sk, c_sv, c_wk, c_wv = 0, 2048, 2560, 3072, 3584, 4096, 4608

    def project(x16):
        qa = _matmul([x16], w_head, c_dq, Q_ROW)
        qi = _matmul([x16], w_head, c_iq, IDX_HEADS * IDX_DIM)
        hc = _matmul([x16], w_head, c_ik, LANES)
        qb = _matmul([x16], w_tail, c_nq, Q_ROW)
        gate = _matmul([x16], w_gates)
        kv = [_kv_proj(x16, w_head, c_dk, c_dv), _kv_proj(x16, w_tail, c_ck, c_cv),
              _kv_proj(x16, w_tail, c_sk, c_sv), _kv_proj(x16, w_tail, c_wk, c_wv)]
        return qa, qi, hc, qb, gate, kv

    qa_p, qi_p, hc_p, qb_p, gate_p, kv_p = project(xp16)
    qa_s, qi_s, hc_s, qb_s, gate_s, kv_s = project(xs16)

    w1r = cmp_w1[0].reshape(2, CMP_BLOCK // CMP_STRIDE, CMP_STRIDE, HEAD_DIM, HEAD_DIM)
    w1r = jnp.transpose(w1r, (0, 2, 3, 1, 4)).reshape(2, CMP_STRIDE * HEAD_DIM, 2 * HEAD_DIM)
    peb = _cmp_pe_bias(cmp_pe[0], cmp_w1[0])
    pages_per_seq = seq // page
    ident = jnp.arange(batch * pages_per_seq, dtype=I32).reshape(batch, pages_per_seq)
    ckv_p = _compress(kv_p[1][0], ident, page, w1r, cmp_w2[0], peb)
    ckv_s = _compress(cache_nsa_cmp_kv.reshape(-1, HEAD_DIM), page_table, page, w1r, cmp_w2[0], peb)

    oa_p = _dsa_prompt(qa_p, qi_p, hc_p, kv_p[0][1], batch, seq)
    ob_p = _nsa_prompt(qb_p, gate_p, ckv_p, kv_p[2][1], kv_p[3][1], batch, seq)

    keep_s = _dsa_sample_mask(jnp.swapaxes(cache_dsa_kidx[0], 1, 2), page_table,
                              qi_s.reshape(n_seq, IDX_HEADS, IDX_DIM),
                              hc_s[:, IDX_DIM:IDX_DIM + IDX_HEADS].reshape(n_seq, IDX_HEADS, 1), hc_s)
    oa_s = _dsa_sample(cache_dsa_kv.reshape(-1, HEAD_DIM), page_table, page,
                       qa_s.reshape(n_seq, DSA_HEADS, HEAD_DIM), keep_s, kv_s[0][1].astype(F32))
    ob_s, win_s = _nsa_sample(cache_nsa_slc_kv.reshape(-1, HEAD_DIM), page_table, page,
                              qb_s.reshape(n_seq, NSA_HEADS, HEAD_DIM),
                              gate_s.reshape(n_seq, NSA_KV_GROUPS, LANES)[:, :, :3 * rep_b].reshape(n_seq, NSA_HEADS, 3),
                              ckv_s,
                              state_nsa_win_kv.reshape(-1, HEAD_DIM), kv_s[2][1].astype(F32),
                              kv_s[3][1].astype(F32), kv_s[3][0])

    w_o = w_out[0].astype(BF16)
    wr = jnp.concatenate([w_router_group[0], w_router_expert[0],
                          jnp.zeros((d_model, LANES - N_GROUPS - N_EXPERTS), F32)], axis=1)
    br = jnp.concatenate([b_router_group[0], b_router_expert[0],
                          jnp.zeros((LANES - N_GROUPS - N_EXPERTS,), F32)]).reshape(1, LANES)
    g1, b1 = ln1_g[0].reshape(1, d_model), ln1_b[0].reshape(1, d_model)
    g2, b2 = ln2_g[0].reshape(1, d_model), ln2_b[0].reshape(1, d_model)
    x1, e_tok, gt_tok = _ln_router(
        xp, _matmul([oa_p, ob_p], w_o),
        xs, _matmul([oa_s.reshape(n_seq, Q_ROW), ob_s.reshape(n_seq, Q_ROW)], w_o), g1, b1, wr, br, alpha)

    e_all = e_tok[:, :EXPERT_TOPK].reshape(-1)
    a_n = e_all.shape[0]
    nb = -(-(a_n + N_EXPERTS * (MOE_ROWS - 1)) // MOE_ROWS)
    dest, plan, src_tok, rows_in_block = _route_plan(e_all, nb)
    xbuf = _dispatch(src_tok, rows_in_block, x1, nb)
    d_expert = w_gate.shape[3]
    hid = _expert_mm(plan, xbuf, [w_gate[0], w_up[0]], _pick(d_expert, (MOE_CHUNK, LANES)), _swiglu, BF16,
                     "moe_gate_up")
    ybuf = _expert_mm(plan, hid, [w_down[0]], _pick(d_model, (2048, 1024, 512, 256, LANES)), lambda y: y, F32,
                      "moe_down")
    y_p = _combine(dest, ybuf, x1, gt_tok, 0, m_p, g2, b2, alpha)
    y_s = _combine(dest, ybuf, x1, gt_tok, m_p, n_seq, g2, b2, alpha)

    def state(slab, lead):
        return slab.reshape((1,) + lead + (NSA_KV_GROUPS, 2, HEAD_DIM))

    n_win = min(WINDOW, seq)
    win_p = state(kv_p[3][0], (batch, seq))[:, :, seq - n_win:]
    return (y_p.reshape(batch, seq, d_model), y_s.reshape(n_seq, 1, d_model),
            state(kv_p[0][0], (batch, seq)), state(kv_s[0][0], (n_seq, 1)),
            hc_p[:, :IDX_DIM].reshape(1, batch, seq, IDX_DIM), hc_s[:, :IDX_DIM].reshape(1, n_seq, 1, IDX_DIM),
            state(kv_p[1][0], (batch, seq)), state(kv_s[1][0], (n_seq, 1)),
            state(kv_p[2][0], (batch, seq)), state(kv_s[2][0], (n_seq, 1)),
            win_p, state(win_s, (n_seq, state_nsa_win_kv.shape[2])))
```

```python
import functools

import jax
import jax.numpy as jnp
from jax import lax
from jax.experimental import pallas as pl
from jax.experimental.pallas import tpu as pltpu

F32 = jnp.float32
BF16 = jnp.bfloat16
I32 = jnp.int32

HEAD_DIM = 128
DSA_HEADS = 16
DSA_KV_HEADS = 4
IDX_HEADS = 16
IDX_DIM = 64
IDX_TOPK = 256
NSA_HEADS = 16
NSA_KV_GROUPS = 4
CMP_STRIDE = 16
CMP_BLOCK = 32
SLC_BLOCK = 64
SLC_TOPN = 16
WINDOW = 512
N_GROUPS = 8
EXPERTS_PER_GROUP = 8
N_EXPERTS = N_GROUPS * EXPERTS_PER_GROUP
EXPERT_TOPK = 2
LN_EPS = 1e-5
NEG_INF = -1e30
FORCE_SCORE = 1e9

KV_ROW = 2 * DSA_KV_HEADS * HEAD_DIM
KV_SLABS = 2 * DSA_KV_HEADS
KV_HALF = DSA_KV_HEADS * HEAD_DIM
Q_ROW = DSA_HEADS * HEAD_DIM
Q_TILE = 128
KEY_SPAN = 256
SEQ_GROUP = 8
LANES = 128
MOE_ROWS = 384
MOE_RING = 4
MOE_CHUNK = 256
VMEM_LIMIT = 56 * 1024 * 1024
SCALE = HEAD_DIM ** -0.5
LOG2E = 1.4426950408889634
MASKED_DIST = 1e32
INT_MIN = -(2 ** 31)


def _slopes(n):
    return [2.0 ** (-8.0 * i / n) for i in range(1, n + 1)]


def _cparams(sem):
    return pltpu.CompilerParams(dimension_semantics=sem, vmem_limit_bytes=VMEM_LIMIT)


def _dot(a, b):
    return jnp.dot(a, b, preferred_element_type=F32)


def _dot_nt(a, b):
    return lax.dot_general(a, b, (((1,), (1,)), ((), ())), preferred_element_type=F32)


def _dot_hi(a, b):
    return jnp.dot(a, b, preferred_element_type=F32, precision=lax.Precision.HIGHEST)


def _pick(n, cands):
    for c in cands:
        if n % c == 0:
            return c
    return n


def _mm_kernel(*refs, n_lhs):
    x_refs, w_refs, o_ref = refs[:n_lhs], refs[n_lhs:2 * n_lhs], refs[2 * n_lhs]
    acc = _dot(x_refs[0][...], w_refs[0][...])
    for x_ref, w_ref in zip(x_refs[1:], w_refs[1:]):
        acc = acc + _dot(x_ref[...], w_ref[...])
    o_ref[...] = acc.astype(o_ref.dtype)


def _matmul(xs, w, n0=0, n=None):
    m = xs[0].shape[0]
    n = w.shape[1] if n is None else n
    kds = [x.shape[1] for x in xs]
    assert len(set(kds)) == 1 and sum(kds) == w.shape[0]
    kd = kds[0]
    tm = _pick(m, (512, 256, 128))
    tn = _pick(n, (1024, 512, 256, 128))
    assert n0 % tn == 0
    j0 = n0 // tn
    x_specs = [pl.BlockSpec((tm, kd), lambda j, i: (i, 0)) for _ in xs]
    w_specs = [pl.BlockSpec((kd, tn), functools.partial(lambda j, i, a: (a, j + j0), a=a)) for a in range(len(xs))]
    return pl.pallas_call(
        functools.partial(_mm_kernel, n_lhs=len(xs)),
        grid=(n // tn, m // tm),
        in_specs=x_specs + w_specs,
        out_specs=pl.BlockSpec((tm, tn), lambda j, i: (i, j)),
        out_shape=jax.ShapeDtypeStruct((m, n), F32),
        compiler_params=_cparams(("parallel", "parallel")),
        name="matmul",
    )(*xs, *([w] * len(xs)))


def _kv_proj_kernel(x_ref, wk_ref, wv_ref, slab_ref, tile_ref):
    x = x_ref[...]
    rows = x.shape[0]
    for c, w_ref in enumerate((wk_ref, wv_ref)):
        res = _dot(x, w_ref[...])
        for g in range(DSA_KV_HEADS):
            piece = res[:, g * HEAD_DIM:(g + 1) * HEAD_DIM]
            slab_ref[pl.ds(g * 2 + c, rows, stride=KV_SLABS), :] = piece
            tile_ref[:, (g * 2 + c) * HEAD_DIM:(g * 2 + c + 1) * HEAD_DIM] = piece.astype(BF16)


def _kv_proj(x, w, k_col, v_col):
    m, kd = x.shape
    tm = _pick(m, (512, 256, 128))
    assert k_col % KV_HALF == 0 and v_col % KV_HALF == 0
    kb, vb = k_col // KV_HALF, v_col // KV_HALF
    return pl.pallas_call(
        _kv_proj_kernel,
        grid=(m // tm,),
        in_specs=[pl.BlockSpec((tm, kd), lambda i: (i, 0)),
                  pl.BlockSpec((kd, KV_HALF), lambda i: (0, kb)),
                  pl.BlockSpec((kd, KV_HALF), lambda i: (0, vb))],
        out_specs=[pl.BlockSpec((tm * KV_SLABS, HEAD_DIM), lambda i: (i, 0)),
                   pl.BlockSpec((tm, KV_ROW), lambda i: (i, 0))],
        out_shape=[jax.ShapeDtypeStruct((m * KV_SLABS, HEAD_DIM), F32), jax.ShapeDtypeStruct((m, KV_ROW), BF16)],
        compiler_params=_cparams(("parallel",)),
        name="kv_proj",
    )(x, w, w)


def _slab(ref, n_tok, g, c):
    return ref[pl.ds(g * 2 + c, n_tok, stride=KV_SLABS), :]


def _slab_all(pages, n_tok, g, c):
    return jnp.concatenate([_slab(p, n_tok, g, c) for p in pages], axis=0).astype(BF16)


def _page_specs(n_pages, rows, width, index):
    return [pl.BlockSpec((rows, width), functools.partial(lambda *a, p: (index(*a, p), 0), p=p))
            for p in range(n_pages)]


def _sort_key(x):
    b = pltpu.bitcast(x, I32)
    return jnp.where(b < 0, b ^ jnp.int32(0x7FFFFFFF), b)


def _topk_mask(sc, idx, k, idx_bits, keep_all=False):
    key = _sort_key(sc)
    search = jnp.logical_not(keep_all)

    def count(m):
        return jnp.sum(m.astype(I32), axis=1, keepdims=True)

    t0 = jnp.where(count(key >= 0) >= k, jnp.int32(0), jnp.int32(INT_MIN))

    def vstep(i, t):
        cand = t | (jnp.int32(1) << (30 - i))
        return jnp.where(count(key >= cand) >= k, cand, t)

    thr = lax.fori_loop(0, jnp.where(search, 31, 0), vstep, t0)
    gt = key > thr
    eq = key == thr
    need = k - count(gt)
    crowded = count(eq) > need
    n_steps = jnp.where(search & (jnp.max(crowded.astype(I32)) > 0), idx_bits, 0)

    def istep(i, c):
        cand = c | (jnp.int32(1) << (idx_bits - 1 - i))
        return jnp.where(count(eq & (idx < cand)) < need, cand, c)

    cut = lax.fori_loop(0, n_steps, istep, jnp.zeros_like(thr))
    return keep_all | gt | (eq & (jnp.logical_not(crowded) | (idx <= cut)))


def _stack_heads(ref, n_heads, scale):
    return jnp.concatenate([(ref[:, r * HEAD_DIM:(r + 1) * HEAD_DIM] * scale).astype(BF16) for r in range(n_heads)],
                           axis=0)


def _slope_stack(slopes):
    return jnp.concatenate([jnp.full((1, 1, 1), s, F32) for s in slopes], axis=0)


def _attend(q2, k, v, dm, slope2):
    n_heads = slope2.shape[0]
    rows, keys = dm.shape
    s = _dot_nt(q2, k).reshape(n_heads, rows, keys) - slope2 * dm[None]
    m = jnp.max(s, axis=2, keepdims=True)
    p = jnp.exp2(s - m)
    l = jnp.sum(p, axis=2, keepdims=True)
    o = _dot(p.reshape(n_heads * rows, keys).astype(BF16), v)
    return o * (1.0 / l).reshape(n_heads * rows, 1)


def _spans(seq):
    span = KEY_SPAN if seq % KEY_SPAN == 0 else seq
    return span, seq // span


def _group_slope(g, r, n_heads, n_groups):
    rep = n_heads // n_groups
    table = _slopes(n_heads)
    out = jnp.float32(table[r])
    for gg in range(1, n_groups):
        out = jnp.where(g == gg, jnp.float32(table[gg * rep + r]), out)
    return out


def _dsa_prompt_mask(c, qi_ref, cq_ref, ck_ref, dm_ref, kw, k_keep):
    t = c * Q_TILE + lax.broadcasted_iota(I32, (Q_TILE, 1), 0)
    kpos = lax.broadcasted_iota(I32, (Q_TILE, kw), 1)
    causal = kpos <= t
    ki = ck_ref[0:kw, 0:IDX_DIM].astype(BF16)
    wi = cq_ref[:, IDX_DIM:IDX_DIM + IDX_HEADS] * (IDX_HEADS ** -0.5)
    sc = jnp.zeros((Q_TILE, kw), F32)
    for h in range(IDX_HEADS):
        a = _dot_nt(qi_ref[:, h * IDX_DIM:(h + 1) * IDX_DIM].astype(BF16), ki)
        sc = sc + jnp.maximum(a, 0.0) * wi[:, h:h + 1]
    sc = jnp.where(causal, sc, NEG_INF)
    keep = _topk_mask(sc, kpos, k_keep, max(1, (kw - 1).bit_length()), keep_all=(c + 1) * Q_TILE <= k_keep)
    dm_ref[:, 0:kw] = jnp.where(keep & causal, (t - kpos).astype(F32), MASKED_DIST)


def _dsa_prompt_kernel(q_ref, qi_ref, cq_ref, ck_ref, kv_ref, o_ref, dm_ref, *, seq, k_keep):
    c, g = pl.program_id(1), pl.program_id(2)
    rep = DSA_HEADS // DSA_KV_HEADS
    span, n_var = _spans(seq)
    for v in range(n_var):
        @pl.when((c * Q_TILE) // span == v)
        def _(kw=(v + 1) * span):
            @pl.when(g == 0)
            def _():
                _dsa_prompt_mask(c, qi_ref, cq_ref, ck_ref, dm_ref, kw, k_keep)

            dm = dm_ref[:, 0:kw]
            kg = kv_ref[0:kw, 0:HEAD_DIM]
            vg = kv_ref[0:kw, HEAD_DIM:2 * HEAD_DIM]
            slope2 = _slope_stack([_group_slope(g, r, DSA_HEADS, DSA_KV_HEADS) * LOG2E for r in range(rep)])
            o = _attend(_stack_heads(q_ref, rep, SCALE * LOG2E), kg, vg, dm, slope2)
            for r in range(rep):
                o_ref[:, r * HEAD_DIM:(r + 1) * HEAD_DIM] = o[r * Q_TILE:(r + 1) * Q_TILE].astype(BF16)


def _dsa_prompt(q, qi, h_c, kv, batch, seq):
    nc = seq // Q_TILE
    k_keep = min(IDX_TOPK, seq // 4)
    rep = DSA_HEADS // DSA_KV_HEADS
    tile = lambda b, c, g: (b * nc + c, 0)
    tile_g = lambda b, c, g: (b * nc + c, g)
    return pl.pallas_call(
        functools.partial(_dsa_prompt_kernel, seq=seq, k_keep=k_keep),
        grid=(batch, nc, DSA_KV_HEADS),
        in_specs=[pl.BlockSpec((Q_TILE, rep * HEAD_DIM), tile_g),
                  pl.BlockSpec((Q_TILE, IDX_HEADS * IDX_DIM), tile),
                  pl.BlockSpec((Q_TILE, LANES), tile),
                  pl.BlockSpec((seq, LANES), lambda b, c, g: (b, 0)),
                  pl.BlockSpec((seq, 2 * HEAD_DIM), lambda b, c, g: (b, g))],
        out_specs=pl.BlockSpec((Q_TILE, rep * HEAD_DIM), tile_g),
        out_shape=jax.ShapeDtypeStruct((batch * seq, Q_ROW), BF16),
        scratch_shapes=[pltpu.VMEM((Q_TILE, seq), F32)],
        compiler_params=_cparams(("parallel", "arbitrary", "arbitrary")),
        name="dsa_prompt",
    )(q, qi, h_c, h_c, kv)


def _peb_kernel(pe_ref, w1_ref, o_ref):
    for c in range(2):
        o_ref[c] = _dot_hi(pe_ref[c], w1_ref[c])


def _cmp_pe_bias(cmp_pe, cmp_w1):
    return pl.pallas_call(
        _peb_kernel,
        out_shape=jax.ShapeDtypeStruct((2, 1, HEAD_DIM), F32),
        name="cmp_pe_bias",
    )(cmp_pe.reshape(2, 1, CMP_BLOCK * HEAD_DIM), cmp_w1.reshape(2, CMP_BLOCK * HEAD_DIM, HEAD_DIM))


def _compress_kernel(pt_ref, *refs, n_pages, page):
    pages = refs[:n_pages]
    w1_ref, w2_ref, peb_ref, o_ref = refs[n_pages:]
    sub_per_page = page // CMP_STRIDE
    n_sub = n_pages * sub_per_page
    prow = lax.broadcasted_iota(I32, (page, page), 0)
    pcol = lax.broadcasted_iota(I32, (page, page), 1)
    regroup = (pcol == (prow % sub_per_page) * CMP_STRIDE + prow // sub_per_page).astype(BF16)
    by_pos = []
    for p in pages:
        x = jnp.concatenate([_slab(p, page, gc // 2, gc % 2) for gc in range(KV_SLABS)], axis=1).astype(BF16)
        by_pos.append(_dot(regroup, x))
    for c in range(2):
        lhs = jnp.concatenate([
            jnp.concatenate([
                jnp.concatenate([bp[j * sub_per_page:(j + 1) * sub_per_page,
                                    (g * 2 + c) * HEAD_DIM:(g * 2 + c + 1) * HEAD_DIM] for bp in by_pos],
                                axis=0).astype(BF16)
                for j in range(CMP_STRIDE)], axis=1)
            for g in range(NSA_KV_GROUPS)], axis=0)
        part = _dot(lhs, w1_ref[c].astype(BF16))
        w2 = w2_ref[c].astype(BF16)
        for g in range(NSA_KV_GROUPS):
            col = g * 256 + c * 128
            pg = part[g * n_sub:(g + 1) * n_sub]
            nxt = pltpu.roll(pg[:, HEAD_DIM:], n_sub - 1, 0)
            hid = pg[:, :HEAD_DIM] + nxt + peb_ref[c]
            o_ref[0, :, col:col + HEAD_DIM] = _dot(jax.nn.gelu(hid).astype(BF16), w2)


def _compress(pool, page_table, page, w1r, w2, peb):
    n_seq, n_pages = page_table.shape
    n_sub = n_pages * page // CMP_STRIDE
    page_specs = _page_specs(n_pages, page * KV_SLABS, HEAD_DIM, lambda n, pt, p: pt[n, p])
    return pl.pallas_call(
        functools.partial(_compress_kernel, n_pages=n_pages, page=page),
        grid_spec=pltpu.PrefetchScalarGridSpec(
            num_scalar_prefetch=1,
            grid=(n_seq,),
            in_specs=page_specs + [
                pl.BlockSpec((2, CMP_STRIDE * HEAD_DIM, 2 * HEAD_DIM), lambda n, pt: (0, 0, 0)),
                pl.BlockSpec((2, HEAD_DIM, HEAD_DIM), lambda n, pt: (0, 0, 0)),
                pl.BlockSpec((2, 1, HEAD_DIM), lambda n, pt: (0, 0, 0))],
            out_specs=pl.BlockSpec((1, n_sub, KV_ROW), lambda n, pt: (n, 0, 0))),
        out_shape=jax.ShapeDtypeStruct((n_seq, n_sub, KV_ROW), F32),
        compiler_params=_cparams(("arbitrary",)),
        name="nsa_compress",
    )(page_table, *([pool] * n_pages), w1r, w2, peb)


def _cover(n_cmp_pad, n_slc_pad):
    cs = lax.broadcasted_iota(I32, (n_cmp_pad, n_slc_pad), 0) * CMP_STRIDE
    bs = lax.broadcasted_iota(I32, (n_cmp_pad, n_slc_pad), 1) * SLC_BLOCK
    return ((cs < bs + SLC_BLOCK) & (cs + CMP_BLOCK > bs)).astype(F32)


def _select_blocks(score, t, n_slc, n_keep):
    rows, width = score.shape
    j = lax.broadcasted_iota(I32, (rows, width), 1)
    cur = t // SLC_BLOCK
    forced = (j == 0) | (j == cur) | (j == cur - 1)
    admissible = j * SLC_BLOCK <= t
    score = jnp.where(forced, FORCE_SCORE, jnp.where(admissible, score, NEG_INF))
    rank = jnp.zeros((rows, width), I32)
    for k in range(n_slc):
        sk = score[:, k:k + 1]
        ahead = (sk > score) | ((sk == score) & (j > k))
        rank = rank + ahead.astype(I32)
    return ((rank < n_keep) & (j < n_slc)).astype(F32)


def _select_blocks_t(score_t, t_row, n_keep):
    n_slc, width = score_t.shape
    j = lax.broadcasted_iota(I32, (n_slc, width), 0)
    cur = t_row // SLC_BLOCK
    forced = (j == 0) | (j == cur) | (j == cur - 1)
    admissible = j * SLC_BLOCK <= t_row
    score_t = jnp.where(forced, FORCE_SCORE, jnp.where(admissible, score_t, NEG_INF))
    rank = jnp.zeros((n_slc, width), I32)
    for k in range(n_slc):
        sk = score_t[k:k + 1, :]
        ahead = (sk > score_t) | ((sk == score_t) & (j > k))
        rank = rank + ahead.astype(I32)
    return (rank < n_keep).astype(F32)


def _nsa_prompt_body(c, g, q_ref, gate_ref, ckv_ref, slc_ref, win_ref, o_ref, kw, seq, n_cmp, win_keys):
    t = c * Q_TILE + lax.broadcasted_iota(I32, (Q_TILE, 1), 0)
    t_row = c * Q_TILE + lax.broadcasted_iota(I32, (1, Q_TILE), 1)
    rep = NSA_HEADS // NSA_KV_GROUPS
    slope2 = _slope_stack([_group_slope(g, r, NSA_HEADS, NSA_KV_GROUPS) * LOG2E for r in range(rep)])
    n_cmp_pad = ckv_ref.shape[1]
    n_slc = seq // SLC_BLOCK
    n_keep = min(SLC_TOPN, n_slc)

    kc = lax.broadcasted_iota(I32, (Q_TILE, n_cmp_pad), 1)
    dist_c_i = t - (kc * CMP_STRIDE + (CMP_BLOCK - 1))
    valid_c = (dist_c_i >= 0) & (kc < n_cmp)
    dist_c = dist_c_i.astype(F32)
    cover_bs = lax.broadcasted_iota(I32, (n_slc, n_cmp_pad), 0) * SLC_BLOCK
    cover_cs = lax.broadcasted_iota(I32, (n_slc, n_cmp_pad), 1) * CMP_STRIDE
    cover_t = ((cover_cs < cover_bs + SLC_BLOCK) & (cover_cs + CMP_BLOCK > cover_bs)).astype(F32)
    slc_rows = -(-n_slc // LANES) * LANES
    expand = (lax.broadcasted_iota(I32, (slc_rows, kw), 1) // SLC_BLOCK
              == lax.broadcasted_iota(I32, (slc_rows, kw), 0)).astype(BF16)

    kpos = lax.broadcasted_iota(I32, (Q_TILE, kw), 1)
    causal = kpos <= t
    dist_s = (t - kpos).astype(F32)

    w0 = pl.multiple_of(jnp.maximum(c * Q_TILE + Q_TILE - win_keys, 0), Q_TILE)
    wpos = w0 + lax.broadcasted_iota(I32, (Q_TILE, win_keys), 1)
    dist_w_i = t - wpos
    dm_w = jnp.where((dist_w_i >= 0) & (dist_w_i <= WINDOW), dist_w_i.astype(F32), MASKED_DIST)

    gates = jax.nn.sigmoid(gate_ref[:, 0:3 * rep])

    ck = ckv_ref[0, :, 0:HEAD_DIM].astype(BF16)
    cv = ckv_ref[0, :, HEAD_DIM:2 * HEAD_DIM].astype(BF16)
    q2 = _stack_heads(q_ref, rep, SCALE * LOG2E)
    s = jnp.where(valid_c[None], _dot_nt(q2, ck).reshape(rep, Q_TILE, n_cmp_pad) - slope2 * dist_c[None], NEG_INF)
    m = jnp.max(s, axis=2, keepdims=True)
    p = jnp.where(valid_c[None], jnp.exp2(s - m), 0.0)
    l = jnp.sum(p, axis=2, keepdims=True)
    p = p * (1.0 / jnp.where(l > 0.0, l, 1.0))
    imp = jnp.sum(p, axis=0)
    o_c = _dot(p.reshape(rep * Q_TILE, n_cmp_pad).astype(BF16), cv)
    sel_t = _select_blocks_t(_dot_hi(cover_t, imp.T), t_row, n_keep)
    sel_t = jnp.concatenate([sel_t, jnp.zeros((slc_rows - n_slc, Q_TILE), F32)], axis=0)
    picked = _dot(sel_t.T.astype(BF16), expand) > 0.5
    dm_s = jnp.where(picked & causal, dist_s, MASKED_DIST)
    sk = slc_ref[0:kw, 0:HEAD_DIM]
    sv = slc_ref[0:kw, HEAD_DIM:2 * HEAD_DIM]
    wk = win_ref[pl.ds(w0, win_keys), 0:HEAD_DIM]
    wv = win_ref[pl.ds(w0, win_keys), HEAD_DIM:2 * HEAD_DIM]
    o_s = _attend(q2, sk, sv, dm_s, slope2)
    o_w = _attend(q2, wk, wv, dm_w, slope2)
    gate_col = [jnp.concatenate([gates[:, 3 * r + k:3 * r + k + 1] for r in range(rep)], axis=0) for k in range(3)]
    o = gate_col[0] * o_c + gate_col[1] * o_s + gate_col[2] * o_w
    for r in range(rep):
        o_ref[:, r * HEAD_DIM:(r + 1) * HEAD_DIM] = o[r * Q_TILE:(r + 1) * Q_TILE].astype(BF16)


def _nsa_prompt_kernel(q_ref, gate_ref, ckv_ref, slc_ref, win_ref, o_ref, *, seq, n_cmp, win_keys):
    c, g = pl.program_id(1), pl.program_id(2)
    span, n_var = _spans(seq)
    for v in range(n_var):
        @pl.when((c * Q_TILE) // span == v)
        def _(kw=(v + 1) * span):
            _nsa_prompt_body(c, g, q_ref, gate_ref, ckv_ref, slc_ref, win_ref, o_ref, kw, seq, n_cmp, win_keys)


def _nsa_prompt(q, gate, ckv, slc, win, batch, seq):
    nc = seq // Q_TILE
    n_cmp_pad = ckv.shape[1]
    n_cmp = seq // CMP_STRIDE - CMP_BLOCK // CMP_STRIDE + 1
    win_keys = min(seq, WINDOW + Q_TILE)
    rep = NSA_HEADS // NSA_KV_GROUPS
    tile_g = lambda b, c, g: (b * nc + c, g)
    return pl.pallas_call(
        functools.partial(_nsa_prompt_kernel, seq=seq, n_cmp=n_cmp, win_keys=win_keys),
        grid=(batch, nc, NSA_KV_GROUPS),
        in_specs=[pl.BlockSpec((Q_TILE, rep * HEAD_DIM), tile_g),
                  pl.BlockSpec((Q_TILE, LANES), tile_g),
                  pl.BlockSpec((1, n_cmp_pad, 2 * HEAD_DIM), lambda b, c, g: (b, 0, g)),
                  pl.BlockSpec((seq, 2 * HEAD_DIM), lambda b, c, g: (b, g)),
                  pl.BlockSpec((seq, 2 * HEAD_DIM), lambda b, c, g: (b, g))],
        out_specs=pl.BlockSpec((Q_TILE, rep * HEAD_DIM), tile_g),
        out_shape=jax.ShapeDtypeStruct((batch * seq, Q_ROW), BF16),
        compiler_params=_cparams(("parallel", "arbitrary", "arbitrary")),
        name="nsa_prompt",
    )(q, gate, ckv, slc, win)


def _group_rows(per_group, rep):
    row = lax.broadcasted_iota(I32, per_group[0].shape, 0) // rep
    out = per_group[0]
    for g in range(1, len(per_group)):
        out = jnp.where(row == g, per_group[g], out)
    return out


def _dsa_sample_mask_kernel(pt_ref, *refs, n_pages, page, k_keep):
    n_ki = SEQ_GROUP * n_pages
    ki_pages = refs[:n_ki]
    qi_ref, wi_ref, cnew_ref, keep_ref = refs[n_ki:]
    past = n_pages * page
    width = past + LANES
    rows = []
    for s in range(SEQ_GROUP):
        qi = qi_ref[s].astype(BF16)
        wi = wi_ref[s] * (IDX_HEADS ** -0.5)
        knew = cnew_ref[pl.ds(s, 1), 0:IDX_DIM]
        a_past = jnp.concatenate([_dot(qi, ki_pages[s * n_pages + p][0].astype(BF16))
                                  for p in range(n_pages)], axis=1)
        a_new = _dot_nt(qi, jnp.broadcast_to(knew, (8, IDX_DIM)).astype(BF16))[:, 0:1]
        sc_past = jnp.sum(jnp.maximum(a_past, 0.0) * wi, axis=0, keepdims=True)
        sc_new = jnp.sum(jnp.maximum(a_new, 0.0) * wi, axis=0, keepdims=True)
        rows.append(jnp.concatenate([sc_past, jnp.broadcast_to(sc_new, (1, LANES))], axis=1))
    col = lax.broadcasted_iota(I32, (SEQ_GROUP, width), 1)
    sc = jnp.where(col <= past, jnp.concatenate(rows, axis=0), NEG_INF)
    keep = _topk_mask(sc, col, k_keep, max(1, (width - 1).bit_length())) & (col <= past)
    keep_ref[...] = keep.astype(F32)


def _dsa_sample_mask(kidx_pool_t, page_table, qi, wi, h_c):
    n_seq, n_pages = page_table.shape
    assert n_seq % SEQ_GROUP == 0
    page = kidx_pool_t.shape[2]
    total = n_pages * page + 1
    k_keep = min(IDX_TOPK, total // 4)
    width = n_pages * page + LANES
    ki_specs = [pl.BlockSpec((1, IDX_DIM, page),
                             functools.partial(lambda i, pt, s, p: (pt[i * SEQ_GROUP + s, p], 0, 0), s=s, p=p))
                for s in range(SEQ_GROUP) for p in range(n_pages)]
    return pl.pallas_call(
        functools.partial(_dsa_sample_mask_kernel, n_pages=n_pages, page=page, k_keep=k_keep),
        grid_spec=pltpu.PrefetchScalarGridSpec(
            num_scalar_prefetch=1,
            grid=(n_seq // SEQ_GROUP,),
            in_specs=ki_specs + [
                pl.BlockSpec((SEQ_GROUP, IDX_HEADS, IDX_DIM), lambda i, pt: (i, 0, 0)),
                pl.BlockSpec((SEQ_GROUP, IDX_HEADS, 1), lambda i, pt: (i, 0, 0)),
                pl.BlockSpec((SEQ_GROUP, LANES), lambda i, pt: (i, 0))],
            out_specs=pl.BlockSpec((SEQ_GROUP, width), lambda i, pt: (i, 0))),
        out_shape=jax.ShapeDtypeStruct((n_seq, width), F32),
        compiler_params=_cparams(("arbitrary",)),
        name="dsa_sample_mask",
    )(page_table, *([kidx_pool_t] * (SEQ_GROUP * n_pages)), qi, wi, h_c)


def _dsa_sample_kernel(pt_ref, *refs, n_pages, page):
    kv_pages = refs[:n_pages]
    q_ref, keep_ref, kvnew_ref, o_ref = refs[n_pages:]
    r = pl.program_id(0) % SEQ_GROUP
    past = n_pages * page
    rep = DSA_HEADS // DSA_KV_HEADS
    slopes = _slopes(DSA_HEADS)

    keep = keep_ref[pl.ds(r, 1), :] > 0.5
    keep_past = jnp.broadcast_to(keep[:, 0:past], (DSA_HEADS, past))
    keep_new = jnp.broadcast_to(keep[:, past:past + 1], (DSA_HEADS, 1))

    q = (q_ref[0] * SCALE).astype(BF16)
    q32 = q.astype(F32)
    kvnew = kvnew_ref[pl.ds(r, 1), :]
    slope_col = jnp.concatenate([jnp.full((1, 1), s, F32) for s in slopes], axis=0)
    dist = (past - lax.broadcasted_iota(I32, (DSA_HEADS, past), 1)).astype(F32)
    s_g, s_new_g = [], []
    for g in range(DSA_KV_HEADS):
        s_g.append(_dot_nt(q, _slab_all(kv_pages, page, g, 0)))
        s_new_g.append(jnp.sum(q32 * kvnew[:, g * 256:g * 256 + 128], axis=1, keepdims=True))
    s = jnp.where(keep_past, _group_rows(s_g, rep) - slope_col * dist, NEG_INF)
    s_new = jnp.where(keep_new, _group_rows(s_new_g, rep), NEG_INF)
    m = jnp.maximum(jnp.max(s, axis=1, keepdims=True), s_new)
    p = jnp.where(keep_past, jnp.exp(s - m), 0.0)
    p_new = jnp.where(keep_new, jnp.exp(s_new - m), 0.0)
    l = jnp.sum(p, axis=1, keepdims=True) + p_new
    inv = 1.0 / jnp.where(l > 0.0, l, 1.0)
    pb = p.astype(BF16)
    p_new = p_new.astype(BF16).astype(F32)
    o_g = []
    for g in range(DSA_KV_HEADS):
        o_g.append(p_new * kvnew[:, g * 256 + 128:g * 256 + 256] + _dot(pb, _slab_all(kv_pages, page, g, 1)))
    o_ref[0] = (_group_rows(o_g, rep) * inv).astype(BF16)


def _dsa_sample(kv_pool, page_table, page, q, keep, kv_new):
    n_seq, n_pages = page_table.shape
    width = keep.shape[1]
    kv_specs = _page_specs(n_pages, page * KV_SLABS, HEAD_DIM, lambda n, pt, p: pt[n, p])
    return pl.pallas_call(
        functools.partial(_dsa_sample_kernel, n_pages=n_pages, page=page),
        grid_spec=pltpu.PrefetchScalarGridSpec(
            num_scalar_prefetch=1,
            grid=(n_seq,),
            in_specs=kv_specs + [
                pl.BlockSpec((1, DSA_HEADS, HEAD_DIM), lambda n, pt: (n, 0, 0)),
                pl.BlockSpec((SEQ_GROUP, width), lambda n, pt: (n // SEQ_GROUP, 0)),
                pl.BlockSpec((SEQ_GROUP, KV_ROW), lambda n, pt: (n // SEQ_GROUP, 0))],
            out_specs=pl.BlockSpec((1, DSA_HEADS, HEAD_DIM), lambda n, pt: (n, 0, 0))),
        out_shape=jax.ShapeDtypeStruct((n_seq, DSA_HEADS, HEAD_DIM), BF16),
        compiler_params=_cparams(("arbitrary",)),
        name="dsa_sample",
    )(page_table, *([kv_pool] * n_pages), q, keep, kv_new)


def _nsa_sample_kernel(pt_ref, *refs, n_pages, page, n_cmp):
    slc_pages = refs[:n_pages]
    q_ref, gate_ref, ckv_ref, win_ref, slcnew_ref, winnew_ref, winslab_ref, o_ref, wout_ref = refs[n_pages:]
    n = pl.program_id(0)
    past = n_pages * page
    heads = NSA_HEADS
    rep = NSA_HEADS // NSA_KV_GROUPS
    slopes = _slopes(NSA_HEADS)
    slope_col = jnp.concatenate([jnp.full((1, 1), s, F32) for s in slopes], axis=0)
    q = (q_ref[0] * SCALE).astype(BF16)
    q32 = q.astype(F32)
    gates = jax.nn.sigmoid(gate_ref[0])
    n_cmp_pad = ckv_ref.shape[1]
    total = past + 1
    n_slc = -(-total // SLC_BLOCK)
    n_keep = min(SLC_TOPN, n_slc)
    slc_pad = -(-n_slc // LANES) * LANES

    kc = lax.broadcasted_iota(I32, (heads, n_cmp_pad), 1)
    dist_c_i = past - (kc * CMP_STRIDE + (CMP_BLOCK - 1))
    valid_c = (dist_c_i >= 0) & (kc < n_cmp)
    s_g = [_dot_nt(q, ckv_ref[0, :, g * 256:g * 256 + 128].astype(BF16)) for g in range(NSA_KV_GROUPS)]
    s = jnp.where(valid_c, _group_rows(s_g, rep) - slope_col * dist_c_i.astype(F32), NEG_INF)
    m = jnp.max(s, axis=1, keepdims=True)
    p = jnp.where(valid_c, jnp.exp(s - m), 0.0)
    l = jnp.sum(p, axis=1, keepdims=True)
    p = p * (1.0 / jnp.where(l > 0.0, l, 1.0))
    pb = p.astype(BF16)
    o_c = _group_rows([_dot(pb, ckv_ref[0, :, g * 256 + 128:g * 256 + 256].astype(BF16))
                       for g in range(NSA_KV_GROUPS)], rep)
    same_group = (lax.broadcasted_iota(I32, (heads, heads), 0) // rep
                  == lax.broadcasted_iota(I32, (heads, heads), 1) // rep).astype(F32)
    imp = _dot_hi(same_group, p)
    score = _dot_hi(imp, _cover(n_cmp_pad, slc_pad))
    sel = _select_blocks(score, jnp.full((heads, 1), past, I32), n_slc, n_keep)
    expand = (lax.broadcasted_iota(I32, (slc_pad, past), 1) // SLC_BLOCK
              == lax.broadcasted_iota(I32, (slc_pad, past), 0)).astype(BF16)
    valid_s = _dot(sel.astype(BF16), expand) > 0.5
    new_blk = past // SLC_BLOCK
    valid_s_new = sel[:, new_blk:new_blk + 1] > 0.5

    def attend_with_new(s_past, valid_past, s_new, valid_new, dist_past, v_of, v_new_of):
        s_p = jnp.where(valid_past, s_past - slope_col * dist_past, NEG_INF)
        s_n = jnp.where(valid_new, s_new, NEG_INF)
        mm = jnp.maximum(jnp.max(s_p, axis=1, keepdims=True), s_n)
        pp = jnp.where(valid_past, jnp.exp(s_p - mm), 0.0)
        pn = jnp.where(valid_new, jnp.exp(s_n - mm), 0.0)
        ll = jnp.sum(pp, axis=1, keepdims=True) + pn
        inv = 1.0 / jnp.where(ll > 0.0, ll, 1.0)
        ppb = pp.astype(BF16)
        pn = pn.astype(BF16).astype(F32)
        outs = [v_of(ppb, g) + pn * v_new_of(g) for g in range(NSA_KV_GROUPS)]
        return _group_rows(outs, rep) * inv

    slcnew = slcnew_ref[pl.ds(n, 1), :]
    s_g, s_new_g = [], []
    for g in range(NSA_KV_GROUPS):
        s_g.append(_dot_nt(q, _slab_all(slc_pages, page, g, 0)))
        s_new_g.append(jnp.sum(q32 * slcnew[:, g * 256:g * 256 + 128], axis=1, keepdims=True))
    dist_s = (past - lax.broadcasted_iota(I32, (heads, past), 1)).astype(F32)

    def slc_v(ppb, g):
        return _dot(ppb, _slab_all(slc_pages, page, g, 1))

    o_s = attend_with_new(_group_rows(s_g, rep), valid_s, _group_rows(s_new_g, rep), valid_s_new, dist_s, slc_v,
                          lambda g: slcnew[:, g * 256 + 128:g * 256 + 256])

    n_buf = win_ref.shape[0] // KV_SLABS
    winnew = winnew_ref[pl.ds(n, 1), :]
    dist_w_i = n_buf - lax.broadcasted_iota(I32, (heads, n_buf), 1)
    valid_w = dist_w_i <= WINDOW
    s_g = [_dot_nt(q, _slab(win_ref, n_buf, g, 0).astype(BF16)) for g in range(NSA_KV_GROUPS)]
    s_new_g = [jnp.sum(q32 * winnew[:, g * 256:g * 256 + 128], axis=1, keepdims=True)
               for g in range(NSA_KV_GROUPS)]
    o_w = attend_with_new(_group_rows(s_g, rep), valid_w, _group_rows(s_new_g, rep),
                          jnp.full((heads, 1), True), dist_w_i.astype(F32),
                          lambda ppb, g: _dot(ppb, _slab(win_ref, n_buf, g, 1).astype(BF16)),
                          lambda g: winnew[:, g * 256 + 128:g * 256 + 256])

    o_ref[0] = (gates[:, 0:1] * o_c + gates[:, 1:2] * o_s + gates[:, 2:3] * o_w).astype(BF16)

    keep_rows = (n_buf - 1) * KV_SLABS
    wout_ref[0:keep_rows, :] = win_ref[KV_SLABS:n_buf * KV_SLABS, :]
    wout_ref[keep_rows:keep_rows + KV_SLABS, :] = winslab_ref[pl.ds(pl.multiple_of(n * KV_SLABS, KV_SLABS), KV_SLABS), :]


def _nsa_sample(slc_pool, page_table, page, q, gate, ckv, win_state, slc_new, win_new, win_new_slab):
    n_seq, n_pages = page_table.shape
    n_cmp_pad = ckv.shape[1]
    n_cmp = n_pages * page // CMP_STRIDE - CMP_BLOCK // CMP_STRIDE + 1
    n_buf = win_state.shape[0] // (n_seq * KV_SLABS)
    slc_specs = _page_specs(n_pages, page * KV_SLABS, HEAD_DIM, lambda n, pt, p: pt[n, p])
    return pl.pallas_call(
        functools.partial(_nsa_sample_kernel, n_pages=n_pages, page=page, n_cmp=n_cmp),
        grid_spec=pltpu.PrefetchScalarGridSpec(
            num_scalar_prefetch=1,
            grid=(n_seq,),
            in_specs=slc_specs + [
                pl.BlockSpec((1, NSA_HEADS, HEAD_DIM), lambda n, pt: (n, 0, 0)),
                pl.BlockSpec((1, NSA_HEADS, 3), lambda n, pt: (n, 0, 0)),
                pl.BlockSpec((1, n_cmp_pad, KV_ROW), lambda n, pt: (n, 0, 0)),
                pl.BlockSpec((n_buf * KV_SLABS, HEAD_DIM), lambda n, pt: (n, 0)),
                pl.BlockSpec((n_seq, KV_ROW), lambda n, pt: (0, 0)),
                pl.BlockSpec((n_seq, KV_ROW), lambda n, pt: (0, 0)),
                pl.BlockSpec((n_seq * KV_SLABS, HEAD_DIM), lambda n, pt: (0, 0))],
            out_specs=[pl.BlockSpec((1, NSA_HEADS, HEAD_DIM), lambda n, pt: (n, 0, 0)),
                       pl.BlockSpec((n_buf * KV_SLABS, HEAD_DIM), lambda n, pt: (n, 0))]),
        out_shape=[jax.ShapeDtypeStruct((n_seq, NSA_HEADS, HEAD_DIM), BF16),
                   jax.ShapeDtypeStruct(win_state.shape, F32)],
        compiler_params=_cparams(("arbitrary",)),
        name="nsa_sample",
    )(page_table, *([slc_pool] * n_pages), q, gate, ckv, win_state, slc_new, win_new, win_new_slab)


def _ln(v, g, b):
    mu = jnp.mean(v, axis=1, keepdims=True)
    d = v - mu
    var = jnp.mean(d * d, axis=1, keepdims=True)
    return d * lax.rsqrt(var + LN_EPS) * g + b


def _ln_router_kernel(xa_ref, mixa_ref, xb_ref, mixb_ref, g_ref, b_ref, wr_ref, br_ref, x1_ref, e_ref, gate_ref, *,
                      alpha, blocks_a):
    from_a = pl.program_id(0) < blocks_a
    pre = jnp.where(from_a, alpha * xa_ref[...] + mixa_ref[...], alpha * xb_ref[...] + mixb_ref[...])
    x1 = _ln(pre, g_ref[...], b_ref[...])
    x1_ref[...] = x1
    logits = _dot_hi(x1, wr_ref[...]) + br_ref[...]
    rows = logits.shape[0]
    lane = lax.broadcasted_iota(I32, (rows, LANES), 1)
    big = jnp.int32(LANES)
    is_grp = lane < N_GROUPS
    lg = jnp.where(is_grp, logits, -jnp.inf)
    mg = jnp.max(lg, axis=1, keepdims=True)
    grp = jnp.min(jnp.where(lg == mg, lane, big), axis=1, keepdims=True)
    p_grp = 1.0 / jnp.sum(jnp.where(is_grp, jnp.exp(lg - mg), 0.0), axis=1, keepdims=True)
    ex = lane - N_GROUPS
    in_grp = (ex >= 0) & (ex < N_EXPERTS) & (ex // EXPERTS_PER_GROUP == grp)
    le = jnp.where(in_grp, logits, -jnp.inf)
    m1 = jnp.max(le, axis=1, keepdims=True)
    e1 = jnp.min(jnp.where(le == m1, ex, big), axis=1, keepdims=True)
    le2 = jnp.where(ex == e1, -jnp.inf, le)
    m2 = jnp.max(le2, axis=1, keepdims=True)
    e2 = jnp.min(jnp.where(le2 == m2, ex, big), axis=1, keepdims=True)
    z = jnp.sum(jnp.where(in_grp, jnp.exp(le - m1), 0.0), axis=1, keepdims=True)
    p1 = 1.0 / z
    p2 = jnp.exp(m2 - m1) / z
    g1 = p_grp * p1 / (p1 + p2)
    g2 = p_grp * p2 / (p1 + p2)
    e_ref[...] = jnp.where(lane == 0, e1, jnp.where(lane == 1, e2, 0))
    gate_ref[...] = jnp.where(lane == 0, g1, jnp.where(lane == 1, g2, 0.0))


def _ln_router(xa, mixa, xb, mixb, g, b, wr, br, alpha):
    (ma, d), mb = xa.shape, xb.shape[0]
    tm = _pick(mb, (128, 64, 32, 16, 8))
    assert ma % tm == 0 and mb % tm == 0
    blocks_a, blocks_b = ma // tm, mb // tm
    seg_a = lambda i: (jnp.minimum(i, blocks_a - 1), 0)
    seg_b = lambda i: (jnp.maximum(i - blocks_a, 0), 0)
    row = lambda i: (i, 0)
    fixed = lambda i: (0, 0)
    m = ma + mb
    return pl.pallas_call(
        functools.partial(_ln_router_kernel, alpha=alpha, blocks_a=blocks_a),
        grid=(blocks_a + blocks_b,),
        in_specs=[pl.BlockSpec((tm, d), seg_a), pl.BlockSpec((tm, d), seg_a),
                  pl.BlockSpec((tm, d), seg_b), pl.BlockSpec((tm, d), seg_b),
                  pl.BlockSpec((1, d), fixed), pl.BlockSpec((1, d), fixed),
                  pl.BlockSpec((d, LANES), fixed), pl.BlockSpec((1, LANES), fixed)],
        out_specs=[pl.BlockSpec((tm, d), row), pl.BlockSpec((tm, LANES), row), pl.BlockSpec((tm, LANES), row)],
        out_shape=[jax.ShapeDtypeStruct((m, d), F32), jax.ShapeDtypeStruct((m, LANES), I32),
                   jax.ShapeDtypeStruct((m, LANES), F32)],
        compiler_params=_cparams(("arbitrary",)),
        name="ln_router",
    )(xa, mixa, xb, mixb, g, b, wr, br)


DISPATCH_UNROLL = 8


def _dispatch_kernel(src_ref, rows_ref, x_ref, o_ref, buf_ref, sem):
    i = pl.program_id(0)
    n_rows = rows_ref[i]

    @pl.when(i == 0)
    def _():
        buf_ref[...] = jnp.zeros_like(buf_ref)

    def row_copy(r):
        tok = src_ref[i * MOE_ROWS + r]
        return pltpu.make_async_copy(x_ref.at[pl.ds(tok, 1)], buf_ref.at[pl.ds(r, 1)], sem)

    def trips(fn):
        def trip(t, carry):
            for u in range(DISPATCH_UNROLL):
                r = t * DISPATCH_UNROLL + u

                @pl.when(r < n_rows)
                def _():
                    fn(row_copy(r))

            return carry

        lax.fori_loop(0, (n_rows + DISPATCH_UNROLL - 1) // DISPATCH_UNROLL, trip, 0)

    trips(lambda cp: cp.start())
    trips(lambda cp: cp.wait())
    o_ref[...] = buf_ref[...].astype(BF16)


def _dispatch(src_tok, rows_in_block, x1, nb):
    d = x1.shape[1]
    return pl.pallas_call(
        _dispatch_kernel,
        grid_spec=pltpu.PrefetchScalarGridSpec(
            num_scalar_prefetch=2,
            grid=(nb,),
            in_specs=[pl.BlockSpec(memory_space=pl.ANY)],
            out_specs=pl.BlockSpec((MOE_ROWS, d), lambda i, s, u: (i, 0)),
            scratch_shapes=[pltpu.VMEM((MOE_ROWS, d), F32), pltpu.SemaphoreType.DMA(())]),
        out_shape=jax.ShapeDtypeStruct((nb * MOE_ROWS, d), BF16),
        compiler_params=_cparams(("arbitrary",)),
        name="moe_dispatch",
    )(src_tok, rows_in_block, x1)


def _expert_mm_kernel(run_ref, rune_ref, nruns_ref, nused_ref, x_ref, *refs, n_w, tn, epilogue):
    w_hbm = refs[:n_w]
    o_ref = refs[n_w]
    w_buf = refs[n_w + 1:2 * n_w + 1]
    w16 = refs[2 * n_w + 1:3 * n_w + 1]
    sem, state = refs[3 * n_w + 1], refs[3 * n_w + 2]
    j, i = pl.program_id(0), pl.program_id(1)
    n_runs = nruns_ref[0]
    total = pl.num_programs(0) * n_runs
    cur = j * n_runs + run_ref[i]

    def item_copies(item):
        e = rune_ref[item % n_runs]
        col = pl.multiple_of((item // n_runs) * tn, tn)
        slot = item % MOE_RING
        return [pltpu.make_async_copy(w_hbm[a].at[e, :, pl.ds(col, tn)], w_buf[a].at[slot], sem.at[a, slot])
                for a in range(n_w)]

    @pl.when((j == 0) & (i == 0))
    def _():
        state[0] = 0
        state[1] = -1

    def request(item, carry):
        for cp in item_copies(item):
            cp.start()
        return carry

    limit = jnp.minimum(cur + MOE_RING, total)
    lax.fori_loop(state[0], limit, request, 0)
    state[0] = jnp.maximum(state[0], limit)

    @pl.when(state[1] != cur)
    def _():
        for cp in item_copies(cur):
            cp.wait()
        slot = cur % MOE_RING
        for a in range(n_w):
            w16[a][...] = w_buf[a][slot].astype(BF16)
        state[1] = cur

    @pl.when(i < nused_ref[0])
    def _():
        x = x_ref[...]
        o_ref[...] = epilogue(*[_dot(x, w16[a][...]) for a in range(n_w)]).astype(o_ref.dtype)

    @pl.when(i >= nused_ref[0])
    def _():
        o_ref[...] = jnp.zeros_like(o_ref)


def _expert_mm(plan, x, weights, tn, epilogue, out_dtype, name):
    run_of_block, run_e, n_runs, n_used = plan
    nb = run_of_block.shape[0]
    kd, n = weights[0].shape[1], weights[0].shape[2]
    n_w = len(weights)
    return pl.pallas_call(
        functools.partial(_expert_mm_kernel, n_w=n_w, tn=tn, epilogue=epilogue),
        grid_spec=pltpu.PrefetchScalarGridSpec(
            num_scalar_prefetch=4,
            grid=(n // tn, nb),
            in_specs=[pl.BlockSpec((MOE_ROWS, kd), lambda j, i, *_: (i, 0))]
            + [pl.BlockSpec(memory_space=pl.ANY)] * n_w,
            out_specs=pl.BlockSpec((MOE_ROWS, tn), lambda j, i, *_: (i, j)),
            scratch_shapes=[pltpu.VMEM((MOE_RING, kd, tn), F32) for _ in range(n_w)]
            + [pltpu.VMEM((kd, tn), BF16) for _ in range(n_w)]
            + [pltpu.SemaphoreType.DMA((n_w, MOE_RING)), pltpu.SMEM((2,), I32)]),
        out_shape=jax.ShapeDtypeStruct((nb * MOE_ROWS, n), out_dtype),
        compiler_params=_cparams(("arbitrary", "arbitrary")),
        name=name,
    )(run_of_block, run_e, n_runs, n_used, x, *weights)


def _swiglu(a, u):
    return a * jax.nn.sigmoid(a) * u


def _combine_kernel(dest_ref, y_ref, x1_ref, gate_ref, g_ref, b_ref, o_ref, buf_ref, sem, *, tm, alpha, off):
    i = pl.program_id(0) + off

    def row_copy(r, k):
        slot = dest_ref[(i * tm + r) * EXPERT_TOPK + k]
        return pltpu.make_async_copy(y_ref.at[pl.ds(slot, 1)], buf_ref.at[k, pl.ds(r, 1)], sem)

    def start(r, carry):
        for k in range(EXPERT_TOPK):
            row_copy(r, k).start()
        return carry

    lax.fori_loop(0, tm, start, 0, unroll=4)

    def wait(r, carry):
        for k in range(EXPERT_TOPK):
            row_copy(r, k).wait()
        return carry

    lax.fori_loop(0, tm, wait, 0, unroll=4)
    gate = gate_ref[...]
    f = gate[:, 0:1] * buf_ref[0] + gate[:, 1:2] * buf_ref[1]
    o_ref[...] = _ln(alpha * x1_ref[...] + f, g_ref[...], b_ref[...])


def _combine(dest, ybuf, x1, gate, row_offset, m, g, b, alpha):
    d = x1.shape[1]
    tm = _pick(m, (128,))
    assert row_offset % tm == 0
    off = row_offset // tm
    return pl.pallas_call(
        functools.partial(_combine_kernel, tm=tm, alpha=alpha, off=off),
        grid_spec=pltpu.PrefetchScalarGridSpec(
            num_scalar_prefetch=1,
            grid=(m // tm,),
            in_specs=[pl.BlockSpec(memory_space=pl.ANY),
                      pl.BlockSpec((tm, d), lambda i, dd: (i + off, 0)),
                      pl.BlockSpec((tm, LANES), lambda i, dd: (i + off, 0)),
                      pl.BlockSpec((1, d), lambda i, dd: (0, 0)),
                      pl.BlockSpec((1, d), lambda i, dd: (0, 0))],
            out_specs=pl.BlockSpec((tm, d), lambda i, dd: (i, 0)),
            scratch_shapes=[pltpu.VMEM((EXPERT_TOPK, tm, d), F32), pltpu.SemaphoreType.DMA(())]),
        out_shape=jax.ShapeDtypeStruct((m, d), F32),
        compiler_params=_cparams(("arbitrary",)),
        name="moe_combine",
    )(dest, ybuf, x1, gate, g, b)


def _route_plan(e_all, nb):
    a_n = e_all.shape[0]
    onehot = (e_all[:, None] == jnp.arange(N_EXPERTS, dtype=I32)[None, :]).astype(I32)
    before = jnp.cumsum(onehot, axis=0) - onehot
    rank = jnp.sum(before * onehot, axis=1)
    counts = jnp.sum(onehot, axis=0)
    pad_counts = (counts + MOE_ROWS - 1) // MOE_ROWS * MOE_ROWS
    pad_ends = jnp.cumsum(pad_counts)
    pad_starts = pad_ends - pad_counts
    dest = (pad_starts[e_all] + rank).astype(I32)
    n_used = (pad_ends[-1] // MOE_ROWS).astype(I32)
    block_e = jnp.searchsorted(pad_ends, jnp.arange(nb, dtype=I32) * MOE_ROWS, side="right").astype(I32)
    owns = (counts > 0).astype(I32)
    run_id = jnp.cumsum(owns) - owns
    n_runs = jnp.sum(owns).astype(I32)
    last_e = jnp.max(jnp.where(counts > 0, jnp.arange(N_EXPERTS, dtype=I32), 0)).astype(I32)
    block_e = jnp.where(jnp.arange(nb, dtype=I32) < n_used, jnp.minimum(block_e, N_EXPERTS - 1), last_e)
    run_of_block = run_id[block_e].astype(I32)
    run_e = jnp.full((N_EXPERTS,), last_e, I32).at[jnp.where(counts > 0, run_id, N_EXPERTS)].set(
        jnp.arange(N_EXPERTS, dtype=I32), mode="drop")
    src_tok = jnp.zeros((nb * MOE_ROWS,), I32).at[dest].set(jnp.arange(a_n, dtype=I32) // EXPERT_TOPK)
    blk = jnp.arange(nb, dtype=I32)
    rows_in_block = jnp.where(
        blk < n_used, jnp.clip(counts[block_e] - (blk * MOE_ROWS - pad_starts[block_e]), 0, MOE_ROWS), 0).astype(I32)
    return dest, (run_of_block, run_e, n_runs.reshape(1), n_used.reshape(1)), src_tok, rows_in_block


def kernel(x_prompt, x_sample, cache_dsa_kv, cache_dsa_kidx, cache_nsa_cmp_kv, cache_nsa_slc_kv, state_nsa_win_kv, page_table, w_in, w_out, cmp_w1, cmp_w2, cmp_pe, ln1_g, ln1_b, w_router_group, b_router_group, w_router_expert, b_router_expert, w_gate, w_up, w_down, ln2_g, ln2_b):
    depth = w_in.shape[0]
    assert depth == 1 and x_sample.shape[1] == 1
    batch, seq, d_model = x_prompt.shape
    n_seq = x_sample.shape[0]
    n_pool, page = cache_dsa_kv.shape[1], cache_dsa_kv.shape[2]
    alpha = (2.0 * depth) ** 0.25
    m_p = batch * seq
    xp = x_prompt.reshape(m_p, d_model)
    xs = x_sample.reshape(n_seq, d_model)
    xp16, xs16 = xp.astype(BF16), xs.astype(BF16)

    w = w_in[0]
    t0, t1 = 4176, 9296
    rep_b = NSA_HEADS // NSA_KV_GROUPS
    w_head = w[:, :t0 + 48].astype(BF16)
    w_tail = w[:, t0:t1].astype(BF16)
    w_gates = jnp.pad(w[:, t1:].reshape(d_model, NSA_KV_GROUPS, 3 * rep_b),
                      ((0, 0), (0, 0), (0, LANES - 3 * rep_b))).reshape(d_model, NSA_KV_GROUPS * LANES).astype(BF16)
    c_dq, c_dk, c_dv, c_iq, c_ik = 0, 2048, 2560, 3072, 4096
    c_nq, c_ck, c_cv, c_sk, c_sv, c_wk, c_wv = 0, 2048, 2560, 3072, 3584, 4096, 4608

    def project(x16):
        qa = _matmul([x16], w_head, c_dq, Q_ROW)
        qi = _matmul([x16], w_head, c_iq, IDX_HEADS * IDX_DIM)
        hc = _matmul([x16], w_head, c_ik, LANES)
        qb = _matmul([x16], w_tail, c_nq, Q_ROW)
        gate = _matmul([x16], w_gates)
        kv = [_kv_proj(x16, w_head, c_dk, c_dv), _kv_proj(x16, w_tail, c_ck, c_cv),
              _kv_proj(x16, w_tail, c_sk, c_sv), _kv_proj(x16, w_tail, c_wk, c_wv)]
        return qa, qi, hc, qb, gate, kv

    qa_p, qi_p, hc_p, qb_p, gate_p, kv_p = project(xp16)
    qa_s, qi_s, hc_s, qb_s, gate_s, kv_s = project(xs16)

    w1r = cmp_w1[0].reshape(2, CMP_BLOCK // CMP_STRIDE, CMP_STRIDE, HEAD_DIM, HEAD_DIM)
    w1r = jnp.transpose(w1r, (0, 2, 3, 1, 4)).reshape(2, CMP_STRIDE * HEAD_DIM, 2 * HEAD_DIM)
    peb = _cmp_pe_bias(cmp_pe[0], cmp_w1[0])
    pages_per_seq = seq // page
    ident = jnp.arange(batch * pages_per_seq, dtype=I32).reshape(batch, pages_per_seq)
    ckv_p = _compress(kv_p[1][0], ident, page, w1r, cmp_w2[0], peb)
    ckv_s = _compress(cache_nsa_cmp_kv.reshape(-1, HEAD_DIM), page_table, page, w1r, cmp_w2[0], peb)

    oa_p = _dsa_prompt(qa_p, qi_p, hc_p, kv_p[0][1], batch, seq)
    ob_p = _nsa_prompt(qb_p, gate_p, ckv_p, kv_p[2][1], kv_p[3][1], batch, seq)

    keep_s = _dsa_sample_mask(jnp.swapaxes(cache_dsa_kidx[0], 1, 2), page_table,
                              qi_s.reshape(n_seq, IDX_HEADS, IDX_DIM),
                              hc_s[:, IDX_DIM:IDX_DIM + IDX_HEADS].reshape(n_seq, IDX_HEADS, 1), hc_s)
    oa_s = _dsa_sample(cache_dsa_kv.reshape(-1, HEAD_DIM), page_table, page,
                       qa_s.reshape(n_seq, DSA_HEADS, HEAD_DIM), keep_s, kv_s[0][1].astype(F32))
    ob_s, win_s = _nsa_sample(cache_nsa_slc_kv.reshape(-1, HEAD_DIM), page_table, page,
                              qb_s.reshape(n_seq, NSA_HEADS, HEAD_DIM),
                              gate_s.reshape(n_seq, NSA_KV_GROUPS, LANES)[:, :, :3 * rep_b].reshape(n_seq, NSA_HEADS, 3),
                              ckv_s,
                              state_nsa_win_kv.reshape(-1, HEAD_DIM), kv_s[2][1].astype(F32),
                              kv_s[3][1].astype(F32), kv_s[3][0])

    w_o = w_out[0].astype(BF16)
    wr = jnp.concatenate([w_router_group[0], w_router_expert[0],
                          jnp.zeros((d_model, LANES - N_GROUPS - N_EXPERTS), F32)], axis=1)
    br = jnp.concatenate([b_router_group[0], b_router_expert[0],
                          jnp.zeros((LANES - N_GROUPS - N_EXPERTS,), F32)]).reshape(1, LANES)
    g1, b1 = ln1_g[0].reshape(1, d_model), ln1_b[0].reshape(1, d_model)
    g2, b2 = ln2_g[0].reshape(1, d_model), ln2_b[0].reshape(1, d_model)
    x1, e_tok, gt_tok = _ln_router(
        xp, _matmul([oa_p, ob_p], w_o),
        xs, _matmul([oa_s.reshape(n_seq, Q_ROW), ob_s.reshape(n_seq, Q_ROW)], w_o), g1, b1, wr, br, alpha)

    e_all = e_tok[:, :EXPERT_TOPK].reshape(-1)
    a_n = e_all.shape[0]
    nb = -(-(a_n + N_EXPERTS * (MOE_ROWS - 1)) // MOE_ROWS)
    dest, plan, src_tok, rows_in_block = _route_plan(e_all, nb)
    xbuf = _dispatch(src_tok, rows_in_block, x1, nb)
    d_expert = w_gate.shape[3]
    hid = _expert_mm(plan, xbuf, [w_gate[0], w_up[0]], _pick(d_expert, (MOE_CHUNK, LANES)), _swiglu, BF16,
                     "moe_gate_up")
    ybuf = _expert_mm(plan, hid, [w_down[0]], _pick(d_model, (2048, 1024, 512, 256, LANES)), lambda y: y, F32,
                      "moe_down")
    y_p = _combine(dest, ybuf, x1, gt_tok, 0, m_p, g2, b2, alpha)
    y_s = _combine(dest, ybuf, x1, gt_tok, m_p, n_seq, g2, b2, alpha)

    def state(slab, lead):
        return slab.reshape((1,) + lead + (NSA_KV_GROUPS, 2, HEAD_DIM))

    n_win = min(WINDOW, seq)
    win_p = state(kv_p[3][0], (batch, seq))[:, :, seq - n_win:]
    return (y_p.reshape(batch, seq, d_model), y_s.reshape(n_seq, 1, d_model),
            state(kv_p[0][0], (batch, seq)), state(kv_s[0][0], (n_seq, 1)),
            hc_p[:, :IDX_DIM].reshape(1, batch, seq, IDX_DIM), hc_s[:, :IDX_DIM].reshape(1, n_seq, 1, IDX_DIM),
            state(kv_p[1][0], (batch, seq)), state(kv_s[1][0], (n_seq, 1)),
            state(kv_p[2][0], (batch, seq)), state(kv_s[2][0], (n_seq, 1)),
            win_p, state(win_s, (n_seq, state_nsa_win_kv.shape[2])))
```

```python
import functools

import jax
import jax.numpy as jnp
from jax import lax
from jax.experimental import pallas as pl
from jax.experimental.pallas import tpu as pltpu

F32 = jnp.float32
BF16 = jnp.bfloat16
I32 = jnp.int32

HEAD_DIM = 128
DSA_HEADS = 16
DSA_KV_HEADS = 4
IDX_HEADS = 16
IDX_DIM = 64
IDX_TOPK = 256
NSA_HEADS = 16
NSA_KV_GROUPS = 4
CMP_STRIDE = 16
CMP_BLOCK = 32
SLC_BLOCK = 64
SLC_TOPN = 16
WINDOW = 512
N_GROUPS = 8
EXPERTS_PER_GROUP = 8
N_EXPERTS = N_GROUPS * EXPERTS_PER_GROUP
EXPERT_TOPK = 2
LN_EPS = 1e-5
NEG_INF = -1e30
FORCE_SCORE = 1e9

KV_ROW = 2 * DSA_KV_HEADS * HEAD_DIM
KV_SLABS = 2 * DSA_KV_HEADS
KV_HALF = DSA_KV_HEADS * HEAD_DIM
Q_ROW = DSA_HEADS * HEAD_DIM
Q_TILE = 128
KEY_SPAN = 256
SEQ_GROUP = 8
LANES = 128
MOE_ROWS = 384
MOE_RING = 4
MOE_CHUNK = 256
VMEM_LIMIT = 56 * 1024 * 1024
SCALE = HEAD_DIM ** -0.5
LOG2E = 1.4426950408889634
MASKED_DIST = 1e32
INT_MIN = -(2 ** 31)


def _slopes(n):
    return [2.0 ** (-8.0 * i / n) for i in range(1, n + 1)]


def _cparams(sem):
    return pltpu.CompilerParams(dimension_semantics=sem, vmem_limit_bytes=VMEM_LIMIT)


def _dot(a, b):
    return jnp.dot(a, b, preferred_element_type=F32)


def _dot_nt(a, b):
    return lax.dot_general(a, b, (((1,), (1,)), ((), ())), preferred_element_type=F32)


def _dot_hi(a, b):
    return jnp.dot(a, b, preferred_element_type=F32, precision=lax.Precision.HIGHEST)


def _pick(n, cands):
    for c in cands:
        if n % c == 0:
            return c
    return n


def _mm_kernel(*refs, n_lhs):
    x_refs, w_refs, o_ref = refs[:n_lhs], refs[n_lhs:2 * n_lhs], refs[2 * n_lhs]
    acc = _dot(x_refs[0][...], w_refs[0][...])
    for x_ref, w_ref in zip(x_refs[1:], w_refs[1:]):
        acc = acc + _dot(x_ref[...], w_ref[...])
    o_ref[...] = acc.astype(o_ref.dtype)


def _matmul(xs, w, n0=0, n=None):
    m = xs[0].shape[0]
    n = w.shape[1] if n is None else n
    kds = [x.shape[1] for x in xs]
    assert len(set(kds)) == 1 and sum(kds) == w.shape[0]
    kd = kds[0]
    tm = _pick(m, (512, 256, 128))
    tn = _pick(n, (1024, 512, 256, 128))
    assert n0 % tn == 0
    j0 = n0 // tn
    x_specs = [pl.BlockSpec((tm, kd), lambda j, i: (i, 0)) for _ in xs]
    w_specs = [pl.BlockSpec((kd, tn), functools.partial(lambda j, i, a: (a, j + j0), a=a)) for a in range(len(xs))]
    return pl.pallas_call(
        functools.partial(_mm_kernel, n_lhs=len(xs)),
        grid=(n // tn, m // tm),
        in_specs=x_specs + w_specs,
        out_specs=pl.BlockSpec((tm, tn), lambda j, i: (i, j)),
        out_shape=jax.ShapeDtypeStruct((m, n), F32),
        compiler_params=_cparams(("parallel", "parallel")),
        name="matmul",
    )(*xs, *([w] * len(xs)))


def _kv_proj_kernel(x_ref, wk_ref, wv_ref, slab_ref, tile_ref):
    x = x_ref[...]
    rows = x.shape[0]
    for c, w_ref in enumerate((wk_ref, wv_ref)):
        res = _dot(x, w_ref[...])
        for g in range(DSA_KV_HEADS):
            piece = res[:, g * HEAD_DIM:(g + 1) * HEAD_DIM]
            slab_ref[pl.ds(g * 2 + c, rows, stride=KV_SLABS), :] = piece
            tile_ref[:, (g * 2 + c) * HEAD_DIM:(g * 2 + c + 1) * HEAD_DIM] = piece.astype(BF16)


def _kv_proj(x, w, k_col, v_col):
    m, kd = x.shape
    tm = _pick(m, (512, 256, 128))
    assert k_col % KV_HALF == 0 and v_col % KV_HALF == 0
    kb, vb = k_col // KV_HALF, v_col // KV_HALF
    return pl.pallas_call(
        _kv_proj_kernel,
        grid=(m // tm,),
        in_specs=[pl.BlockSpec((tm, kd), lambda i: (i, 0)),
                  pl.BlockSpec((kd, KV_HALF), lambda i: (0, kb)),
                  pl.BlockSpec((kd, KV_HALF), lambda i: (0, vb))],
        out_specs=[pl.BlockSpec((tm * KV_SLABS, HEAD_DIM), lambda i: (i, 0)),
                   pl.BlockSpec((tm, KV_ROW), lambda i: (i, 0))],
        out_shape=[jax.ShapeDtypeStruct((m * KV_SLABS, HEAD_DIM), F32), jax.ShapeDtypeStruct((m, KV_ROW), BF16)],
        compiler_params=_cparams(("parallel",)),
        name="kv_proj",
    )(x, w, w)


def _slab(ref, n_tok, g, c):
    return ref[pl.ds(g * 2 + c, n_tok, stride=KV_SLABS), :]


def _slab_all(pages, n_tok, g, c):
    return jnp.concatenate([_slab(p, n_tok, g, c) for p in pages], axis=0).astype(BF16)


def _page_specs(n_pages, rows, width, index):
    return [pl.BlockSpec((rows, width), functools.partial(lambda *a, p: (index(*a, p), 0), p=p))
            for p in range(n_pages)]


def _sort_key(x):
    b = pltpu.bitcast(x, I32)
    return jnp.where(b < 0, b ^ jnp.int32(0x7FFFFFFF), b)


def _topk_mask(sc, idx, k, idx_bits, keep_all=False):
    key = _sort_key(sc)
    search = jnp.logical_not(keep_all)

    def count(m):
        return jnp.sum(m.astype(I32), axis=1, keepdims=True)

    t0 = jnp.where(count(key >= 0) >= k, jnp.int32(0), jnp.int32(INT_MIN))

    def vstep(i, t):
        cand = t | (jnp.int32(1) << (30 - i))
        return jnp.where(count(key >= cand) >= k, cand, t)

    thr = lax.fori_loop(0, jnp.where(search, 31, 0), vstep, t0)
    gt = key > thr
    eq = key == thr
    need = k - count(gt)
    crowded = count(eq) > need
    n_steps = jnp.where(search & (jnp.max(crowded.astype(I32)) > 0), idx_bits, 0)

    def istep(i, c):
        cand = c | (jnp.int32(1) << (idx_bits - 1 - i))
        return jnp.where(count(eq & (idx < cand)) < need, cand, c)

    cut = lax.fori_loop(0, n_steps, istep, jnp.zeros_like(thr))
    return keep_all | gt | (eq & (jnp.logical_not(crowded) | (idx <= cut)))


def _stack_heads(ref, n_heads, scale):
    return jnp.concatenate([(ref[:, r * HEAD_DIM:(r + 1) * HEAD_DIM] * scale).astype(BF16) for r in range(n_heads)],
                           axis=0)


def _slope_stack(slopes):
    return jnp.concatenate([jnp.full((1, 1, 1), s, F32) for s in slopes], axis=0)


def _attend_stacked(q2, k, v, dm, slope2):
    n_heads = slope2.shape[0]
    rows, keys = dm.shape
    s = _dot_nt(q2, k).reshape(n_heads, rows, keys) - slope2 * dm[None]
    m = jnp.max(s, axis=2, keepdims=True)
    p = jnp.exp2(s - m)
    l = jnp.sum(p, axis=2, keepdims=True)
    o = _dot(p.reshape(n_heads * rows, keys).astype(BF16), v)
    return o * (1.0 / l).reshape(n_heads * rows, 1)


def _attend(q2, k_of, v, dm_of, slope2, keys, s_ref, pb_ref):
    n_heads = len(slope2)
    rows = q2.shape[0] // n_heads
    chunk = 2 * LANES if keys % (2 * LANES) == 0 else LANES
    bounds = [(a, a + chunk) for a in range(0, keys, chunk)]
    inv_l = []
    for r in range(n_heads):
        q_r = q2[r * rows:(r + 1) * rows]
        m_vec = jnp.full((rows, LANES), -jnp.inf, F32)
        for a, b in bounds:
            s = _dot_nt(q_r, k_of(a, b)) - slope2[r] * dm_of(a, b)
            s_ref[r, :, a:b] = s
            for i in range(chunk // LANES):
                m_vec = jnp.maximum(m_vec, s[:, i * LANES:(i + 1) * LANES])
        m = jnp.max(m_vec, axis=1, keepdims=True)
        l_vec = jnp.zeros((rows, LANES), F32)
        for a, b in bounds:
            p = jnp.exp2(s_ref[r, :, a:b] - m)
            for i in range(chunk // LANES):
                l_vec = l_vec + p[:, i * LANES:(i + 1) * LANES]
            pb_ref[r * rows:(r + 1) * rows, a:b] = p.astype(BF16)
        inv_l.append(1.0 / jnp.sum(l_vec, axis=1, keepdims=True))
    return _dot(pb_ref[:, 0:keys], v) * jnp.concatenate(inv_l, axis=0)


def _spans(seq):
    span = KEY_SPAN if seq % KEY_SPAN == 0 else seq
    return span, seq // span


def _group_slope(g, r, n_heads, n_groups):
    rep = n_heads // n_groups
    table = _slopes(n_heads)
    out = jnp.float32(table[r])
    for gg in range(1, n_groups):
        out = jnp.where(g == gg, jnp.float32(table[gg * rep + r]), out)
    return out


def _dsa_prompt_mask(c, qi_ref, cq_ref, ck_ref, dm_ref, kw, k_keep):
    t = c * Q_TILE + lax.broadcasted_iota(I32, (Q_TILE, 1), 0)
    kpos = lax.broadcasted_iota(I32, (Q_TILE, kw), 1)
    causal = kpos <= t
    ki = ck_ref[0:kw, 0:IDX_DIM].astype(BF16)
    wi = cq_ref[:, IDX_DIM:IDX_DIM + IDX_HEADS] * (IDX_HEADS ** -0.5)
    sc = jnp.zeros((Q_TILE, kw), F32)
    for h in range(IDX_HEADS):
        a = _dot_nt(qi_ref[:, h * IDX_DIM:(h + 1) * IDX_DIM].astype(BF16), ki)
        sc = sc + jnp.maximum(a, 0.0) * wi[:, h:h + 1]
    sc = jnp.where(causal, sc, NEG_INF)
    keep = _topk_mask(sc, kpos, k_keep, max(1, (kw - 1).bit_length()), keep_all=(c + 1) * Q_TILE <= k_keep)
    dm_ref[:, 0:kw] = jnp.where(keep & causal, (t - kpos).astype(F32), MASKED_DIST)


def _dsa_prompt_kernel(q_ref, qi_ref, cq_ref, ck_ref, kv_ref, o_ref, dm_ref, s_ref, pb_ref, *, seq, k_keep):
    c, g = pl.program_id(1), pl.program_id(2)
    rep = DSA_HEADS // DSA_KV_HEADS
    span, n_var = _spans(seq)
    for v in range(n_var):
        @pl.when((c * Q_TILE) // span == v)
        def _(kw=(v + 1) * span):
            @pl.when(g == 0)
            def _():
                _dsa_prompt_mask(c, qi_ref, cq_ref, ck_ref, dm_ref, kw, k_keep)

            slope2 = [_group_slope(g, r, DSA_HEADS, DSA_KV_HEADS) * LOG2E for r in range(rep)]
            o = _attend(_stack_heads(q_ref, rep, SCALE * LOG2E), lambda a, b: kv_ref[a:b, 0:HEAD_DIM],
                        kv_ref[0:kw, HEAD_DIM:2 * HEAD_DIM], lambda a, b: dm_ref[:, a:b], slope2, kw, s_ref, pb_ref)
            for r in range(rep):
                o_ref[:, r * HEAD_DIM:(r + 1) * HEAD_DIM] = o[r * Q_TILE:(r + 1) * Q_TILE].astype(BF16)


def _dsa_prompt(q, qi, h_c, kv, batch, seq):
    nc = seq // Q_TILE
    k_keep = min(IDX_TOPK, seq // 4)
    rep = DSA_HEADS // DSA_KV_HEADS
    tile = lambda b, c, g: (b * nc + c, 0)
    tile_g = lambda b, c, g: (b * nc + c, g)
    return pl.pallas_call(
        functools.partial(_dsa_prompt_kernel, seq=seq, k_keep=k_keep),
        grid=(batch, nc, DSA_KV_HEADS),
        in_specs=[pl.BlockSpec((Q_TILE, rep * HEAD_DIM), tile_g),
                  pl.BlockSpec((Q_TILE, IDX_HEADS * IDX_DIM), tile),
                  pl.BlockSpec((Q_TILE, LANES), tile),
                  pl.BlockSpec((seq, LANES), lambda b, c, g: (b, 0)),
                  pl.BlockSpec((seq, 2 * HEAD_DIM), lambda b, c, g: (b, g))],
        out_specs=pl.BlockSpec((Q_TILE, rep * HEAD_DIM), tile_g),
        out_shape=jax.ShapeDtypeStruct((batch * seq, Q_ROW), BF16),
        scratch_shapes=[pltpu.VMEM((Q_TILE, seq), F32), pltpu.VMEM((rep, Q_TILE, seq), F32),
                        pltpu.VMEM((rep * Q_TILE, seq), BF16)],
        compiler_params=_cparams(("parallel", "arbitrary", "arbitrary")),
        name="dsa_prompt",
    )(q, qi, h_c, h_c, kv)


def _peb_kernel(pe_ref, w1_ref, o_ref):
    for c in range(2):
        o_ref[c] = _dot_hi(pe_ref[c], w1_ref[c])


def _cmp_pe_bias(cmp_pe, cmp_w1):
    return pl.pallas_call(
        _peb_kernel,
        out_shape=jax.ShapeDtypeStruct((2, 1, HEAD_DIM), F32),
        name="cmp_pe_bias",
    )(cmp_pe.reshape(2, 1, CMP_BLOCK * HEAD_DIM), cmp_w1.reshape(2, CMP_BLOCK * HEAD_DIM, HEAD_DIM))


def _compress_kernel(pt_ref, *refs, n_pages, page):
    pages = refs[:n_pages]
    w1_ref, w2_ref, peb_ref, o_ref = refs[n_pages:]
    sub_per_page = page // CMP_STRIDE
    n_sub = n_pages * sub_per_page
    prow = lax.broadcasted_iota(I32, (page, page), 0)
    pcol = lax.broadcasted_iota(I32, (page, page), 1)
    regroup = (pcol == (prow % sub_per_page) * CMP_STRIDE + prow // sub_per_page).astype(BF16)
    by_pos = []
    for p in pages:
        x = jnp.concatenate([_slab(p, page, gc // 2, gc % 2) for gc in range(KV_SLABS)], axis=1).astype(BF16)
        by_pos.append(_dot(regroup, x))
    for c in range(2):
        lhs = jnp.concatenate([
            jnp.concatenate([
                jnp.concatenate([bp[j * sub_per_page:(j + 1) * sub_per_page,
                                    (g * 2 + c) * HEAD_DIM:(g * 2 + c + 1) * HEAD_DIM] for bp in by_pos],
                                axis=0).astype(BF16)
                for j in range(CMP_STRIDE)], axis=1)
            for g in range(NSA_KV_GROUPS)], axis=0)
        part = _dot(lhs, w1_ref[c].astype(BF16))
        w2 = w2_ref[c].astype(BF16)
        for g in range(NSA_KV_GROUPS):
            col = g * 256 + c * 128
            pg = part[g * n_sub:(g + 1) * n_sub]
            nxt = pltpu.roll(pg[:, HEAD_DIM:], n_sub - 1, 0)
            hid = pg[:, :HEAD_DIM] + nxt + peb_ref[c]
            o_ref[0, :, col:col + HEAD_DIM] = _dot(jax.nn.gelu(hid).astype(BF16), w2)


def _compress(pool, page_table, page, w1r, w2, peb):
    n_seq, n_pages = page_table.shape
    n_sub = n_pages * page // CMP_STRIDE
    page_specs = _page_specs(n_pages, page * KV_SLABS, HEAD_DIM, lambda n, pt, p: pt[n, p])
    return pl.pallas_call(
        functools.partial(_compress_kernel, n_pages=n_pages, page=page),
        grid_spec=pltpu.PrefetchScalarGridSpec(
            num_scalar_prefetch=1,
            grid=(n_seq,),
            in_specs=page_specs + [
                pl.BlockSpec((2, CMP_STRIDE * HEAD_DIM, 2 * HEAD_DIM), lambda n, pt: (0, 0, 0)),
                pl.BlockSpec((2, HEAD_DIM, HEAD_DIM), lambda n, pt: (0, 0, 0)),
                pl.BlockSpec((2, 1, HEAD_DIM), lambda n, pt: (0, 0, 0))],
            out_specs=pl.BlockSpec((1, n_sub, KV_ROW), lambda n, pt: (n, 0, 0))),
        out_shape=jax.ShapeDtypeStruct((n_seq, n_sub, KV_ROW), F32),
        compiler_params=_cparams(("arbitrary",)),
        name="nsa_compress",
    )(page_table, *([pool] * n_pages), w1r, w2, peb)


def _cover(n_cmp_pad, n_slc_pad):
    cs = lax.broadcasted_iota(I32, (n_cmp_pad, n_slc_pad), 0) * CMP_STRIDE
    bs = lax.broadcasted_iota(I32, (n_cmp_pad, n_slc_pad), 1) * SLC_BLOCK
    return ((cs < bs + SLC_BLOCK) & (cs + CMP_BLOCK > bs)).astype(F32)


def _select_blocks(score, t, n_slc, n_keep):
    rows, width = score.shape
    j = lax.broadcasted_iota(I32, (rows, width), 1)
    cur = t // SLC_BLOCK
    forced = (j == 0) | (j == cur) | (j == cur - 1)
    admissible = j * SLC_BLOCK <= t
    score = jnp.where(forced, FORCE_SCORE, jnp.where(admissible, score, NEG_INF))
    rank = jnp.zeros((rows, width), I32)
    for k in range(n_slc):
        sk = score[:, k:k + 1]
        ahead = (sk > score) | ((sk == score) & (j > k))
        rank = rank + ahead.astype(I32)
    return ((rank < n_keep) & (j < n_slc)).astype(F32)


def _select_blocks_t(score_t, t_row, n_keep):
    n_slc, width = score_t.shape
    j = lax.broadcasted_iota(I32, (n_slc, width), 0)
    cur = t_row // SLC_BLOCK
    forced = (j == 0) | (j == cur) | (j == cur - 1)
    admissible = j * SLC_BLOCK <= t_row
    score_t = jnp.where(forced, FORCE_SCORE, jnp.where(admissible, score_t, NEG_INF))
    rank = jnp.zeros((n_slc, width), I32)
    for k in range(n_slc):
        sk = score_t[k:k + 1, :]
        ahead = (sk > score_t) | ((sk == score_t) & (j > k))
        rank = rank + ahead.astype(I32)
    return (rank < n_keep).astype(F32)


def _nsa_prompt_body(c, g, q_ref, gate_ref, ckv_ref, slc_ref, win_ref, o_ref, kw, seq, n_cmp, win_keys):
    t = c * Q_TILE + lax.broadcasted_iota(I32, (Q_TILE, 1), 0)
    t_row = c * Q_TILE + lax.broadcasted_iota(I32, (1, Q_TILE), 1)
    rep = NSA_HEADS // NSA_KV_GROUPS
    slope2 = _slope_stack([_group_slope(g, r, NSA_HEADS, NSA_KV_GROUPS) * LOG2E for r in range(rep)])
    n_cmp_pad = ckv_ref.shape[1]
    n_slc = seq // SLC_BLOCK
    n_keep = min(SLC_TOPN, n_slc)

    kc = lax.broadcasted_iota(I32, (Q_TILE, n_cmp_pad), 1)
    dist_c_i = t - (kc * CMP_STRIDE + (CMP_BLOCK - 1))
    valid_c = (dist_c_i >= 0) & (kc < n_cmp)
    dist_c = dist_c_i.astype(F32)
    cover_bs = lax.broadcasted_iota(I32, (n_slc, n_cmp_pad), 0) * SLC_BLOCK
    cover_cs = lax.broadcasted_iota(I32, (n_slc, n_cmp_pad), 1) * CMP_STRIDE
    cover_t = ((cover_cs < cover_bs + SLC_BLOCK) & (cover_cs + CMP_BLOCK > cover_bs)).astype(F32)
    slc_rows = -(-n_slc // LANES) * LANES
    expand = (lax.broadcasted_iota(I32, (slc_rows, kw), 1) // SLC_BLOCK
              == lax.broadcasted_iota(I32, (slc_rows, kw), 0)).astype(BF16)

    kpos = lax.broadcasted_iota(I32, (Q_TILE, kw), 1)
    causal = kpos <= t
    dist_s = (t - kpos).astype(F32)

    w0 = pl.multiple_of(jnp.maximum(c * Q_TILE + Q_TILE - win_keys, 0), Q_TILE)
    wpos = w0 + lax.broadcasted_iota(I32, (Q_TILE, win_keys), 1)
    dist_w_i = t - wpos
    dm_w = jnp.where((dist_w_i >= 0) & (dist_w_i <= WINDOW), dist_w_i.astype(F32), MASKED_DIST)

    gates = jax.nn.sigmoid(gate_ref[:, 0:3 * rep])

    ck = ckv_ref[0, :, 0:HEAD_DIM].astype(BF16)
    cv = ckv_ref[0, :, HEAD_DIM:2 * HEAD_DIM].astype(BF16)
    q2 = _stack_heads(q_ref, rep, SCALE * LOG2E)
    s = jnp.where(valid_c[None], _dot_nt(q2, ck).reshape(rep, Q_TILE, n_cmp_pad) - slope2 * dist_c[None], NEG_INF)
    m = jnp.max(s, axis=2, keepdims=True)
    p = jnp.where(valid_c[None], jnp.exp2(s - m), 0.0)
    l = jnp.sum(p, axis=2, keepdims=True)
    p = p * (1.0 / jnp.where(l > 0.0, l, 1.0))
    imp = jnp.sum(p, axis=0)
    o_c = _dot(p.reshape(rep * Q_TILE, n_cmp_pad).astype(BF16), cv)
    sel_t = _select_blocks_t(_dot_hi(cover_t, imp.T), t_row, n_keep)
    sel_t = jnp.concatenate([sel_t, jnp.zeros((slc_rows - n_slc, Q_TILE), F32)], axis=0)
    picked = _dot(sel_t.T.astype(BF16), expand) > 0.5
    dm_s = jnp.where(picked & causal, dist_s, MASKED_DIST)
    o_s = _attend_stacked(q2, slc_ref[0:kw, 0:HEAD_DIM], slc_ref[0:kw, HEAD_DIM:2 * HEAD_DIM], dm_s, slope2)
    o_w = _attend_stacked(q2, win_ref[pl.ds(w0, win_keys), 0:HEAD_DIM],
                          win_ref[pl.ds(w0, win_keys), HEAD_DIM:2 * HEAD_DIM], dm_w, slope2)
    gate_col = [jnp.concatenate([gates[:, 3 * r + k:3 * r + k + 1] for r in range(rep)], axis=0) for k in range(3)]
    o = gate_col[0] * o_c + gate_col[1] * o_s + gate_col[2] * o_w
    for r in range(rep):
        o_ref[:, r * HEAD_DIM:(r + 1) * HEAD_DIM] = o[r * Q_TILE:(r + 1) * Q_TILE].astype(BF16)


def _nsa_prompt_kernel(q_ref, gate_ref, ckv_ref, slc_ref, win_ref, o_ref, *, seq, n_cmp, win_keys):
    c, g = pl.program_id(1), pl.program_id(2)
    span, n_var = _spans(seq)
    for v in range(n_var):
        @pl.when((c * Q_TILE) // span == v)
        def _(kw=(v + 1) * span):
            _nsa_prompt_body(c, g, q_ref, gate_ref, ckv_ref, slc_ref, win_ref, o_ref, kw, seq, n_cmp, win_keys)


def _nsa_prompt(q, gate, ckv, slc, win, batch, seq):
    nc = seq // Q_TILE
    n_cmp_pad = ckv.shape[1]
    n_cmp = seq // CMP_STRIDE - CMP_BLOCK // CMP_STRIDE + 1
    win_keys = min(seq, WINDOW + Q_TILE)
    rep = NSA_HEADS // NSA_KV_GROUPS
    tile_g = lambda b, c, g: (b * nc + c, g)
    return pl.pallas_call(
        functools.partial(_nsa_prompt_kernel, seq=seq, n_cmp=n_cmp, win_keys=win_keys),
        grid=(batch, nc, NSA_KV_GROUPS),
        in_specs=[pl.BlockSpec((Q_TILE, rep * HEAD_DIM), tile_g),
                  pl.BlockSpec((Q_TILE, LANES), tile_g),
                  pl.BlockSpec((1, n_cmp_pad, 2 * HEAD_DIM), lambda b, c, g: (b, 0, g)),
                  pl.BlockSpec((seq, 2 * HEAD_DIM), lambda b, c, g: (b, g)),
                  pl.BlockSpec((seq, 2 * HEAD_DIM), lambda b, c, g: (b, g))],
        out_specs=pl.BlockSpec((Q_TILE, rep * HEAD_DIM), tile_g),
        out_shape=jax.ShapeDtypeStruct((batch * seq, Q_ROW), BF16),
        compiler_params=_cparams(("parallel", "arbitrary", "arbitrary")),
        name="nsa_prompt",
    )(q, gate, ckv, slc, win)


def _group_rows(per_group, rep):
    row = lax.broadcasted_iota(I32, per_group[0].shape, 0) // rep
    out = per_group[0]
    for g in range(1, len(per_group)):
        out = jnp.where(row == g, per_group[g], out)
    return out


def _dsa_sample_mask_kernel(pt_ref, *refs, n_pages, page, k_keep):
    n_ki = SEQ_GROUP * n_pages
    ki_pages = refs[:n_ki]
    qi_ref, wi_ref, cnew_ref, keep_ref = refs[n_ki:]
    past = n_pages * page
    width = past + LANES
    rows = []
    for s in range(SEQ_GROUP):
        qi = qi_ref[s].astype(BF16)
        wi = wi_ref[s] * (IDX_HEADS ** -0.5)
        knew = cnew_ref[pl.ds(s, 1), 0:IDX_DIM]
        a_past = jnp.concatenate([_dot(qi, ki_pages[s * n_pages + p][0].astype(BF16))
                                  for p in range(n_pages)], axis=1)
        a_new = _dot_nt(qi, jnp.broadcast_to(knew, (8, IDX_DIM)).astype(BF16))[:, 0:1]
        sc_past = jnp.sum(jnp.maximum(a_past, 0.0) * wi, axis=0, keepdims=True)
        sc_new = jnp.sum(jnp.maximum(a_new, 0.0) * wi, axis=0, keepdims=True)
        rows.append(jnp.concatenate([sc_past, jnp.broadcast_to(sc_new, (1, LANES))], axis=1))
    col = lax.broadcasted_iota(I32, (SEQ_GROUP, width), 1)
    sc = jnp.where(col <= past, jnp.concatenate(rows, axis=0), NEG_INF)
    keep = _topk_mask(sc, col, k_keep, max(1, (width - 1).bit_length())) & (col <= past)
    keep_ref[...] = keep.astype(F32)


def _dsa_sample_mask(kidx_pool_t, page_table, qi, wi, h_c):
    n_seq, n_pages = page_table.shape
    assert n_seq % SEQ_GROUP == 0
    page = kidx_pool_t.shape[2]
    total = n_pages * page + 1
    k_keep = min(IDX_TOPK, total // 4)
    width = n_pages * page + LANES
    ki_specs = [pl.BlockSpec((1, IDX_DIM, page),
                             functools.partial(lambda i, pt, s, p: (pt[i * SEQ_GROUP + s, p], 0, 0), s=s, p=p))
                for s in range(SEQ_GROUP) for p in range(n_pages)]
    return pl.pallas_call(
        functools.partial(_dsa_sample_mask_kernel, n_pages=n_pages, page=page, k_keep=k_keep),
        grid_spec=pltpu.PrefetchScalarGridSpec(
            num_scalar_prefetch=1,
            grid=(n_seq // SEQ_GROUP,),
            in_specs=ki_specs + [
                pl.BlockSpec((SEQ_GROUP, IDX_HEADS, IDX_DIM), lambda i, pt: (i, 0, 0)),
                pl.BlockSpec((SEQ_GROUP, IDX_HEADS, 1), lambda i, pt: (i, 0, 0)),
                pl.BlockSpec((SEQ_GROUP, LANES), lambda i, pt: (i, 0))],
            out_specs=pl.BlockSpec((SEQ_GROUP, width), lambda i, pt: (i, 0))),
        out_shape=jax.ShapeDtypeStruct((n_seq, width), F32),
        compiler_params=_cparams(("arbitrary",)),
        name="dsa_sample_mask",
    )(page_table, *([kidx_pool_t] * (SEQ_GROUP * n_pages)), qi, wi, h_c)


def _dsa_sample_kernel(pt_ref, *refs, n_pages, page):
    kv_pages = refs[:n_pages]
    q_ref, keep_ref, kvnew_ref, o_ref = refs[n_pages:]
    r = pl.program_id(0) % SEQ_GROUP
    past = n_pages * page
    rep = DSA_HEADS // DSA_KV_HEADS
    slopes = _slopes(DSA_HEADS)

    keep = keep_ref[pl.ds(r, 1), :] > 0.5
    keep_past = jnp.broadcast_to(keep[:, 0:past], (DSA_HEADS, past))
    keep_new = jnp.broadcast_to(keep[:, past:past + 1], (DSA_HEADS, 1))

    q = (q_ref[0] * SCALE).astype(BF16)
    q32 = q.astype(F32)
    kvnew = kvnew_ref[pl.ds(r, 1), :]
    slope_col = jnp.concatenate([jnp.full((1, 1), s, F32) for s in slopes], axis=0)
    dist = (past - lax.broadcasted_iota(I32, (DSA_HEADS, past), 1)).astype(F32)
    s_g, s_new_g = [], []
    for g in range(DSA_KV_HEADS):
        s_g.append(_dot_nt(q, _slab_all(kv_pages, page, g, 0)))
        s_new_g.append(jnp.sum(q32 * kvnew[:, g * 256:g * 256 + 128], axis=1, keepdims=True))
    s = jnp.where(keep_past, _group_rows(s_g, rep) - slope_col * dist, NEG_INF)
    s_new = jnp.where(keep_new, _group_rows(s_new_g, rep), NEG_INF)
    m = jnp.maximum(jnp.max(s, axis=1, keepdims=True), s_new)
    p = jnp.where(keep_past, jnp.exp(s - m), 0.0)
    p_new = jnp.where(keep_new, jnp.exp(s_new - m), 0.0)
    l = jnp.sum(p, axis=1, keepdims=True) + p_new
    inv = 1.0 / jnp.where(l > 0.0, l, 1.0)
    pb = p.astype(BF16)
    p_new = p_new.astype(BF16).astype(F32)
    o_g = []
    for g in range(DSA_KV_HEADS):
        o_g.append(p_new * kvnew[:, g * 256 + 128:g * 256 + 256] + _dot(pb, _slab_all(kv_pages, page, g, 1)))
    o_ref[0] = (_group_rows(o_g, rep) * inv).astype(BF16)


def _dsa_sample(kv_pool, page_table, page, q, keep, kv_new):
    n_seq, n_pages = page_table.shape
    width = keep.shape[1]
    kv_specs = _page_specs(n_pages, page * KV_SLABS, HEAD_DIM, lambda n, pt, p: pt[n, p])
    return pl.pallas_call(
        functools.partial(_dsa_sample_kernel, n_pages=n_pages, page=page),
        grid_spec=pltpu.PrefetchScalarGridSpec(
            num_scalar_prefetch=1,
            grid=(n_seq,),
            in_specs=kv_specs + [
                pl.BlockSpec((1, DSA_HEADS, HEAD_DIM), lambda n, pt: (n, 0, 0)),
                pl.BlockSpec((SEQ_GROUP, width), lambda n, pt: (n // SEQ_GROUP, 0)),
                pl.BlockSpec((SEQ_GROUP, KV_ROW), lambda n, pt: (n // SEQ_GROUP, 0))],
            out_specs=pl.BlockSpec((1, DSA_HEADS, HEAD_DIM), lambda n, pt: (n, 0, 0))),
        out_shape=jax.ShapeDtypeStruct((n_seq, DSA_HEADS, HEAD_DIM), BF16),
        compiler_params=_cparams(("arbitrary",)),
        name="dsa_sample",
    )(page_table, *([kv_pool] * n_pages), q, keep, kv_new)


def _nsa_sample_kernel(pt_ref, *refs, n_pages, page, n_cmp):
    slc_pages = refs[:n_pages]
    q_ref, gate_ref, ckv_ref, win_ref, slcnew_ref, winnew_ref, winslab_ref, o_ref, wout_ref = refs[n_pages:]
    n = pl.program_id(0)
    past = n_pages * page
    heads = NSA_HEADS
    rep = NSA_HEADS // NSA_KV_GROUPS
    slopes = _slopes(NSA_HEADS)
    slope_col = jnp.concatenate([jnp.full((1, 1), s, F32) for s in slopes], axis=0)
    q = (q_ref[0] * SCALE).astype(BF16)
    q32 = q.astype(F32)
    gates = jax.nn.sigmoid(gate_ref[0])
    n_cmp_pad = ckv_ref.shape[1]
    total = past + 1
    n_slc = -(-total // SLC_BLOCK)
    n_keep = min(SLC_TOPN, n_slc)
    slc_pad = -(-n_slc // LANES) * LANES

    kc = lax.broadcasted_iota(I32, (heads, n_cmp_pad), 1)
    dist_c_i = past - (kc * CMP_STRIDE + (CMP_BLOCK - 1))
    valid_c = (dist_c_i >= 0) & (kc < n_cmp)
    s_g = [_dot_nt(q, ckv_ref[0, :, g * 256:g * 256 + 128].astype(BF16)) for g in range(NSA_KV_GROUPS)]
    s = jnp.where(valid_c, _group_rows(s_g, rep) - slope_col * dist_c_i.astype(F32), NEG_INF)
    m = jnp.max(s, axis=1, keepdims=True)
    p = jnp.where(valid_c, jnp.exp(s - m), 0.0)
    l = jnp.sum(p, axis=1, keepdims=True)
    p = p * (1.0 / jnp.where(l > 0.0, l, 1.0))
    pb = p.astype(BF16)
    o_c = _group_rows([_dot(pb, ckv_ref[0, :, g * 256 + 128:g * 256 + 256].astype(BF16))
                       for g in range(NSA_KV_GROUPS)], rep)
    same_group = (lax.broadcasted_iota(I32, (heads, heads), 0) // rep
                  == lax.broadcasted_iota(I32, (heads, heads), 1) // rep).astype(F32)
    imp = _dot_hi(same_group, p)
    score = _dot_hi(imp, _cover(n_cmp_pad, slc_pad))
    sel = _select_blocks(score, jnp.full((heads, 1), past, I32), n_slc, n_keep)
    expand = (lax.broadcasted_iota(I32, (slc_pad, past), 1) // SLC_BLOCK
              == lax.broadcasted_iota(I32, (slc_pad, past), 0)).astype(BF16)
    valid_s = _dot(sel.astype(BF16), expand) > 0.5
    new_blk = past // SLC_BLOCK
    valid_s_new = sel[:, new_blk:new_blk + 1] > 0.5

    def attend_with_new(s_past, valid_past, s_new, valid_new, dist_past, v_of, v_new_of):
        s_p = jnp.where(valid_past, s_past - slope_col * dist_past, NEG_INF)
        s_n = jnp.where(valid_new, s_new, NEG_INF)
        mm = jnp.maximum(jnp.max(s_p, axis=1, keepdims=True), s_n)
        pp = jnp.where(valid_past, jnp.exp(s_p - mm), 0.0)
        pn = jnp.where(valid_new, jnp.exp(s_n - mm), 0.0)
        ll = jnp.sum(pp, axis=1, keepdims=True) + pn
        inv = 1.0 / jnp.where(ll > 0.0, ll, 1.0)
        ppb = pp.astype(BF16)
        pn = pn.astype(BF16).astype(F32)
        outs = [v_of(ppb, g) + pn * v_new_of(g) for g in range(NSA_KV_GROUPS)]
        return _group_rows(outs, rep) * inv

    slcnew = slcnew_ref[pl.ds(n, 1), :]
    s_g, s_new_g = [], []
    for g in range(NSA_KV_GROUPS):
        s_g.append(_dot_nt(q, _slab_all(slc_pages, page, g, 0)))
        s_new_g.append(jnp.sum(q32 * slcnew[:, g * 256:g * 256 + 128], axis=1, keepdims=True))
    dist_s = (past - lax.broadcasted_iota(I32, (heads, past), 1)).astype(F32)

    def slc_v(ppb, g):
        return _dot(ppb, _slab_all(slc_pages, page, g, 1))

    o_s = attend_with_new(_group_rows(s_g, rep), valid_s, _group_rows(s_new_g, rep), valid_s_new, dist_s, slc_v,
                          lambda g: slcnew[:, g * 256 + 128:g * 256 + 256])

    n_buf = win_ref.shape[0] // KV_SLABS
    winnew = winnew_ref[pl.ds(n, 1), :]
    dist_w_i = n_buf - lax.broadcasted_iota(I32, (heads, n_buf), 1)
    valid_w = dist_w_i <= WINDOW
    s_g = [_dot_nt(q, _slab(win_ref, n_buf, g, 0).astype(BF16)) for g in range(NSA_KV_GROUPS)]
    s_new_g = [jnp.sum(q32 * winnew[:, g * 256:g * 256 + 128], axis=1, keepdims=True)
               for g in range(NSA_KV_GROUPS)]
    o_w = attend_with_new(_group_rows(s_g, rep), valid_w, _group_rows(s_new_g, rep),
                          jnp.full((heads, 1), True), dist_w_i.astype(F32),
                          lambda ppb, g: _dot(ppb, _slab(win_ref, n_buf, g, 1).astype(BF16)),
                          lambda g: winnew[:, g * 256 + 128:g * 256 + 256])

    o_ref[0] = (gates[:, 0:1] * o_c + gates[:, 1:2] * o_s + gates[:, 2:3] * o_w).astype(BF16)

    keep_rows = (n_buf - 1) * KV_SLABS
    wout_ref[0:keep_rows, :] = win_ref[KV_SLABS:n_buf * KV_SLABS, :]
    wout_ref[keep_rows:keep_rows + KV_SLABS, :] = winslab_ref[pl.ds(pl.multiple_of(n * KV_SLABS, KV_SLABS), KV_SLABS), :]


def _nsa_sample(slc_pool, page_table, page, q, gate, ckv, win_state, slc_new, win_new, win_new_slab):
    n_seq, n_pages = page_table.shape
    n_cmp_pad = ckv.shape[1]
    n_cmp = n_pages * page // CMP_STRIDE - CMP_BLOCK // CMP_STRIDE + 1
    n_buf = win_state.shape[0] // (n_seq * KV_SLABS)
    slc_specs = _page_specs(n_pages, page * KV_SLABS, HEAD_DIM, lambda n, pt, p: pt[n, p])
    return pl.pallas_call(
        functools.partial(_nsa_sample_kernel, n_pages=n_pages, page=page, n_cmp=n_cmp),
        grid_spec=pltpu.PrefetchScalarGridSpec(
            num_scalar_prefetch=1,
            grid=(n_seq,),
            in_specs=slc_specs + [
                pl.BlockSpec((1, NSA_HEADS, HEAD_DIM), lambda n, pt: (n, 0, 0)),
                pl.BlockSpec((1, NSA_HEADS, 3), lambda n, pt: (n, 0, 0)),
                pl.BlockSpec((1, n_cmp_pad, KV_ROW), lambda n, pt: (n, 0, 0)),
                pl.BlockSpec((n_buf * KV_SLABS, HEAD_DIM), lambda n, pt: (n, 0)),
                pl.BlockSpec((n_seq, KV_ROW), lambda n, pt: (0, 0)),
                pl.BlockSpec((n_seq, KV_ROW), lambda n, pt: (0, 0)),
                pl.BlockSpec((n_seq * KV_SLABS, HEAD_DIM), lambda n, pt: (0, 0))],
            out_specs=[pl.BlockSpec((1, NSA_HEADS, HEAD_DIM), lambda n, pt: (n, 0, 0)),
                       pl.BlockSpec((n_buf * KV_SLABS, HEAD_DIM), lambda n, pt: (n, 0))]),
        out_shape=[jax.ShapeDtypeStruct((n_seq, NSA_HEADS, HEAD_DIM), BF16),
                   jax.ShapeDtypeStruct(win_state.shape, F32)],
        compiler_params=_cparams(("arbitrary",)),
        name="nsa_sample",
    )(page_table, *([slc_pool] * n_pages), q, gate, ckv, win_state, slc_new, win_new, win_new_slab)


def _ln(v, g, b):
    mu = jnp.mean(v, axis=1, keepdims=True)
    d = v - mu
    var = jnp.mean(d * d, axis=1, keepdims=True)
    return d * lax.rsqrt(var + LN_EPS) * g + b


def _ln_router_kernel(xa_ref, mixa_ref, xb_ref, mixb_ref, g_ref, b_ref, wr_ref, br_ref, x1_ref, e_ref, gate_ref, *,
                      alpha, blocks_a):
    from_a = pl.program_id(0) < blocks_a
    pre = jnp.where(from_a, alpha * xa_ref[...] + mixa_ref[...], alpha * xb_ref[...] + mixb_ref[...])
    x1 = _ln(pre, g_ref[...], b_ref[...])
    x1_ref[...] = x1
    logits = _dot_hi(x1, wr_ref[...]) + br_ref[...]
    rows = logits.shape[0]
    lane = lax.broadcasted_iota(I32, (rows, LANES), 1)
    big = jnp.int32(LANES)
    is_grp = lane < N_GROUPS
    lg = jnp.where(is_grp, logits, -jnp.inf)
    mg = jnp.max(lg, axis=1, keepdims=True)
    grp = jnp.min(jnp.where(lg == mg, lane, big), axis=1, keepdims=True)
    p_grp = 1.0 / jnp.sum(jnp.where(is_grp, jnp.exp(lg - mg), 0.0), axis=1, keepdims=True)
    ex = lane - N_GROUPS
    in_grp = (ex >= 0) & (ex < N_EXPERTS) & (ex // EXPERTS_PER_GROUP == grp)
    le = jnp.where(in_grp, logits, -jnp.inf)
    m1 = jnp.max(le, axis=1, keepdims=True)
    e1 = jnp.min(jnp.where(le == m1, ex, big), axis=1, keepdims=True)
    le2 = jnp.where(ex == e1, -jnp.inf, le)
    m2 = jnp.max(le2, axis=1, keepdims=True)
    e2 = jnp.min(jnp.where(le2 == m2, ex, big), axis=1, keepdims=True)
    z = jnp.sum(jnp.where(in_grp, jnp.exp(le - m1), 0.0), axis=1, keepdims=True)
    p1 = 1.0 / z
    p2 = jnp.exp(m2 - m1) / z
    g1 = p_grp * p1 / (p1 + p2)
    g2 = p_grp * p2 / (p1 + p2)
    e_ref[...] = jnp.where(lane == 0, e1, jnp.where(lane == 1, e2, 0))
    gate_ref[...] = jnp.where(lane == 0, g1, jnp.where(lane == 1, g2, 0.0))


def _ln_router(xa, mixa, xb, mixb, g, b, wr, br, alpha):
    (ma, d), mb = xa.shape, xb.shape[0]
    tm = _pick(mb, (128, 64, 32, 16, 8))
    assert ma % tm == 0 and mb % tm == 0
    blocks_a, blocks_b = ma // tm, mb // tm
    seg_a = lambda i: (jnp.minimum(i, blocks_a - 1), 0)
    seg_b = lambda i: (jnp.maximum(i - blocks_a, 0), 0)
    row = lambda i: (i, 0)
    fixed = lambda i: (0, 0)
    m = ma + mb
    return pl.pallas_call(
        functools.partial(_ln_router_kernel, alpha=alpha, blocks_a=blocks_a),
        grid=(blocks_a + blocks_b,),
        in_specs=[pl.BlockSpec((tm, d), seg_a), pl.BlockSpec((tm, d), seg_a),
                  pl.BlockSpec((tm, d), seg_b), pl.BlockSpec((tm, d), seg_b),
                  pl.BlockSpec((1, d), fixed), pl.BlockSpec((1, d), fixed),
                  pl.BlockSpec((d, LANES), fixed), pl.BlockSpec((1, LANES), fixed)],
        out_specs=[pl.BlockSpec((tm, d), row), pl.BlockSpec((tm, LANES), row), pl.BlockSpec((tm, LANES), row)],
        out_shape=[jax.ShapeDtypeStruct((m, d), F32), jax.ShapeDtypeStruct((m, LANES), I32),
                   jax.ShapeDtypeStruct((m, LANES), F32)],
        compiler_params=_cparams(("arbitrary",)),
        name="ln_router",
    )(xa, mixa, xb, mixb, g, b, wr, br)


DISPATCH_UNROLL = 8


def _dispatch_kernel(src_ref, rows_ref, x_ref, o_ref, buf_ref, sem):
    i = pl.program_id(0)
    n_rows = rows_ref[i]

    @pl.when(i == 0)
    def _():
        buf_ref[...] = jnp.zeros_like(buf_ref)

    def row_copy(r):
        tok = src_ref[i * MOE_ROWS + r]
        return pltpu.make_async_copy(x_ref.at[pl.ds(tok, 1)], buf_ref.at[pl.ds(r, 1)], sem)

    def trips(fn):
        def trip(t, carry):
            for u in range(DISPATCH_UNROLL):
                r = t * DISPATCH_UNROLL + u

                @pl.when(r < n_rows)
                def _():
                    fn(row_copy(r))

            return carry

        lax.fori_loop(0, (n_rows + DISPATCH_UNROLL - 1) // DISPATCH_UNROLL, trip, 0)

    trips(lambda cp: cp.start())
    trips(lambda cp: cp.wait())
    o_ref[...] = buf_ref[...].astype(BF16)


def _dispatch(src_tok, rows_in_block, x1, nb):
    d = x1.shape[1]
    return pl.pallas_call(
        _dispatch_kernel,
        grid_spec=pltpu.PrefetchScalarGridSpec(
            num_scalar_prefetch=2,
            grid=(nb,),
            in_specs=[pl.BlockSpec(memory_space=pl.ANY)],
            out_specs=pl.BlockSpec((MOE_ROWS, d), lambda i, s, u: (i, 0)),
            scratch_shapes=[pltpu.VMEM((MOE_ROWS, d), F32), pltpu.SemaphoreType.DMA(())]),
        out_shape=jax.ShapeDtypeStruct((nb * MOE_ROWS, d), BF16),
        compiler_params=_cparams(("arbitrary",)),
        name="moe_dispatch",
    )(src_tok, rows_in_block, x1)


def _expert_mm_kernel(run_ref, rune_ref, nruns_ref, nused_ref, x_ref, *refs, n_w, tn, epilogue):
    w_hbm = refs[:n_w]
    o_ref = refs[n_w]
    w_buf = refs[n_w + 1:2 * n_w + 1]
    w16 = refs[2 * n_w + 1:3 * n_w + 1]
    sem, state = refs[3 * n_w + 1], refs[3 * n_w + 2]
    j, i = pl.program_id(0), pl.program_id(1)
    n_runs = nruns_ref[0]
    total = pl.num_programs(0) * n_runs
    cur = j * n_runs + run_ref[i]

    def item_copies(item):
        e = rune_ref[item % n_runs]
        col = pl.multiple_of((item // n_runs) * tn, tn)
        slot = item % MOE_RING
        return [pltpu.make_async_copy(w_hbm[a].at[e, :, pl.ds(col, tn)], w_buf[a].at[slot], sem.at[a, slot])
                for a in range(n_w)]

    @pl.when((j == 0) & (i == 0))
    def _():
        state[0] = 0
        state[1] = -1

    def request(item, carry):
        for cp in item_copies(item):
            cp.start()
        return carry

    limit = jnp.minimum(cur + MOE_RING, total)
    lax.fori_loop(state[0], limit, request, 0)
    state[0] = jnp.maximum(state[0], limit)

    @pl.when(state[1] != cur)
    def _():
        for cp in item_copies(cur):
            cp.wait()
        slot = cur % MOE_RING
        for a in range(n_w):
            w16[a][...] = w_buf[a][slot].astype(BF16)
        state[1] = cur

    @pl.when(i < nused_ref[0])
    def _():
        x = x_ref[...]
        o_ref[...] = epilogue(*[_dot(x, w16[a][...]) for a in range(n_w)]).astype(o_ref.dtype)

    @pl.when(i >= nused_ref[0])
    def _():
        o_ref[...] = jnp.zeros_like(o_ref)


def _expert_mm(plan, x, weights, tn, epilogue, out_dtype, name):
    run_of_block, run_e, n_runs, n_used = plan
    nb = run_of_block.shape[0]
    kd, n = weights[0].shape[1], weights[0].shape[2]
    n_w = len(weights)
    return pl.pallas_call(
        functools.partial(_expert_mm_kernel, n_w=n_w, tn=tn, epilogue=epilogue),
        grid_spec=pltpu.PrefetchScalarGridSpec(
            num_scalar_prefetch=4,
            grid=(n // tn, nb),
            in_specs=[pl.BlockSpec((MOE_ROWS, kd), lambda j, i, run, rune, nruns, nused: (jnp.minimum(i, nused[0] - 1), 0))]
            + [pl.BlockSpec(memory_space=pl.ANY)] * n_w,
            out_specs=pl.BlockSpec((MOE_ROWS, tn), lambda j, i, *_: (i, j)),
            scratch_shapes=[pltpu.VMEM((MOE_RING, kd, tn), F32) for _ in range(n_w)]
            + [pltpu.VMEM((kd, tn), BF16) for _ in range(n_w)]
            + [pltpu.SemaphoreType.DMA((n_w, MOE_RING)), pltpu.SMEM((2,), I32)]),
        out_shape=jax.ShapeDtypeStruct((nb * MOE_ROWS, n), out_dtype),
        compiler_params=_cparams(("arbitrary", "arbitrary")),
        name=name,
    )(run_of_block, run_e, n_runs, n_used, x, *weights)


def _swiglu(a, u):
    return a * jax.nn.sigmoid(a) * u


def _combine_kernel(dest_ref, y_ref, x1_ref, gate_ref, g_ref, b_ref, o_ref, buf_ref, sem, *, tm, alpha, off):
    i = pl.program_id(0) + off

    def row_copy(r, k):
        slot = dest_ref[(i * tm + r) * EXPERT_TOPK + k]
        return pltpu.make_async_copy(y_ref.at[pl.ds(slot, 1)], buf_ref.at[k, pl.ds(r, 1)], sem)

    def start(r, carry):
        for k in range(EXPERT_TOPK):
            row_copy(r, k).start()
        return carry

    lax.fori_loop(0, tm, start, 0, unroll=4)

    def wait(r, carry):
        for k in range(EXPERT_TOPK):
            row_copy(r, k).wait()
        return carry

    lax.fori_loop(0, tm, wait, 0, unroll=4)
    gate = gate_ref[...]
    f = gate[:, 0:1] * buf_ref[0] + gate[:, 1:2] * buf_ref[1]
    o_ref[...] = _ln(alpha * x1_ref[...] + f, g_ref[...], b_ref[...])


def _combine(dest, ybuf, x1, gate, row_offset, m, g, b, alpha):
    d = x1.shape[1]
    tm = _pick(m, (128,))
    assert row_offset % tm == 0
    off = row_offset // tm
    return pl.pallas_call(
        functools.partial(_combine_kernel, tm=tm, alpha=alpha, off=off),
        grid_spec=pltpu.PrefetchScalarGridSpec(
            num_scalar_prefetch=1,
            grid=(m // tm,),
            in_specs=[pl.BlockSpec(memory_space=pl.ANY),
                      pl.BlockSpec((tm, d), lambda i, dd: (i + off, 0)),
                      pl.BlockSpec((tm, LANES), lambda i, dd: (i + off, 0)),
                      pl.BlockSpec((1, d), lambda i, dd: (0, 0)),
                      pl.BlockSpec((1, d), lambda i, dd: (0, 0))],
            out_specs=pl.BlockSpec((tm, d), lambda i, dd: (i, 0)),
            scratch_shapes=[pltpu.VMEM((EXPERT_TOPK, tm, d), F32), pltpu.SemaphoreType.DMA(())]),
        out_shape=jax.ShapeDtypeStruct((m, d), F32),
        compiler_params=_cparams(("arbitrary",)),
        name="moe_combine",
    )(dest, ybuf, x1, gate, g, b)


def _route_plan(e_all, nb):
    a_n = e_all.shape[0]
    onehot = (e_all[:, None] == jnp.arange(N_EXPERTS, dtype=I32)[None, :]).astype(I32)
    before = jnp.cumsum(onehot, axis=0) - onehot
    rank = jnp.sum(before * onehot, axis=1)
    counts = jnp.sum(onehot, axis=0)
    pad_counts = (counts + MOE_ROWS - 1) // MOE_ROWS * MOE_ROWS
    pad_ends = jnp.cumsum(pad_counts)
    pad_starts = pad_ends - pad_counts
    dest = (pad_starts[e_all] + rank).astype(I32)
    n_used = (pad_ends[-1] // MOE_ROWS).astype(I32)
    block_e = jnp.searchsorted(pad_ends, jnp.arange(nb, dtype=I32) * MOE_ROWS, side="right").astype(I32)
    owns = (counts > 0).astype(I32)
    run_id = jnp.cumsum(owns) - owns
    n_runs = jnp.sum(owns).astype(I32)
    last_e = jnp.max(jnp.where(counts > 0, jnp.arange(N_EXPERTS, dtype=I32), 0)).astype(I32)
    block_e = jnp.where(jnp.arange(nb, dtype=I32) < n_used, jnp.minimum(block_e, N_EXPERTS - 1), last_e)
    run_of_block = run_id[block_e].astype(I32)
    run_e = jnp.full((N_EXPERTS,), last_e, I32).at[jnp.where(counts > 0, run_id, N_EXPERTS)].set(
        jnp.arange(N_EXPERTS, dtype=I32), mode="drop")
    src_tok = jnp.zeros((nb * MOE_ROWS,), I32).at[dest].set(jnp.arange(a_n, dtype=I32) // EXPERT_TOPK)
    blk = jnp.arange(nb, dtype=I32)
    rows_in_block = jnp.where(
        blk < n_used, jnp.clip(counts[block_e] - (blk * MOE_ROWS - pad_starts[block_e]), 0, MOE_ROWS), 0).astype(I32)
    return dest, (run_of_block, run_e, n_runs.reshape(1), n_used.reshape(1)), src_tok, rows_in_block


def kernel(x_prompt, x_sample, cache_dsa_kv, cache_dsa_kidx, cache_nsa_cmp_kv, cache_nsa_slc_kv, state_nsa_win_kv, page_table, w_in, w_out, cmp_w1, cmp_w2, cmp_pe, ln1_g, ln1_b, w_router_group, b_router_group, w_router_expert, b_router_expert, w_gate, w_up, w_down, ln2_g, ln2_b):
    depth = w_in.shape[0]
    assert depth == 1 and x_sample.shape[1] == 1
    batch, seq, d_model = x_prompt.shape
    n_seq = x_sample.shape[0]
    n_pool, page = cache_dsa_kv.shape[1], cache_dsa_kv.shape[2]
    alpha = (2.0 * depth) ** 0.25
    m_p = batch * seq
    xp = x_prompt.reshape(m_p, d_model)
    xs = x_sample.reshape(n_seq, d_model)
    xp16, xs16 = xp.astype(BF16), xs.astype(BF16)

    w = w_in[0]
    t0, t1 = 4176, 9296
    rep_b = NSA_HEADS // NSA_KV_GROUPS
    w_head = w[:, :t0 + 48].astype(BF16)
    w_tail = w[:, t0:t1].astype(BF16)
    w_gates = jnp.pad(w[:, t1:].reshape(d_model, NSA_KV_GROUPS, 3 * rep_b),
                      ((0, 0), (0, 0), (0, LANES - 3 * rep_b))).reshape(d_model, NSA_KV_GROUPS * LANES).astype(BF16)
    c_dq, c_dk, c_dv, c_iq, c_ik = 0, 2048, 2560, 3072, 4096
    c_nq, c_ck, c_cv, c_sk, c_sv, c_wk, c_wv = 0, 2048, 2560, 3072, 3584, 4096, 4608

    def project(x16):
        qa = _matmul([x16], w_head, c_dq, Q_ROW)
        qi = _matmul([x16], w_head, c_iq, IDX_HEADS * IDX_DIM)
        hc = _matmul([x16], w_head, c_ik, LANES)
        qb = _matmul([x16], w_tail, c_nq, Q_ROW)
        gate = _matmul([x16], w_gates)
        kv = [_kv_proj(x16, w_head, c_dk, c_dv), _kv_proj(x16, w_tail, c_ck, c_cv),
              _kv_proj(x16, w_tail, c_sk, c_sv), _kv_proj(x16, w_tail, c_wk, c_wv)]
        return qa, qi, hc, qb, gate, kv

    qa_p, qi_p, hc_p, qb_p, gate_p, kv_p = project(xp16)
    qa_s, qi_s, hc_s, qb_s, gate_s, kv_s = project(xs16)

    w1r = cmp_w1[0].reshape(2, CMP_BLOCK // CMP_STRIDE, CMP_STRIDE, HEAD_DIM, HEAD_DIM)
    w1r = jnp.transpose(w1r, (0, 2, 3, 1, 4)).reshape(2, CMP_STRIDE * HEAD_DIM, 2 * HEAD_DIM)
    peb = _cmp_pe_bias(cmp_pe[0], cmp_w1[0])
    pages_per_seq = seq // page
    ident = jnp.arange(batch * pages_per_seq, dtype=I32).reshape(batch, pages_per_seq)
    ckv_p = _compress(kv_p[1][0], ident, page, w1r, cmp_w2[0], peb)
    ckv_s = _compress(cache_nsa_cmp_kv.reshape(-1, HEAD_DIM), page_table, page, w1r, cmp_w2[0], peb)

    oa_p = _dsa_prompt(qa_p, qi_p, hc_p, kv_p[0][1], batch, seq)
    ob_p = _nsa_prompt(qb_p, gate_p, ckv_p, kv_p[2][1], kv_p[3][1], batch, seq)

    keep_s = _dsa_sample_mask(jnp.swapaxes(cache_dsa_kidx[0], 1, 2), page_table,
                              qi_s.reshape(n_seq, IDX_HEADS, IDX_DIM),
                              hc_s[:, IDX_DIM:IDX_DIM + IDX_HEADS].reshape(n_seq, IDX_HEADS, 1), hc_s)
    oa_s = _dsa_sample(cache_dsa_kv.reshape(-1, HEAD_DIM), page_table, page,
                       qa_s.reshape(n_seq, DSA_HEADS, HEAD_DIM), keep_s, kv_s[0][1].astype(F32))
    ob_s, win_s = _nsa_sample(cache_nsa_slc_kv.reshape(-1, HEAD_DIM), page_table, page,
                              qb_s.reshape(n_seq, NSA_HEADS, HEAD_DIM),
                              gate_s.reshape(n_seq, NSA_KV_GROUPS, LANES)[:, :, :3 * rep_b].reshape(n_seq, NSA_HEADS, 3),
                              ckv_s,
                              state_nsa_win_kv.reshape(-1, HEAD_DIM), kv_s[2][1].astype(F32),
                              kv_s[3][1].astype(F32), kv_s[3][0])

    w_o = w_out[0].astype(BF16)
    wr = jnp.concatenate([w_router_group[0], w_router_expert[0],
                          jnp.zeros((d_model, LANES - N_GROUPS - N_EXPERTS), F32)], axis=1)
    br = jnp.concatenate([b_router_group[0], b_router_expert[0],
                          jnp.zeros((LANES - N_GROUPS - N_EXPERTS,), F32)]).reshape(1, LANES)
    g1, b1 = ln1_g[0].reshape(1, d_model), ln1_b[0].reshape(1, d_model)
    g2, b2 = ln2_g[0].reshape(1, d_model), ln2_b[0].reshape(1, d_model)
    x1, e_tok, gt_tok = _ln_router(
        xp, _matmul([oa_p, ob_p], w_o),
        xs, _matmul([oa_s.reshape(n_seq, Q_ROW), ob_s.reshape(n_seq, Q_ROW)], w_o), g1, b1, wr, br, alpha)

    e_all = e_tok[:, :EXPERT_TOPK].reshape(-1)
    a_n = e_all.shape[0]
    nb = -(-(a_n + N_EXPERTS * (MOE_ROWS - 1)) // MOE_ROWS)
    dest, plan, src_tok, rows_in_block = _route_plan(e_all, nb)
    xbuf = _dispatch(src_tok, rows_in_block, x1, nb)
    d_expert = w_gate.shape[3]
    hid = _expert_mm(plan, xbuf, [w_gate[0], w_up[0]], _pick(d_expert, (MOE_CHUNK, LANES)), _swiglu, BF16,
                     "moe_gate_up")
    ybuf = _expert_mm(plan, hid, [w_down[0]], _pick(d_model, (2048, 1024, 512, 256, LANES)), lambda y: y, F32,
                      "moe_down")
    y_p = _combine(dest, ybuf, x1, gt_tok, 0, m_p, g2, b2, alpha)
    y_s = _combine(dest, ybuf, x1, gt_tok, m_p, n_seq, g2, b2, alpha)

    def state(slab, lead):
        return slab.reshape((1,) + lead + (NSA_KV_GROUPS, 2, HEAD_DIM))

    n_win = min(WINDOW, seq)
    win_p = state(kv_p[3][0], (batch, seq))[:, :, seq - n_win:]
    return (y_p.reshape(batch, seq, d_model), y_s.reshape(n_seq, 1, d_model),
            state(kv_p[0][0], (batch, seq)), state(kv_s[0][0], (n_seq, 1)),
            hc_p[:, :IDX_DIM].reshape(1, batch, seq, IDX_DIM), hc_s[:, :IDX_DIM].reshape(1, n_seq, 1, IDX_DIM),
            state(kv_p[1][0], (batch, seq)), state(kv_s[1][0], (n_seq, 1)),
            state(kv_p[2][0], (batch, seq)), state(kv_s[2][0], (n_seq, 1)),
            win_p, state(win_s, (n_seq, state_nsa_win_kv.shape[2])))
```

```python
import functools

import jax
import jax.numpy as jnp
from jax import lax
from jax.experimental import pallas as pl
from jax.experimental.pallas import tpu as pltpu

F32 = jnp.float32
BF16 = jnp.bfloat16
I32 = jnp.int32

HEAD_DIM = 128
DSA_HEADS = 16
DSA_KV_HEADS = 4
IDX_HEADS = 16
IDX_DIM = 64
IDX_TOPK = 256
NSA_HEADS = 16
NSA_KV_GROUPS = 4
CMP_STRIDE = 16
CMP_BLOCK = 32
SLC_BLOCK = 64
SLC_TOPN = 16
WINDOW = 512
N_GROUPS = 8
EXPERTS_PER_GROUP = 8
N_EXPERTS = N_GROUPS * EXPERTS_PER_GROUP
EXPERT_TOPK = 2
LN_EPS = 1e-5
NEG_INF = -1e30
FORCE_SCORE = 1e9

KV_ROW = 2 * DSA_KV_HEADS * HEAD_DIM
KV_SLABS = 2 * DSA_KV_HEADS
KV_HALF = DSA_KV_HEADS * HEAD_DIM
Q_ROW = DSA_HEADS * HEAD_DIM
Q_TILE = 128
KEY_SPAN = 256
SEQ_GROUP = 16
LANES = 128
MOE_ROWS = 384
MOE_RING = 4
MOE_CHUNK = 256
VMEM_LIMIT = 56 * 1024 * 1024
SCALE = HEAD_DIM ** -0.5
LOG2E = 1.4426950408889634
MASKED_DIST = 1e32
INT_MIN = -(2 ** 31)


def _slopes(n):
    return [2.0 ** (-8.0 * i / n) for i in range(1, n + 1)]


def _cparams(sem):
    return pltpu.CompilerParams(dimension_semantics=sem, vmem_limit_bytes=VMEM_LIMIT)


def _dot(a, b):
    return jnp.dot(a, b, preferred_element_type=F32)


def _dot_nt(a, b):
    return lax.dot_general(a, b, (((1,), (1,)), ((), ())), preferred_element_type=F32)


def _dot_hi(a, b):
    return jnp.dot(a, b, preferred_element_type=F32, precision=lax.Precision.HIGHEST)


def _pick(n, cands):
    for c in cands:
        if n % c == 0:
            return c
    return n


def _mm_kernel(*refs, n_lhs):
    x_refs, w_refs, o_ref = refs[:n_lhs], refs[n_lhs:2 * n_lhs], refs[2 * n_lhs]
    acc = _dot(x_refs[0][...], w_refs[0][...])
    for x_ref, w_ref in zip(x_refs[1:], w_refs[1:]):
        acc = acc + _dot(x_ref[...], w_ref[...])
    o_ref[...] = acc.astype(o_ref.dtype)


def _matmul(xs, w, n0=0, n=None):
    m = xs[0].shape[0]
    n = w.shape[1] if n is None else n
    kds = [x.shape[1] for x in xs]
    assert len(set(kds)) == 1 and sum(kds) == w.shape[0]
    kd = kds[0]
    tm = _pick(m, (512, 256, 128))
    tn = _pick(n, (1024, 512, 256, 128))
    assert n0 % tn == 0
    j0 = n0 // tn
    x_specs = [pl.BlockSpec((tm, kd), lambda j, i: (i, 0)) for _ in xs]
    w_specs = [pl.BlockSpec((kd, tn), functools.partial(lambda j, i, a: (a, j + j0), a=a)) for a in range(len(xs))]
    return pl.pallas_call(
        functools.partial(_mm_kernel, n_lhs=len(xs)),
        grid=(n // tn, m // tm),
        in_specs=x_specs + w_specs,
        out_specs=pl.BlockSpec((tm, tn), lambda j, i: (i, j)),
        out_shape=jax.ShapeDtypeStruct((m, n), F32),
        compiler_params=_cparams(("parallel", "parallel")),
        name="matmul",
    )(*xs, *([w] * len(xs)))


def _kv_proj_kernel(x_ref, wk_ref, wv_ref, slab_ref, tile_ref):
    x = x_ref[...]
    rows = x.shape[0]
    for c, w_ref in enumerate((wk_ref, wv_ref)):
        res = _dot(x, w_ref[...])
        for g in range(DSA_KV_HEADS):
            piece = res[:, g * HEAD_DIM:(g + 1) * HEAD_DIM]
            slab_ref[pl.ds(g * 2 + c, rows, stride=KV_SLABS), :] = piece
            tile_ref[:, (g * 2 + c) * HEAD_DIM:(g * 2 + c + 1) * HEAD_DIM] = piece.astype(BF16)


def _kv_proj(x, w, k_col, v_col):
    m, kd = x.shape
    tm = _pick(m, (512, 256, 128))
    assert k_col % KV_HALF == 0 and v_col % KV_HALF == 0
    kb, vb = k_col // KV_HALF, v_col // KV_HALF
    return pl.pallas_call(
        _kv_proj_kernel,
        grid=(m // tm,),
        in_specs=[pl.BlockSpec((tm, kd), lambda i: (i, 0)),
                  pl.BlockSpec((kd, KV_HALF), lambda i: (0, kb)),
                  pl.BlockSpec((kd, KV_HALF), lambda i: (0, vb))],
        out_specs=[pl.BlockSpec((tm * KV_SLABS, HEAD_DIM), lambda i: (i, 0)),
                   pl.BlockSpec((tm, KV_ROW), lambda i: (i, 0))],
        out_shape=[jax.ShapeDtypeStruct((m * KV_SLABS, HEAD_DIM), F32), jax.ShapeDtypeStruct((m, KV_ROW), BF16)],
        compiler_params=_cparams(("parallel",)),
        name="kv_proj",
    )(x, w, w)


def _slab(ref, n_tok, g, c):
    return ref[pl.ds(g * 2 + c, n_tok, stride=KV_SLABS), :]


def _slab_all(pages, n_tok, g, c):
    return jnp.concatenate([_slab(p, n_tok, g, c) for p in pages], axis=0).astype(BF16)


def _page_specs(n_pages, rows, width, index):
    return [pl.BlockSpec((rows, width), functools.partial(lambda *a, p: (index(*a, p), 0), p=p))
            for p in range(n_pages)]


def _sort_key(x):
    b = pltpu.bitcast(x, I32)
    return jnp.where(b < 0, b ^ jnp.int32(0x7FFFFFFF), b)


def _topk_mask(sc, idx, k, idx_bits, keep_all=False):
    key = _sort_key(sc)
    search = jnp.logical_not(keep_all)

    def count(m):
        return jnp.sum(m.astype(I32), axis=1, keepdims=True)

    t0 = jnp.where(count(key >= 0) >= k, jnp.int32(0), jnp.int32(INT_MIN))

    def vstep(i, t):
        cand = t | (jnp.int32(1) << (30 - i))
        return jnp.where(count(key >= cand) >= k, cand, t)

    thr = lax.fori_loop(0, jnp.where(search, 31, 0), vstep, t0)
    gt = key > thr
    eq = key == thr
    need = k - count(gt)
    crowded = count(eq) > need
    n_steps = jnp.where(search & (jnp.max(crowded.astype(I32)) > 0), idx_bits, 0)

    def istep(i, c):
        cand = c | (jnp.int32(1) << (idx_bits - 1 - i))
        return jnp.where(count(eq & (idx < cand)) < need, cand, c)

    cut = lax.fori_loop(0, n_steps, istep, jnp.zeros_like(thr))
    return keep_all | gt | (eq & (jnp.logical_not(crowded) | (idx <= cut)))


def _stack_heads(ref, n_heads, scale):
    return jnp.concatenate([(ref[:, r * HEAD_DIM:(r + 1) * HEAD_DIM] * scale).astype(BF16) for r in range(n_heads)],
                           axis=0)


def _slope_stack(slopes):
    return jnp.concatenate([jnp.full((1, 1, 1), s, F32) for s in slopes], axis=0)


def _attend_stacked(q2, k, v, dm, slope2):
    n_heads = slope2.shape[0]
    rows, keys = dm.shape
    s = _dot_nt(q2, k).reshape(n_heads, rows, keys) - slope2 * dm[None]
    m = jnp.max(s, axis=2, keepdims=True)
    p = jnp.exp2(s - m)
    l = jnp.sum(p, axis=2, keepdims=True)
    o = _dot(p.reshape(n_heads * rows, keys).astype(BF16), v)
    return o * (1.0 / l).reshape(n_heads * rows, 1)


def _attend(q2, k_of, v, dm_of, slope2, keys, s_ref, pb_ref):
    n_heads = len(slope2)
    rows = q2.shape[0] // n_heads
    chunk = 2 * LANES if keys % (2 * LANES) == 0 else LANES
    bounds = [(a, a + chunk) for a in range(0, keys, chunk)]
    inv_l = []
    for r in range(n_heads):
        q_r = q2[r * rows:(r + 1) * rows]
        m_vec = jnp.full((rows, LANES), -jnp.inf, F32)
        for a, b in bounds:
            s = _dot_nt(q_r, k_of(a, b)) - slope2[r] * dm_of(a, b)
            s_ref[r, :, a:b] = s
            for i in range(chunk // LANES):
                m_vec = jnp.maximum(m_vec, s[:, i * LANES:(i + 1) * LANES])
        m = jnp.max(m_vec, axis=1, keepdims=True)
        l_vec = jnp.zeros((rows, LANES), F32)
        for a, b in bounds:
            p = jnp.exp2(s_ref[r, :, a:b] - m)
            for i in range(chunk // LANES):
                l_vec = l_vec + p[:, i * LANES:(i + 1) * LANES]
            pb_ref[r * rows:(r + 1) * rows, a:b] = p.astype(BF16)
        inv_l.append(1.0 / jnp.sum(l_vec, axis=1, keepdims=True))
    return _dot(pb_ref[:, 0:keys], v) * jnp.concatenate(inv_l, axis=0)


def _spans(seq):
    span = KEY_SPAN if seq % KEY_SPAN == 0 else seq
    return span, seq // span


def _group_slope(g, r, n_heads, n_groups):
    rep = n_heads // n_groups
    table = _slopes(n_heads)
    out = jnp.float32(table[r])
    for gg in range(1, n_groups):
        out = jnp.where(g == gg, jnp.float32(table[gg * rep + r]), out)
    return out


def _dsa_prompt_mask(c, qi_ref, cq_ref, ck_ref, dm_ref, kw, k_keep):
    t = c * Q_TILE + lax.broadcasted_iota(I32, (Q_TILE, 1), 0)
    kpos = lax.broadcasted_iota(I32, (Q_TILE, kw), 1)
    causal = kpos <= t
    ki = ck_ref[0:kw, 0:IDX_DIM].astype(BF16)
    wi = cq_ref[:, IDX_DIM:IDX_DIM + IDX_HEADS] * (IDX_HEADS ** -0.5)
    sc = jnp.zeros((Q_TILE, kw), F32)
    for h in range(IDX_HEADS):
        a = _dot_nt(qi_ref[:, h * IDX_DIM:(h + 1) * IDX_DIM].astype(BF16), ki)
        sc = sc + jnp.maximum(a, 0.0) * wi[:, h:h + 1]
    sc = jnp.where(causal, sc, NEG_INF)
    keep = _topk_mask(sc, kpos, k_keep, max(1, (kw - 1).bit_length()), keep_all=(c + 1) * Q_TILE <= k_keep)
    dm_ref[:, 0:kw] = jnp.where(keep & causal, (t - kpos).astype(F32), MASKED_DIST)


def _dsa_prompt_kernel(q_ref, qi_ref, cq_ref, ck_ref, kv_ref, o_ref, dm_ref, s_ref, pb_ref, *, seq, k_keep):
    c, g = pl.program_id(1), pl.program_id(2)
    rep = DSA_HEADS // DSA_KV_HEADS
    span, n_var = _spans(seq)
    for v in range(n_var):
        @pl.when((c * Q_TILE) // span == v)
        def _(kw=(v + 1) * span):
            @pl.when(g == 0)
            def _():
                _dsa_prompt_mask(c, qi_ref, cq_ref, ck_ref, dm_ref, kw, k_keep)

            slope2 = [_group_slope(g, r, DSA_HEADS, DSA_KV_HEADS) * LOG2E for r in range(rep)]
            o = _attend(_stack_heads(q_ref, rep, SCALE * LOG2E), lambda a, b: kv_ref[a:b, 0:HEAD_DIM],
                        kv_ref[0:kw, HEAD_DIM:2 * HEAD_DIM], lambda a, b: dm_ref[:, a:b], slope2, kw, s_ref, pb_ref)
            for r in range(rep):
                o_ref[:, r * HEAD_DIM:(r + 1) * HEAD_DIM] = o[r * Q_TILE:(r + 1) * Q_TILE].astype(BF16)


def _dsa_prompt(q, qi, h_c, kv, batch, seq):
    nc = seq // Q_TILE
    k_keep = min(IDX_TOPK, seq // 4)
    rep = DSA_HEADS // DSA_KV_HEADS
    tile = lambda b, c, g: (b * nc + c, 0)
    tile_g = lambda b, c, g: (b * nc + c, g)
    return pl.pallas_call(
        functools.partial(_dsa_prompt_kernel, seq=seq, k_keep=k_keep),
        grid=(batch, nc, DSA_KV_HEADS),
        in_specs=[pl.BlockSpec((Q_TILE, rep * HEAD_DIM), tile_g),
                  pl.BlockSpec((Q_TILE, IDX_HEADS * IDX_DIM), tile),
                  pl.BlockSpec((Q_TILE, LANES), tile),
                  pl.BlockSpec((seq, LANES), lambda b, c, g: (b, 0)),
                  pl.BlockSpec((seq, 2 * HEAD_DIM), lambda b, c, g: (b, g))],
        out_specs=pl.BlockSpec((Q_TILE, rep * HEAD_DIM), tile_g),
        out_shape=jax.ShapeDtypeStruct((batch * seq, Q_ROW), BF16),
        scratch_shapes=[pltpu.VMEM((Q_TILE, seq), F32), pltpu.VMEM((rep, Q_TILE, seq), F32),
                        pltpu.VMEM((rep * Q_TILE, seq), BF16)],
        compiler_params=_cparams(("parallel", "arbitrary", "arbitrary")),
        name="dsa_prompt",
    )(q, qi, h_c, h_c, kv)


def _peb_kernel(pe_ref, w1_ref, o_ref):
    for c in range(2):
        o_ref[c] = _dot_hi(pe_ref[c], w1_ref[c])


def _cmp_pe_bias(cmp_pe, cmp_w1):
    return pl.pallas_call(
        _peb_kernel,
        out_shape=jax.ShapeDtypeStruct((2, 1, HEAD_DIM), F32),
        name="cmp_pe_bias",
    )(cmp_pe.reshape(2, 1, CMP_BLOCK * HEAD_DIM), cmp_w1.reshape(2, CMP_BLOCK * HEAD_DIM, HEAD_DIM))


def _compress_kernel(pt_ref, *refs, n_pages, page):
    pages = refs[:n_pages]
    w1_ref, w2_ref, peb_ref, o_ref = refs[n_pages:]
    sub_per_page = page // CMP_STRIDE
    n_sub = n_pages * sub_per_page
    prow = lax.broadcasted_iota(I32, (page, page), 0)
    pcol = lax.broadcasted_iota(I32, (page, page), 1)
    regroup = (pcol == (prow % sub_per_page) * CMP_STRIDE + prow // sub_per_page).astype(BF16)
    by_pos = []
    for p in pages:
        x = jnp.concatenate([_slab(p, page, gc // 2, gc % 2) for gc in range(KV_SLABS)], axis=1).astype(BF16)
        by_pos.append(_dot(regroup, x))
    for c in range(2):
        lhs = jnp.concatenate([
            jnp.concatenate([
                jnp.concatenate([bp[j * sub_per_page:(j + 1) * sub_per_page,
                                    (g * 2 + c) * HEAD_DIM:(g * 2 + c + 1) * HEAD_DIM] for bp in by_pos],
                                axis=0).astype(BF16)
                for j in range(CMP_STRIDE)], axis=1)
            for g in range(NSA_KV_GROUPS)], axis=0)
        part = _dot(lhs, w1_ref[c].astype(BF16))
        w2 = w2_ref[c].astype(BF16)
        for g in range(NSA_KV_GROUPS):
            col = g * 256 + c * 128
            pg = part[g * n_sub:(g + 1) * n_sub]
            nxt = pltpu.roll(pg[:, HEAD_DIM:], n_sub - 1, 0)
            hid = pg[:, :HEAD_DIM] + nxt + peb_ref[c]
            o_ref[0, :, col:col + HEAD_DIM] = _dot(jax.nn.gelu(hid).astype(BF16), w2)


def _compress(pool, page_table, page, w1r, w2, peb):
    n_seq, n_pages = page_table.shape
    n_sub = n_pages * page // CMP_STRIDE
    page_specs = _page_specs(n_pages, page * KV_SLABS, HEAD_DIM, lambda n, pt, p: pt[n, p])
    return pl.pallas_call(
        functools.partial(_compress_kernel, n_pages=n_pages, page=page),
        grid_spec=pltpu.PrefetchScalarGridSpec(
            num_scalar_prefetch=1,
            grid=(n_seq,),
            in_specs=page_specs + [
                pl.BlockSpec((2, CMP_STRIDE * HEAD_DIM, 2 * HEAD_DIM), lambda n, pt: (0, 0, 0)),
                pl.BlockSpec((2, HEAD_DIM, HEAD_DIM), lambda n, pt: (0, 0, 0)),
                pl.BlockSpec((2, 1, HEAD_DIM), lambda n, pt: (0, 0, 0))],
            out_specs=pl.BlockSpec((1, n_sub, KV_ROW), lambda n, pt: (n, 0, 0))),
        out_shape=jax.ShapeDtypeStruct((n_seq, n_sub, KV_ROW), F32),
        compiler_params=_cparams(("arbitrary",)),
        name="nsa_compress",
    )(page_table, *([pool] * n_pages), w1r, w2, peb)


def _cover(n_cmp_pad, n_slc_pad):
    cs = lax.broadcasted_iota(I32, (n_cmp_pad, n_slc_pad), 0) * CMP_STRIDE
    bs = lax.broadcasted_iota(I32, (n_cmp_pad, n_slc_pad), 1) * SLC_BLOCK
    return ((cs < bs + SLC_BLOCK) & (cs + CMP_BLOCK > bs)).astype(F32)


def _select_blocks(score, t, n_slc, n_keep):
    rows, width = score.shape
    j = lax.broadcasted_iota(I32, (rows, width), 1)
    cur = t // SLC_BLOCK
    forced = (j == 0) | (j == cur) | (j == cur - 1)
    admissible = j * SLC_BLOCK <= t
    score = jnp.where(forced, FORCE_SCORE, jnp.where(admissible, score, NEG_INF))
    rank = jnp.zeros((rows, width), I32)
    for k in range(n_slc):
        sk = score[:, k:k + 1]
        ahead = (sk > score) | ((sk == score) & (j > k))
        rank = rank + ahead.astype(I32)
    return ((rank < n_keep) & (j < n_slc)).astype(F32)


def _select_blocks_t(score_t, t_row, n_keep):
    n_slc, width = score_t.shape
    j = lax.broadcasted_iota(I32, (n_slc, width), 0)
    cur = t_row // SLC_BLOCK
    forced = (j == 0) | (j == cur) | (j == cur - 1)
    admissible = j * SLC_BLOCK <= t_row
    score_t = jnp.where(forced, FORCE_SCORE, jnp.where(admissible, score_t, NEG_INF))
    rank = jnp.zeros((n_slc, width), I32)
    for k in range(n_slc):
        sk = score_t[k:k + 1, :]
        ahead = (sk > score_t) | ((sk == score_t) & (j > k))
        rank = rank + ahead.astype(I32)
    return (rank < n_keep).astype(F32)


def _nsa_prompt_body(c, g, q_ref, gate_ref, ckv_ref, slc_ref, win_ref, o_ref, kw, seq, n_cmp, win_keys):
    t = c * Q_TILE + lax.broadcasted_iota(I32, (Q_TILE, 1), 0)
    t_row = c * Q_TILE + lax.broadcasted_iota(I32, (1, Q_TILE), 1)
    rep = NSA_HEADS // NSA_KV_GROUPS
    slope2 = _slope_stack([_group_slope(g, r, NSA_HEADS, NSA_KV_GROUPS) * LOG2E for r in range(rep)])
    n_cmp_pad = ckv_ref.shape[1]
    n_slc = seq // SLC_BLOCK
    n_keep = min(SLC_TOPN, n_slc)

    kc = lax.broadcasted_iota(I32, (Q_TILE, n_cmp_pad), 1)
    dist_c_i = t - (kc * CMP_STRIDE + (CMP_BLOCK - 1))
    valid_c = (dist_c_i >= 0) & (kc < n_cmp)
    dist_c = dist_c_i.astype(F32)
    cover_bs = lax.broadcasted_iota(I32, (n_slc, n_cmp_pad), 0) * SLC_BLOCK
    cover_cs = lax.broadcasted_iota(I32, (n_slc, n_cmp_pad), 1) * CMP_STRIDE
    cover_t = ((cover_cs < cover_bs + SLC_BLOCK) & (cover_cs + CMP_BLOCK > cover_bs)).astype(F32)
    slc_rows = -(-n_slc // LANES) * LANES
    expand = (lax.broadcasted_iota(I32, (slc_rows, kw), 1) // SLC_BLOCK
              == lax.broadcasted_iota(I32, (slc_rows, kw), 0)).astype(BF16)

    kpos = lax.broadcasted_iota(I32, (Q_TILE, kw), 1)
    causal = kpos <= t
    dist_s = (t - kpos).astype(F32)

    w0 = pl.multiple_of(jnp.maximum(c * Q_TILE + Q_TILE - win_keys, 0), Q_TILE)
    wpos = w0 + lax.broadcasted_iota(I32, (Q_TILE, win_keys), 1)
    dist_w_i = t - wpos
    dm_w = jnp.where((dist_w_i >= 0) & (dist_w_i <= WINDOW), dist_w_i.astype(F32), MASKED_DIST)

    gates = jax.nn.sigmoid(gate_ref[:, 0:3 * rep])

    ck = ckv_ref[0, :, 0:HEAD_DIM].astype(BF16)
    cv = ckv_ref[0, :, HEAD_DIM:2 * HEAD_DIM].astype(BF16)
    q2 = _stack_heads(q_ref, rep, SCALE * LOG2E)
    s = jnp.where(valid_c[None], _dot_nt(q2, ck).reshape(rep, Q_TILE, n_cmp_pad) - slope2 * dist_c[None], NEG_INF)
    m = jnp.max(s, axis=2, keepdims=True)
    p = jnp.where(valid_c[None], jnp.exp2(s - m), 0.0)
    l = jnp.sum(p, axis=2, keepdims=True)
    p = p * (1.0 / jnp.where(l > 0.0, l, 1.0))
    imp = jnp.sum(p, axis=0)
    o_c = _dot(p.reshape(rep * Q_TILE, n_cmp_pad).astype(BF16), cv)
    sel_t = _select_blocks_t(_dot_hi(cover_t, imp.T), t_row, n_keep)
    sel_t = jnp.concatenate([sel_t, jnp.zeros((slc_rows - n_slc, Q_TILE), F32)], axis=0)
    picked = _dot(sel_t.T.astype(BF16), expand) > 0.5
    dm_s = jnp.where(picked & causal, dist_s, MASKED_DIST)
    o_s = _attend_stacked(q2, slc_ref[0:kw, 0:HEAD_DIM], slc_ref[0:kw, HEAD_DIM:2 * HEAD_DIM], dm_s, slope2)
    o_w = _attend_stacked(q2, win_ref[pl.ds(w0, win_keys), 0:HEAD_DIM],
                          win_ref[pl.ds(w0, win_keys), HEAD_DIM:2 * HEAD_DIM], dm_w, slope2)
    gate_col = [jnp.concatenate([gates[:, 3 * r + k:3 * r + k + 1] for r in range(rep)], axis=0) for k in range(3)]
    o = gate_col[0] * o_c + gate_col[1] * o_s + gate_col[2] * o_w
    for r in range(rep):
        o_ref[:, r * HEAD_DIM:(r + 1) * HEAD_DIM] = o[r * Q_TILE:(r + 1) * Q_TILE].astype(BF16)


def _nsa_prompt_kernel(q_ref, gate_ref, ckv_ref, slc_ref, win_ref, o_ref, *, seq, n_cmp, win_keys):
    c, g = pl.program_id(1), pl.program_id(2)
    span, n_var = _spans(seq)
    for v in range(n_var):
        @pl.when((c * Q_TILE) // span == v)
        def _(kw=(v + 1) * span):
            _nsa_prompt_body(c, g, q_ref, gate_ref, ckv_ref, slc_ref, win_ref, o_ref, kw, seq, n_cmp, win_keys)


def _nsa_prompt(q, gate, ckv, slc, win, batch, seq):
    nc = seq // Q_TILE
    n_cmp_pad = ckv.shape[1]
    n_cmp = seq // CMP_STRIDE - CMP_BLOCK // CMP_STRIDE + 1
    win_keys = min(seq, WINDOW + Q_TILE)
    rep = NSA_HEADS // NSA_KV_GROUPS
    tile_g = lambda b, c, g: (b * nc + c, g)
    return pl.pallas_call(
        functools.partial(_nsa_prompt_kernel, seq=seq, n_cmp=n_cmp, win_keys=win_keys),
        grid=(batch, nc, NSA_KV_GROUPS),
        in_specs=[pl.BlockSpec((Q_TILE, rep * HEAD_DIM), tile_g),
                  pl.BlockSpec((Q_TILE, LANES), tile_g),
                  pl.BlockSpec((1, n_cmp_pad, 2 * HEAD_DIM), lambda b, c, g: (b, 0, g)),
                  pl.BlockSpec((seq, 2 * HEAD_DIM), lambda b, c, g: (b, g)),
                  pl.BlockSpec((seq, 2 * HEAD_DIM), lambda b, c, g: (b, g))],
        out_specs=pl.BlockSpec((Q_TILE, rep * HEAD_DIM), tile_g),
        out_shape=jax.ShapeDtypeStruct((batch * seq, Q_ROW), BF16),
        compiler_params=_cparams(("parallel", "arbitrary", "arbitrary")),
        name="nsa_prompt",
    )(q, gate, ckv, slc, win)


def _group_rows(per_group, rep):
    row = lax.broadcasted_iota(I32, per_group[0].shape, 0) // rep
    out = per_group[0]
    for g in range(1, len(per_group)):
        out = jnp.where(row == g, per_group[g], out)
    return out


def _dsa_sample_mask_kernel(pt_ref, *refs, n_pages, page, k_keep):
    n_ki = SEQ_GROUP * n_pages
    ki_pages = refs[:n_ki]
    qi_ref, wi_ref, cnew_ref, keep_ref = refs[n_ki:]
    past = n_pages * page
    width = past + LANES
    rows = []
    for s in range(SEQ_GROUP):
        qi = qi_ref[s].astype(BF16)
        wi = wi_ref[s] * (IDX_HEADS ** -0.5)
        knew = cnew_ref[pl.ds(s, 1), 0:IDX_DIM]
        a_past = jnp.concatenate([_dot(qi, ki_pages[s * n_pages + p][0].astype(BF16))
                                  for p in range(n_pages)], axis=1)
        a_new = _dot_nt(qi, jnp.broadcast_to(knew, (8, IDX_DIM)).astype(BF16))[:, 0:1]
        sc_past = jnp.sum(jnp.maximum(a_past, 0.0) * wi, axis=0, keepdims=True)
        sc_new = jnp.sum(jnp.maximum(a_new, 0.0) * wi, axis=0, keepdims=True)
        rows.append(jnp.concatenate([sc_past, jnp.broadcast_to(sc_new, (1, LANES))], axis=1))
    col = lax.broadcasted_iota(I32, (SEQ_GROUP, width), 1)
    sc = jnp.where(col <= past, jnp.concatenate(rows, axis=0), NEG_INF)
    keep = _topk_mask(sc, col, k_keep, max(1, (width - 1).bit_length())) & (col <= past)
    keep_ref[...] = keep.astype(F32)


def _dsa_sample_mask(kidx_pool_t, page_table, qi, wi, h_c):
    n_seq, n_pages = page_table.shape
    assert n_seq % SEQ_GROUP == 0
    page = kidx_pool_t.shape[2]
    total = n_pages * page + 1
    k_keep = min(IDX_TOPK, total // 4)
    width = n_pages * page + LANES
    ki_specs = [pl.BlockSpec((1, IDX_DIM, page),
                             functools.partial(lambda i, pt, s, p: (pt[i * SEQ_GROUP + s, p], 0, 0), s=s, p=p))
                for s in range(SEQ_GROUP) for p in range(n_pages)]
    return pl.pallas_call(
        functools.partial(_dsa_sample_mask_kernel, n_pages=n_pages, page=page, k_keep=k_keep),
        grid_spec=pltpu.PrefetchScalarGridSpec(
            num_scalar_prefetch=1,
            grid=(n_seq // SEQ_GROUP,),
            in_specs=ki_specs + [
                pl.BlockSpec((SEQ_GROUP, IDX_HEADS, IDX_DIM), lambda i, pt: (i, 0, 0)),
                pl.BlockSpec((SEQ_GROUP, IDX_HEADS, 1), lambda i, pt: (i, 0, 0)),
                pl.BlockSpec((SEQ_GROUP, LANES), lambda i, pt: (i, 0))],
            out_specs=pl.BlockSpec((SEQ_GROUP, width), lambda i, pt: (i, 0))),
        out_shape=jax.ShapeDtypeStruct((n_seq, width), F32),
        compiler_params=_cparams(("arbitrary",)),
        name="dsa_sample_mask",
    )(page_table, *([kidx_pool_t] * (SEQ_GROUP * n_pages)), qi, wi, h_c)


def _dsa_sample_kernel(pt_ref, *refs, n_pages, page):
    kv_pages = refs[:n_pages]
    q_ref, keep_ref, kvnew_ref, o_ref = refs[n_pages:]
    r = pl.program_id(0) % SEQ_GROUP
    past = n_pages * page
    rep = DSA_HEADS // DSA_KV_HEADS
    slopes = _slopes(DSA_HEADS)

    keep = keep_ref[pl.ds(r, 1), :] > 0.5
    keep_past = jnp.broadcast_to(keep[:, 0:past], (DSA_HEADS, past))
    keep_new = jnp.broadcast_to(keep[:, past:past + 1], (DSA_HEADS, 1))

    q = (q_ref[0] * SCALE).astype(BF16)
    q32 = q.astype(F32)
    kvnew = kvnew_ref[pl.ds(r, 1), :]
    slope_col = jnp.concatenate([jnp.full((1, 1), s, F32) for s in slopes], axis=0)
    dist = (past - lax.broadcasted_iota(I32, (DSA_HEADS, past), 1)).astype(F32)
    s_g, s_new_g = [], []
    for g in range(DSA_KV_HEADS):
        s_g.append(_dot_nt(q, _slab_all(kv_pages, page, g, 0)))
        s_new_g.append(jnp.sum(q32 * kvnew[:, g * 256:g * 256 + 128], axis=1, keepdims=True))
    s = jnp.where(keep_past, _group_rows(s_g, rep) - slope_col * dist, NEG_INF)
    s_new = jnp.where(keep_new, _group_rows(s_new_g, rep), NEG_INF)
    m = jnp.maximum(jnp.max(s, axis=1, keepdims=True), s_new)
    p = jnp.where(keep_past, jnp.exp(s - m), 0.0)
    p_new = jnp.where(keep_new, jnp.exp(s_new - m), 0.0)
    l = jnp.sum(p, axis=1, keepdims=True) + p_new
    inv = 1.0 / jnp.where(l > 0.0, l, 1.0)
    pb = p.astype(BF16)
    p_new = p_new.astype(BF16).astype(F32)
    o_g = []
    for g in range(DSA_KV_HEADS):
        o_g.append(p_new * kvnew[:, g * 256 + 128:g * 256 + 256] + _dot(pb, _slab_all(kv_pages, page, g, 1)))
    o_ref[0] = (_group_rows(o_g, rep) * inv).astype(BF16)


def _dsa_sample(kv_pool, page_table, page, q, keep, kv_new):
    n_seq, n_pages = page_table.shape
    width = keep.shape[1]
    kv_specs = _page_specs(n_pages, page * KV_SLABS, HEAD_DIM, lambda n, pt, p: pt[n, p])
    return pl.pallas_call(
        functools.partial(_dsa_sample_kernel, n_pages=n_pages, page=page),
        grid_spec=pltpu.PrefetchScalarGridSpec(
            num_scalar_prefetch=1,
            grid=(n_seq,),
            in_specs=kv_specs + [
                pl.BlockSpec((1, DSA_HEADS, HEAD_DIM), lambda n, pt: (n, 0, 0)),
                pl.BlockSpec((SEQ_GROUP, width), lambda n, pt: (n // SEQ_GROUP, 0)),
                pl.BlockSpec((SEQ_GROUP, KV_ROW), lambda n, pt: (n // SEQ_GROUP, 0))],
            out_specs=pl.BlockSpec((1, DSA_HEADS, HEAD_DIM), lambda n, pt: (n, 0, 0))),
        out_shape=jax.ShapeDtypeStruct((n_seq, DSA_HEADS, HEAD_DIM), BF16),
        compiler_params=_cparams(("arbitrary",)),
        name="dsa_sample",
    )(page_table, *([kv_pool] * n_pages), q, keep, kv_new)


def _nsa_sample_kernel(pt_ref, *refs, n_pages, page, n_cmp):
    slc_pages = refs[:n_pages]
    q_ref, gate_ref, ckv_ref, win_ref, slcnew_ref, winnew_ref, winslab_ref, o_ref, wout_ref = refs[n_pages:]
    n = pl.program_id(0)
    past = n_pages * page
    heads = NSA_HEADS
    rep = NSA_HEADS // NSA_KV_GROUPS
    slopes = _slopes(NSA_HEADS)
    slope_col = jnp.concatenate([jnp.full((1, 1), s, F32) for s in slopes], axis=0)
    q = (q_ref[0] * SCALE).astype(BF16)
    q32 = q.astype(F32)
    gates = jax.nn.sigmoid(gate_ref[0])
    n_cmp_pad = ckv_ref.shape[1]
    total = past + 1
    n_slc = -(-total // SLC_BLOCK)
    n_keep = min(SLC_TOPN, n_slc)
    slc_pad = -(-n_slc // LANES) * LANES

    kc = lax.broadcasted_iota(I32, (heads, n_cmp_pad), 1)
    dist_c_i = past - (kc * CMP_STRIDE + (CMP_BLOCK - 1))
    valid_c = (dist_c_i >= 0) & (kc < n_cmp)
    s_g = [_dot_nt(q, ckv_ref[0, :, g * 256:g * 256 + 128].astype(BF16)) for g in range(NSA_KV_GROUPS)]
    s = jnp.where(valid_c, _group_rows(s_g, rep) - slope_col * dist_c_i.astype(F32), NEG_INF)
    m = jnp.max(s, axis=1, keepdims=True)
    p = jnp.where(valid_c, jnp.exp(s - m), 0.0)
    l = jnp.sum(p, axis=1, keepdims=True)
    p = p * (1.0 / jnp.where(l > 0.0, l, 1.0))
    pb = p.astype(BF16)
    o_c = _group_rows([_dot(pb, ckv_ref[0, :, g * 256 + 128:g * 256 + 256].astype(BF16))
                       for g in range(NSA_KV_GROUPS)], rep)
    same_group = (lax.broadcasted_iota(I32, (heads, heads), 0) // rep
                  == lax.broadcasted_iota(I32, (heads, heads), 1) // rep).astype(F32)
    imp = _dot_hi(same_group, p)
    score = _dot_hi(imp, _cover(n_cmp_pad, slc_pad))
    sel = _select_blocks(score, jnp.full((heads, 1), past, I32), n_slc, n_keep)
    expand = (lax.broadcasted_iota(I32, (slc_pad, past), 1) // SLC_BLOCK
              == lax.broadcasted_iota(I32, (slc_pad, past), 0)).astype(BF16)
    valid_s = _dot(sel.astype(BF16), expand) > 0.5
    new_blk = past // SLC_BLOCK
    valid_s_new = sel[:, new_blk:new_blk + 1] > 0.5

    def attend_with_new(s_past, valid_past, s_new, valid_new, dist_past, v_of, v_new_of):
        s_p = jnp.where(valid_past, s_past - slope_col * dist_past, NEG_INF)
        s_n = jnp.where(valid_new, s_new, NEG_INF)
        mm = jnp.maximum(jnp.max(s_p, axis=1, keepdims=True), s_n)
        pp = jnp.where(valid_past, jnp.exp(s_p - mm), 0.0)
        pn = jnp.where(valid_new, jnp.exp(s_n - mm), 0.0)
        ll = jnp.sum(pp, axis=1, keepdims=True) + pn
        inv = 1.0 / jnp.where(ll > 0.0, ll, 1.0)
        ppb = pp.astype(BF16)
        pn = pn.astype(BF16).astype(F32)
        outs = [v_of(ppb, g) + pn * v_new_of(g) for g in range(NSA_KV_GROUPS)]
        return _group_rows(outs, rep) * inv

    slcnew = slcnew_ref[pl.ds(n, 1), :]
    s_g, s_new_g = [], []
    for g in range(NSA_KV_GROUPS):
        s_g.append(_dot_nt(q, _slab_all(slc_pages, page, g, 0)))
        s_new_g.append(jnp.sum(q32 * slcnew[:, g * 256:g * 256 + 128], axis=1, keepdims=True))
    dist_s = (past - lax.broadcasted_iota(I32, (heads, past), 1)).astype(F32)

    def slc_v(ppb, g):
        return _dot(ppb, _slab_all(slc_pages, page, g, 1))

    o_s = attend_with_new(_group_rows(s_g, rep), valid_s, _group_rows(s_new_g, rep), valid_s_new, dist_s, slc_v,
                          lambda g: slcnew[:, g * 256 + 128:g * 256 + 256])

    n_buf = win_ref.shape[0] // KV_SLABS
    winnew = winnew_ref[pl.ds(n, 1), :]
    dist_w_i = n_buf - lax.broadcasted_iota(I32, (heads, n_buf), 1)
    valid_w = dist_w_i <= WINDOW
    s_g = [_dot_nt(q, _slab(win_ref, n_buf, g, 0).astype(BF16)) for g in range(NSA_KV_GROUPS)]
    s_new_g = [jnp.sum(q32 * winnew[:, g * 256:g * 256 + 128], axis=1, keepdims=True)
               for g in range(NSA_KV_GROUPS)]
    o_w = attend_with_new(_group_rows(s_g, rep), valid_w, _group_rows(s_new_g, rep),
                          jnp.full((heads, 1), True), dist_w_i.astype(F32),
                          lambda ppb, g: _dot(ppb, _slab(win_ref, n_buf, g, 1).astype(BF16)),
                          lambda g: winnew[:, g * 256 + 128:g * 256 + 256])

    o_ref[0] = (gates[:, 0:1] * o_c + gates[:, 1:2] * o_s + gates[:, 2:3] * o_w).astype(BF16)

    keep_rows = (n_buf - 1) * KV_SLABS
    wout_ref[0:keep_rows, :] = win_ref[KV_SLABS:n_buf * KV_SLABS, :]
    wout_ref[keep_rows:keep_rows + KV_SLABS, :] = winslab_ref[pl.ds(pl.multiple_of(n * KV_SLABS, KV_SLABS), KV_SLABS), :]


def _nsa_sample(slc_pool, page_table, page, q, gate, ckv, win_state, slc_new, win_new, win_new_slab):
    n_seq, n_pages = page_table.shape
    n_cmp_pad = ckv.shape[1]
    n_cmp = n_pages * page // CMP_STRIDE - CMP_BLOCK // CMP_STRIDE + 1
    n_buf = win_state.shape[0] // (n_seq * KV_SLABS)
    slc_specs = _page_specs(n_pages, page * KV_SLABS, HEAD_DIM, lambda n, pt, p: pt[n, p])
    return pl.pallas_call(
        functools.partial(_nsa_sample_kernel, n_pages=n_pages, page=page, n_cmp=n_cmp),
        grid_spec=pltpu.PrefetchScalarGridSpec(
            num_scalar_prefetch=1,
            grid=(n_seq,),
            in_specs=slc_specs + [
                pl.BlockSpec((1, NSA_HEADS, HEAD_DIM), lambda n, pt: (n, 0, 0)),
                pl.BlockSpec((1, NSA_HEADS, 3), lambda n, pt: (n, 0, 0)),
                pl.BlockSpec((1, n_cmp_pad, KV_ROW), lambda n, pt: (n, 0, 0)),
                pl.BlockSpec((n_buf * KV_SLABS, HEAD_DIM), lambda n, pt: (n, 0)),
                pl.BlockSpec((n_seq, KV_ROW), lambda n, pt: (0, 0)),
                pl.BlockSpec((n_seq, KV_ROW), lambda n, pt: (0, 0)),
                pl.BlockSpec((n_seq * KV_SLABS, HEAD_DIM), lambda n, pt: (0, 0))],
            out_specs=[pl.BlockSpec((1, NSA_HEADS, HEAD_DIM), lambda n, pt: (n, 0, 0)),
                       pl.BlockSpec((n_buf * KV_SLABS, HEAD_DIM), lambda n, pt: (n, 0))]),
        out_shape=[jax.ShapeDtypeStruct((n_seq, NSA_HEADS, HEAD_DIM), BF16),
                   jax.ShapeDtypeStruct(win_state.shape, F32)],
        compiler_params=_cparams(("arbitrary",)),
        name="nsa_sample",
    )(page_table, *([slc_pool] * n_pages), q, gate, ckv, win_state, slc_new, win_new, win_new_slab)


def _ln(v, g, b):
    mu = jnp.mean(v, axis=1, keepdims=True)
    d = v - mu
    var = jnp.mean(d * d, axis=1, keepdims=True)
    return d * lax.rsqrt(var + LN_EPS) * g + b


def _ln_router_kernel(xa_ref, mixa_ref, xb_ref, mixb_ref, g_ref, b_ref, wr_ref, br_ref, x1_ref, e_ref, gate_ref, *,
                      alpha, blocks_a):
    from_a = pl.program_id(0) < blocks_a
    @pl.when(from_a)
    def _():
        x1_ref[...] = _ln(alpha * xa_ref[...] + mixa_ref[...], g_ref[...], b_ref[...])

    @pl.when(jnp.logical_not(from_a))
    def _():
        x1_ref[...] = _ln(alpha * xb_ref[...] + mixb_ref[...], g_ref[...], b_ref[...])

    logits = _dot_hi(x1_ref[...], wr_ref[...]) + br_ref[...]
    rows = logits.shape[0]
    lane = lax.broadcasted_iota(I32, (rows, LANES), 1)
    big = jnp.int32(LANES)
    is_grp = lane < N_GROUPS
    lg = jnp.where(is_grp, logits, -jnp.inf)
    mg = jnp.max(lg, axis=1, keepdims=True)
    grp = jnp.min(jnp.where(lg == mg, lane, big), axis=1, keepdims=True)
    p_grp = 1.0 / jnp.sum(jnp.where(is_grp, jnp.exp(lg - mg), 0.0), axis=1, keepdims=True)
    ex = lane - N_GROUPS
    in_grp = (ex >= 0) & (ex < N_EXPERTS) & (ex // EXPERTS_PER_GROUP == grp)
    le = jnp.where(in_grp, logits, -jnp.inf)
    m1 = jnp.max(le, axis=1, keepdims=True)
    e1 = jnp.min(jnp.where(le == m1, ex, big), axis=1, keepdims=True)
    le2 = jnp.where(ex == e1, -jnp.inf, le)
    m2 = jnp.max(le2, axis=1, keepdims=True)
    e2 = jnp.min(jnp.where(le2 == m2, ex, big), axis=1, keepdims=True)
    z = jnp.sum(jnp.where(in_grp, jnp.exp(le - m1), 0.0), axis=1, keepdims=True)
    p1 = 1.0 / z
    p2 = jnp.exp(m2 - m1) / z
    g1 = p_grp * p1 / (p1 + p2)
    g2 = p_grp * p2 / (p1 + p2)
    e_ref[...] = jnp.where(lane == 0, e1, jnp.where(lane == 1, e2, 0))
    gate_ref[...] = jnp.where(lane == 0, g1, jnp.where(lane == 1, g2, 0.0))


def _ln_router(xa, mixa, xb, mixb, g, b, wr, br, alpha):
    (ma, d), mb = xa.shape, xb.shape[0]
    tm = _pick(mb, (128, 64, 32, 16, 8))
    assert ma % tm == 0 and mb % tm == 0
    blocks_a, blocks_b = ma // tm, mb // tm
    seg_a = lambda i: (jnp.minimum(i, blocks_a - 1), 0)
    seg_b = lambda i: (jnp.maximum(i - blocks_a, 0), 0)
    row = lambda i: (i, 0)
    fixed = lambda i: (0, 0)
    m = ma + mb
    return pl.pallas_call(
        functools.partial(_ln_router_kernel, alpha=alpha, blocks_a=blocks_a),
        grid=(blocks_a + blocks_b,),
        in_specs=[pl.BlockSpec((tm, d), seg_a), pl.BlockSpec((tm, d), seg_a),
                  pl.BlockSpec((tm, d), seg_b), pl.BlockSpec((tm, d), seg_b),
                  pl.BlockSpec((1, d), fixed), pl.BlockSpec((1, d), fixed),
                  pl.BlockSpec((d, LANES), fixed), pl.BlockSpec((1, LANES), fixed)],
        out_specs=[pl.BlockSpec((tm, d), row), pl.BlockSpec((tm, LANES), row), pl.BlockSpec((tm, LANES), row)],
        out_shape=[jax.ShapeDtypeStruct((m, d), F32), jax.ShapeDtypeStruct((m, LANES), I32),
                   jax.ShapeDtypeStruct((m, LANES), F32)],
        compiler_params=_cparams(("arbitrary",)),
        name="ln_router",
    )(xa, mixa, xb, mixb, g, b, wr, br)


DISPATCH_UNROLL = 8


def _dispatch_kernel(src_ref, rows_ref, x_ref, o_ref, buf_ref, sem):
    i = pl.program_id(0)
    n_rows = rows_ref[i]

    @pl.when(i == 0)
    def _():
        buf_ref[...] = jnp.zeros_like(buf_ref)

    def row_copy(r):
        tok = src_ref[i * MOE_ROWS + r]
        return pltpu.make_async_copy(x_ref.at[pl.ds(tok, 1)], buf_ref.at[pl.ds(r, 1)], sem)

    def trips(fn):
        def trip(t, carry):
            for u in range(DISPATCH_UNROLL):
                r = t * DISPATCH_UNROLL + u

                @pl.when(r < n_rows)
                def _():
                    fn(row_copy(r))

            return carry

        lax.fori_loop(0, (n_rows + DISPATCH_UNROLL - 1) // DISPATCH_UNROLL, trip, 0)

    trips(lambda cp: cp.start())
    trips(lambda cp: cp.wait())
    o_ref[...] = buf_ref[...].astype(BF16)


def _dispatch(src_tok, rows_in_block, x1, nb):
    d = x1.shape[1]
    return pl.pallas_call(
        _dispatch_kernel,
        grid_spec=pltpu.PrefetchScalarGridSpec(
            num_scalar_prefetch=2,
            grid=(nb,),
            in_specs=[pl.BlockSpec(memory_space=pl.ANY)],
            out_specs=pl.BlockSpec((MOE_ROWS, d), lambda i, s, u: (i, 0)),
            scratch_shapes=[pltpu.VMEM((MOE_ROWS, d), F32), pltpu.SemaphoreType.DMA(())]),
        out_shape=jax.ShapeDtypeStruct((nb * MOE_ROWS, d), BF16),
        compiler_params=_cparams(("arbitrary",)),
        name="moe_dispatch",
    )(src_tok, rows_in_block, x1)


def _expert_mm_kernel(run_ref, rune_ref, nruns_ref, nused_ref, x_ref, *refs, n_w, tn, epilogue):
    w_hbm = refs[:n_w]
    o_ref = refs[n_w]
    w_buf = refs[n_w + 1:2 * n_w + 1]
    w16 = refs[2 * n_w + 1:3 * n_w + 1]
    sem, state = refs[3 * n_w + 1], refs[3 * n_w + 2]
    j, i = pl.program_id(0), pl.program_id(1)
    n_runs = nruns_ref[0]
    total = pl.num_programs(0) * n_runs
    cur = j * n_runs + run_ref[i]

    def item_copies(item):
        e = rune_ref[item % n_runs]
        col = pl.multiple_of((item // n_runs) * tn, tn)
        slot = item % MOE_RING
        return [pltpu.make_async_copy(w_hbm[a].at[e, :, pl.ds(col, tn)], w_buf[a].at[slot], sem.at[a, slot])
                for a in range(n_w)]

    @pl.when((j == 0) & (i == 0))
    def _():
        state[0] = 0
        state[1] = -1

    def request(item, carry):
        for cp in item_copies(item):
            cp.start()
        return carry

    limit = jnp.minimum(cur + MOE_RING, total)
    lax.fori_loop(state[0], limit, request, 0)
    state[0] = jnp.maximum(state[0], limit)

    @pl.when(state[1] != cur)
    def _():
        for cp in item_copies(cur):
            cp.wait()
        slot = cur % MOE_RING
        for a in range(n_w):
            w16[a][...] = w_buf[a][slot].astype(BF16)
        state[1] = cur

    @pl.when(i < nused_ref[0])
    def _():
        x = x_ref[...]
        o_ref[...] = epilogue(*[_dot(x, w16[a][...]) for a in range(n_w)]).astype(o_ref.dtype)

    @pl.when(i >= nused_ref[0])
    def _():
        o_ref[...] = jnp.zeros_like(o_ref)


def _expert_mm(plan, x, weights, tn, epilogue, out_dtype, name):
    run_of_block, run_e, n_runs, n_used = plan
    nb = run_of_block.shape[0]
    kd, n = weights[0].shape[1], weights[0].shape[2]
    n_w = len(weights)
    return pl.pallas_call(
        functools.partial(_expert_mm_kernel, n_w=n_w, tn=tn, epilogue=epilogue),
        grid_spec=pltpu.PrefetchScalarGridSpec(
            num_scalar_prefetch=4,
            grid=(n // tn, nb),
            in_specs=[pl.BlockSpec((MOE_ROWS, kd), lambda j, i, run, rune, nruns, nused: (jnp.minimum(i, nused[0] - 1), 0))]
            + [pl.BlockSpec(memory_space=pl.ANY)] * n_w,
            out_specs=pl.BlockSpec((MOE_ROWS, tn), lambda j, i, *_: (i, j)),
            scratch_shapes=[pltpu.VMEM((MOE_RING, kd, tn), F32) for _ in range(n_w)]
            + [pltpu.VMEM((kd, tn), BF16) for _ in range(n_w)]
            + [pltpu.SemaphoreType.DMA((n_w, MOE_RING)), pltpu.SMEM((2,), I32)]),
        out_shape=jax.ShapeDtypeStruct((nb * MOE_ROWS, n), out_dtype),
        compiler_params=_cparams(("arbitrary", "arbitrary")),
        name=name,
    )(run_of_block, run_e, n_runs, n_used, x, *weights)


def _swiglu(a, u):
    return a * jax.nn.sigmoid(a) * u


def _combine_kernel(dest_ref, y_ref, x1_ref, gate_ref, g_ref, b_ref, o_ref, buf_ref, sem, *, tm, alpha, off):
    i = pl.program_id(0) + off

    def row_copy(r, k):
        slot = dest_ref[(i * tm + r) * EXPERT_TOPK + k]
        return pltpu.make_async_copy(y_ref.at[pl.ds(slot, 1)], buf_ref.at[k, pl.ds(r, 1)], sem)

    def start(r, carry):
        for k in range(EXPERT_TOPK):
            row_copy(r, k).start()
        return carry

    lax.fori_loop(0, tm, start, 0, unroll=4)

    def wait(r, carry):
        for k in range(EXPERT_TOPK):
            row_copy(r, k).wait()
        return carry

    lax.fori_loop(0, tm, wait, 0, unroll=4)
    gate = gate_ref[...]
    f = gate[:, 0:1] * buf_ref[0] + gate[:, 1:2] * buf_ref[1]
    o_ref[...] = _ln(alpha * x1_ref[...] + f, g_ref[...], b_ref[...])


def _combine(dest, ybuf, x1, gate, row_offset, m, g, b, alpha):
    d = x1.shape[1]
    tm = _pick(m, (128,))
    assert row_offset % tm == 0
    off = row_offset // tm
    return pl.pallas_call(
        functools.partial(_combine_kernel, tm=tm, alpha=alpha, off=off),
        grid_spec=pltpu.PrefetchScalarGridSpec(
            num_scalar_prefetch=1,
            grid=(m // tm,),
            in_specs=[pl.BlockSpec(memory_space=pl.ANY),
                      pl.BlockSpec((tm, d), lambda i, dd: (i + off, 0)),
                      pl.BlockSpec((tm, LANES), lambda i, dd: (i + off, 0)),
                      pl.BlockSpec((1, d), lambda i, dd: (0, 0)),
                      pl.BlockSpec((1, d), lambda i, dd: (0, 0))],
            out_specs=pl.BlockSpec((tm, d), lambda i, dd: (i, 0)),
            scratch_shapes=[pltpu.VMEM((EXPERT_TOPK, tm, d), F32), pltpu.SemaphoreType.DMA(())]),
        out_shape=jax.ShapeDtypeStruct((m, d), F32),
        compiler_params=_cparams(("arbitrary",)),
        name="moe_combine",
    )(dest, ybuf, x1, gate, g, b)


def _route_plan(e_all, nb):
    a_n = e_all.shape[0]
    onehot = (e_all[:, None] == jnp.arange(N_EXPERTS, dtype=I32)[None, :]).astype(I32)
    before = jnp.cumsum(onehot, axis=0) - onehot
    rank = jnp.sum(before * onehot, axis=1)
    counts = jnp.sum(onehot, axis=0)
    pad_counts = (counts + MOE_ROWS - 1) // MOE_ROWS * MOE_ROWS
    pad_ends = jnp.cumsum(pad_counts)
    pad_starts = pad_ends - pad_counts
    dest = (pad_starts[e_all] + rank).astype(I32)
    n_used = (pad_ends[-1] // MOE_ROWS).astype(I32)
    block_e = jnp.searchsorted(pad_ends, jnp.arange(nb, dtype=I32) * MOE_ROWS, side="right").astype(I32)
    owns = (counts > 0).astype(I32)
    run_id = jnp.cumsum(owns) - owns
    n_runs = jnp.sum(owns).astype(I32)
    last_e = jnp.max(jnp.where(counts > 0, jnp.arange(N_EXPERTS, dtype=I32), 0)).astype(I32)
    block_e = jnp.where(jnp.arange(nb, dtype=I32) < n_used, jnp.minimum(block_e, N_EXPERTS - 1), last_e)
    run_of_block = run_id[block_e].astype(I32)
    run_e = jnp.full((N_EXPERTS,), last_e, I32).at[jnp.where(counts > 0, run_id, N_EXPERTS)].set(
        jnp.arange(N_EXPERTS, dtype=I32), mode="drop")
    src_tok = jnp.zeros((nb * MOE_ROWS,), I32).at[dest].set(jnp.arange(a_n, dtype=I32) // EXPERT_TOPK)
    blk = jnp.arange(nb, dtype=I32)
    rows_in_block = jnp.where(
        blk < n_used, jnp.clip(counts[block_e] - (blk * MOE_ROWS - pad_starts[block_e]), 0, MOE_ROWS), 0).astype(I32)
    return dest, (run_of_block, run_e, n_runs.reshape(1), n_used.reshape(1)), src_tok, rows_in_block


def kernel(x_prompt, x_sample, cache_dsa_kv, cache_dsa_kidx, cache_nsa_cmp_kv, cache_nsa_slc_kv, state_nsa_win_kv, page_table, w_in, w_out, cmp_w1, cmp_w2, cmp_pe, ln1_g, ln1_b, w_router_group, b_router_group, w_router_expert, b_router_expert, w_gate, w_up, w_down, ln2_g, ln2_b):
    depth = w_in.shape[0]
    assert depth == 1 and x_sample.shape[1] == 1
    batch, seq, d_model = x_prompt.shape
    n_seq = x_sample.shape[0]
    n_pool, page = cache_dsa_kv.shape[1], cache_dsa_kv.shape[2]
    alpha = (2.0 * depth) ** 0.25
    m_p = batch * seq
    xp = x_prompt.reshape(m_p, d_model)
    xs = x_sample.reshape(n_seq, d_model)
    xp16, xs16 = xp.astype(BF16), xs.astype(BF16)

    w = w_in[0]
    t0, t1 = 4176, 9296
    rep_b = NSA_HEADS // NSA_KV_GROUPS
    w_head = w[:, :t0 + 48].astype(BF16)
    w_tail = w[:, t0:t1].astype(BF16)
    w_gates = jnp.pad(w[:, t1:].reshape(d_model, NSA_KV_GROUPS, 3 * rep_b),
                      ((0, 0), (0, 0), (0, LANES - 3 * rep_b))).reshape(d_model, NSA_KV_GROUPS * LANES).astype(BF16)
    c_dq, c_dk, c_dv, c_iq, c_ik = 0, 2048, 2560, 3072, 4096
    c_nq, c_ck, c_cv, c_sk, c_sv, c_wk, c_wv = 0, 2048, 2560, 3072, 3584, 4096, 4608

    def project(x16):
        qa = _matmul([x16], w_head, c_dq, Q_ROW)
        qi = _matmul([x16], w_head, c_iq, IDX_HEADS * IDX_DIM)
        hc = _matmul([x16], w_head, c_ik, LANES)
        qb = _matmul([x16], w_tail, c_nq, Q_ROW)
        gate = _matmul([x16], w_gates)
        kv = [_kv_proj(x16, w_head, c_dk, c_dv), _kv_proj(x16, w_tail, c_ck, c_cv),
              _kv_proj(x16, w_tail, c_sk, c_sv), _kv_proj(x16, w_tail, c_wk, c_wv)]
        return qa, qi, hc, qb, gate, kv

    qa_p, qi_p, hc_p, qb_p, gate_p, kv_p = project(xp16)
    qa_s, qi_s, hc_s, qb_s, gate_s, kv_s = project(xs16)

    w1r = cmp_w1[0].reshape(2, CMP_BLOCK // CMP_STRIDE, CMP_STRIDE, HEAD_DIM, HEAD_DIM)
    w1r = jnp.transpose(w1r, (0, 2, 3, 1, 4)).reshape(2, CMP_STRIDE * HEAD_DIM, 2 * HEAD_DIM)
    peb = _cmp_pe_bias(cmp_pe[0], cmp_w1[0])
    pages_per_seq = seq // page
    ident = jnp.arange(batch * pages_per_seq, dtype=I32).reshape(batch, pages_per_seq)
    ckv_p = _compress(kv_p[1][0], ident, page, w1r, cmp_w2[0], peb)
    ckv_s = _compress(cache_nsa_cmp_kv.reshape(-1, HEAD_DIM), page_table, page, w1r, cmp_w2[0], peb)

    oa_p = _dsa_prompt(qa_p, qi_p, hc_p, kv_p[0][1], batch, seq)
    ob_p = _nsa_prompt(qb_p, gate_p, ckv_p, kv_p[2][1], kv_p[3][1], batch, seq)

    keep_s = _dsa_sample_mask(jnp.swapaxes(cache_dsa_kidx[0], 1, 2), page_table,
                              qi_s.reshape(n_seq, IDX_HEADS, IDX_DIM),
                              hc_s[:, IDX_DIM:IDX_DIM + IDX_HEADS].reshape(n_seq, IDX_HEADS, 1), hc_s)
    oa_s = _dsa_sample(cache_dsa_kv.reshape(-1, HEAD_DIM), page_table, page,
                       qa_s.reshape(n_seq, DSA_HEADS, HEAD_DIM), keep_s, kv_s[0][1].astype(F32))
    ob_s, win_s = _nsa_sample(cache_nsa_slc_kv.reshape(-1, HEAD_DIM), page_table, page,
                              qb_s.reshape(n_seq, NSA_HEADS, HEAD_DIM),
                              gate_s.reshape(n_seq, NSA_KV_GROUPS, LANES)[:, :, :3 * rep_b].reshape(n_seq, NSA_HEADS, 3),
                              ckv_s,
                              state_nsa_win_kv.reshape(-1, HEAD_DIM), kv_s[2][1].astype(F32),
                              kv_s[3][1].astype(F32), kv_s[3][0])

    w_o = w_out[0].astype(BF16)
    wr = jnp.concatenate([w_router_group[0], w_router_expert[0],
                          jnp.zeros((d_model, LANES - N_GROUPS - N_EXPERTS), F32)], axis=1)
    br = jnp.concatenate([b_router_group[0], b_router_expert[0],
                          jnp.zeros((LANES - N_GROUPS - N_EXPERTS,), F32)]).reshape(1, LANES)
    g1, b1 = ln1_g[0].reshape(1, d_model), ln1_b[0].reshape(1, d_model)
    g2, b2 = ln2_g[0].reshape(1, d_model), ln2_b[0].reshape(1, d_model)
    x1, e_tok, gt_tok = _ln_router(
        xp, _matmul([oa_p, ob_p], w_o),
        xs, _matmul([oa_s.reshape(n_seq, Q_ROW), ob_s.reshape(n_seq, Q_ROW)], w_o), g1, b1, wr, br, alpha)

    e_all = e_tok[:, :EXPERT_TOPK].reshape(-1)
    a_n = e_all.shape[0]
    nb = -(-(a_n + N_EXPERTS * (MOE_ROWS - 1)) // MOE_ROWS)
    dest, plan, src_tok, rows_in_block = _route_plan(e_all, nb)
    xbuf = _dispatch(src_tok, rows_in_block, x1, nb)
    d_expert = w_gate.shape[3]
    hid = _expert_mm(plan, xbuf, [w_gate[0], w_up[0]], _pick(d_expert, (MOE_CHUNK, LANES)), _swiglu, BF16,
                     "moe_gate_up")
    ybuf = _expert_mm(plan, hid, [w_down[0]], _pick(d_model, (2048, 1024, 512, 256, LANES)), lambda y: y, F32,
                      "moe_down")
    y_p = _combine(dest, ybuf, x1, gt_tok, 0, m_p, g2, b2, alpha)
    y_s = _combine(dest, ybuf, x1, gt_tok, m_p, n_seq, g2, b2, alpha)

    def state(slab, lead):
        return slab.reshape((1,) + lead + (NSA_KV_GROUPS, 2, HEAD_DIM))

    n_win = min(WINDOW, seq)
    win_p = state(kv_p[3][0], (batch, seq))[:, :, seq - n_win:]
    return (y_p.reshape(batch, seq, d_model), y_s.reshape(n_seq, 1, d_model),
            state(kv_p[0][0], (batch, seq)), state(kv_s[0][0], (n_seq, 1)),
            hc_p[:, :IDX_DIM].reshape(1, batch, seq, IDX_DIM), hc_s[:, :IDX_DIM].reshape(1, n_seq, 1, IDX_DIM),
            state(kv_p[1][0], (batch, seq)), state(kv_s[1][0], (n_seq, 1)),
            state(kv_p[2][0], (batch, seq)), state(kv_s[2][0], (n_seq, 1)),
            win_p, state(win_s, (n_seq, state_nsa_win_kv.shape[2])))
```

```python
import functools

import jax
import jax.numpy as jnp
from jax import lax
from jax.experimental import pallas as pl
from jax.experimental.pallas import tpu as pltpu

F32 = jnp.float32
BF16 = jnp.bfloat16
I32 = jnp.int32

HEAD_DIM = 128
DSA_HEADS = 16
DSA_KV_HEADS = 4
IDX_HEADS = 16
IDX_DIM = 64
IDX_TOPK = 256
NSA_HEADS = 16
NSA_KV_GROUPS = 4
CMP_STRIDE = 16
CMP_BLOCK = 32
SLC_BLOCK = 64
SLC_TOPN = 16
WINDOW = 512
N_GROUPS = 8
EXPERTS_PER_GROUP = 8
N_EXPERTS = N_GROUPS * EXPERTS_PER_GROUP
EXPERT_TOPK = 2
LN_EPS = 1e-5
NEG_INF = -1e30
FORCE_SCORE = 1e9

KV_ROW = 2 * DSA_KV_HEADS * HEAD_DIM
KV_SLABS = 2 * DSA_KV_HEADS
KV_HALF = DSA_KV_HEADS * HEAD_DIM
Q_ROW = DSA_HEADS * HEAD_DIM
Q_TILE = 128
KEY_SPAN = 256
SEQ_GROUP = 16
LANES = 128
MOE_ROWS = 384
MOE_RING = 4
MOE_CHUNK = 256
VMEM_LIMIT = 56 * 1024 * 1024
SCALE = HEAD_DIM ** -0.5
LOG2E = 1.4426950408889634
MASKED_DIST = 1e32
INT_MIN = -(2 ** 31)


def _slopes(n):
    return [2.0 ** (-8.0 * i / n) for i in range(1, n + 1)]


def _cparams(sem):
    return pltpu.CompilerParams(dimension_semantics=sem, vmem_limit_bytes=VMEM_LIMIT)


def _dot(a, b):
    return jnp.dot(a, b, preferred_element_type=F32)


def _dot_nt(a, b):
    return lax.dot_general(a, b, (((1,), (1,)), ((), ())), preferred_element_type=F32)


def _dot_hi(a, b):
    return jnp.dot(a, b, preferred_element_type=F32, precision=lax.Precision.HIGHEST)


def _pick(n, cands):
    for c in cands:
        if n % c == 0:
            return c
    return n


def _mm_kernel(*refs, n_lhs):
    x_refs, w_refs, o_ref = refs[:n_lhs], refs[n_lhs:2 * n_lhs], refs[2 * n_lhs]
    acc = _dot(x_refs[0][...], w_refs[0][...])
    for x_ref, w_ref in zip(x_refs[1:], w_refs[1:]):
        acc = acc + _dot(x_ref[...], w_ref[...])
    o_ref[...] = acc.astype(o_ref.dtype)


def _matmul(xs, w, n0=0, n=None):
    m = xs[0].shape[0]
    n = w.shape[1] if n is None else n
    kds = [x.shape[1] for x in xs]
    assert len(set(kds)) == 1 and sum(kds) == w.shape[0]
    kd = kds[0]
    tm = _pick(m, (512, 256, 128))
    tn = _pick(n, (1024, 512, 256, 128))
    assert n0 % tn == 0
    j0 = n0 // tn
    x_specs = [pl.BlockSpec((tm, kd), lambda j, i: (i, 0)) for _ in xs]
    w_specs = [pl.BlockSpec((kd, tn), functools.partial(lambda j, i, a: (a, j + j0), a=a)) for a in range(len(xs))]
    return pl.pallas_call(
        functools.partial(_mm_kernel, n_lhs=len(xs)),
        grid=(n // tn, m // tm),
        in_specs=x_specs + w_specs,
        out_specs=pl.BlockSpec((tm, tn), lambda j, i: (i, j)),
        out_shape=jax.ShapeDtypeStruct((m, n), F32),
        compiler_params=_cparams(("parallel", "parallel")),
        name="matmul",
    )(*xs, *([w] * len(xs)))


def _kv_proj_kernel(x_ref, wk_ref, wv_ref, slab_ref, tile_ref):
    x = x_ref[...]
    rows = x.shape[0]
    for c, w_ref in enumerate((wk_ref, wv_ref)):
        res = _dot(x, w_ref[...])
        for g in range(DSA_KV_HEADS):
            piece = res[:, g * HEAD_DIM:(g + 1) * HEAD_DIM]
            slab_ref[pl.ds(g * 2 + c, rows, stride=KV_SLABS), :] = piece
            tile_ref[:, (g * 2 + c) * HEAD_DIM:(g * 2 + c + 1) * HEAD_DIM] = piece.astype(BF16)


def _kv_proj(x, w, k_col, v_col):
    m, kd = x.shape
    tm = _pick(m, (512, 256, 128))
    assert k_col % KV_HALF == 0 and v_col % KV_HALF == 0
    kb, vb = k_col // KV_HALF, v_col // KV_HALF
    return pl.pallas_call(
        _kv_proj_kernel,
        grid=(m // tm,),
        in_specs=[pl.BlockSpec((tm, kd), lambda i: (i, 0)),
                  pl.BlockSpec((kd, KV_HALF), lambda i: (0, kb)),
                  pl.BlockSpec((kd, KV_HALF), lambda i: (0, vb))],
        out_specs=[pl.BlockSpec((tm * KV_SLABS, HEAD_DIM), lambda i: (i, 0)),
                   pl.BlockSpec((tm, KV_ROW), lambda i: (i, 0))],
        out_shape=[jax.ShapeDtypeStruct((m * KV_SLABS, HEAD_DIM), F32), jax.ShapeDtypeStruct((m, KV_ROW), BF16)],
        compiler_params=_cparams(("parallel",)),
        name="kv_proj",
    )(x, w, w)


def _slab(ref, n_tok, g, c):
    return ref[pl.ds(g * 2 + c, n_tok, stride=KV_SLABS), :]


def _slab_all(pages, n_tok, g, c):
    return jnp.concatenate([_slab(p, n_tok, g, c) for p in pages], axis=0).astype(BF16)


def _page_specs(n_pages, rows, width, index):
    return [pl.BlockSpec((rows, width), functools.partial(lambda *a, p: (index(*a, p), 0), p=p))
            for p in range(n_pages)]


def _sort_key(x):
    b = pltpu.bitcast(x, I32)
    return jnp.where(b < 0, b ^ jnp.int32(0x7FFFFFFF), b)


def _topk_mask(sc, idx, k, idx_bits, keep_all=False):
    key = _sort_key(sc)
    search = jnp.logical_not(keep_all)

    def count(m):
        return jnp.sum(m.astype(I32), axis=1, keepdims=True)

    t0 = jnp.where(count(key >= 0) >= k, jnp.int32(0), jnp.int32(INT_MIN))

    def vstep(i, t):
        cand = t | (jnp.int32(1) << (30 - i))
        return jnp.where(count(key >= cand) >= k, cand, t)

    thr = lax.fori_loop(0, jnp.where(search, 31, 0), vstep, t0)
    gt = key > thr
    eq = key == thr
    need = k - count(gt)
    crowded = count(eq) > need
    n_steps = jnp.where(search & (jnp.max(crowded.astype(I32)) > 0), idx_bits, 0)

    def istep(i, c):
        cand = c | (jnp.int32(1) << (idx_bits - 1 - i))
        return jnp.where(count(eq & (idx < cand)) < need, cand, c)

    cut = lax.fori_loop(0, n_steps, istep, jnp.zeros_like(thr))
    return keep_all | gt | (eq & (jnp.logical_not(crowded) | (idx <= cut)))


def _stack_heads(ref, n_heads, scale):
    return jnp.concatenate([(ref[:, r * HEAD_DIM:(r + 1) * HEAD_DIM] * scale).astype(BF16) for r in range(n_heads)],
                           axis=0)


def _slope_stack(slopes):
    return jnp.concatenate([jnp.full((1, 1, 1), s, F32) for s in slopes], axis=0)


def _attend_stacked(q2, k, v, dm, slope2):
    n_heads = slope2.shape[0]
    rows, keys = dm.shape
    s = _dot_nt(q2, k).reshape(n_heads, rows, keys) - slope2 * dm[None]
    m = jnp.max(s, axis=2, keepdims=True)
    p = jnp.exp2(s - m)
    l = jnp.sum(p, axis=2, keepdims=True)
    o = _dot(p.reshape(n_heads * rows, keys).astype(BF16), v)
    return o * (1.0 / l).reshape(n_heads * rows, 1)


def _attend(q2, k_of, v, dm_of, slope2, keys, s_ref, pb_ref):
    n_heads = len(slope2)
    rows = q2.shape[0] // n_heads
    chunk = 2 * LANES if keys % (2 * LANES) == 0 else LANES
    bounds = [(a, a + chunk) for a in range(0, keys, chunk)]
    inv_l = []
    for r in range(n_heads):
        q_r = q2[r * rows:(r + 1) * rows]
        m_vec = jnp.full((rows, LANES), -jnp.inf, F32)
        for a, b in bounds:
            s = _dot_nt(q_r, k_of(a, b)) - slope2[r] * dm_of(a, b)
            s_ref[r, :, a:b] = s
            for i in range(chunk // LANES):
                m_vec = jnp.maximum(m_vec, s[:, i * LANES:(i + 1) * LANES])
        m = jnp.max(m_vec, axis=1, keepdims=True)
        l_vec = jnp.zeros((rows, LANES), F32)
        for a, b in bounds:
            p = jnp.exp2(s_ref[r, :, a:b] - m)
            for i in range(chunk // LANES):
                l_vec = l_vec + p[:, i * LANES:(i + 1) * LANES]
            pb_ref[r * rows:(r + 1) * rows, a:b] = p.astype(BF16)
        inv_l.append(1.0 / jnp.sum(l_vec, axis=1, keepdims=True))
    return _dot(pb_ref[:, 0:keys], v) * jnp.concatenate(inv_l, axis=0)


def _spans(seq):
    span = KEY_SPAN if seq % KEY_SPAN == 0 else seq
    return span, seq // span


def _group_slope(g, r, n_heads, n_groups):
    rep = n_heads // n_groups
    table = _slopes(n_heads)
    out = jnp.float32(table[r])
    for gg in range(1, n_groups):
        out = jnp.where(g == gg, jnp.float32(table[gg * rep + r]), out)
    return out


def _dsa_prompt_mask(c, qi_ref, cq_ref, ck_ref, dm_ref, kw, k_keep):
    t = c * Q_TILE + lax.broadcasted_iota(I32, (Q_TILE, 1), 0)
    kpos = lax.broadcasted_iota(I32, (Q_TILE, kw), 1)
    causal = kpos <= t
    ki = ck_ref[0:kw, 0:IDX_DIM].astype(BF16)
    wi = cq_ref[:, IDX_DIM:IDX_DIM + IDX_HEADS] * (IDX_HEADS ** -0.5)
    sc = jnp.zeros((Q_TILE, kw), F32)
    for h in range(IDX_HEADS):
        a = _dot_nt(qi_ref[:, h * IDX_DIM:(h + 1) * IDX_DIM].astype(BF16), ki)
        sc = sc + jnp.maximum(a, 0.0) * wi[:, h:h + 1]
    sc = jnp.where(causal, sc, NEG_INF)
    keep = _topk_mask(sc, kpos, k_keep, max(1, (kw - 1).bit_length()), keep_all=(c + 1) * Q_TILE <= k_keep)
    dm_ref[:, 0:kw] = jnp.where(keep & causal, (t - kpos).astype(F32), MASKED_DIST)


def _dsa_prompt_kernel(q_ref, qi_ref, cq_ref, ck_ref, kv_ref, o_ref, dm_ref, s_ref, pb_ref, *, seq, k_keep):
    c, g = pl.program_id(1), pl.program_id(2)
    rep = DSA_HEADS // DSA_KV_HEADS
    span, n_var = _spans(seq)
    for v in range(n_var):
        @pl.when((c * Q_TILE) // span == v)
        def _(kw=(v + 1) * span):
            @pl.when(g == 0)
            def _():
                _dsa_prompt_mask(c, qi_ref, cq_ref, ck_ref, dm_ref, kw, k_keep)

            slope2 = [_group_slope(g, r, DSA_HEADS, DSA_KV_HEADS) * LOG2E for r in range(rep)]
            o = _attend(_stack_heads(q_ref, rep, SCALE * LOG2E), lambda a, b: kv_ref[a:b, 0:HEAD_DIM],
                        kv_ref[0:kw, HEAD_DIM:2 * HEAD_DIM], lambda a, b: dm_ref[:, a:b], slope2, kw, s_ref, pb_ref)
            for r in range(rep):
                o_ref[:, r * HEAD_DIM:(r + 1) * HEAD_DIM] = o[r * Q_TILE:(r + 1) * Q_TILE].astype(BF16)


def _dsa_prompt(q, qi, h_c, kv, batch, seq):
    nc = seq // Q_TILE
    k_keep = min(IDX_TOPK, seq // 4)
    rep = DSA_HEADS // DSA_KV_HEADS
    tile = lambda b, c, g: (b * nc + c, 0)
    tile_g = lambda b, c, g: (b * nc + c, g)
    return pl.pallas_call(
        functools.partial(_dsa_prompt_kernel, seq=seq, k_keep=k_keep),
        grid=(batch, nc, DSA_KV_HEADS),
        in_specs=[pl.BlockSpec((Q_TILE, rep * HEAD_DIM), tile_g),
                  pl.BlockSpec((Q_TILE, IDX_HEADS * IDX_DIM), tile),
                  pl.BlockSpec((Q_TILE, LANES), tile),
                  pl.BlockSpec((seq, LANES), lambda b, c, g: (b, 0)),
                  pl.BlockSpec((seq, 2 * HEAD_DIM), lambda b, c, g: (b, g))],
        out_specs=pl.BlockSpec((Q_TILE, rep * HEAD_DIM), tile_g),
        out_shape=jax.ShapeDtypeStruct((batch * seq, Q_ROW), BF16),
        scratch_shapes=[pltpu.VMEM((Q_TILE, seq), F32), pltpu.VMEM((rep, Q_TILE, seq), F32),
                        pltpu.VMEM((rep * Q_TILE, seq), BF16)],
        compiler_params=_cparams(("parallel", "arbitrary", "arbitrary")),
        name="dsa_prompt",
    )(q, qi, h_c, h_c, kv)


def _peb_kernel(pe_ref, w1_ref, o_ref):
    for c in range(2):
        o_ref[c] = _dot_hi(pe_ref[c], w1_ref[c])


def _cmp_pe_bias(cmp_pe, cmp_w1):
    return pl.pallas_call(
        _peb_kernel,
        out_shape=jax.ShapeDtypeStruct((2, 1, HEAD_DIM), F32),
        name="cmp_pe_bias",
    )(cmp_pe.reshape(2, 1, CMP_BLOCK * HEAD_DIM), cmp_w1.reshape(2, CMP_BLOCK * HEAD_DIM, HEAD_DIM))


def _compress_kernel(pt_ref, *refs, n_pages, page):
    pages = refs[:n_pages]
    w1_ref, w2_ref, peb_ref, o_ref = refs[n_pages:]
    sub_per_page = page // CMP_STRIDE
    n_sub = n_pages * sub_per_page
    prow = lax.broadcasted_iota(I32, (page, page), 0)
    pcol = lax.broadcasted_iota(I32, (page, page), 1)
    regroup = (pcol == (prow % sub_per_page) * CMP_STRIDE + prow // sub_per_page).astype(BF16)
    by_pos = []
    for p in pages:
        x = jnp.concatenate([_slab(p, page, gc // 2, gc % 2) for gc in range(KV_SLABS)], axis=1).astype(BF16)
        by_pos.append(_dot(regroup, x))
    for c in range(2):
        lhs = jnp.concatenate([
            jnp.concatenate([
                jnp.concatenate([bp[j * sub_per_page:(j + 1) * sub_per_page,
                                    (g * 2 + c) * HEAD_DIM:(g * 2 + c + 1) * HEAD_DIM] for bp in by_pos],
                                axis=0).astype(BF16)
                for j in range(CMP_STRIDE)], axis=1)
            for g in range(NSA_KV_GROUPS)], axis=0)
        part = _dot(lhs, w1_ref[c].astype(BF16))
        w2 = w2_ref[c].astype(BF16)
        for g in range(NSA_KV_GROUPS):
            col = g * 256 + c * 128
            pg = part[g * n_sub:(g + 1) * n_sub]
            nxt = pltpu.roll(pg[:, HEAD_DIM:], n_sub - 1, 0)
            hid = pg[:, :HEAD_DIM] + nxt + peb_ref[c]
            o_ref[0, :, col:col + HEAD_DIM] = _dot(jax.nn.gelu(hid).astype(BF16), w2)


def _compress(pool, page_table, page, w1r, w2, peb):
    n_seq, n_pages = page_table.shape
    n_sub = n_pages * page // CMP_STRIDE
    page_specs = _page_specs(n_pages, page * KV_SLABS, HEAD_DIM, lambda n, pt, p: pt[n, p])
    return pl.pallas_call(
        functools.partial(_compress_kernel, n_pages=n_pages, page=page),
        grid_spec=pltpu.PrefetchScalarGridSpec(
            num_scalar_prefetch=1,
            grid=(n_seq,),
            in_specs=page_specs + [
                pl.BlockSpec((2, CMP_STRIDE * HEAD_DIM, 2 * HEAD_DIM), lambda n, pt: (0, 0, 0)),
                pl.BlockSpec((2, HEAD_DIM, HEAD_DIM), lambda n, pt: (0, 0, 0)),
                pl.BlockSpec((2, 1, HEAD_DIM), lambda n, pt: (0, 0, 0))],
            out_specs=pl.BlockSpec((1, n_sub, KV_ROW), lambda n, pt: (n, 0, 0))),
        out_shape=jax.ShapeDtypeStruct((n_seq, n_sub, KV_ROW), F32),
        compiler_params=_cparams(("arbitrary",)),
        name="nsa_compress",
    )(page_table, *([pool] * n_pages), w1r, w2, peb)


def _cover(n_cmp_pad, n_slc_pad):
    cs = lax.broadcasted_iota(I32, (n_cmp_pad, n_slc_pad), 0) * CMP_STRIDE
    bs = lax.broadcasted_iota(I32, (n_cmp_pad, n_slc_pad), 1) * SLC_BLOCK
    return ((cs < bs + SLC_BLOCK) & (cs + CMP_BLOCK > bs)).astype(F32)


def _select_blocks(score, t, n_slc, n_keep):
    rows, width = score.shape
    j = lax.broadcasted_iota(I32, (rows, width), 1)
    cur = t // SLC_BLOCK
    forced = (j == 0) | (j == cur) | (j == cur - 1)
    admissible = j * SLC_BLOCK <= t
    score = jnp.where(forced, FORCE_SCORE, jnp.where(admissible, score, NEG_INF))
    rank = jnp.zeros((rows, width), I32)
    for k in range(n_slc):
        sk = score[:, k:k + 1]
        ahead = (sk > score) | ((sk == score) & (j > k))
        rank = rank + ahead.astype(I32)
    return ((rank < n_keep) & (j < n_slc)).astype(F32)


def _select_blocks_t(score_t, t_row, n_keep):
    n_slc, width = score_t.shape
    j = lax.broadcasted_iota(I32, (n_slc, width), 0)
    cur = t_row // SLC_BLOCK
    forced = (j == 0) | (j == cur) | (j == cur - 1)
    admissible = j * SLC_BLOCK <= t_row
    score_t = jnp.where(forced, FORCE_SCORE, jnp.where(admissible, score_t, NEG_INF))
    rank = jnp.zeros((n_slc, width), I32)
    for k in range(n_slc):
        sk = score_t[k:k + 1, :]
        ahead = (sk > score_t) | ((sk == score_t) & (j > k))
        rank = rank + ahead.astype(I32)
    return (rank < n_keep).astype(F32)


def _nsa_prompt_body(c, g, q_ref, gate_ref, ckv_ref, slc_ref, win_ref, o_ref, kw, seq, n_cmp, win_keys):
    t = c * Q_TILE + lax.broadcasted_iota(I32, (Q_TILE, 1), 0)
    t_row = c * Q_TILE + lax.broadcasted_iota(I32, (1, Q_TILE), 1)
    rep = NSA_HEADS // NSA_KV_GROUPS
    slope2 = _slope_stack([_group_slope(g, r, NSA_HEADS, NSA_KV_GROUPS) * LOG2E for r in range(rep)])
    n_cmp_pad = ckv_ref.shape[1]
    n_slc = seq // SLC_BLOCK
    n_keep = min(SLC_TOPN, n_slc)

    kc = lax.broadcasted_iota(I32, (Q_TILE, n_cmp_pad), 1)
    dist_c_i = t - (kc * CMP_STRIDE + (CMP_BLOCK - 1))
    valid_c = (dist_c_i >= 0) & (kc < n_cmp)
    dist_c = dist_c_i.astype(F32)
    cover_bs = lax.broadcasted_iota(I32, (n_slc, n_cmp_pad), 0) * SLC_BLOCK
    cover_cs = lax.broadcasted_iota(I32, (n_slc, n_cmp_pad), 1) * CMP_STRIDE
    cover_t = ((cover_cs < cover_bs + SLC_BLOCK) & (cover_cs + CMP_BLOCK > cover_bs)).astype(F32)
    slc_rows = -(-n_slc // LANES) * LANES
    expand = (lax.broadcasted_iota(I32, (slc_rows, kw), 1) // SLC_BLOCK
              == lax.broadcasted_iota(I32, (slc_rows, kw), 0)).astype(BF16)

    kpos = lax.broadcasted_iota(I32, (Q_TILE, kw), 1)
    causal = kpos <= t
    dist_s = (t - kpos).astype(F32)

    w0 = pl.multiple_of(jnp.maximum(c * Q_TILE + Q_TILE - win_keys, 0), Q_TILE)
    wpos = w0 + lax.broadcasted_iota(I32, (Q_TILE, win_keys), 1)
    dist_w_i = t - wpos
    dm_w = jnp.where((dist_w_i >= 0) & (dist_w_i <= WINDOW), dist_w_i.astype(F32), MASKED_DIST)

    gates = jax.nn.sigmoid(gate_ref[:, 0:3 * rep])

    ck = ckv_ref[0, :, 0:HEAD_DIM].astype(BF16)
    cv = ckv_ref[0, :, HEAD_DIM:2 * HEAD_DIM].astype(BF16)
    q2 = _stack_heads(q_ref, rep, SCALE * LOG2E)
    s = jnp.where(valid_c[None], _dot_nt(q2, ck).reshape(rep, Q_TILE, n_cmp_pad) - slope2 * dist_c[None], NEG_INF)
    m = jnp.max(s, axis=2, keepdims=True)
    p = jnp.where(valid_c[None], jnp.exp2(s - m), 0.0)
    l = jnp.sum(p, axis=2, keepdims=True)
    p = p * (1.0 / jnp.where(l > 0.0, l, 1.0))
    imp = jnp.sum(p, axis=0)
    o_c = _dot(p.reshape(rep * Q_TILE, n_cmp_pad).astype(BF16), cv)
    sel_t = _select_blocks_t(_dot_hi(cover_t, imp.T), t_row, n_keep)
    sel_t = jnp.concatenate([sel_t, jnp.zeros((slc_rows - n_slc, Q_TILE), F32)], axis=0)
    picked = _dot(sel_t.T.astype(BF16), expand) > 0.5
    dm_s = jnp.where(picked & causal, dist_s, MASKED_DIST)
    o_s = _attend_stacked(q2, slc_ref[0:kw, 0:HEAD_DIM], slc_ref[0:kw, HEAD_DIM:2 * HEAD_DIM], dm_s, slope2)
    o_w = _attend_stacked(q2, win_ref[pl.ds(w0, win_keys), 0:HEAD_DIM],
                          win_ref[pl.ds(w0, win_keys), HEAD_DIM:2 * HEAD_DIM], dm_w, slope2)
    gate_col = [jnp.concatenate([gates[:, 3 * r + k:3 * r + k + 1] for r in range(rep)], axis=0) for k in range(3)]
    o = gate_col[0] * o_c + gate_col[1] * o_s + gate_col[2] * o_w
    for r in range(rep):
        o_ref[:, r * HEAD_DIM:(r + 1) * HEAD_DIM] = o[r * Q_TILE:(r + 1) * Q_TILE].astype(BF16)


def _nsa_prompt_kernel(q_ref, gate_ref, ckv_ref, slc_ref, win_ref, o_ref, *, seq, n_cmp, win_keys):
    c, g = pl.program_id(1), pl.program_id(2)
    span, n_var = _spans(seq)
    for v in range(n_var):
        @pl.when((c * Q_TILE) // span == v)
        def _(kw=(v + 1) * span):
            _nsa_prompt_body(c, g, q_ref, gate_ref, ckv_ref, slc_ref, win_ref, o_ref, kw, seq, n_cmp, win_keys)


def _nsa_prompt(q, gate, ckv, slc, win, batch, seq):
    nc = seq // Q_TILE
    n_cmp_pad = ckv.shape[1]
    n_cmp = seq // CMP_STRIDE - CMP_BLOCK // CMP_STRIDE + 1
    win_keys = min(seq, WINDOW + Q_TILE)
    rep = NSA_HEADS // NSA_KV_GROUPS
    tile_g = lambda b, c, g: (b * nc + c, g)
    return pl.pallas_call(
        functools.partial(_nsa_prompt_kernel, seq=seq, n_cmp=n_cmp, win_keys=win_keys),
        grid=(batch, nc, NSA_KV_GROUPS),
        in_specs=[pl.BlockSpec((Q_TILE, rep * HEAD_DIM), tile_g),
                  pl.BlockSpec((Q_TILE, LANES), tile_g),
                  pl.BlockSpec((1, n_cmp_pad, 2 * HEAD_DIM), lambda b, c, g: (b, 0, g)),
                  pl.BlockSpec((seq, 2 * HEAD_DIM), lambda b, c, g: (b, g)),
                  pl.BlockSpec((seq, 2 * HEAD_DIM), lambda b, c, g: (b, g))],
        out_specs=pl.BlockSpec((Q_TILE, rep * HEAD_DIM), tile_g),
        out_shape=jax.ShapeDtypeStruct((batch * seq, Q_ROW), BF16),
        compiler_params=_cparams(("parallel", "arbitrary", "arbitrary")),
        name="nsa_prompt",
    )(q, gate, ckv, slc, win)


def _group_rows(per_group, rep):
    row = lax.broadcasted_iota(I32, per_group[0].shape, 0) // rep
    out = per_group[0]
    for g in range(1, len(per_group)):
        out = jnp.where(row == g, per_group[g], out)
    return out


def _dsa_sample_mask_kernel(pt_ref, *refs, n_pages, page, k_keep):
    n_ki = SEQ_GROUP * n_pages
    ki_pages = refs[:n_ki]
    qi_ref, wi_ref, cnew_ref, keep_ref = refs[n_ki:]
    past = n_pages * page
    width = past + LANES
    rows = []
    for s in range(SEQ_GROUP):
        qi = qi_ref[s].astype(BF16)
        wi = wi_ref[s] * (IDX_HEADS ** -0.5)
        knew = cnew_ref[pl.ds(s, 1), 0:IDX_DIM]
        a_past = jnp.concatenate([_dot(qi, ki_pages[s * n_pages + p][0].astype(BF16))
                                  for p in range(n_pages)], axis=1)
        a_new = _dot_nt(qi, jnp.broadcast_to(knew, (8, IDX_DIM)).astype(BF16))[:, 0:1]
        sc_past = jnp.sum(jnp.maximum(a_past, 0.0) * wi, axis=0, keepdims=True)
        sc_new = jnp.sum(jnp.maximum(a_new, 0.0) * wi, axis=0, keepdims=True)
        rows.append(jnp.concatenate([sc_past, jnp.broadcast_to(sc_new, (1, LANES))], axis=1))
    col = lax.broadcasted_iota(I32, (SEQ_GROUP, width), 1)
    sc = jnp.where(col <= past, jnp.concatenate(rows, axis=0), NEG_INF)
    keep = _topk_mask(sc, col, k_keep, max(1, (width - 1).bit_length())) & (col <= past)
    keep_ref[...] = keep.astype(F32)


def _dsa_sample_mask(kidx_pool_t, page_table, qi, wi, h_c):
    n_seq, n_pages = page_table.shape
    assert n_seq % SEQ_GROUP == 0
    page = kidx_pool_t.shape[2]
    total = n_pages * page + 1
    k_keep = min(IDX_TOPK, total // 4)
    width = n_pages * page + LANES
    ki_specs = [pl.BlockSpec((1, IDX_DIM, page),
                             functools.partial(lambda i, pt, s, p: (pt[i * SEQ_GROUP + s, p], 0, 0), s=s, p=p))
                for s in range(SEQ_GROUP) for p in range(n_pages)]
    return pl.pallas_call(
        functools.partial(_dsa_sample_mask_kernel, n_pages=n_pages, page=page, k_keep=k_keep),
        grid_spec=pltpu.PrefetchScalarGridSpec(
            num_scalar_prefetch=1,
            grid=(n_seq // SEQ_GROUP,),
            in_specs=ki_specs + [
                pl.BlockSpec((SEQ_GROUP, IDX_HEADS, IDX_DIM), lambda i, pt: (i, 0, 0)),
                pl.BlockSpec((SEQ_GROUP, IDX_HEADS, 1), lambda i, pt: (i, 0, 0)),
                pl.BlockSpec((SEQ_GROUP, LANES), lambda i, pt: (i, 0))],
            out_specs=pl.BlockSpec((SEQ_GROUP, width), lambda i, pt: (i, 0))),
        out_shape=jax.ShapeDtypeStruct((n_seq, width), F32),
        compiler_params=_cparams(("arbitrary",)),
        name="dsa_sample_mask",
    )(page_table, *([kidx_pool_t] * (SEQ_GROUP * n_pages)), qi, wi, h_c)


def _dsa_sample_kernel(pt_ref, *refs, n_pages, page):
    kv_pages = refs[:n_pages]
    q_ref, keep_ref, kvnew_ref, o_ref = refs[n_pages:]
    r = pl.program_id(0) % SEQ_GROUP
    past = n_pages * page
    rep = DSA_HEADS // DSA_KV_HEADS
    slopes = _slopes(DSA_HEADS)

    keep = keep_ref[pl.ds(r, 1), :] > 0.5
    keep_past = jnp.broadcast_to(keep[:, 0:past], (DSA_HEADS, past))
    keep_new = jnp.broadcast_to(keep[:, past:past + 1], (DSA_HEADS, 1))

    q = (q_ref[0] * SCALE).astype(BF16)
    q32 = q.astype(F32)
    kvnew = kvnew_ref[pl.ds(r, 1), :]
    slope_col = jnp.concatenate([jnp.full((1, 1), s, F32) for s in slopes], axis=0)
    dist = (past - lax.broadcasted_iota(I32, (DSA_HEADS, past), 1)).astype(F32)
    s_g, s_new_g = [], []
    for g in range(DSA_KV_HEADS):
        s_g.append(_dot_nt(q, _slab_all(kv_pages, page, g, 0)))
        s_new_g.append(jnp.sum(q32 * kvnew[:, g * 256:g * 256 + 128], axis=1, keepdims=True))
    s = jnp.where(keep_past, _group_rows(s_g, rep) - slope_col * dist, NEG_INF)
    s_new = jnp.where(keep_new, _group_rows(s_new_g, rep), NEG_INF)
    m = jnp.maximum(jnp.max(s, axis=1, keepdims=True), s_new)
    p = jnp.where(keep_past, jnp.exp(s - m), 0.0)
    p_new = jnp.where(keep_new, jnp.exp(s_new - m), 0.0)
    l = jnp.sum(p, axis=1, keepdims=True) + p_new
    inv = 1.0 / jnp.where(l > 0.0, l, 1.0)
    pb = p.astype(BF16)
    p_new = p_new.astype(BF16).astype(F32)
    o_g = []
    for g in range(DSA_KV_HEADS):
        o_g.append(p_new * kvnew[:, g * 256 + 128:g * 256 + 256] + _dot(pb, _slab_all(kv_pages, page, g, 1)))
    o_ref[0] = (_group_rows(o_g, rep) * inv).astype(BF16)


def _dsa_sample(kv_pool, page_table, page, q, keep, kv_new):
    n_seq, n_pages = page_table.shape
    width = keep.shape[1]
    kv_specs = _page_specs(n_pages, page * KV_SLABS, HEAD_DIM, lambda n, pt, p: pt[n, p])
    return pl.pallas_call(
        functools.partial(_dsa_sample_kernel, n_pages=n_pages, page=page),
        grid_spec=pltpu.PrefetchScalarGridSpec(
            num_scalar_prefetch=1,
            grid=(n_seq,),
            in_specs=kv_specs + [
                pl.BlockSpec((1, DSA_HEADS, HEAD_DIM), lambda n, pt: (n, 0, 0)),
                pl.BlockSpec((SEQ_GROUP, width), lambda n, pt: (n // SEQ_GROUP, 0)),
                pl.BlockSpec((SEQ_GROUP, KV_ROW), lambda n, pt: (n // SEQ_GROUP, 0))],
            out_specs=pl.BlockSpec((1, DSA_HEADS, HEAD_DIM), lambda n, pt: (n, 0, 0))),
        out_shape=jax.ShapeDtypeStruct((n_seq, DSA_HEADS, HEAD_DIM), BF16),
        compiler_params=_cparams(("arbitrary",)),
        name="dsa_sample",
    )(page_table, *([kv_pool] * n_pages), q, keep, kv_new)


def _nsa_sample_kernel(pt_ref, *refs, n_pages, page, n_cmp):
    slc_pages = refs[:n_pages]
    q_ref, gate_ref, ckv_ref, win_ref, slcnew_ref, winnew_ref, winslab_ref, o_ref, wout_ref = refs[n_pages:]
    n = pl.program_id(0)
    past = n_pages * page
    heads = NSA_HEADS
    rep = NSA_HEADS // NSA_KV_GROUPS
    slopes = _slopes(NSA_HEADS)
    slope_col = jnp.concatenate([jnp.full((1, 1), s, F32) for s in slopes], axis=0)
    q = (q_ref[0] * SCALE).astype(BF16)
    q32 = q.astype(F32)
    gates = jax.nn.sigmoid(gate_ref[0])
    n_cmp_pad = ckv_ref.shape[1]
    total = past + 1
    n_slc = -(-total // SLC_BLOCK)
    n_keep = min(SLC_TOPN, n_slc)
    slc_pad = -(-n_slc // LANES) * LANES

    kc = lax.broadcasted_iota(I32, (heads, n_cmp_pad), 1)
    dist_c_i = past - (kc * CMP_STRIDE + (CMP_BLOCK - 1))
    valid_c = (dist_c_i >= 0) & (kc < n_cmp)
    s_g = [_dot_nt(q, ckv_ref[0, :, g * 256:g * 256 + 128].astype(BF16)) for g in range(NSA_KV_GROUPS)]
    s = jnp.where(valid_c, _group_rows(s_g, rep) - slope_col * dist_c_i.astype(F32), NEG_INF)
    m = jnp.max(s, axis=1, keepdims=True)
    p = jnp.where(valid_c, jnp.exp(s - m), 0.0)
    l = jnp.sum(p, axis=1, keepdims=True)
    p = p * (1.0 / jnp.where(l > 0.0, l, 1.0))
    pb = p.astype(BF16)
    o_c = _group_rows([_dot(pb, ckv_ref[0, :, g * 256 + 128:g * 256 + 256].astype(BF16))
                       for g in range(NSA_KV_GROUPS)], rep)
    same_group = (lax.broadcasted_iota(I32, (heads, heads), 0) // rep
                  == lax.broadcasted_iota(I32, (heads, heads), 1) // rep).astype(F32)
    imp = _dot_hi(same_group, p)
    score = _dot_hi(imp, _cover(n_cmp_pad, slc_pad))
    sel = _select_blocks(score, jnp.full((heads, 1), past, I32), n_slc, n_keep)
    expand = (lax.broadcasted_iota(I32, (slc_pad, past), 1) // SLC_BLOCK
              == lax.broadcasted_iota(I32, (slc_pad, past), 0)).astype(BF16)
    valid_s = _dot(sel.astype(BF16), expand) > 0.5
    new_blk = past // SLC_BLOCK
    valid_s_new = sel[:, new_blk:new_blk + 1] > 0.5

    def attend_with_new(s_past, valid_past, s_new, valid_new, dist_past, v_of, v_new_of):
        s_p = jnp.where(valid_past, s_past - slope_col * dist_past, NEG_INF)
        s_n = jnp.where(valid_new, s_new, NEG_INF)
        mm = jnp.maximum(jnp.max(s_p, axis=1, keepdims=True), s_n)
        pp = jnp.where(valid_past, jnp.exp(s_p - mm), 0.0)
        pn = jnp.where(valid_new, jnp.exp(s_n - mm), 0.0)
        ll = jnp.sum(pp, axis=1, keepdims=True) + pn
        inv = 1.0 / jnp.where(ll > 0.0, ll, 1.0)
        ppb = pp.astype(BF16)
        pn = pn.astype(BF16).astype(F32)
        outs = [v_of(ppb, g) + pn * v_new_of(g) for g in range(NSA_KV_GROUPS)]
        return _group_rows(outs, rep) * inv

    slcnew = slcnew_ref[pl.ds(n, 1), :]
    s_g, s_new_g = [], []
    for g in range(NSA_KV_GROUPS):
        s_g.append(_dot_nt(q, _slab_all(slc_pages, page, g, 0)))
        s_new_g.append(jnp.sum(q32 * slcnew[:, g * 256:g * 256 + 128], axis=1, keepdims=True))
    dist_s = (past - lax.broadcasted_iota(I32, (heads, past), 1)).astype(F32)

    def slc_v(ppb, g):
        return _dot(ppb, _slab_all(slc_pages, page, g, 1))

    o_s = attend_with_new(_group_rows(s_g, rep), valid_s, _group_rows(s_new_g, rep), valid_s_new, dist_s, slc_v,
                          lambda g: slcnew[:, g * 256 + 128:g * 256 + 256])

    n_buf = win_ref.shape[0] // KV_SLABS
    winnew = winnew_ref[pl.ds(n, 1), :]
    dist_w_i = n_buf - lax.broadcasted_iota(I32, (heads, n_buf), 1)
    valid_w = dist_w_i <= WINDOW
    s_g = [_dot_nt(q, _slab(win_ref, n_buf, g, 0).astype(BF16)) for g in range(NSA_KV_GROUPS)]
    s_new_g = [jnp.sum(q32 * winnew[:, g * 256:g * 256 + 128], axis=1, keepdims=True)
               for g in range(NSA_KV_GROUPS)]
    o_w = attend_with_new(_group_rows(s_g, rep), valid_w, _group_rows(s_new_g, rep),
                          jnp.full((heads, 1), True), dist_w_i.astype(F32),
                          lambda ppb, g: _dot(ppb, _slab(win_ref, n_buf, g, 1).astype(BF16)),
                          lambda g: winnew[:, g * 256 + 128:g * 256 + 256])

    o_ref[0] = (gates[:, 0:1] * o_c + gates[:, 1:2] * o_s + gates[:, 2:3] * o_w).astype(BF16)

    keep_rows = (n_buf - 1) * KV_SLABS
    wout_ref[0:keep_rows, :] = win_ref[KV_SLABS:n_buf * KV_SLABS, :]
    wout_ref[keep_rows:keep_rows + KV_SLABS, :] = winslab_ref[pl.ds(pl.multiple_of(n * KV_SLABS, KV_SLABS), KV_SLABS), :]


def _nsa_sample(slc_pool, page_table, page, q, gate, ckv, win_state, slc_new, win_new, win_new_slab):
    n_seq, n_pages = page_table.shape
    n_cmp_pad = ckv.shape[1]
    n_cmp = n_pages * page // CMP_STRIDE - CMP_BLOCK // CMP_STRIDE + 1
    n_buf = win_state.shape[0] // (n_seq * KV_SLABS)
    slc_specs = _page_specs(n_pages, page * KV_SLABS, HEAD_DIM, lambda n, pt, p: pt[n, p])
    return pl.pallas_call(
        functools.partial(_nsa_sample_kernel, n_pages=n_pages, page=page, n_cmp=n_cmp),
        grid_spec=pltpu.PrefetchScalarGridSpec(
            num_scalar_prefetch=1,
            grid=(n_seq,),
            in_specs=slc_specs + [
                pl.BlockSpec((1, NSA_HEADS, HEAD_DIM), lambda n, pt: (n, 0, 0)),
                pl.BlockSpec((1, NSA_HEADS, 3), lambda n, pt: (n, 0, 0)),
                pl.BlockSpec((1, n_cmp_pad, KV_ROW), lambda n, pt: (n, 0, 0)),
                pl.BlockSpec((n_buf * KV_SLABS, HEAD_DIM), lambda n, pt: (n, 0)),
                pl.BlockSpec((n_seq, KV_ROW), lambda n, pt: (0, 0)),
                pl.BlockSpec((n_seq, KV_ROW), lambda n, pt: (0, 0)),
                pl.BlockSpec((n_seq * KV_SLABS, HEAD_DIM), lambda n, pt: (0, 0))],
            out_specs=[pl.BlockSpec((1, NSA_HEADS, HEAD_DIM), lambda n, pt: (n, 0, 0)),
                       pl.BlockSpec((n_buf * KV_SLABS, HEAD_DIM), lambda n, pt: (n, 0))]),
        out_shape=[jax.ShapeDtypeStruct((n_seq, NSA_HEADS, HEAD_DIM), BF16),
                   jax.ShapeDtypeStruct(win_state.shape, F32)],
        compiler_params=_cparams(("arbitrary",)),
        name="nsa_sample",
    )(page_table, *([slc_pool] * n_pages), q, gate, ckv, win_state, slc_new, win_new, win_new_slab)


def _ln(v, g, b):
    mu = jnp.mean(v, axis=1, keepdims=True)
    d = v - mu
    var = jnp.mean(d * d, axis=1, keepdims=True)
    return d * lax.rsqrt(var + LN_EPS) * g + b


def _ln_router_kernel(xa_ref, mixa_ref, xb_ref, mixb_ref, g_ref, b_ref, wr_ref, br_ref, x1_ref, e_ref, gate_ref, *,
                      alpha, blocks_a):
    from_a = pl.program_id(0) < blocks_a
    @pl.when(from_a)
    def _():
        x1_ref[...] = _ln(alpha * xa_ref[...] + mixa_ref[...], g_ref[...], b_ref[...])

    @pl.when(jnp.logical_not(from_a))
    def _():
        x1_ref[...] = _ln(alpha * xb_ref[...] + mixb_ref[...], g_ref[...], b_ref[...])

    logits = _dot_hi(x1_ref[...], wr_ref[...]) + br_ref[...]
    rows = logits.shape[0]
    lane = lax.broadcasted_iota(I32, (rows, LANES), 1)
    big = jnp.int32(LANES)
    is_grp = lane < N_GROUPS
    lg = jnp.where(is_grp, logits, -jnp.inf)
    mg = jnp.max(lg, axis=1, keepdims=True)
    grp = jnp.min(jnp.where(lg == mg, lane, big), axis=1, keepdims=True)
    p_grp = 1.0 / jnp.sum(jnp.where(is_grp, jnp.exp(lg - mg), 0.0), axis=1, keepdims=True)
    ex = lane - N_GROUPS
    in_grp = (ex >= 0) & (ex < N_EXPERTS) & (ex // EXPERTS_PER_GROUP == grp)
    le = jnp.where(in_grp, logits, -jnp.inf)
    m1 = jnp.max(le, axis=1, keepdims=True)
    e1 = jnp.min(jnp.where(le == m1, ex, big), axis=1, keepdims=True)
    le2 = jnp.where(ex == e1, -jnp.inf, le)
    m2 = jnp.max(le2, axis=1, keepdims=True)
    e2 = jnp.min(jnp.where(le2 == m2, ex, big), axis=1, keepdims=True)
    z = jnp.sum(jnp.where(in_grp, jnp.exp(le - m1), 0.0), axis=1, keepdims=True)
    p1 = 1.0 / z
    p2 = jnp.exp(m2 - m1) / z
    g1 = p_grp * p1 / (p1 + p2)
    g2 = p_grp * p2 / (p1 + p2)
    e_ref[...] = jnp.where(lane == 0, e1, jnp.where(lane == 1, e2, 0))
    gate_ref[...] = jnp.where(lane == 0, g1, jnp.where(lane == 1, g2, 0.0))


def _ln_router(xa, mixa, xb, mixb, g, b, wr, br, alpha):
    (ma, d), mb = xa.shape, xb.shape[0]
    tm = _pick(mb, (128, 64, 32, 16, 8))
    assert ma % tm == 0 and mb % tm == 0
    blocks_a, blocks_b = ma // tm, mb // tm
    seg_a = lambda i: (jnp.minimum(i, blocks_a - 1), 0)
    seg_b = lambda i: (jnp.maximum(i - blocks_a, 0), 0)
    row = lambda i: (i, 0)
    fixed = lambda i: (0, 0)
    m = ma + mb
    return pl.pallas_call(
        functools.partial(_ln_router_kernel, alpha=alpha, blocks_a=blocks_a),
        grid=(blocks_a + blocks_b,),
        in_specs=[pl.BlockSpec((tm, d), seg_a), pl.BlockSpec((tm, d), seg_a),
                  pl.BlockSpec((tm, d), seg_b), pl.BlockSpec((tm, d), seg_b),
                  pl.BlockSpec((1, d), fixed), pl.BlockSpec((1, d), fixed),
                  pl.BlockSpec((d, LANES), fixed), pl.BlockSpec((1, LANES), fixed)],
        out_specs=[pl.BlockSpec((tm, d), row), pl.BlockSpec((tm, LANES), row), pl.BlockSpec((tm, LANES), row)],
        out_shape=[jax.ShapeDtypeStruct((m, d), F32), jax.ShapeDtypeStruct((m, LANES), I32),
                   jax.ShapeDtypeStruct((m, LANES), F32)],
        compiler_params=_cparams(("arbitrary",)),
        name="ln_router",
    )(xa, mixa, xb, mixb, g, b, wr, br)


DISPATCH_UNROLL = 8


def _dispatch_kernel(src_ref, rows_ref, x_ref, o_ref, buf_ref, sem):
    i = pl.program_id(0)
    n_rows = rows_ref[i]

    @pl.when(i == 0)
    def _():
        buf_ref[...] = jnp.zeros_like(buf_ref)

    def row_copy(r):
        tok = src_ref[i * MOE_ROWS + r]
        return pltpu.make_async_copy(x_ref.at[pl.ds(tok, 1)], buf_ref.at[pl.ds(r, 1)], sem)

    def trips(fn):
        def trip(t, carry):
            for u in range(DISPATCH_UNROLL):
                r = t * DISPATCH_UNROLL + u

                @pl.when(r < n_rows)
                def _(u=u):
                    fn(row_copy(r), u)

            return carry

        lax.fori_loop(0, (n_rows + DISPATCH_UNROLL - 1) // DISPATCH_UNROLL, trip, 0)

    trips(lambda cp, u: cp.start(priority=u % 2))
    trips(lambda cp, u: cp.wait())
    o_ref[...] = buf_ref[...].astype(BF16)


def _dispatch(src_tok, rows_in_block, x1, nb):
    d = x1.shape[1]
    return pl.pallas_call(
        _dispatch_kernel,
        grid_spec=pltpu.PrefetchScalarGridSpec(
            num_scalar_prefetch=2,
            grid=(nb,),
            in_specs=[pl.BlockSpec(memory_space=pl.ANY)],
            out_specs=pl.BlockSpec((MOE_ROWS, d), lambda i, s, u: (i, 0)),
            scratch_shapes=[pltpu.VMEM((MOE_ROWS, d), F32), pltpu.SemaphoreType.DMA(())]),
        out_shape=jax.ShapeDtypeStruct((nb * MOE_ROWS, d), BF16),
        compiler_params=_cparams(("arbitrary",)),
        name="moe_dispatch",
    )(src_tok, rows_in_block, x1)


def _expert_mm_kernel(run_ref, rune_ref, nruns_ref, nused_ref, x_ref, *refs, n_w, tn, epilogue):
    w_hbm = refs[:n_w]
    o_ref = refs[n_w]
    w_buf = refs[n_w + 1:2 * n_w + 1]
    w16 = refs[2 * n_w + 1:3 * n_w + 1]
    sem, state = refs[3 * n_w + 1], refs[3 * n_w + 2]
    j, i = pl.program_id(0), pl.program_id(1)
    n_runs = nruns_ref[0]
    total = pl.num_programs(0) * n_runs
    cur = j * n_runs + run_ref[i]

    def item_copies(item):
        e = rune_ref[item % n_runs]
        col = pl.multiple_of((item // n_runs) * tn, tn)
        slot = item % MOE_RING
        return [pltpu.make_async_copy(w_hbm[a].at[e, :, pl.ds(col, tn)], w_buf[a].at[slot], sem.at[a, slot])
                for a in range(n_w)]

    @pl.when((j == 0) & (i == 0))
    def _():
        state[0] = 0
        state[1] = -1

    def request(item, carry):
        for cp in item_copies(item):
            cp.start()
        return carry

    limit = jnp.minimum(cur + MOE_RING, total)
    lax.fori_loop(state[0], limit, request, 0)
    state[0] = jnp.maximum(state[0], limit)

    @pl.when(state[1] != cur)
    def _():
        for cp in item_copies(cur):
            cp.wait()
        slot = cur % MOE_RING
        for a in range(n_w):
            w16[a][...] = w_buf[a][slot].astype(BF16)
        state[1] = cur

    @pl.when(i < nused_ref[0])
    def _():
        x = x_ref[...]
        o_ref[...] = epilogue(*[_dot(x, w16[a][...]) for a in range(n_w)]).astype(o_ref.dtype)

    @pl.when(i >= nused_ref[0])
    def _():
        o_ref[...] = jnp.zeros_like(o_ref)


def _expert_mm(plan, x, weights, tn, epilogue, out_dtype, name):
    run_of_block, run_e, n_runs, n_used = plan
    nb = run_of_block.shape[0]
    kd, n = weights[0].shape[1], weights[0].shape[2]
    n_w = len(weights)
    return pl.pallas_call(
        functools.partial(_expert_mm_kernel, n_w=n_w, tn=tn, epilogue=epilogue),
        grid_spec=pltpu.PrefetchScalarGridSpec(
            num_scalar_prefetch=4,
            grid=(n // tn, nb),
            in_specs=[pl.BlockSpec((MOE_ROWS, kd), lambda j, i, run, rune, nruns, nused: (jnp.minimum(i, nused[0] - 1), 0))]
            + [pl.BlockSpec(memory_space=pl.ANY)] * n_w,
            out_specs=pl.BlockSpec((MOE_ROWS, tn), lambda j, i, *_: (i, j)),
            scratch_shapes=[pltpu.VMEM((MOE_RING, kd, tn), F32) for _ in range(n_w)]
            + [pltpu.VMEM((kd, tn), BF16) for _ in range(n_w)]
            + [pltpu.SemaphoreType.DMA((n_w, MOE_RING)), pltpu.SMEM((2,), I32)]),
        out_shape=jax.ShapeDtypeStruct((nb * MOE_ROWS, n), out_dtype),
        compiler_params=_cparams(("arbitrary", "arbitrary")),
        name=name,
    )(run_of_block, run_e, n_runs, n_used, x, *weights)


def _swiglu(a, u):
    return a * jax.nn.sigmoid(a) * u


def _combine_kernel(dest_ref, y_ref, x1_ref, gate_ref, g_ref, b_ref, o_ref, buf_ref, sem, *, tm, alpha, off):
    i = pl.program_id(0) + off

    def row_copy(r, k):
        slot = dest_ref[(i * tm + r) * EXPERT_TOPK + k]
        return pltpu.make_async_copy(y_ref.at[pl.ds(slot, 1)], buf_ref.at[k, pl.ds(r, 1)], sem)

    def start(r, carry):
        for k in range(EXPERT_TOPK):
            row_copy(r, k).start(priority=k % 2)
        return carry

    lax.fori_loop(0, tm, start, 0, unroll=4)

    def wait(r, carry):
        for k in range(EXPERT_TOPK):
            row_copy(r, k).wait()
        return carry

    lax.fori_loop(0, tm, wait, 0, unroll=4)
    gate = gate_ref[...]
    f = gate[:, 0:1] * buf_ref[0] + gate[:, 1:2] * buf_ref[1]
    o_ref[...] = _ln(alpha * x1_ref[...] + f, g_ref[...], b_ref[...])


def _combine(dest, ybuf, x1, gate, row_offset, m, g, b, alpha):
    d = x1.shape[1]
    tm = _pick(m, (128,))
    assert row_offset % tm == 0
    off = row_offset // tm
    return pl.pallas_call(
        functools.partial(_combine_kernel, tm=tm, alpha=alpha, off=off),
        grid_spec=pltpu.PrefetchScalarGridSpec(
            num_scalar_prefetch=1,
            grid=(m // tm,),
            in_specs=[pl.BlockSpec(memory_space=pl.ANY),
                      pl.BlockSpec((tm, d), lambda i, dd: (i + off, 0)),
                      pl.BlockSpec((tm, LANES), lambda i, dd: (i + off, 0)),
                      pl.BlockSpec((1, d), lambda i, dd: (0, 0)),
                      pl.BlockSpec((1, d), lambda i, dd: (0, 0))],
            out_specs=pl.BlockSpec((tm, d), lambda i, dd: (i, 0)),
            scratch_shapes=[pltpu.VMEM((EXPERT_TOPK, tm, d), F32), pltpu.SemaphoreType.DMA(())]),
        out_shape=jax.ShapeDtypeStruct((m, d), F32),
        compiler_params=_cparams(("arbitrary",)),
        name="moe_combine",
    )(dest, ybuf, x1, gate, g, b)


def _route_plan(e_all, nb):
    a_n = e_all.shape[0]
    onehot = (e_all[:, None] == jnp.arange(N_EXPERTS, dtype=I32)[None, :]).astype(I32)
    before = jnp.cumsum(onehot, axis=0) - onehot
    rank = jnp.sum(before * onehot, axis=1)
    counts = jnp.sum(onehot, axis=0)
    pad_counts = (counts + MOE_ROWS - 1) // MOE_ROWS * MOE_ROWS
    pad_ends = jnp.cumsum(pad_counts)
    pad_starts = pad_ends - pad_counts
    dest = (pad_starts[e_all] + rank).astype(I32)
    n_used = (pad_ends[-1] // MOE_ROWS).astype(I32)
    block_e = jnp.searchsorted(pad_ends, jnp.arange(nb, dtype=I32) * MOE_ROWS, side="right").astype(I32)
    owns = (counts > 0).astype(I32)
    run_id = jnp.cumsum(owns) - owns
    n_runs = jnp.sum(owns).astype(I32)
    last_e = jnp.max(jnp.where(counts > 0, jnp.arange(N_EXPERTS, dtype=I32), 0)).astype(I32)
    block_e = jnp.where(jnp.arange(nb, dtype=I32) < n_used, jnp.minimum(block_e, N_EXPERTS - 1), last_e)
    run_of_block = run_id[block_e].astype(I32)
    run_e = jnp.full((N_EXPERTS,), last_e, I32).at[jnp.where(counts > 0, run_id, N_EXPERTS)].set(
        jnp.arange(N_EXPERTS, dtype=I32), mode="drop")
    src_tok = jnp.zeros((nb * MOE_ROWS,), I32).at[dest].set(jnp.arange(a_n, dtype=I32) // EXPERT_TOPK)
    blk = jnp.arange(nb, dtype=I32)
    rows_in_block = jnp.where(
        blk < n_used, jnp.clip(counts[block_e] - (blk * MOE_ROWS - pad_starts[block_e]), 0, MOE_ROWS), 0).astype(I32)
    return dest, (run_of_block, run_e, n_runs.reshape(1), n_used.reshape(1)), src_tok, rows_in_block


def kernel(x_prompt, x_sample, cache_dsa_kv, cache_dsa_kidx, cache_nsa_cmp_kv, cache_nsa_slc_kv, state_nsa_win_kv, page_table, w_in, w_out, cmp_w1, cmp_w2, cmp_pe, ln1_g, ln1_b, w_router_group, b_router_group, w_router_expert, b_router_expert, w_gate, w_up, w_down, ln2_g, ln2_b):
    depth = w_in.shape[0]
    assert depth == 1 and x_sample.shape[1] == 1
    batch, seq, d_model = x_prompt.shape
    n_seq = x_sample.shape[0]
    n_pool, page = cache_dsa_kv.shape[1], cache_dsa_kv.shape[2]
    alpha = (2.0 * depth) ** 0.25
    m_p = batch * seq
    xp = x_prompt.reshape(m_p, d_model)
    xs = x_sample.reshape(n_seq, d_model)
    xp16, xs16 = xp.astype(BF16), xs.astype(BF16)

    w = w_in[0]
    t0, t1 = 4176, 9296
    rep_b = NSA_HEADS // NSA_KV_GROUPS
    w_head = w[:, :t0 + 48].astype(BF16)
    w_tail = w[:, t0:t1].astype(BF16)
    w_gates = jnp.pad(w[:, t1:].reshape(d_model, NSA_KV_GROUPS, 3 * rep_b),
                      ((0, 0), (0, 0), (0, LANES - 3 * rep_b))).reshape(d_model, NSA_KV_GROUPS * LANES).astype(BF16)
    c_dq, c_dk, c_dv, c_iq, c_ik = 0, 2048, 2560, 3072, 4096
    c_nq, c_ck, c_cv, c_sk, c_sv, c_wk, c_wv = 0, 2048, 2560, 3072, 3584, 4096, 4608

    def project(x16):
        qa = _matmul([x16], w_head, c_dq, Q_ROW)
        qi = _matmul([x16], w_head, c_iq, IDX_HEADS * IDX_DIM)
        hc = _matmul([x16], w_head, c_ik, LANES)
        qb = _matmul([x16], w_tail, c_nq, Q_ROW)
        gate = _matmul([x16], w_gates)
        kv = [_kv_proj(x16, w_head, c_dk, c_dv), _kv_proj(x16, w_tail, c_ck, c_cv),
              _kv_proj(x16, w_tail, c_sk, c_sv), _kv_proj(x16, w_tail, c_wk, c_wv)]
        return qa, qi, hc, qb, gate, kv

    qa_p, qi_p, hc_p, qb_p, gate_p, kv_p = project(xp16)
    qa_s, qi_s, hc_s, qb_s, gate_s, kv_s = project(xs16)

    w1r = cmp_w1[0].reshape(2, CMP_BLOCK // CMP_STRIDE, CMP_STRIDE, HEAD_DIM, HEAD_DIM)
    w1r = jnp.transpose(w1r, (0, 2, 3, 1, 4)).reshape(2, CMP_STRIDE * HEAD_DIM, 2 * HEAD_DIM)
    peb = _cmp_pe_bias(cmp_pe[0], cmp_w1[0])
    pages_per_seq = seq // page
    ident = jnp.arange(batch * pages_per_seq, dtype=I32).reshape(batch, pages_per_seq)
    ckv_p = _compress(kv_p[1][0], ident, page, w1r, cmp_w2[0], peb)
    ckv_s = _compress(cache_nsa_cmp_kv.reshape(-1, HEAD_DIM), page_table, page, w1r, cmp_w2[0], peb)

    oa_p = _dsa_prompt(qa_p, qi_p, hc_p, kv_p[0][1], batch, seq)
    ob_p = _nsa_prompt(qb_p, gate_p, ckv_p, kv_p[2][1], kv_p[3][1], batch, seq)

    keep_s = _dsa_sample_mask(jnp.swapaxes(cache_dsa_kidx[0], 1, 2), page_table,
                              qi_s.reshape(n_seq, IDX_HEADS, IDX_DIM),
                              hc_s[:, IDX_DIM:IDX_DIM + IDX_HEADS].reshape(n_seq, IDX_HEADS, 1), hc_s)
    oa_s = _dsa_sample(cache_dsa_kv.reshape(-1, HEAD_DIM), page_table, page,
                       qa_s.reshape(n_seq, DSA_HEADS, HEAD_DIM), keep_s, kv_s[0][1].astype(F32))
    ob_s, win_s = _nsa_sample(cache_nsa_slc_kv.reshape(-1, HEAD_DIM), page_table, page,
                              qb_s.reshape(n_seq, NSA_HEADS, HEAD_DIM),
                              gate_s.reshape(n_seq, NSA_KV_GROUPS, LANES)[:, :, :3 * rep_b].reshape(n_seq, NSA_HEADS, 3),
                              ckv_s,
                              state_nsa_win_kv.reshape(-1, HEAD_DIM), kv_s[2][1].astype(F32),
                              kv_s[3][1].astype(F32), kv_s[3][0])

    w_o = w_out[0].astype(BF16)
    wr = jnp.concatenate([w_router_group[0], w_router_expert[0],
                          jnp.zeros((d_model, LANES - N_GROUPS - N_EXPERTS), F32)], axis=1)
    br = jnp.concatenate([b_router_group[0], b_router_expert[0],
                          jnp.zeros((LANES - N_GROUPS - N_EXPERTS,), F32)]).reshape(1, LANES)
    g1, b1 = ln1_g[0].reshape(1, d_model), ln1_b[0].reshape(1, d_model)
    g2, b2 = ln2_g[0].reshape(1, d_model), ln2_b[0].reshape(1, d_model)
    x1, e_tok, gt_tok = _ln_router(
        xp, _matmul([oa_p, ob_p], w_o),
        xs, _matmul([oa_s.reshape(n_seq, Q_ROW), ob_s.reshape(n_seq, Q_ROW)], w_o), g1, b1, wr, br, alpha)

    e_all = e_tok[:, :EXPERT_TOPK].reshape(-1)
    a_n = e_all.shape[0]
    nb = -(-(a_n + N_EXPERTS * (MOE_ROWS - 1)) // MOE_ROWS)
    dest, plan, src_tok, rows_in_block = _route_plan(e_all, nb)
    xbuf = _dispatch(src_tok, rows_in_block, x1, nb)
    d_expert = w_gate.shape[3]
    hid = _expert_mm(plan, xbuf, [w_gate[0], w_up[0]], _pick(d_expert, (MOE_CHUNK, LANES)), _swiglu, BF16,
                     "moe_gate_up")
    ybuf = _expert_mm(plan, hid, [w_down[0]], _pick(d_model, (2048, 1024, 512, 256, LANES)), lambda y: y, F32,
                      "moe_down")
    y_p = _combine(dest, ybuf, x1, gt_tok, 0, m_p, g2, b2, alpha)
    y_s = _combine(dest, ybuf, x1, gt_tok, m_p, n_seq, g2, b2, alpha)

    def state(slab, lead):
        return slab.reshape((1,) + lead + (NSA_KV_GROUPS, 2, HEAD_DIM))

    n_win = min(WINDOW, seq)
    win_p = state(kv_p[3][0], (batch, seq))[:, :, seq - n_win:]
    return (y_p.reshape(batch, seq, d_model), y_s.reshape(n_seq, 1, d_model),
            state(kv_p[0][0], (batch, seq)), state(kv_s[0][0], (n_seq, 1)),
            hc_p[:, :IDX_DIM].reshape(1, batch, seq, IDX_DIM), hc_s[:, :IDX_DIM].reshape(1, n_seq, 1, IDX_DIM),
            state(kv_p[1][0], (batch, seq)), state(kv_s[1][0], (n_seq, 1)),
            state(kv_p[2][0], (batch, seq)), state(kv_s[2][0], (n_seq, 1)),
            win_p, state(win_s, (n_seq, state_nsa_win_kv.shape[2])))
```
